```python
import math, functools
import jax, jax.numpy as jnp
from jax import lax
import numpy as np

D_MODEL = 1024
BATCH = 2
SEQ = 8192
DEPTH = 2

ATT_HEADS = 8
ATT_HEAD_DIM = 64
ATT_WIDTH = ATT_HEADS * ATT_HEAD_DIM
ATT_PATTERNS = ((128, 1), (512, 4), (2048, 16))
ATT_BLOCK = 128
SSD_HEADS = 8
SSD_HEAD_DIM = 64
SSD_WIDTH = SSD_HEADS * SSD_HEAD_DIM
SSD_GROUPS = 2
SSD_STATE = 128
SSD_CHUNK = 128
SSD_CONV_CH = SSD_WIDTH + 2 * SSD_GROUPS * SSD_STATE
DT_MIN = 0.001
DT_MAX = 0.1
LRU_WIDTH = 512
LRU_BLOCKS = 8
LRU_BLOCK_W = LRU_WIDTH // LRU_BLOCKS
LRU_C = 8.0
CONV_WIDTH = 4
D_MIX = ATT_WIDTH + SSD_WIDTH + LRU_WIDTH
IN_COLS = 3 * ATT_WIDTH + SSD_WIDTH + SSD_CONV_CH + SSD_HEADS + 2 * LRU_WIDTH
D_FF = ((8 * D_MODEL + 3 * 256 - 1) // (3 * 256)) * 256
NORM_EPS = 1e-6
SSD_NORM_EPS = 1e-5

kernel_name = 'hybrid_dilated_attn_ssd_rglru_block'


def rmsnorm(x, g, eps=NORM_EPS):
    xf = x.astype(jnp.float32)
    y = xf * lax.rsqrt(jnp.mean(xf * xf, axis=-1, keepdims=True) + eps)
    return (y * g.astype(jnp.float32)).astype(x.dtype)


def causal_depthwise_conv(x, w, b):
    k_width, s = w.shape[0], x.shape[1]
    xp = jnp.pad(x, ((0, 0), (k_width - 1, 0), (0, 0)))
    y = b + xp[:, k_width - 1:] * w[k_width - 1]
    for k in range(k_width - 1):
        y = y + xp[:, k:k + s] * w[k]
    return y


def split_cols(proj):
    sizes = (ATT_WIDTH, ATT_WIDTH, ATT_WIDTH, SSD_WIDTH, SSD_CONV_CH, SSD_HEADS, LRU_WIDTH, LRU_WIDTH)
    offsets = np.cumsum(sizes)[:-1].tolist()
    return jnp.split(proj, offsets, axis=-1)


def dilated_branch(q, k, v, window, dilation, slopes):
    b, s, h, dh = q.shape
    length = s // dilation
    nb = -(-length // ATT_BLOCK)
    pad = nb * ATT_BLOCK - length

    def to_blocks(t):
        t = t.reshape(b, length, dilation, h, dh).transpose(0, 2, 3, 1, 4)
        t = jnp.pad(t, ((0, 0), (0, 0), (0, 0), (0, pad), (0, 0)))
        return t.reshape(b, dilation, h, nb, ATT_BLOCK, dh)

    def with_prev(t):
        prev = jnp.pad(t[:, :, :, :-1], ((0, 0), (0, 0), (0, 0), (1, 0), (0, 0), (0, 0)))
        return jnp.concatenate([prev, t], axis=4)

    qb, kb, vb = to_blocks(q), to_blocks(k), to_blocks(v)
    kk, vv = with_prev(kb), with_prev(vb)
    scores = jnp.einsum('brhnqd,brhnkd->brhnqk', qb, kk,
                        preferred_element_type=jnp.float32) * (ATT_HEAD_DIM ** -0.5)
    qi = jnp.arange(ATT_BLOCK)[:, None]
    ki = jnp.arange(2 * ATT_BLOCK)[None, :]
    dist = ATT_BLOCK + qi - ki
    band = (dist >= 0) & (dist <= window // dilation)
    valid = band[None] & ((jnp.arange(nb)[:, None, None] > 0) | (ki[None] >= ATT_BLOCK))
    alibi = -slopes[:, None, None] * (dilation * dist).astype(jnp.float32)
    scores = scores + alibi[None, None, :, None]
    scores = jnp.where(valid[None, None, None], scores, -jnp.inf)
    m = jnp.max(scores, axis=-1)
    p = jnp.exp(scores - m[..., None])
    den = jnp.sum(p, axis=-1)
    num = jnp.einsum('brhnqk,brhnkd->brhnqd', p, vv.astype(jnp.float32))

    def from_blocks(t):
        t = t.reshape(b, dilation, h, nb * ATT_BLOCK, *t.shape[5:])[:, :, :, :length]
        t = jnp.moveaxis(t, 3, 1)
        return t.reshape(b, s, h, *t.shape[4:])

    return from_blocks(num), from_blocks(m), from_blocks(den)


def dilated_attention(q, k, v):
    b, s, _ = q.shape
    shp = (b, s, ATT_HEADS, ATT_HEAD_DIM)
    q, k, v = q.reshape(shp), k.reshape(shp), v.reshape(shp)
    slopes = jnp.exp2(-8.0 * jnp.arange(1, ATT_HEADS + 1, dtype=jnp.float32) / ATT_HEADS)
    branches = [dilated_branch(q, k, v, w, d, slopes) for (w, d) in ATT_PATTERNS]
    m_all = functools.reduce(jnp.maximum, [br[1] for br in branches])
    num = jnp.zeros(shp, jnp.float32)
    den = jnp.zeros(shp[:3], jnp.float32)
    for n_g, m_g, d_g in branches:
        e = jnp.exp(m_g - m_all)
        num = num + n_g * e[..., None]
        den = den + d_g * e
    return (num / den[..., None]).reshape(b, s, ATT_WIDTH)


def ssd_chunked_scan(x, dt, a, bm, cm):
    b, s, h, p = x.shape
    n = bm.shape[-1]
    q = SSD_CHUNK
    nc = s // q
    rep = h // bm.shape[2]
    x = x.reshape(b, nc, q, h, p)
    dt = dt.reshape(b, nc, q, h)
    bh = jnp.repeat(bm, rep, axis=2).reshape(b, nc, q, h, n)
    ch = jnp.repeat(cm, rep, axis=2).reshape(b, nc, q, h, n)
    acs = jnp.cumsum(dt * a, axis=2)
    seg = acs[:, :, :, None, :] - acs[:, :, None, :, :]
    causal = jnp.tril(jnp.ones((q, q), dtype=bool))[None, None, :, :, None]
    lmat = jnp.exp(jnp.where(causal, seg, -jnp.inf))
    scores = jnp.einsum('bcihn,bcjhn->bcijh', ch, bh) * lmat * dt[:, :, None, :, :]
    y_diag = jnp.einsum('bcijh,bcjhp->bcihp', scores, x)
    decay_to_end = jnp.exp(acs[:, :, -1:, :] - acs)
    states = jnp.einsum('bcjhn,bcjh,bcjhp->bchpn', bh, decay_to_end * dt, x)
    chunk_decay = jnp.exp(acs[:, :, -1, :])

    def step(carry, inp):
        st, dc = inp
        return dc[:, :, None, None] * carry + st, carry

    _, prev = lax.scan(step, jnp.zeros((b, h, p, n), x.dtype),
                       (jnp.moveaxis(states, 1, 0), jnp.moveaxis(chunk_decay, 1, 0)))
    prev = jnp.moveaxis(prev, 0, 1)
    y_off = jnp.einsum('bcihn,bchpn,bcih->bcihp', ch, prev, jnp.exp(acs))
    return (y_diag + y_off).reshape(b, s, h, p)


def ssd_mixer(z, xbc, dt_raw, conv_w, conv_b, dt_bias, a_log, d_skip, norm_w):
    b, s, _ = z.shape
    f32 = jnp.float32
    xbc = jax.nn.silu(causal_depthwise_conv(xbc, conv_w, conv_b)).astype(f32)
    xs, bm, cm = jnp.split(xbc, [SSD_WIDTH, SSD_WIDTH + SSD_GROUPS * SSD_STATE], axis=-1)
    xs = xs.reshape(b, s, SSD_HEADS, SSD_HEAD_DIM)
    bm = bm.reshape(b, s, SSD_GROUPS, SSD_STATE)
    cm = cm.reshape(b, s, SSD_GROUPS, SSD_STATE)
    dt = jax.nn.softplus(dt_raw.astype(f32) + dt_bias.astype(f32))
    a = -jnp.exp(a_log.astype(f32))
    y = ssd_chunked_scan(xs, dt, a, bm, cm) + d_skip.astype(f32)[:, None] * xs
    y = y.reshape(b, s, SSD_WIDTH) * jax.nn.silu(z.astype(f32))
    yg = y.reshape(b, s, SSD_GROUPS, SSD_WIDTH // SSD_GROUPS)
    yg = yg * lax.rsqrt(jnp.mean(yg * yg, axis=-1, keepdims=True) + SSD_NORM_EPS)
    return yg.reshape(b, s, SSD_WIDTH) * norm_w.astype(f32)


def rglru_mixer(gate_in, x_in, conv_w, conv_b, wa, ba, wx, bx, lam):
    b, s, _ = x_in.shape
    f32 = jnp.float32
    gate = jax.nn.gelu(gate_in.astype(f32))
    xc = causal_depthwise_conv(x_in, conv_w, conv_b).astype(f32)
    xb = xc.reshape(b, s, LRU_BLOCKS, LRU_BLOCK_W)
    r = jax.nn.sigmoid(jnp.einsum('bsnc,ncd->bsnd', xb, wa.astype(f32)).reshape(b, s, LRU_WIDTH) + ba.astype(f32))
    i = jax.nn.sigmoid(jnp.einsum('bsnc,ncd->bsnd', xb, wx.astype(f32)).reshape(b, s, LRU_WIDTH) + bx.astype(f32))
    log_a = -LRU_C * r * jax.nn.softplus(-lam.astype(f32))
    a = jnp.exp(log_a)
    u = jnp.sqrt(-jnp.expm1(2.0 * log_a)) * (i * xc)

    def combine(left, right):
        a_l, h_l = left
        a_r, h_r = right
        return a_l * a_r, a_r * h_l + h_r

    _, h = lax.associative_scan(combine, (a, u), axis=1)
    return h * gate


def swiglu(h, w_gate, w_up, w_down):
    return (jax.nn.silu(h @ w_gate) * (h @ w_up)) @ w_down


def hybrid_layer(x, norm_mix, w_in, ssd_conv_w, ssd_conv_b, ssd_dt_bias, ssd_a_log, ssd_d, ssd_norm,
                 lru_conv_w, lru_conv_b, lru_wa, lru_ba, lru_wx, lru_bx, lru_lambda, w_out,
                 norm_ffn, w_gate, w_up, w_down):
    h = rmsnorm(x, norm_mix)
    q, k, v, z, xbc, dt_raw, g_lru, x_lru = split_cols(h @ w_in)
    att = dilated_attention(q, k, v).astype(x.dtype)
    ssd = ssd_mixer(z, xbc, dt_raw, ssd_conv_w, ssd_conv_b, ssd_dt_bias, ssd_a_log, ssd_d, ssd_norm).astype(x.dtype)
    lru = rglru_mixer(g_lru, x_lru, lru_conv_w, lru_conv_b, lru_wa, lru_ba, lru_wx, lru_bx, lru_lambda).astype(x.dtype)
    x = x + jnp.concatenate([att, ssd, lru], axis=-1) @ w_out
    x = x + swiglu(rmsnorm(x, norm_ffn), w_gate, w_up, w_down)
    return x


def setup_inputs(seed: int = 0) -> dict:
    key = jax.random.key(seed)
    ks = jax.random.split(key, 24)
    f32 = jnp.float32
    L = DEPTH

    def nrm(k, shape, scale):
        return scale * jax.random.normal(k, shape, f32)

    x = nrm(ks[0], (BATCH, SEQ, D_MODEL), 1.0)
    norm_mix = 1.0 + nrm(ks[1], (L, D_MODEL), 0.05)
    w_in = nrm(ks[2], (L, D_MODEL, IN_COLS), D_MODEL ** -0.5)
    ssd_conv_w = nrm(ks[3], (L, CONV_WIDTH, SSD_CONV_CH), CONV_WIDTH ** -0.5)
    ssd_conv_b = nrm(ks[4], (L, SSD_CONV_CH), 0.02)
    dt0 = jnp.exp(jax.random.uniform(ks[5], (L, SSD_HEADS), f32, math.log(DT_MIN), math.log(DT_MAX)))
    ssd_dt_bias = dt0 + jnp.log(-jnp.expm1(-dt0))
    ssd_a_log = jnp.log(jax.random.uniform(ks[6], (L, SSD_HEADS), f32, 1.0, 16.0))
    ssd_d = 1.0 + nrm(ks[7], (L, SSD_HEADS), 0.1)
    ssd_norm = 1.0 + nrm(ks[8], (L, SSD_WIDTH), 0.05)
    lru_conv_w = nrm(ks[9], (L, CONV_WIDTH, LRU_WIDTH), CONV_WIDTH ** -0.5)
    lru_conv_b = nrm(ks[10], (L, LRU_WIDTH), 0.02)
    lru_wa = nrm(ks[11], (L, LRU_BLOCKS, LRU_BLOCK_W, LRU_BLOCK_W), LRU_BLOCK_W ** -0.5)
    lru_ba = nrm(ks[12], (L, LRU_WIDTH), 0.02)
    lru_wx = nrm(ks[13], (L, LRU_BLOCKS, LRU_BLOCK_W, LRU_BLOCK_W), LRU_BLOCK_W ** -0.5)
    lru_bx = nrm(ks[14], (L, LRU_WIDTH), 0.02)
    a_c = jax.random.uniform(ks[15], (L, LRU_WIDTH), f32, 0.9, 0.999)
    a_base = a_c ** (1.0 / LRU_C)
    lru_lambda = jnp.log(a_base) - jnp.log1p(-a_base)
    w_out = nrm(ks[16], (L, D_MIX, D_MODEL), D_MIX ** -0.5)
    norm_ffn = 1.0 + nrm(ks[17], (L, D_MODEL), 0.05)
    w_gate = nrm(ks[18], (L, D_MODEL, D_FF), D_MODEL ** -0.5)
    w_up = nrm(ks[19], (L, D_MODEL, D_FF), D_MODEL ** -0.5)
    w_down = nrm(ks[20], (L, D_FF, D_MODEL), D_FF ** -0.5)
    norm_final = 1.0 + nrm(ks[21], (D_MODEL,), 0.05)
    return {'x': x, 'norm_mix': norm_mix, 'w_in': w_in, 'ssd_conv_w': ssd_conv_w, 'ssd_conv_b': ssd_conv_b,
            'ssd_dt_bias': ssd_dt_bias, 'ssd_a_log': ssd_a_log, 'ssd_d': ssd_d, 'ssd_norm': ssd_norm,
            'lru_conv_w': lru_conv_w, 'lru_conv_b': lru_conv_b, 'lru_wa': lru_wa, 'lru_ba': lru_ba,
            'lru_wx': lru_wx, 'lru_bx': lru_bx, 'lru_lambda': lru_lambda, 'w_out': w_out,
            'norm_ffn': norm_ffn, 'w_gate': w_gate, 'w_up': w_up, 'w_down': w_down, 'norm_final': norm_final}


def reference(x, norm_mix, w_in, ssd_conv_w, ssd_conv_b, ssd_dt_bias, ssd_a_log, ssd_d, ssd_norm,
              lru_conv_w, lru_conv_b, lru_wa, lru_ba, lru_wx, lru_bx, lru_lambda, w_out,
              norm_ffn, w_gate, w_up, w_down, norm_final):
    for l in range(DEPTH):
        x = hybrid_layer(x, norm_mix[l], w_in[l], ssd_conv_w[l], ssd_conv_b[l], ssd_dt_bias[l], ssd_a_log[l],
                         ssd_d[l], ssd_norm[l], lru_conv_w[l], lru_conv_b[l], lru_wa[l], lru_ba[l],
                         lru_wx[l], lru_bx[l], lru_lambda[l], w_out[l], norm_ffn[l], w_gate[l], w_up[l], w_down[l])
    return rmsnorm(x, norm_final)
```

```python
import functools
import math

import jax
import jax.numpy as jnp
from jax import lax
from jax.experimental import pallas as pl
from jax.experimental.pallas import tpu as pltpu

F32 = jnp.float32
BF16 = jnp.bfloat16

D_MODEL = 1024
ATT_HEADS = 8
ATT_HEAD_DIM = 64
ATT_WIDTH = ATT_HEADS * ATT_HEAD_DIM
ATT_PATTERNS = ((128, 1), (512, 4), (2048, 16))
ATT_BLOCK = 128
SSD_HEADS = 8
SSD_HEAD_DIM = 64
SSD_WIDTH = SSD_HEADS * SSD_HEAD_DIM
SSD_GROUPS = 2
SSD_STATE = 128
SSD_CHUNK = 128
SSD_CONV_CH = SSD_WIDTH + 2 * SSD_GROUPS * SSD_STATE
LRU_WIDTH = 512
LRU_BLOCKS = 8
LRU_BLOCK_W = LRU_WIDTH // LRU_BLOCKS
LRU_C = 8.0
CONV_WIDTH = 4
D_MIX = ATT_WIDTH + SSD_WIDTH + LRU_WIDTH
D_FF = 2816
NORM_EPS = 1e-6
SSD_NORM_EPS = 1e-5

LANES = 128
SUBLANES = 8
VMEM_LIMIT_BYTES = 56 * 1024 * 1024

ROW_TILE = 512
ATT_Q_TILE = 512
LRU_TILE = 256
DT_PAD = LANES
FF_CHUNKS = ((0, 768), (768, 1536), (1536, 2304), (2304, 2816))
CONV_PAD = SUBLANES

NEG_INF = float("-inf")


def _params(n_axes):
    return pltpu.CompilerParams(
        dimension_semantics=("arbitrary",) * n_axes,
        vmem_limit_bytes=VMEM_LIMIT_BYTES)


def _const_spec(shape):
    nd = len(shape)
    return pl.BlockSpec(shape, lambda *_: (0,) * nd, pipeline_mode=pl.Buffered(1))


def _rmsnorm(x, g):
    return x * lax.rsqrt(jnp.mean(x * x, axis=-1, keepdims=True) + NORM_EPS) * g


def _softplus(x):
    return jnp.maximum(x, 0.0) + jnp.log1p(jnp.exp(-jnp.abs(x)))


def _silu(x):
    return x * jax.nn.sigmoid(x)


_C_QKV = 3 * ATT_WIDTH
_C_Z = _C_QKV + SSD_WIDTH
_C_XBC = _C_Z + SSD_CONV_CH
_C_GL = _C_XBC + LRU_WIDTH
_C_XL = _C_GL + LRU_WIDTH
_C_DT = _C_XL + DT_PAD


def _inproj_kernel(x_ref, g_ref, w_ref, qkv_ref, z_ref, xbc_ref, gl_ref, xl_ref, dt_ref):
    h = _rmsnorm(x_ref[...], g_ref[...]).astype(BF16)

    def seg(a, b):
        return jnp.dot(h, w_ref[:, a:b], preferred_element_type=F32)

    qkv_ref[:, 0:ATT_WIDTH] = (seg(0, ATT_WIDTH) * (ATT_HEAD_DIM ** -0.5)).astype(BF16)
    qkv_ref[:, ATT_WIDTH:_C_QKV] = seg(ATT_WIDTH, _C_QKV).astype(BF16)
    z_ref[...] = seg(_C_QKV, _C_Z)
    xbc_ref[...] = seg(_C_Z, _C_XBC)
    gl_ref[...] = seg(_C_XBC, _C_GL)
    xl_ref[...] = seg(_C_GL, _C_XL)
    dt_ref[...] = seg(_C_XL, _C_DT)


def _inproj(x2, g, w):
    t = x2.shape[0]
    row = lambda width: pl.BlockSpec((ROW_TILE, width), lambda i: (i, 0))
    widths = (_C_QKV, SSD_WIDTH, SSD_CONV_CH, LRU_WIDTH, LRU_WIDTH, DT_PAD)
    dtypes = (BF16, F32, F32, F32, F32, F32)
    return pl.pallas_call(
        _inproj_kernel,
        grid=(t // ROW_TILE,),
        in_specs=[row(D_MODEL), _const_spec((1, D_MODEL)), _const_spec((D_MODEL, _C_DT))],
        out_specs=[row(wd) for wd in widths],
        out_shape=[jax.ShapeDtypeStruct((t, wd), dt) for wd, dt in zip(widths, dtypes)],
        compiler_params=_params(1),
        name="inproj",
    )(x2, g, w)


def _attn_kernel(*refs, dilation, first_branch, last_branch):
    q_ref, kc_ref, kp_ref, vc_ref, vp_ref = refs[:5]
    pos = 5
    if not first_branch:
        num_in, m_in, den_in = refs[pos:pos + 3]
        pos += 3
    if last_branch:
        att_ref = refs[pos]
        pos += 1
    else:
        num_out, m_out, den_out = refs[pos:pos + 3]
        pos += 3
    kk_s, vv_s, bias_s = refs[pos:pos + 3]

    n = pl.program_id(2)
    blk = ATT_BLOCK

    @pl.when((pl.program_id(0) == 0) & (pl.program_id(1) == 0) & (n == 0))
    def _init_bias():
        qi = lax.broadcasted_iota(jnp.int32, (blk, 2 * blk), 0)
        ki = lax.broadcasted_iota(jnp.int32, (blk, 2 * blk), 1)
        dist = blk + qi - ki
        band = (dist >= 0) & (dist <= blk)
        steps = (dilation * dist).astype(F32)
        for h in range(ATT_HEADS):
            slope = 2.0 ** (-8.0 * (h + 1) / ATT_HEADS)
            alibi = -slope * steps
            bias_s[h] = jnp.where(band, alibi, NEG_INF)
            bias_s[ATT_HEADS + h] = jnp.where(band & (ki >= blk), alibi, NEG_INF)

    kk_s[0:blk, :] = kp_ref[...]
    kk_s[blk:, :] = kc_ref[...]
    vv_s[0:blk, :] = vp_ref[...]
    vv_s[blk:, :] = vc_ref[...]

    lane = lax.broadcasted_iota(jnp.int32, (blk, LANES), 1)
    low_half = lane < ATT_HEAD_DIM

    def sub_block(i, carry):
        r0 = pl.multiple_of(i * blk, blk)
        rows = pl.ds(r0, blk)
        q = q_ref[rows, :]
        kk = kk_s[pl.ds(r0, 2 * blk), :]
        vv = vv_s[pl.ds(r0, 2 * blk), :]
        seq_start = jnp.where((n == 0) & (i == 0), ATT_HEADS, 0)
        for p in range(ATT_HEADS // 2):
            cols = slice(p * LANES, (p + 1) * LANES)
            qp, kp, vp = q[:, cols], kk[:, cols], vv[:, cols]
            parts = []
            for hh in range(2):
                qm = jnp.where(low_half if hh == 0 else ~low_half, qp, jnp.zeros_like(qp))
                s = lax.dot_general(qm, kp, (((1,), (1,)), ((), ())),
                                    preferred_element_type=F32)
                s = s + bias_s[seq_start + (2 * p + hh)]
                m = jnp.max(s, axis=-1, keepdims=True)
                e = jnp.exp(s - m)
                den = jnp.sum(e, axis=-1, keepdims=True)
                o = jnp.dot(e.astype(BF16), vp, preferred_element_type=F32)
                parts.append((o, m, den))
            (o0, m0, d0), (o1, m1, d1) = parts
            o_b = jnp.where(low_half, o0, o1)
            m_b = jnp.where(low_half, m0, m1)
            d_b = jnp.where(low_half, d0, d1)
            if first_branch:
                num, m_new, den = o_b, m_b, d_b
            else:
                m_old = m_in[rows, cols]
                m_new = jnp.maximum(m_old, m_b)
                a_old = jnp.exp(m_old - m_new)
                a_b = jnp.exp(m_b - m_new)
                num = num_in[rows, cols] * a_old + o_b * a_b
                den = den_in[rows, cols] * a_old + d_b * a_b
            if last_branch:
                att_ref[rows, cols] = (num / den).astype(BF16)
            else:
                num_out[rows, cols] = num
                m_out[rows, cols] = m_new
                den_out[rows, cols] = den
        return carry

    lax.fori_loop(0, ATT_Q_TILE // blk, sub_block, 0)


def _attn_branch(qkv, state, dilation, first_branch, last_branch):
    b, s, _ = qkv.shape
    length = s // dilation
    nq = length // ATT_Q_TILE
    sub = ATT_Q_TILE // ATT_BLOCK
    qkv_v = qkv.reshape(b, length, dilation * _C_QKV)

    def col(j):
        return lambda bi, r, n: (bi, n, 3 * r + j)

    def col_prev(j):
        return lambda bi, r, n: (bi, jnp.maximum(n * sub - 1, 0), 3 * r + j)

    cur = lambda j: pl.BlockSpec((None, ATT_Q_TILE, ATT_WIDTH), col(j))
    prev = lambda j: pl.BlockSpec((None, ATT_BLOCK, ATT_WIDTH), col_prev(j))
    st_spec = pl.BlockSpec((None, ATT_Q_TILE, ATT_WIDTH), lambda bi, r, n: (bi, n, r))

    in_specs = [cur(0), cur(1), prev(1), cur(2), prev(2)]
    operands = [qkv_v] * 5
    if not first_branch:
        in_specs += [st_spec] * 3
        operands += [a.reshape(b, length, dilation * ATT_WIDTH) for a in state]
    view = (b, length, dilation * ATT_WIDTH)
    if last_branch:
        out_specs = [st_spec]
        out_shape = [jax.ShapeDtypeStruct(view, BF16)]
    else:
        out_specs = [st_spec] * 3
        out_shape = [jax.ShapeDtypeStruct(view, F32)] * 3
    outs = pl.pallas_call(
        functools.partial(_attn_kernel, dilation=dilation, first_branch=first_branch,
                          last_branch=last_branch),
        grid=(b, dilation, nq),
        in_specs=in_specs,
        out_specs=out_specs,
        out_shape=out_shape,
        scratch_shapes=[
            pltpu.VMEM((ATT_Q_TILE + ATT_BLOCK, ATT_WIDTH), BF16),
            pltpu.VMEM((ATT_Q_TILE + ATT_BLOCK, ATT_WIDTH), BF16),
            pltpu.VMEM((2 * ATT_HEADS, ATT_BLOCK, 2 * ATT_BLOCK), F32),
        ],
        compiler_params=_params(3),
        name=f"attn_d{dilation}",
    )(*operands)
    return [o.reshape(b, s, ATT_WIDTH) for o in outs]


def _dilated_attention(qkv):
    state = None
    n_br = len(ATT_PATTERNS)
    for idx, (_, dilation) in enumerate(ATT_PATTERNS):
        state = _attn_branch(qkv, state, dilation, idx == 0, idx == n_br - 1)
    return state[0]


def _expand_heads(v, expand):
    return jnp.dot(v, expand, preferred_element_type=F32, precision=lax.Precision.HIGHEST)


def _ssd_kernel(z_ref, xbc_ref, dt_ref, cw_ref, cb_ref, dtb_ref, alog_ref, dsk_ref, nw_ref,
                tri_ref, exp_ref, y_ref, win_s, state_s):
    c = pl.program_id(1)
    q = SSD_CHUNK
    pad = CONV_PAD

    @pl.when(c == 0)
    def _reset():
        win_s[0:pad, :] = jnp.zeros((pad, SSD_CONV_CH), F32)
        state_s[...] = jnp.zeros_like(state_s)

    win_s[pad:, :] = xbc_ref[...]
    k_w = CONV_WIDTH
    conv = cb_ref[...] + win_s[pad:pad + q, :] * cw_ref[k_w - 1:k_w, :]
    for k in range(k_w - 1):
        off = pad - (k_w - 1) + k
        conv = conv + win_s[off:off + q, :] * cw_ref[k:k + 1, :]
    win_s[0:pad, :] = win_s[q:q + pad, :]
    xbc = _silu(conv)
    xs = xbc[:, :SSD_WIDTH]
    bm = xbc[:, SSD_WIDTH:SSD_WIDTH + SSD_GROUPS * SSD_STATE]
    cm = xbc[:, SSD_WIDTH + SSD_GROUPS * SSD_STATE:]

    dt = _softplus(dt_ref[...] + dtb_ref[...])
    a = -jnp.exp(alog_ref[...])
    acs = jnp.dot(tri_ref[...], dt * a, preferred_element_type=F32,
                  precision=lax.Precision.HIGHEST)
    acs_last = acs[q - 1:q, :]
    acs_t = acs.T
    dt_t = dt.T
    expand = exp_ref[...]
    w_end = _expand_heads(jnp.exp(acs_last - acs) * dt, expand)
    e_acs = _expand_heads(jnp.exp(acs), expand)
    chunk_decay = _expand_heads(jnp.broadcast_to(jnp.exp(acs_last), (SUBLANES, LANES)),
                                expand)[0:1, :]

    ti = lax.broadcasted_iota(jnp.int32, (q, q), 0)
    tj = lax.broadcasted_iota(jnp.int32, (q, q), 1)
    causal = ti >= tj
    lane = lax.broadcasted_iota(jnp.int32, (q, LANES), 1)
    low_half = lane < SSD_HEAD_DIM

    xs_b = xs.astype(BF16)
    xw_b = (xs * w_end).astype(BF16)
    heads_per_group = SSD_HEADS // SSD_GROUPS
    gw = heads_per_group * SSD_HEAD_DIM
    y_parts = []
    for g in range(SSD_GROUPS):
        bm_g = bm[:, g * SSD_STATE:(g + 1) * SSD_STATE]
        cm_g = cm[:, g * SSD_STATE:(g + 1) * SSD_STATE].astype(BF16)
        gmat = lax.dot_general(cm_g, bm_g.astype(BF16), (((1,), (1,)), ((), ())),
                               preferred_element_type=F32)
        glanes = slice(g * gw, (g + 1) * gw)
        st_old = state_s[:, glanes]
        y_off = jnp.dot(cm_g, st_old.astype(BF16), preferred_element_type=F32) * e_acs[:, glanes]
        st_new = st_old * chunk_decay[:, glanes] + jnp.dot(
            bm_g.T.astype(BF16), xw_b[:, glanes], preferred_element_type=F32)
        state_s[:, glanes] = st_new
        for pp in range(heads_per_group // 2):
            p = g * (heads_per_group // 2) + pp
            cols = slice(p * LANES, (p + 1) * LANES)
            x_pair = xs_b[:, cols]
            halves = []
            for hh in range(2):
                h = 2 * p + hh
                seg = acs[:, h:h + 1] - acs_t[h:h + 1, :]
                lmat = jnp.exp(jnp.where(causal, seg, NEG_INF))
                sc = gmat * lmat * dt_t[h:h + 1, :]
                halves.append(jnp.dot(sc.astype(BF16), x_pair, preferred_element_type=F32))
            y_parts.append(jnp.where(low_half, halves[0], halves[1])
                           + y_off[:, pp * LANES:(pp + 1) * LANES])
    y = jnp.concatenate(y_parts, axis=-1) + dsk_ref[...] * xs
    y = y * _silu(z_ref[...])
    outs = []
    for g in range(SSD_GROUPS):
        yg = y[:, g * gw:(g + 1) * gw]
        outs.append(yg * lax.rsqrt(jnp.mean(yg * yg, axis=-1, keepdims=True) + SSD_NORM_EPS))
    y_ref[...] = (jnp.concatenate(outs, axis=-1) * nw_ref[...]).astype(BF16)


def _ssd(z, xbc, dt, conv_w, conv_b, dt_bias, a_log, d_skip, norm_w, batch):
    t = z.shape[0]
    s = t // batch
    nc = s // SSD_CHUNK
    row = lambda width: pl.BlockSpec((SSD_CHUNK, width), lambda b, c: (b * nc + c, 0))
    pad_heads = lambda v: jnp.pad(v.astype(F32), (0, DT_PAD - SSD_HEADS)).reshape(1, DT_PAD)
    tri = jnp.tril(jnp.ones((SSD_CHUNK, SSD_CHUNK), F32))
    expand = (jnp.arange(LANES)[:, None] == (jnp.arange(SSD_WIDTH)[None, :] // SSD_HEAD_DIM)
              ).astype(F32)
    d_exp = jnp.repeat(d_skip.astype(F32), SSD_HEAD_DIM).reshape(1, SSD_WIDTH)
    return pl.pallas_call(
        _ssd_kernel,
        grid=(batch, nc),
        in_specs=[row(SSD_WIDTH), row(SSD_CONV_CH), row(DT_PAD),
                  _const_spec((CONV_WIDTH, SSD_CONV_CH)), _const_spec((1, SSD_CONV_CH)),
                  _const_spec((1, DT_PAD)), _const_spec((1, DT_PAD)),
                  _const_spec((1, SSD_WIDTH)), _const_spec((1, SSD_WIDTH)),
                  _const_spec((SSD_CHUNK, SSD_CHUNK)), _const_spec((LANES, SSD_WIDTH))],
        out_specs=row(SSD_WIDTH),
        out_shape=jax.ShapeDtypeStruct((t, SSD_WIDTH), BF16),
        scratch_shapes=[pltpu.VMEM((CONV_PAD + SSD_CHUNK, SSD_CONV_CH), F32),
                        pltpu.VMEM((SSD_STATE, SSD_WIDTH), F32)],
        compiler_params=_params(2),
        name="ssd",
    )(z, xbc, dt, conv_w, conv_b.reshape(1, -1), pad_heads(dt_bias), pad_heads(a_log),
      d_exp, norm_w.reshape(1, -1), tri, expand)


def _gelu_tanh(x):
    c = math.sqrt(2.0 / math.pi)
    return 0.5 * x * (1.0 + jnp.tanh(c * (x + 0.044715 * (x * x * x))))


def _lru_kernel(g_ref, x_ref, cw_ref, cb_ref, wa_ref, ba_ref, wx_ref, bx_ref, lam_ref,
                y_ref, win_s, h_s):
    c = pl.program_id(1)
    tt = LRU_TILE
    pad = CONV_PAD

    @pl.when(c == 0)
    def _reset():
        win_s[0:pad, :] = jnp.zeros((pad, LRU_WIDTH), F32)
        h_s[...] = jnp.zeros_like(h_s)

    win_s[pad:, :] = x_ref[...]
    k_w = CONV_WIDTH
    xc = cb_ref[...] + win_s[pad:pad + tt, :] * cw_ref[k_w - 1:k_w, :]
    for k in range(k_w - 1):
        off = pad - (k_w - 1) + k
        xc = xc + win_s[off:off + tt, :] * cw_ref[k:k + 1, :]
    win_s[0:pad, :] = win_s[tt:tt + pad, :]

    xb = xc.astype(BF16)
    r = jax.nn.sigmoid(jnp.dot(xb, wa_ref[...], preferred_element_type=F32) + ba_ref[...])
    i = jax.nn.sigmoid(jnp.dot(xb, wx_ref[...], preferred_element_type=F32) + bx_ref[...])
    log_a = -LRU_C * r * _softplus(-lam_ref[...])
    a = jnp.exp(log_a)
    u = jnp.sqrt(-jnp.tanh(log_a) * (a * a + 1.0)) * (i * xc)

    row = lax.broadcasted_iota(jnp.int32, (tt, LRU_WIDTH), 0)
    step = 1
    while step < tt:
        keep = row >= step
        a_sh = jnp.where(keep, pltpu.roll(a, step, 0), 1.0)
        u_sh = jnp.where(keep, pltpu.roll(u, step, 0), 0.0)
        u = a * u_sh + u
        a = a * a_sh
        step *= 2
    h = a * h_s[0:1, :] + u
    h_s[...] = jnp.broadcast_to(h[tt - 1:tt, :], h_s.shape)
    y_ref[...] = (h * _gelu_tanh(g_ref[...])).astype(BF16)


def _block_diag(w):
    nb, c, d = w.shape
    eye = jnp.eye(nb, dtype=w.dtype)
    return (eye[:, None, :, None] * w[:, :, None, :]).reshape(nb * c, nb * d)


def _lru(g_in, x_in, conv_w, conv_b, wa, ba, wx, bx, lam, batch):
    t = g_in.shape[0]
    s = t // batch
    nt = s // LRU_TILE
    row = pl.BlockSpec((LRU_TILE, LRU_WIDTH), lambda b, c: (b * nt + c, 0))
    vec = lambda v: v.astype(F32).reshape(1, LRU_WIDTH)
    mat = _const_spec((LRU_WIDTH, LRU_WIDTH))
    one = _const_spec((1, LRU_WIDTH))
    return pl.pallas_call(
        _lru_kernel,
        grid=(batch, nt),
        in_specs=[row, row, _const_spec((CONV_WIDTH, LRU_WIDTH)), one, mat, one, mat, one, one],
        out_specs=row,
        out_shape=jax.ShapeDtypeStruct((t, LRU_WIDTH), BF16),
        scratch_shapes=[pltpu.VMEM((CONV_PAD + LRU_TILE, LRU_WIDTH), F32),
                        pltpu.VMEM((SUBLANES, LRU_WIDTH), F32)],
        compiler_params=_params(2),
        name="rglru",
    )(g_in, x_in, conv_w, vec(conv_b), _block_diag(wa).astype(BF16), vec(ba),
      _block_diag(wx).astype(BF16), vec(bx), vec(lam))


def _outffn_kernel(x_ref, att_ref, ssd_ref, lru_ref, wo_ref, gf_ref, wg_ref, wu_ref, wd_ref,
                   nfin_ref, o_ref, hn_s, *, final_norm):
    x1 = x_ref[...]
    for j, m_ref in enumerate((att_ref, ssd_ref, lru_ref)):
        x1 = x1 + jnp.dot(m_ref[...], wo_ref[j * ATT_WIDTH:(j + 1) * ATT_WIDTH, :],
                          preferred_element_type=F32)
    hn_s[...] = _rmsnorm(x1, gf_ref[...]).astype(BF16)
    o_ref[...] = x1
    for c0, c1 in FF_CHUNKS:
        hn = hn_s[...]
        gate = jnp.dot(hn, wg_ref[:, c0:c1], preferred_element_type=F32)
        up = jnp.dot(hn, wu_ref[:, c0:c1], preferred_element_type=F32)
        act = (_silu(gate) * up).astype(BF16)
        o_ref[...] += jnp.dot(act, wd_ref[c0:c1, :], preferred_element_type=F32)
    if final_norm:
        o_ref[...] = _rmsnorm(o_ref[...], nfin_ref[...])


def _outffn(x2, att, ssd, lru, w_out, norm_ffn, w_gate, w_up, w_down, norm_final, final_norm):
    t = x2.shape[0]
    row = lambda width: pl.BlockSpec((ROW_TILE, width), lambda i: (i, 0))
    return pl.pallas_call(
        functools.partial(_outffn_kernel, final_norm=final_norm),
        grid=(t // ROW_TILE,),
        in_specs=[row(D_MODEL), row(ATT_WIDTH), row(SSD_WIDTH), row(LRU_WIDTH),
                  _const_spec((D_MIX, D_MODEL)), _const_spec((1, D_MODEL)),
                  _const_spec((D_MODEL, D_FF)), _const_spec((D_MODEL, D_FF)),
                  _const_spec((D_FF, D_MODEL)), _const_spec((1, D_MODEL))],
        out_specs=row(D_MODEL),
        out_shape=jax.ShapeDtypeStruct((t, D_MODEL), F32),
        scratch_shapes=[pltpu.VMEM((ROW_TILE, D_MODEL), BF16)],
        compiler_params=_params(1),
        name="outffn",
    )(x2, att, ssd, lru, w_out.astype(BF16), norm_ffn.reshape(1, -1), w_gate.astype(BF16),
      w_up.astype(BF16), w_down.astype(BF16), norm_final.reshape(1, -1))


def _prep_w_in(w):
    o_dt = 3 * ATT_WIDTH + SSD_WIDTH + SSD_CONV_CH
    o_gl = o_dt + SSD_HEADS
    dt_cols = jnp.pad(w[:, o_dt:o_gl], ((0, 0), (0, DT_PAD - SSD_HEADS)))
    return jnp.concatenate([w[:, :o_dt], w[:, o_gl:], dt_cols], axis=1).astype(BF16)


def _layer(x2, batch, p, norm_final, final_norm):
    t = x2.shape[0]
    qkv, z, xbc, g_lru, x_lru, dt = _inproj(x2, p["norm_mix"].reshape(1, -1), _prep_w_in(p["w_in"]))
    att = _dilated_attention(qkv.reshape(batch, t // batch, _C_QKV)).reshape(t, ATT_WIDTH)
    ssd = _ssd(z, xbc, dt, p["ssd_conv_w"], p["ssd_conv_b"], p["ssd_dt_bias"], p["ssd_a_log"],
               p["ssd_d"], p["ssd_norm"], batch)
    lru = _lru(g_lru, x_lru, p["lru_conv_w"], p["lru_conv_b"], p["lru_wa"], p["lru_ba"],
               p["lru_wx"], p["lru_bx"], p["lru_lambda"], batch)
    return _outffn(x2, att, ssd, lru, p["w_out"], p["norm_ffn"], p["w_gate"], p["w_up"],
                   p["w_down"], norm_final, final_norm)


def kernel(x, norm_mix, w_in, ssd_conv_w, ssd_conv_b, ssd_dt_bias, ssd_a_log, ssd_d, ssd_norm,
           lru_conv_w, lru_conv_b, lru_wa, lru_ba, lru_wx, lru_bx, lru_lambda, w_out,
           norm_ffn, w_gate, w_up, w_down, norm_final):
    batch, seq, _ = x.shape
    stacked = dict(norm_mix=norm_mix, w_in=w_in, ssd_conv_w=ssd_conv_w, ssd_conv_b=ssd_conv_b,
                   ssd_dt_bias=ssd_dt_bias, ssd_a_log=ssd_a_log, ssd_d=ssd_d, ssd_norm=ssd_norm,
                   lru_conv_w=lru_conv_w, lru_conv_b=lru_conv_b, lru_wa=lru_wa, lru_ba=lru_ba,
                   lru_wx=lru_wx, lru_bx=lru_bx, lru_lambda=lru_lambda, w_out=w_out,
                   norm_ffn=norm_ffn, w_gate=w_gate, w_up=w_up, w_down=w_down)
    depth = w_in.shape[0]
    x2 = x.reshape(batch * seq, D_MODEL)
    for l in range(depth):
        p = {k: v[l] for k, v in stacked.items()}
        x2 = _layer(x2, batch, p, norm_final, l == depth - 1)
    return x2.reshape(batch, seq, D_MODEL)
```

```python
import functools
import math

import jax
import jax.numpy as jnp
from jax import lax
from jax.experimental import pallas as pl
from jax.experimental.pallas import tpu as pltpu

F32 = jnp.float32
BF16 = jnp.bfloat16

D_MODEL = 1024
ATT_HEADS = 8
ATT_HEAD_DIM = 64
ATT_WIDTH = ATT_HEADS * ATT_HEAD_DIM
ATT_BLOCK = 128
ATT_DILATIONS = (1, 4, 16)
SSD_HEADS = 8
SSD_HEAD_DIM = 64
SSD_WIDTH = SSD_HEADS * SSD_HEAD_DIM
SSD_GROUPS = 2
SSD_STATE = 128
SSD_CHUNK = 128
SSD_CONV_CH = SSD_WIDTH + 2 * SSD_GROUPS * SSD_STATE
LRU_WIDTH = 512
LRU_BLOCKS = 8
LRU_BLOCK_W = LRU_WIDTH // LRU_BLOCKS
LRU_C = 8.0
CONV_WIDTH = 4
D_MIX = ATT_WIDTH + SSD_WIDTH + LRU_WIDTH
D_FF = 2816
NORM_EPS = 1e-6
SSD_NORM_EPS = 1e-5

LANES = 128
SUBLANES = 8
VMEM_LIMIT_BYTES = 56 * 1024 * 1024

ROW_TILE = 512
LRU_TILE = 256
DT_PAD = LANES
FF_CHUNKS = ((0, 768), (768, 1536), (1536, 2304), (2304, 2816))
CONV_PAD = SUBLANES

DIL_MID, DIL_MAX = ATT_DILATIONS[1], ATT_DILATIONS[2]
DIL_STEP = DIL_MAX // DIL_MID
ATT_TILE = ATT_BLOCK * DIL_MAX
MID_LEN = ATT_TILE // DIL_MID
HEADS_PER_PAIR = LANES // ATT_HEAD_DIM
ATT_PAIRS = ATT_WIDTH // LANES
ATT_UNROLL = 8
assert ATT_HEADS == 8 and HEADS_PER_PAIR == 2 and DIL_STEP == DIL_MID and ATT_DILATIONS[0] == 1

NEG_INF = float("-inf")


def _params(n_axes):
    return pltpu.CompilerParams(
        dimension_semantics=("arbitrary",) * n_axes,
        vmem_limit_bytes=VMEM_LIMIT_BYTES)


def _const_spec(shape):
    nd = len(shape)
    return pl.BlockSpec(shape, lambda *_: (0,) * nd, pipeline_mode=pl.Buffered(1))


def _rmsnorm(x, g):
    return x * lax.rsqrt(jnp.mean(x * x, axis=-1, keepdims=True) + NORM_EPS) * g


def _softplus(x):
    return jnp.maximum(x, 0.0) + jnp.log1p(jnp.exp(-jnp.abs(x)))


def _silu(x):
    return x * jax.nn.sigmoid(x)


_C_QKV = 3 * ATT_WIDTH
_C_Z = _C_QKV + SSD_WIDTH
_C_XBC = _C_Z + SSD_CONV_CH
_C_GL = _C_XBC + LRU_WIDTH
_C_XL = _C_GL + LRU_WIDTH
_C_DT = _C_XL + DT_PAD


def _inproj_kernel(x_ref, g_ref, w_ref, qkv_ref, z_ref, xbc_ref, gl_ref, xl_ref, dt_ref):
    h = _rmsnorm(x_ref[...], g_ref[...]).astype(BF16)

    def seg(a, b):
        return jnp.dot(h, w_ref[:, a:b], preferred_element_type=F32)

    qkv_ref[:, 0:ATT_WIDTH] = seg(0, ATT_WIDTH) * (ATT_HEAD_DIM ** -0.5)
    qkv_ref[:, ATT_WIDTH:_C_QKV] = seg(ATT_WIDTH, _C_QKV)
    z_ref[...] = seg(_C_QKV, _C_Z)
    xbc_ref[...] = seg(_C_Z, _C_XBC)
    gl_ref[...] = seg(_C_XBC, _C_GL)
    xl_ref[...] = seg(_C_GL, _C_XL)
    dt_ref[...] = seg(_C_XL, _C_DT)


def _inproj(x2, g, w):
    t = x2.shape[0]
    row = lambda width: pl.BlockSpec((ROW_TILE, width), lambda i: (i, 0))
    widths = (_C_QKV, SSD_WIDTH, SSD_CONV_CH, LRU_WIDTH, LRU_WIDTH, DT_PAD)
    return pl.pallas_call(
        _inproj_kernel,
        grid=(t // ROW_TILE,),
        in_specs=[row(D_MODEL), _const_spec((1, D_MODEL)), _const_spec((D_MODEL, _C_DT))],
        out_specs=[row(wd) for wd in widths],
        out_shape=[jax.ShapeDtypeStruct((t, wd), F32) for wd in widths],
        compiler_params=_params(1),
        name="inproj",
    )(x2, g, w)


def _pair_block(q_lo, q_hi, kk, vv, bias_lo, bias_hi, low_half):
    parts = []
    for qm, bias in ((q_lo, bias_lo), (q_hi, bias_hi)):
        s = lax.dot_general(qm, kk, (((1,), (1,)), ((), ())), preferred_element_type=F32)
        s = s + bias
        m = jnp.max(s, axis=-1, keepdims=True)
        e = jnp.exp(s - m)
        den = jnp.sum(e, axis=-1, keepdims=True)
        o = jnp.dot(e.astype(BF16), vv, preferred_element_type=F32)
        parts.append((o, m, den))
    (o0, m0, d0), (o1, m1, d1) = parts
    return (jnp.where(low_half, o0, o1), jnp.where(low_half, m0, m1),
            jnp.where(low_half, d0, d1))


def _attn_kernel(q_ref, k_ref, v_ref, att_ref,
                 qn_s, kn_s, vn_s, tmp_s, qm_s, km_s, vm_s, qx_s, kx_s, vx_s,
                 num_s, m_s, den_s, bias_s):
    pair = pl.program_id(1)
    tile = pl.program_id(2)
    blk = ATT_BLOCK
    first_tile = tile == 0

    @pl.when(first_tile)
    def _start_sequence():
        qi = lax.broadcasted_iota(jnp.int32, (blk, 2 * blk), 0)
        ki = lax.broadcasted_iota(jnp.int32, (blk, 2 * blk), 1)
        dist = blk + qi - ki
        band = (dist >= 0) & (dist <= blk)
        band_first = band & (ki >= blk)
        for hh in range(HEADS_PER_PAIR):
            expo = (126 - HEADS_PER_PAIR * pair - hh) << 23
            slope = lax.bitcast_convert_type(jnp.full((blk, 2 * blk), expo, jnp.int32), F32)
            for br, dil in enumerate((DIL_MAX, DIL_MID, 1)):
                alibi = -slope * (dil * dist).astype(F32)
                bias_s[br * 4 + hh] = jnp.where(band, alibi, NEG_INF)
                bias_s[br * 4 + 2 + hh] = jnp.where(band_first, alibi, NEG_INF)
        kn_s[0:blk, :] = jnp.zeros((blk, LANES), BF16)
        vn_s[0:blk, :] = jnp.zeros((blk, LANES), BF16)
        km_s[:, 0:blk, :] = jnp.zeros((DIL_MID, blk, LANES), BF16)
        vm_s[:, 0:blk, :] = jnp.zeros((DIL_MID, blk, LANES), BF16)
        kx_s[:, 0:blk, :] = jnp.zeros((DIL_MAX, blk, LANES), BF16)
        vx_s[:, 0:blk, :] = jnp.zeros((DIL_MAX, blk, LANES), BF16)

    @pl.when(tile > 0)
    def _carry_history():
        for nat, mid, big in ((kn_s, km_s, kx_s), (vn_s, vm_s, vx_s)):
            nat[0:blk, :] = nat[ATT_TILE:ATT_TILE + blk, :]
            mid[:, 0:blk, :] = mid[:, MID_LEN:MID_LEN + blk, :]
            big[:, 0:blk, :] = big[:, blk:2 * blk, :]

    lane_m = lax.broadcasted_iota(jnp.int32, (MID_LEN, LANES), 1) < ATT_HEAD_DIM
    lane_b = lax.broadcasted_iota(jnp.int32, (blk, LANES), 1) < ATT_HEAD_DIM

    def put_q(dst, idx, x, mask):
        dst[(0,) + idx] = jnp.where(mask, x, 0.0).astype(BF16)
        dst[(1,) + idx] = jnp.where(mask, 0.0, x).astype(BF16)

    def put_kv(dst, idx, x, mask):
        del mask
        dst[idx] = x.astype(BF16)

    for src, nat, mid, big, put, hist in (
            (q_ref, qn_s, qm_s, qx_s, put_q, 0),
            (k_ref, kn_s, km_s, kx_s, put_kv, blk),
            (v_ref, vn_s, vm_s, vx_s, put_kv, blk)):
        for c in range(DIL_MID):
            rows = slice(c * MID_LEN, (c + 1) * MID_LEN)
            put(nat, (slice(hist + c * MID_LEN, hist + (c + 1) * MID_LEN), slice(None)),
                src[rows, :], lane_m)
        for r in range(DIL_MID):
            x = src[pl.ds(r, MID_LEN, stride=DIL_MID), :]
            tmp_s[r] = x
            put(mid, (r, slice(hist, hist + MID_LEN), slice(None)), x, lane_m)
        for r in range(DIL_MID):
            for c in range(DIL_STEP):
                y = tmp_s.at[r][pl.ds(c, blk, stride=DIL_STEP), :]
                put(big, (r + DIL_MID * c, slice(hist, hist + blk), slice(None)), y, lane_b)

    def bias_pair(branch, is_first):
        base = branch * 4 + jnp.where(is_first, 2, 0)
        return bias_s[base], bias_s[base + 1]

    def merge(rows, o_b, m_b, d_b):
        m_old = m_s[rows, :]
        m_new = jnp.maximum(m_old, m_b)
        a_old = jnp.exp(m_old - m_new)
        a_b = jnp.exp(m_b - m_new)
        num = num_s[rows, :] * a_old + o_b * a_b
        den = den_s[rows, :] * a_old + d_b * a_b
        return num, m_new, den

    def big_block(r, carry):
        b_lo, b_hi = bias_pair(0, first_tile)
        o_b, m_b, d_b = _pair_block(qx_s[0, r], qx_s[1, r], kx_s[r], vx_s[r], b_lo, b_hi, lane_b)
        rows = pl.ds(r, blk, stride=DIL_MAX)
        num_s[rows, :] = o_b
        m_s[rows, :] = m_b
        den_s[rows, :] = d_b
        return carry

    lax.fori_loop(0, DIL_MAX, big_block, 0, unroll=ATT_UNROLL)

    mid_blocks = MID_LEN // blk

    def mid_block(i, carry):
        r = i // mid_blocks
        j = i % mid_blocks
        q0 = pl.multiple_of(j * blk, blk)
        b_lo, b_hi = bias_pair(1, first_tile & (j == 0))
        o_b, m_b, d_b = _pair_block(
            qm_s[0, r, pl.ds(q0, blk), :], qm_s[1, r, pl.ds(q0, blk), :],
            km_s[r, pl.ds(q0, 2 * blk), :], vm_s[r, pl.ds(q0, 2 * blk), :], b_lo, b_hi, lane_b)
        rows = pl.ds(j * (blk * DIL_MID) + r, blk, stride=DIL_MID)
        num, m_new, den = merge(rows, o_b, m_b, d_b)
        num_s[rows, :] = num
        m_s[rows, :] = m_new
        den_s[rows, :] = den
        return carry

    lax.fori_loop(0, DIL_MID * mid_blocks, mid_block, 0, unroll=ATT_UNROLL)

    def nat_block(j, carry):
        q0 = pl.multiple_of(j * blk, blk)
        rows = pl.ds(q0, blk)
        b_lo, b_hi = bias_pair(2, first_tile & (j == 0))
        o_b, m_b, d_b = _pair_block(
            qn_s[0, rows, :], qn_s[1, rows, :],
            kn_s[pl.ds(q0, 2 * blk), :], vn_s[pl.ds(q0, 2 * blk), :], b_lo, b_hi, lane_b)
        num, _, den = merge(rows, o_b, m_b, d_b)
        att_ref[rows, :] = (num / den).astype(BF16)
        return carry

    lax.fori_loop(0, ATT_TILE // blk, nat_block, 0, unroll=ATT_UNROLL)


def _dilated_attention(qkv, batch):
    t = qkv.shape[0]
    tiles = t // batch // ATT_TILE
    blk = ATT_BLOCK

    def spec(col0):
        return pl.BlockSpec((ATT_TILE, LANES), lambda b, p, i: (b * tiles + i, col0 + p))

    bf = lambda *shape: pltpu.VMEM(shape, BF16)
    return pl.pallas_call(
        _attn_kernel,
        grid=(batch, ATT_PAIRS, tiles),
        in_specs=[spec(0), spec(ATT_PAIRS), spec(2 * ATT_PAIRS)],
        out_specs=spec(0),
        out_shape=jax.ShapeDtypeStruct((t, ATT_WIDTH), BF16),
        scratch_shapes=[
            bf(2, ATT_TILE, LANES), bf(blk + ATT_TILE, LANES), bf(blk + ATT_TILE, LANES),
            pltpu.VMEM((DIL_MID, MID_LEN, LANES), F32),
            bf(2, DIL_MID, MID_LEN, LANES), bf(DIL_MID, blk + MID_LEN, LANES),
            bf(DIL_MID, blk + MID_LEN, LANES),
            bf(2, DIL_MAX, blk, LANES), bf(DIL_MAX, 2 * blk, LANES), bf(DIL_MAX, 2 * blk, LANES),
            pltpu.VMEM((ATT_TILE, LANES), F32), pltpu.VMEM((ATT_TILE, LANES), F32),
            pltpu.VMEM((ATT_TILE, LANES), F32),
            pltpu.VMEM((3 * 2 * HEADS_PER_PAIR, blk, 2 * blk), F32),
        ],
        compiler_params=_params(3),
        name="attn",
    )(qkv, qkv, qkv)


def _expand_heads(v, expand):
    return jnp.dot(v, expand, preferred_element_type=F32, precision=lax.Precision.HIGHEST)


def _ssd_kernel(z_ref, xbc_ref, dt_ref, cw_ref, cb_ref, dtb_ref, alog_ref, dsk_ref, nw_ref,
                tri_ref, exp_ref, y_ref, win_s, state_s):
    c = pl.program_id(1)
    q = SSD_CHUNK
    pad = CONV_PAD

    @pl.when(c == 0)
    def _reset():
        win_s[0:pad, :] = jnp.zeros((pad, SSD_CONV_CH), F32)
        state_s[...] = jnp.zeros_like(state_s)

    win_s[pad:, :] = xbc_ref[...]
    k_w = CONV_WIDTH
    conv = cb_ref[...] + win_s[pad:pad + q, :] * cw_ref[k_w - 1:k_w, :]
    for k in range(k_w - 1):
        off = pad - (k_w - 1) + k
        conv = conv + win_s[off:off + q, :] * cw_ref[k:k + 1, :]
    win_s[0:pad, :] = win_s[q:q + pad, :]
    xbc = _silu(conv)
    xs = xbc[:, :SSD_WIDTH]
    bm = xbc[:, SSD_WIDTH:SSD_WIDTH + SSD_GROUPS * SSD_STATE]
    cm = xbc[:, SSD_WIDTH + SSD_GROUPS * SSD_STATE:]

    dt = _softplus(dt_ref[...] + dtb_ref[...])
    a = -jnp.exp(alog_ref[...])
    acs = jnp.dot(tri_ref[...], dt * a, preferred_element_type=F32,
                  precision=lax.Precision.HIGHEST)
    acs_last = acs[q - 1:q, :]
    acs_t = acs.T
    dt_t = dt.T
    expand = exp_ref[...]
    w_end = _expand_heads(jnp.exp(acs_last - acs) * dt, expand)
    e_acs = _expand_heads(jnp.exp(acs), expand)
    chunk_decay = _expand_heads(jnp.broadcast_to(jnp.exp(acs_last), (SUBLANES, LANES)),
                                expand)[0:1, :]

    ti = lax.broadcasted_iota(jnp.int32, (q, q), 0)
    tj = lax.broadcasted_iota(jnp.int32, (q, q), 1)
    causal = ti >= tj
    lane = lax.broadcasted_iota(jnp.int32, (q, LANES), 1)
    low_half = lane < SSD_HEAD_DIM

    xs_b = xs.astype(BF16)
    xw_b = (xs * w_end).astype(BF16)
    heads_per_group = SSD_HEADS // SSD_GROUPS
    gw = heads_per_group * SSD_HEAD_DIM
    y_parts = []
    for g in range(SSD_GROUPS):
        bm_g = bm[:, g * SSD_STATE:(g + 1) * SSD_STATE]
        cm_g = cm[:, g * SSD_STATE:(g + 1) * SSD_STATE].astype(BF16)
        gmat = lax.dot_general(cm_g, bm_g.astype(BF16), (((1,), (1,)), ((), ())),
                               preferred_element_type=F32)
        glanes = slice(g * gw, (g + 1) * gw)
        st_old = state_s[:, glanes]
        y_off = jnp.dot(cm_g, st_old.astype(BF16), preferred_element_type=F32) * e_acs[:, glanes]
        st_new = st_old * chunk_decay[:, glanes] + jnp.dot(
            bm_g.T.astype(BF16), xw_b[:, glanes], preferred_element_type=F32)
        state_s[:, glanes] = st_new
        for pp in range(heads_per_group // 2):
            p = g * (heads_per_group // 2) + pp
            cols = slice(p * LANES, (p + 1) * LANES)
            x_pair = xs_b[:, cols]
            halves = []
            for hh in range(2):
                h = 2 * p + hh
                seg = acs[:, h:h + 1] - acs_t[h:h + 1, :]
                lmat = jnp.exp(jnp.where(causal, seg, NEG_INF))
                sc = gmat * lmat * dt_t[h:h + 1, :]
                halves.append(jnp.dot(sc.astype(BF16), x_pair, preferred_element_type=F32))
            y_parts.append(jnp.where(low_half, halves[0], halves[1])
                           + y_off[:, pp * LANES:(pp + 1) * LANES])
    y = jnp.concatenate(y_parts, axis=-1) + dsk_ref[...] * xs
    y = y * _silu(z_ref[...])
    outs = []
    for g in range(SSD_GROUPS):
        yg = y[:, g * gw:(g + 1) * gw]
        outs.append(yg * lax.rsqrt(jnp.mean(yg * yg, axis=-1, keepdims=True) + SSD_NORM_EPS))
    y_ref[...] = (jnp.concatenate(outs, axis=-1) * nw_ref[...]).astype(BF16)


def _ssd(z, xbc, dt, conv_w, conv_b, dt_bias, a_log, d_skip, norm_w, batch):
    t = z.shape[0]
    s = t // batch
    nc = s // SSD_CHUNK
    row = lambda width: pl.BlockSpec((SSD_CHUNK, width), lambda b, c: (b * nc + c, 0))
    pad_heads = lambda v: jnp.pad(v.astype(F32), (0, DT_PAD - SSD_HEADS)).reshape(1, DT_PAD)
    tri = jnp.tril(jnp.ones((SSD_CHUNK, SSD_CHUNK), F32))
    expand = (jnp.arange(LANES)[:, None] == (jnp.arange(SSD_WIDTH)[None, :] // SSD_HEAD_DIM)
              ).astype(F32)
    d_exp = jnp.repeat(d_skip.astype(F32), SSD_HEAD_DIM).reshape(1, SSD_WIDTH)
    return pl.pallas_call(
        _ssd_kernel,
        grid=(batch, nc),
        in_specs=[row(SSD_WIDTH), row(SSD_CONV_CH), row(DT_PAD),
                  _const_spec((CONV_WIDTH, SSD_CONV_CH)), _const_spec((1, SSD_CONV_CH)),
                  _const_spec((1, DT_PAD)), _const_spec((1, DT_PAD)),
                  _const_spec((1, SSD_WIDTH)), _const_spec((1, SSD_WIDTH)),
                  _const_spec((SSD_CHUNK, SSD_CHUNK)), _const_spec((LANES, SSD_WIDTH))],
        out_specs=row(SSD_WIDTH),
        out_shape=jax.ShapeDtypeStruct((t, SSD_WIDTH), BF16),
        scratch_shapes=[pltpu.VMEM((CONV_PAD + SSD_CHUNK, SSD_CONV_CH), F32),
                        pltpu.VMEM((SSD_STATE, SSD_WIDTH), F32)],
        compiler_params=_params(2),
        name="ssd",
    )(z, xbc, dt, conv_w, conv_b.reshape(1, -1), pad_heads(dt_bias), pad_heads(a_log),
      d_exp, norm_w.reshape(1, -1), tri, expand)


def _gelu_tanh(x):
    c = math.sqrt(2.0 / math.pi)
    return 0.5 * x * (1.0 + jnp.tanh(c * (x + 0.044715 * (x * x * x))))


def _lru_kernel(g_ref, x_ref, cw_ref, cb_ref, wa_ref, ba_ref, wx_ref, bx_ref, lam_ref,
                y_ref, win_s, h_s):
    c = pl.program_id(1)
    tt = LRU_TILE
    pad = CONV_PAD

    @pl.when(c == 0)
    def _reset():
        win_s[0:pad, :] = jnp.zeros((pad, LRU_WIDTH), F32)
        h_s[...] = jnp.zeros_like(h_s)

    win_s[pad:, :] = x_ref[...]
    k_w = CONV_WIDTH
    xc = cb_ref[...] + win_s[pad:pad + tt, :] * cw_ref[k_w - 1:k_w, :]
    for k in range(k_w - 1):
        off = pad - (k_w - 1) + k
        xc = xc + win_s[off:off + tt, :] * cw_ref[k:k + 1, :]
    win_s[0:pad, :] = win_s[tt:tt + pad, :]

    xb = xc.astype(BF16)
    r = jax.nn.sigmoid(jnp.dot(xb, wa_ref[...], preferred_element_type=F32) + ba_ref[...])
    i = jax.nn.sigmoid(jnp.dot(xb, wx_ref[...], preferred_element_type=F32) + bx_ref[...])
    log_a = -LRU_C * r * _softplus(-lam_ref[...])
    a = jnp.exp(log_a)
    u = jnp.sqrt(-jnp.tanh(log_a) * (a * a + 1.0)) * (i * xc)

    row = lax.broadcasted_iota(jnp.int32, (tt, LRU_WIDTH), 0)
    step = 1
    while step < tt:
        keep = row >= step
        a_sh = jnp.where(keep, pltpu.roll(a, step, 0), 1.0)
        u_sh = jnp.where(keep, pltpu.roll(u, step, 0), 0.0)
        u = a * u_sh + u
        a = a * a_sh
        step *= 2
    h = a * h_s[0:1, :] + u
    h_s[...] = jnp.broadcast_to(h[tt - 1:tt, :], h_s.shape)
    y_ref[...] = (h * _gelu_tanh(g_ref[...])).astype(BF16)


def _block_diag(w):
    nb, c, d = w.shape
    eye = jnp.eye(nb, dtype=w.dtype)
    return (eye[:, None, :, None] * w[:, :, None, :]).reshape(nb * c, nb * d)


def _lru(g_in, x_in, conv_w, conv_b, wa, ba, wx, bx, lam, batch):
    t = g_in.shape[0]
    s = t // batch
    nt = s // LRU_TILE
    row = pl.BlockSpec((LRU_TILE, LRU_WIDTH), lambda b, c: (b * nt + c, 0))
    vec = lambda v: v.astype(F32).reshape(1, LRU_WIDTH)
    mat = _const_spec((LRU_WIDTH, LRU_WIDTH))
    one = _const_spec((1, LRU_WIDTH))
    return pl.pallas_call(
        _lru_kernel,
        grid=(batch, nt),
        in_specs=[row, row, _const_spec((CONV_WIDTH, LRU_WIDTH)), one, mat, one, mat, one, one],
        out_specs=row,
        out_shape=jax.ShapeDtypeStruct((t, LRU_WIDTH), BF16),
        scratch_shapes=[pltpu.VMEM((CONV_PAD + LRU_TILE, LRU_WIDTH), F32),
                        pltpu.VMEM((SUBLANES, LRU_WIDTH), F32)],
        compiler_params=_params(2),
        name="rglru",
    )(g_in, x_in, conv_w, vec(conv_b), _block_diag(wa).astype(BF16), vec(ba),
      _block_diag(wx).astype(BF16), vec(bx), vec(lam))


def _outffn_kernel(x_ref, att_ref, ssd_ref, lru_ref, wo_ref, gf_ref, wg_ref, wu_ref, wd_ref,
                   nfin_ref, o_ref, hn_s, *, final_norm):
    x1 = x_ref[...]
    for j, m_ref in enumerate((att_ref, ssd_ref, lru_ref)):
        x1 = x1 + jnp.dot(m_ref[...], wo_ref[j * ATT_WIDTH:(j + 1) * ATT_WIDTH, :],
                          preferred_element_type=F32)
    hn_s[...] = _rmsnorm(x1, gf_ref[...]).astype(BF16)
    o_ref[...] = x1
    for c0, c1 in FF_CHUNKS:
        hn = hn_s[...]
        gate = jnp.dot(hn, wg_ref[:, c0:c1], preferred_element_type=F32)
        up = jnp.dot(hn, wu_ref[:, c0:c1], preferred_element_type=F32)
        act = (_silu(gate) * up).astype(BF16)
        o_ref[...] += jnp.dot(act, wd_ref[c0:c1, :], preferred_element_type=F32)
    if final_norm:
        o_ref[...] = _rmsnorm(o_ref[...], nfin_ref[...])


def _outffn(x2, att, ssd, lru, w_out, norm_ffn, w_gate, w_up, w_down, norm_final, final_norm):
    t = x2.shape[0]
    row = lambda width: pl.BlockSpec((ROW_TILE, width), lambda i: (i, 0))
    return pl.pallas_call(
        functools.partial(_outffn_kernel, final_norm=final_norm),
        grid=(t // ROW_TILE,),
        in_specs=[row(D_MODEL), row(ATT_WIDTH), row(SSD_WIDTH), row(LRU_WIDTH),
                  _const_spec((D_MIX, D_MODEL)), _const_spec((1, D_MODEL)),
                  _const_spec((D_MODEL, D_FF)), _const_spec((D_MODEL, D_FF)),
                  _const_spec((D_FF, D_MODEL)), _const_spec((1, D_MODEL))],
        out_specs=row(D_MODEL),
        out_shape=jax.ShapeDtypeStruct((t, D_MODEL), F32),
        scratch_shapes=[pltpu.VMEM((ROW_TILE, D_MODEL), BF16)],
        compiler_params=_params(1),
        name="outffn",
    )(x2, att, ssd, lru, w_out.astype(BF16), norm_ffn.reshape(1, -1), w_gate.astype(BF16),
      w_up.astype(BF16), w_down.astype(BF16), norm_final.reshape(1, -1))


def _prep_w_in(w):
    o_dt = 3 * ATT_WIDTH + SSD_WIDTH + SSD_CONV_CH
    o_gl = o_dt + SSD_HEADS
    dt_cols = jnp.pad(w[:, o_dt:o_gl], ((0, 0), (0, DT_PAD - SSD_HEADS)))
    return jnp.concatenate([w[:, :o_dt], w[:, o_gl:], dt_cols], axis=1).astype(BF16)


def _layer(x2, batch, p, norm_final, final_norm):
    qkv, z, xbc, g_lru, x_lru, dt = _inproj(x2, p["norm_mix"].reshape(1, -1), _prep_w_in(p["w_in"]))
    att = _dilated_attention(qkv, batch)
    ssd = _ssd(z, xbc, dt, p["ssd_conv_w"], p["ssd_conv_b"], p["ssd_dt_bias"], p["ssd_a_log"],
               p["ssd_d"], p["ssd_norm"], batch)
    lru = _lru(g_lru, x_lru, p["lru_conv_w"], p["lru_conv_b"], p["lru_wa"], p["lru_ba"],
               p["lru_wx"], p["lru_bx"], p["lru_lambda"], batch)
    return _outffn(x2, att, ssd, lru, p["w_out"], p["norm_ffn"], p["w_gate"], p["w_up"],
                   p["w_down"], norm_final, final_norm)


def kernel(x, norm_mix, w_in, ssd_conv_w, ssd_conv_b, ssd_dt_bias, ssd_a_log, ssd_d, ssd_norm,
           lru_conv_w, lru_conv_b, lru_wa, lru_ba, lru_wx, lru_bx, lru_lambda, w_out,
           norm_ffn, w_gate, w_up, w_down, norm_final):
    batch, seq, _ = x.shape
    stacked = dict(norm_mix=norm_mix, w_in=w_in, ssd_conv_w=ssd_conv_w, ssd_conv_b=ssd_conv_b,
                   ssd_dt_bias=ssd_dt_bias, ssd_a_log=ssd_a_log, ssd_d=ssd_d, ssd_norm=ssd_norm,
                   lru_conv_w=lru_conv_w, lru_conv_b=lru_conv_b, lru_wa=lru_wa, lru_ba=lru_ba,
                   lru_wx=lru_wx, lru_bx=lru_bx, lru_lambda=lru_lambda, w_out=w_out,
                   norm_ffn=norm_ffn, w_gate=w_gate, w_up=w_up, w_down=w_down)
    depth = w_in.shape[0]
    x2 = x.reshape(batch * seq, D_MODEL)
    for l in range(depth):
        p = {k: v[l] for k, v in stacked.items()}
        x2 = _layer(x2, batch, p, norm_final, l == depth - 1)
    return x2.reshape(batch, seq, D_MODEL)
```

```python
import functools
import math

import jax
import jax.numpy as jnp
from jax import lax
from jax.experimental import pallas as pl
from jax.experimental.pallas import tpu as pltpu

F32 = jnp.float32
BF16 = jnp.bfloat16

D_MODEL = 1024
ATT_HEADS = 8
ATT_HEAD_DIM = 64
ATT_WIDTH = ATT_HEADS * ATT_HEAD_DIM
ATT_BLOCK = 128
ATT_DILATIONS = (1, 4, 16)
SSD_HEADS = 8
SSD_HEAD_DIM = 64
SSD_WIDTH = SSD_HEADS * SSD_HEAD_DIM
SSD_GROUPS = 2
SSD_STATE = 128
SSD_CHUNK = 128
SSD_CONV_CH = SSD_WIDTH + 2 * SSD_GROUPS * SSD_STATE
LRU_WIDTH = 512
LRU_BLOCKS = 8
LRU_BLOCK_W = LRU_WIDTH // LRU_BLOCKS
LRU_C = 8.0
CONV_WIDTH = 4
D_MIX = ATT_WIDTH + SSD_WIDTH + LRU_WIDTH
D_FF = 2816
NORM_EPS = 1e-6
SSD_NORM_EPS = 1e-5

LANES = 128
SUBLANES = 8
VMEM_LIMIT_BYTES = 56 * 1024 * 1024

ROW_TILE = 512
LRU_TILE = 512
SSD_STEP_CHUNKS = 4
DT_PAD = LANES
FF_CHUNKS = ((0, 768), (768, 1536), (1536, 2304), (2304, 2816))
CONV_PAD = SUBLANES

DIL_MID, DIL_MAX = ATT_DILATIONS[1], ATT_DILATIONS[2]
DIL_STEP = DIL_MAX // DIL_MID
ATT_TILE = ATT_BLOCK * DIL_MAX
MID_LEN = ATT_TILE // DIL_MID
HEADS_PER_PAIR = LANES // ATT_HEAD_DIM
ATT_PAIRS = ATT_WIDTH // LANES
ATT_UNROLL = 8
assert ATT_HEADS == 8 and HEADS_PER_PAIR == 2 and DIL_STEP == DIL_MID and ATT_DILATIONS[0] == 1

NEG_INF = float("-inf")


def _params(n_axes):
    return pltpu.CompilerParams(
        dimension_semantics=("arbitrary",) * n_axes,
        vmem_limit_bytes=VMEM_LIMIT_BYTES)


def _const_spec(shape):
    nd = len(shape)
    return pl.BlockSpec(shape, lambda *_: (0,) * nd, pipeline_mode=pl.Buffered(1))


def _rmsnorm(x, g):
    return x * lax.rsqrt(jnp.mean(x * x, axis=-1, keepdims=True) + NORM_EPS) * g


def _softplus(x):
    return jnp.maximum(x, 0.0) + jnp.log1p(jnp.exp(-jnp.abs(x)))


def _silu(x):
    return x * jax.nn.sigmoid(x)


_C_QKV = 3 * ATT_WIDTH
_C_Z = _C_QKV + SSD_WIDTH
_C_XBC = _C_Z + SSD_CONV_CH
_C_GL = _C_XBC + LRU_WIDTH
_C_XL = _C_GL + LRU_WIDTH
_C_DT = _C_XL + DT_PAD


def _inproj_kernel(x_ref, g_ref, w_ref, qkv_ref, z_ref, xbc_ref, gl_ref, xl_ref, dt_ref):
    h = _rmsnorm(x_ref[...], g_ref[...]).astype(BF16)

    def seg(a, b):
        return jnp.dot(h, w_ref[:, a:b], preferred_element_type=F32)

    qkv_ref[:, 0:ATT_WIDTH] = seg(0, ATT_WIDTH) * (ATT_HEAD_DIM ** -0.5)
    qkv_ref[:, ATT_WIDTH:_C_QKV] = seg(ATT_WIDTH, _C_QKV)
    z_ref[...] = seg(_C_QKV, _C_Z)
    xbc_ref[...] = seg(_C_Z, _C_XBC)
    gl_ref[...] = seg(_C_XBC, _C_GL)
    xl_ref[...] = seg(_C_GL, _C_XL)
    dt_ref[...] = seg(_C_XL, _C_DT)


def _inproj(x2, g, w):
    t = x2.shape[0]
    row = lambda width: pl.BlockSpec((ROW_TILE, width), lambda i: (i, 0))
    widths = (_C_QKV, SSD_WIDTH, SSD_CONV_CH, LRU_WIDTH, LRU_WIDTH, DT_PAD)
    return pl.pallas_call(
        _inproj_kernel,
        grid=(t // ROW_TILE,),
        in_specs=[row(D_MODEL), _const_spec((1, D_MODEL)), _const_spec((D_MODEL, _C_DT))],
        out_specs=[row(wd) for wd in widths],
        out_shape=[jax.ShapeDtypeStruct((t, wd), F32) for wd in widths],
        compiler_params=_params(1),
        name="inproj",
    )(x2, g, w)


def _pair_block(q_lo, q_hi, kk, vv, bias_lo, bias_hi, low_half):
    parts = []
    for qm, bias in ((q_lo, bias_lo), (q_hi, bias_hi)):
        s = lax.dot_general(qm, kk, (((1,), (1,)), ((), ())), preferred_element_type=F32)
        s = s + bias
        m = jnp.max(s, axis=-1, keepdims=True)
        e = jnp.exp(s - m)
        den = jnp.sum(e, axis=-1, keepdims=True)
        o = jnp.dot(e.astype(BF16), vv, preferred_element_type=F32)
        parts.append((o, m, den))
    (o0, m0, d0), (o1, m1, d1) = parts
    return (jnp.where(low_half, o0, o1), jnp.where(low_half, m0, m1),
            jnp.where(low_half, d0, d1))


def _attn_kernel(q_ref, k_ref, v_ref, att_ref,
                 qn_s, kn_s, vn_s, tmp_s, qm_s, km_s, vm_s, qx_s, kx_s, vx_s,
                 num_s, m_s, den_s, bias_s):
    pair = pl.program_id(1)
    tile = pl.program_id(2)
    blk = ATT_BLOCK
    first_tile = tile == 0

    @pl.when(first_tile)
    def _start_sequence():
        qi = lax.broadcasted_iota(jnp.int32, (blk, 2 * blk), 0)
        ki = lax.broadcasted_iota(jnp.int32, (blk, 2 * blk), 1)
        dist = blk + qi - ki
        band = (dist >= 0) & (dist <= blk)
        band_first = band & (ki >= blk)
        for hh in range(HEADS_PER_PAIR):
            expo = (126 - HEADS_PER_PAIR * pair - hh) << 23
            slope = lax.bitcast_convert_type(jnp.full((blk, 2 * blk), expo, jnp.int32), F32)
            for br, dil in enumerate((DIL_MAX, DIL_MID, 1)):
                alibi = -slope * (dil * dist).astype(F32)
                bias_s[br * 4 + hh] = jnp.where(band, alibi, NEG_INF)
                bias_s[br * 4 + 2 + hh] = jnp.where(band_first, alibi, NEG_INF)
        kn_s[0:blk, :] = jnp.zeros((blk, LANES), BF16)
        vn_s[0:blk, :] = jnp.zeros((blk, LANES), BF16)
        km_s[:, 0:blk, :] = jnp.zeros((DIL_MID, blk, LANES), BF16)
        vm_s[:, 0:blk, :] = jnp.zeros((DIL_MID, blk, LANES), BF16)
        kx_s[:, 0:blk, :] = jnp.zeros((DIL_MAX, blk, LANES), BF16)
        vx_s[:, 0:blk, :] = jnp.zeros((DIL_MAX, blk, LANES), BF16)

    @pl.when(tile > 0)
    def _carry_history():
        for nat, mid, big in ((kn_s, km_s, kx_s), (vn_s, vm_s, vx_s)):
            nat[0:blk, :] = nat[ATT_TILE:ATT_TILE + blk, :]
            mid[:, 0:blk, :] = mid[:, MID_LEN:MID_LEN + blk, :]
            big[:, 0:blk, :] = big[:, blk:2 * blk, :]

    lane_m = lax.broadcasted_iota(jnp.int32, (MID_LEN, LANES), 1) < ATT_HEAD_DIM
    lane_b = lax.broadcasted_iota(jnp.int32, (blk, LANES), 1) < ATT_HEAD_DIM

    def put_q(dst, idx, x, mask):
        dst[(0,) + idx] = jnp.where(mask, x, 0.0).astype(BF16)
        dst[(1,) + idx] = jnp.where(mask, 0.0, x).astype(BF16)

    def put_kv(dst, idx, x, mask):
        del mask
        dst[idx] = x.astype(BF16)

    for src, nat, mid, big, put, hist in (
            (q_ref, qn_s, qm_s, qx_s, put_q, 0),
            (k_ref, kn_s, km_s, kx_s, put_kv, blk),
            (v_ref, vn_s, vm_s, vx_s, put_kv, blk)):
        for c in range(DIL_MID):
            rows = slice(c * MID_LEN, (c + 1) * MID_LEN)
            put(nat, (slice(hist + c * MID_LEN, hist + (c + 1) * MID_LEN), slice(None)),
                src[rows, :], lane_m)
        for r in range(DIL_MID):
            x = src[pl.ds(r, MID_LEN, stride=DIL_MID), :]
            tmp_s[r] = x
            put(mid, (r, slice(hist, hist + MID_LEN), slice(None)), x, lane_m)
        for r in range(DIL_MID):
            for c in range(DIL_STEP):
                y = tmp_s.at[r][pl.ds(c, blk, stride=DIL_STEP), :]
                put(big, (r + DIL_MID * c, slice(hist, hist + blk), slice(None)), y, lane_b)

    def bias_pair(branch, is_first):
        base = branch * 4 + jnp.where(is_first, 2, 0)
        return bias_s[base], bias_s[base + 1]

    def merge(rows, o_b, m_b, d_b):
        m_old = m_s[rows, :]
        m_new = jnp.maximum(m_old, m_b)
        a_old = jnp.exp(m_old - m_new)
        a_b = jnp.exp(m_b - m_new)
        num = num_s[rows, :] * a_old + o_b * a_b
        den = den_s[rows, :] * a_old + d_b * a_b
        return num, m_new, den

    def big_block(r, carry):
        b_lo, b_hi = bias_pair(0, first_tile)
        o_b, m_b, d_b = _pair_block(qx_s[0, r], qx_s[1, r], kx_s[r], vx_s[r], b_lo, b_hi, lane_b)
        rows = pl.ds(r, blk, stride=DIL_MAX)
        num_s[rows, :] = o_b
        m_s[rows, :] = m_b
        den_s[rows, :] = d_b
        return carry

    lax.fori_loop(0, DIL_MAX, big_block, 0, unroll=ATT_UNROLL)

    mid_blocks = MID_LEN // blk

    def mid_block(i, carry):
        r = i // mid_blocks
        j = i % mid_blocks
        q0 = pl.multiple_of(j * blk, blk)
        b_lo, b_hi = bias_pair(1, first_tile & (j == 0))
        o_b, m_b, d_b = _pair_block(
            qm_s[0, r, pl.ds(q0, blk), :], qm_s[1, r, pl.ds(q0, blk), :],
            km_s[r, pl.ds(q0, 2 * blk), :], vm_s[r, pl.ds(q0, 2 * blk), :], b_lo, b_hi, lane_b)
        rows = pl.ds(j * (blk * DIL_MID) + r, blk, stride=DIL_MID)
        num, m_new, den = merge(rows, o_b, m_b, d_b)
        num_s[rows, :] = num
        m_s[rows, :] = m_new
        den_s[rows, :] = den
        return carry

    lax.fori_loop(0, DIL_MID * mid_blocks, mid_block, 0, unroll=ATT_UNROLL)

    def nat_block(j, carry):
        q0 = pl.multiple_of(j * blk, blk)
        rows = pl.ds(q0, blk)
        b_lo, b_hi = bias_pair(2, first_tile & (j == 0))
        o_b, m_b, d_b = _pair_block(
            qn_s[0, rows, :], qn_s[1, rows, :],
            kn_s[pl.ds(q0, 2 * blk), :], vn_s[pl.ds(q0, 2 * blk), :], b_lo, b_hi, lane_b)
        num, _, den = merge(rows, o_b, m_b, d_b)
        att_ref[rows, :] = (num / den).astype(BF16)
        return carry

    lax.fori_loop(0, ATT_TILE // blk, nat_block, 0, unroll=ATT_UNROLL)


def _dilated_attention(qkv, batch):
    t = qkv.shape[0]
    tiles = t // batch // ATT_TILE
    blk = ATT_BLOCK

    def spec(col0):
        return pl.BlockSpec((ATT_TILE, LANES), lambda b, p, i: (b * tiles + i, col0 + p))

    bf = lambda *shape: pltpu.VMEM(shape, BF16)
    return pl.pallas_call(
        _attn_kernel,
        grid=(batch, ATT_PAIRS, tiles),
        in_specs=[spec(0), spec(ATT_PAIRS), spec(2 * ATT_PAIRS)],
        out_specs=spec(0),
        out_shape=jax.ShapeDtypeStruct((t, ATT_WIDTH), BF16),
        scratch_shapes=[
            bf(2, ATT_TILE, LANES), bf(blk + ATT_TILE, LANES), bf(blk + ATT_TILE, LANES),
            pltpu.VMEM((DIL_MID, MID_LEN, LANES), F32),
            bf(2, DIL_MID, MID_LEN, LANES), bf(DIL_MID, blk + MID_LEN, LANES),
            bf(DIL_MID, blk + MID_LEN, LANES),
            bf(2, DIL_MAX, blk, LANES), bf(DIL_MAX, 2 * blk, LANES), bf(DIL_MAX, 2 * blk, LANES),
            pltpu.VMEM((ATT_TILE, LANES), F32), pltpu.VMEM((ATT_TILE, LANES), F32),
            pltpu.VMEM((ATT_TILE, LANES), F32),
            pltpu.VMEM((3 * 2 * HEADS_PER_PAIR, blk, 2 * blk), F32),
        ],
        compiler_params=_params(3),
        name="attn",
    )(qkv, qkv, qkv)


def _split3(x):
    hi = x.astype(BF16)
    r1 = x - hi.astype(F32)
    mid = r1.astype(BF16)
    lo = (r1 - mid.astype(F32)).astype(BF16)
    return hi, mid, lo


def _dot_exact_rhs(x, w):
    return sum(jnp.dot(part, w, preferred_element_type=F32) for part in _split3(x))


def _ssd_kernel(z_ref, xbc_ref, dt_ref, cw_ref, cb_ref, dtb_ref, alog_ref, dsk_ref, nw_ref,
                triu_ref, exp_ref, y_ref, win_s, state_s):
    c = pl.program_id(1)
    q = SSD_CHUNK
    pad = CONV_PAD
    rows_step = SSD_STEP_CHUNKS * q
    heads_per_group = SSD_HEADS // SSD_GROUPS
    gw = heads_per_group * SSD_HEAD_DIM

    @pl.when(c == 0)
    def _reset():
        win_s[0:pad, :] = jnp.zeros((pad, SSD_CONV_CH), F32)
        state_s[...] = jnp.zeros_like(state_s)

    win_s[pad:, :] = xbc_ref[...]

    ti = lax.broadcasted_iota(jnp.int32, (q, q), 0)
    tj = lax.broadcasted_iota(jnp.int32, (q, q), 1)
    causal = ti >= tj
    low_half = lax.broadcasted_iota(jnp.int32, (q, LANES), 1) < SSD_HEAD_DIM
    a_t = -jnp.exp(alog_ref[...])
    k_w = CONV_WIDTH

    for ci in range(SSD_STEP_CHUNKS):
        r0 = ci * q
        conv = cb_ref[...] + win_s[pad + r0:pad + r0 + q, :] * cw_ref[k_w - 1:k_w, :]
        for k in range(k_w - 1):
            off = pad + r0 - (k_w - 1) + k
            conv = conv + win_s[off:off + q, :] * cw_ref[k:k + 1, :]
        xbc = _silu(conv)
        xs = xbc[:, :SSD_WIDTH]
        bm = xbc[:, SSD_WIDTH:SSD_WIDTH + SSD_GROUPS * SSD_STATE]
        cm = xbc[:, SSD_WIDTH + SSD_GROUPS * SSD_STATE:]

        dt_t = _softplus(dt_ref[r0:r0 + q, :].T[0:SSD_HEADS, :] + dtb_ref[...])
        acs_t = _dot_exact_rhs(dt_t * a_t, triu_ref[...])
        e_t = jnp.exp(acs_t)
        w_t = jnp.exp(acs_t[:, q - 1:q] - acs_t) * dt_t
        chunk_decay = jnp.sum(e_t[:, q - 1:q] * exp_ref[...], axis=0, keepdims=True)
        cols_t = jnp.concatenate(
            [acs_t, e_t, jnp.zeros((q - 2 * SSD_HEADS, q), F32)], axis=0).T

        y_parts = []
        for g in range(SSD_GROUPS):
            bm_g = bm[:, g * SSD_STATE:(g + 1) * SSD_STATE]
            cm_g = cm[:, g * SSD_STATE:(g + 1) * SSD_STATE]
            gmat = lax.dot_general(cm_g.astype(BF16), bm_g.astype(BF16),
                                   (((1,), (1,)), ((), ())), preferred_element_type=F32)
            bm_gt = bm_g.T
            for pp in range(heads_per_group // HEADS_PER_PAIR):
                p = g * (heads_per_group // HEADS_PER_PAIR) + pp
                cols = slice(p * LANES, (p + 1) * LANES)
                x_p = xs[:, cols]
                s_p = state_s[:, cols]
                lhs_y, lhs_s, rhs_y, rhs_s = [], [], [], []
                for hh in range(HEADS_PER_PAIR):
                    h = HEADS_PER_PAIR * p + hh
                    keep = low_half if hh == 0 else ~low_half
                    seg = cols_t[:, h:h + 1] - acs_t[h:h + 1, :]
                    lmat = jnp.exp(jnp.where(causal, seg, NEG_INF))
                    scores = gmat * lmat * dt_t[h:h + 1, :]
                    c_dec = cm_g * cols_t[:, SSD_HEADS + h:SSD_HEADS + h + 1]
                    x_h = jnp.where(keep, x_p, 0.0).astype(BF16)
                    s_h = jnp.where(keep, s_p, 0.0).astype(BF16)
                    lhs_y += [scores.astype(BF16), c_dec.astype(BF16)]
                    rhs_y += [x_h, s_h]
                    lhs_s.append((bm_gt * w_t[h:h + 1, :]).astype(BF16))
                    rhs_s.append(x_h)
                y_parts.append(jnp.dot(jnp.concatenate(lhs_y, axis=1),
                                       jnp.concatenate(rhs_y, axis=0),
                                       preferred_element_type=F32))
                state_s[:, cols] = s_p * chunk_decay[:, cols] + jnp.dot(
                    jnp.concatenate(lhs_s, axis=1), jnp.concatenate(rhs_s, axis=0),
                    preferred_element_type=F32)
        y = jnp.concatenate(y_parts, axis=-1) + dsk_ref[...] * xs
        y = y * _silu(z_ref[r0:r0 + q, :])
        outs = []
        for g in range(SSD_GROUPS):
            yg = y[:, g * gw:(g + 1) * gw]
            outs.append(yg * lax.rsqrt(jnp.mean(yg * yg, axis=-1, keepdims=True) + SSD_NORM_EPS))
        y_ref[r0:r0 + q, :] = (jnp.concatenate(outs, axis=-1) * nw_ref[...]).astype(BF16)

    win_s[0:pad, :] = win_s[rows_step:rows_step + pad, :]


def _ssd(z, xbc, dt, conv_w, conv_b, dt_bias, a_log, d_skip, norm_w, batch):
    t = z.shape[0]
    rows_step = SSD_STEP_CHUNKS * SSD_CHUNK
    ns = t // batch // rows_step
    row = lambda width: pl.BlockSpec((rows_step, width), lambda b, c: (b * ns + c, 0))
    per_time = lambda v: jnp.broadcast_to(v.astype(F32)[:, None], (SSD_HEADS, SSD_CHUNK))
    triu = jnp.triu(jnp.ones((SSD_CHUNK, SSD_CHUNK), BF16))
    expand = (jnp.arange(SSD_HEADS)[:, None] == (jnp.arange(SSD_WIDTH)[None, :] // SSD_HEAD_DIM)
              ).astype(F32)
    d_exp = jnp.repeat(d_skip.astype(F32), SSD_HEAD_DIM).reshape(1, SSD_WIDTH)
    return pl.pallas_call(
        _ssd_kernel,
        grid=(batch, ns),
        in_specs=[row(SSD_WIDTH), row(SSD_CONV_CH), row(DT_PAD),
                  _const_spec((CONV_WIDTH, SSD_CONV_CH)), _const_spec((1, SSD_CONV_CH)),
                  _const_spec((SSD_HEADS, SSD_CHUNK)), _const_spec((SSD_HEADS, SSD_CHUNK)),
                  _const_spec((1, SSD_WIDTH)), _const_spec((1, SSD_WIDTH)),
                  _const_spec((SSD_CHUNK, SSD_CHUNK)), _const_spec((SSD_HEADS, SSD_WIDTH))],
        out_specs=row(SSD_WIDTH),
        out_shape=jax.ShapeDtypeStruct((t, SSD_WIDTH), BF16),
        scratch_shapes=[pltpu.VMEM((CONV_PAD + rows_step, SSD_CONV_CH), F32),
                        pltpu.VMEM((SSD_STATE, SSD_WIDTH), F32)],
        compiler_params=_params(2),
        name="ssd",
    )(z, xbc, dt, conv_w, conv_b.reshape(1, -1), per_time(dt_bias), per_time(a_log),
      d_exp, norm_w.reshape(1, -1), triu, expand)


def _gelu_tanh(x):
    c = math.sqrt(2.0 / math.pi)
    return 0.5 * x * (1.0 + jnp.tanh(c * (x + 0.044715 * (x * x * x))))


def _lru_kernel(g_ref, x_ref, cw_ref, cb_ref, wa_ref, ba_ref, wx_ref, bx_ref, lam_ref,
                y_ref, win_s, h_s):
    c = pl.program_id(1)
    tt = LRU_TILE
    pad = CONV_PAD

    @pl.when(c == 0)
    def _reset():
        win_s[0:pad, :] = jnp.zeros((pad, LRU_WIDTH), F32)
        h_s[...] = jnp.zeros_like(h_s)

    win_s[pad:, :] = x_ref[...]
    k_w = CONV_WIDTH
    xc = cb_ref[...] + win_s[pad:pad + tt, :] * cw_ref[k_w - 1:k_w, :]
    for k in range(k_w - 1):
        off = pad - (k_w - 1) + k
        xc = xc + win_s[off:off + tt, :] * cw_ref[k:k + 1, :]
    win_s[0:pad, :] = win_s[tt:tt + pad, :]

    xb = xc.astype(BF16)
    r = jax.nn.sigmoid(jnp.dot(xb, wa_ref[...], preferred_element_type=F32) + ba_ref[...])
    i = jax.nn.sigmoid(jnp.dot(xb, wx_ref[...], preferred_element_type=F32) + bx_ref[...])
    log_a = -LRU_C * r * _softplus(-lam_ref[...])
    a = jnp.exp(log_a)
    u = jnp.sqrt(-jnp.tanh(log_a) * (a * a + 1.0)) * (i * xc)

    groups = tt // SUBLANES
    a = a.reshape(groups, SUBLANES, LRU_WIDTH)
    u = u.reshape(groups, SUBLANES, LRU_WIDTH)
    sub = lax.broadcasted_iota(jnp.int32, (groups, SUBLANES, LRU_WIDTH), 1)
    step = 1
    while step < SUBLANES:
        keep = sub >= step
        a_sh = jnp.where(keep, pltpu.roll(a, step, 1), 1.0)
        u_sh = jnp.where(keep, pltpu.roll(u, step, 1), 0.0)
        u = a * u_sh + u
        a = a * a_sh
        step *= 2
    carry = h_s[0:1, :]
    hs = []
    for gi in range(groups):
        h_g = a[gi] * carry + u[gi]
        hs.append(h_g)
        carry = h_g[SUBLANES - 1:SUBLANES, :]
    h_s[...] = jnp.broadcast_to(carry, h_s.shape)
    h = jnp.concatenate(hs, axis=0)
    y_ref[...] = (h * _gelu_tanh(g_ref[...])).astype(BF16)


def _block_diag(w):
    nb, c, d = w.shape
    eye = jnp.eye(nb, dtype=w.dtype)
    return (eye[:, None, :, None] * w[:, :, None, :]).reshape(nb * c, nb * d)


def _lru(g_in, x_in, conv_w, conv_b, wa, ba, wx, bx, lam, batch):
    t = g_in.shape[0]
    s = t // batch
    nt = s // LRU_TILE
    row = pl.BlockSpec((LRU_TILE, LRU_WIDTH), lambda b, c: (b * nt + c, 0))
    vec = lambda v: v.astype(F32).reshape(1, LRU_WIDTH)
    mat = _const_spec((LRU_WIDTH, LRU_WIDTH))
    one = _const_spec((1, LRU_WIDTH))
    return pl.pallas_call(
        _lru_kernel,
        grid=(batch, nt),
        in_specs=[row, row, _const_spec((CONV_WIDTH, LRU_WIDTH)), one, mat, one, mat, one, one],
        out_specs=row,
        out_shape=jax.ShapeDtypeStruct((t, LRU_WIDTH), BF16),
        scratch_shapes=[pltpu.VMEM((CONV_PAD + LRU_TILE, LRU_WIDTH), F32),
                        pltpu.VMEM((SUBLANES, LRU_WIDTH), F32)],
        compiler_params=_params(2),
        name="rglru",
    )(g_in, x_in, conv_w, vec(conv_b), _block_diag(wa).astype(BF16), vec(ba),
      _block_diag(wx).astype(BF16), vec(bx), vec(lam))


def _outffn_kernel(x_ref, att_ref, ssd_ref, lru_ref, wo_ref, gf_ref, wg_ref, wu_ref, wd_ref,
                   nfin_ref, o_ref, hn_s, *, final_norm):
    x1 = x_ref[...]
    for j, m_ref in enumerate((att_ref, ssd_ref, lru_ref)):
        x1 = x1 + jnp.dot(m_ref[...], wo_ref[j * ATT_WIDTH:(j + 1) * ATT_WIDTH, :],
                          preferred_element_type=F32)
    hn_s[...] = _rmsnorm(x1, gf_ref[...]).astype(BF16)
    o_ref[...] = x1
    for c0, c1 in FF_CHUNKS:
        hn = hn_s[...]
        gate = jnp.dot(hn, wg_ref[:, c0:c1], preferred_element_type=F32)
        up = jnp.dot(hn, wu_ref[:, c0:c1], preferred_element_type=F32)
        act = (_silu(gate) * up).astype(BF16)
        o_ref[...] += jnp.dot(act, wd_ref[c0:c1, :], preferred_element_type=F32)
    if final_norm:
        o_ref[...] = _rmsnorm(o_ref[...], nfin_ref[...])


def _outffn(x2, att, ssd, lru, w_out, norm_ffn, w_gate, w_up, w_down, norm_final, final_norm):
    t = x2.shape[0]
    row = lambda width: pl.BlockSpec((ROW_TILE, width), lambda i: (i, 0))
    return pl.pallas_call(
        functools.partial(_outffn_kernel, final_norm=final_norm),
        grid=(t // ROW_TILE,),
        in_specs=[row(D_MODEL), row(ATT_WIDTH), row(SSD_WIDTH), row(LRU_WIDTH),
                  _const_spec((D_MIX, D_MODEL)), _const_spec((1, D_MODEL)),
                  _const_spec((D_MODEL, D_FF)), _const_spec((D_MODEL, D_FF)),
                  _const_spec((D_FF, D_MODEL)), _const_spec((1, D_MODEL))],
        out_specs=row(D_MODEL),
        out_shape=jax.ShapeDtypeStruct((t, D_MODEL), F32),
        scratch_shapes=[pltpu.VMEM((ROW_TILE, D_MODEL), BF16)],
        compiler_params=_params(1),
        name="outffn",
    )(x2, att, ssd, lru, w_out.astype(BF16), norm_ffn.reshape(1, -1), w_gate.astype(BF16),
      w_up.astype(BF16), w_down.astype(BF16), norm_final.reshape(1, -1))


def _prep_w_in(w):
    o_dt = 3 * ATT_WIDTH + SSD_WIDTH + SSD_CONV_CH
    o_gl = o_dt + SSD_HEADS
    dt_cols = jnp.pad(w[:, o_dt:o_gl], ((0, 0), (0, DT_PAD - SSD_HEADS)))
    return jnp.concatenate([w[:, :o_dt], w[:, o_gl:], dt_cols], axis=1).astype(BF16)


def _layer(x2, batch, p, norm_final, final_norm):
    qkv, z, xbc, g_lru, x_lru, dt = _inproj(x2, p["norm_mix"].reshape(1, -1), _prep_w_in(p["w_in"]))
    att = _dilated_attention(qkv, batch)
    ssd = _ssd(z, xbc, dt, p["ssd_conv_w"], p["ssd_conv_b"], p["ssd_dt_bias"], p["ssd_a_log"],
               p["ssd_d"], p["ssd_norm"], batch)
    lru = _lru(g_lru, x_lru, p["lru_conv_w"], p["lru_conv_b"], p["lru_wa"], p["lru_ba"],
               p["lru_wx"], p["lru_bx"], p["lru_lambda"], batch)
    return _outffn(x2, att, ssd, lru, p["w_out"], p["norm_ffn"], p["w_gate"], p["w_up"],
                   p["w_down"], norm_final, final_norm)


def kernel(x, norm_mix, w_in, ssd_conv_w, ssd_conv_b, ssd_dt_bias, ssd_a_log, ssd_d, ssd_norm,
           lru_conv_w, lru_conv_b, lru_wa, lru_ba, lru_wx, lru_bx, lru_lambda, w_out,
           norm_ffn, w_gate, w_up, w_down, norm_final):
    batch, seq, _ = x.shape
    stacked = dict(norm_mix=norm_mix, w_in=w_in, ssd_conv_w=ssd_conv_w, ssd_conv_b=ssd_conv_b,
                   ssd_dt_bias=ssd_dt_bias, ssd_a_log=ssd_a_log, ssd_d=ssd_d, ssd_norm=ssd_norm,
                   lru_conv_w=lru_conv_w, lru_conv_b=lru_conv_b, lru_wa=lru_wa, lru_ba=lru_ba,
                   lru_wx=lru_wx, lru_bx=lru_bx, lru_lambda=lru_lambda, w_out=w_out,
                   norm_ffn=norm_ffn, w_gate=w_gate, w_up=w_up, w_down=w_down)
    depth = w_in.shape[0]
    x2 = x.reshape(batch * seq, D_MODEL)
    for l in range(depth):
        p = {k: v[l] for k, v in stacked.items()}
        x2 = _layer(x2, batch, p, norm_final, l == depth - 1)
    return x2.reshape(batch, seq, D_MODEL)
```

```python
import functools
import math

import jax
import jax.numpy as jnp
from jax import lax
from jax.experimental import pallas as pl
from jax.experimental.pallas import tpu as pltpu

F32 = jnp.float32
BF16 = jnp.bfloat16

D_MODEL = 1024
ATT_HEADS = 8
ATT_HEAD_DIM = 64
ATT_WIDTH = ATT_HEADS * ATT_HEAD_DIM
ATT_BLOCK = 128
ATT_DILATIONS = (1, 4, 16)
SSD_HEADS = 8
SSD_HEAD_DIM = 64
SSD_WIDTH = SSD_HEADS * SSD_HEAD_DIM
SSD_GROUPS = 2
SSD_STATE = 128
SSD_CHUNK = 128
SSD_CONV_CH = SSD_WIDTH + 2 * SSD_GROUPS * SSD_STATE
LRU_WIDTH = 512
LRU_BLOCKS = 8
LRU_BLOCK_W = LRU_WIDTH // LRU_BLOCKS
LRU_C = 8.0
CONV_WIDTH = 4
D_MIX = ATT_WIDTH + SSD_WIDTH + LRU_WIDTH
D_FF = 2816
NORM_EPS = 1e-6
SSD_NORM_EPS = 1e-5

LANES = 128
SUBLANES = 8
VMEM_LIMIT_BYTES = 56 * 1024 * 1024

ROW_TILE = 512
LRU_TILE = 512
SSD_STEP_CHUNKS = 4
DT_PAD = LANES
FF_CHUNKS = ((0, 768), (768, 1536), (1536, 2304), (2304, 2816))
CONV_PAD = SUBLANES

DIL_MID, DIL_MAX = ATT_DILATIONS[1], ATT_DILATIONS[2]
DIL_STEP = DIL_MAX // DIL_MID
ATT_TILE = ATT_BLOCK * DIL_MAX
MID_LEN = ATT_TILE // DIL_MID
HEADS_PER_PAIR = LANES // ATT_HEAD_DIM
ATT_PAIRS = ATT_WIDTH // LANES
ATT_UNROLL = 8
assert ATT_HEADS == 8 and HEADS_PER_PAIR == 2 and DIL_STEP == DIL_MID and ATT_DILATIONS[0] == 1

NEG_INF = float("-inf")


def _params(n_axes):
    return pltpu.CompilerParams(
        dimension_semantics=("arbitrary",) * n_axes,
        vmem_limit_bytes=VMEM_LIMIT_BYTES)


def _const_spec(shape):
    nd = len(shape)
    return pl.BlockSpec(shape, lambda *_: (0,) * nd, pipeline_mode=pl.Buffered(1))


def _rmsnorm(x, g):
    return x * lax.rsqrt(jnp.mean(x * x, axis=-1, keepdims=True) + NORM_EPS) * g


def _softplus(x):
    return jnp.maximum(x, 0.0) + jnp.log1p(jnp.exp(-jnp.abs(x)))


def _silu(x):
    return x * jax.nn.sigmoid(x)


_C_QKV = 3 * ATT_WIDTH
_C_Z = _C_QKV + SSD_WIDTH
_C_XBC = _C_Z + SSD_CONV_CH
_C_XL = _C_XBC + LRU_WIDTH
_C_GL = _C_XL + LRU_WIDTH
_C_DT = _C_GL + DT_PAD
CONV_CH = SSD_CONV_CH + LRU_WIDTH


def _gelu_tanh(x):
    c = math.sqrt(2.0 / math.pi)
    return 0.5 * x * (1.0 + jnp.tanh(c * (x + 0.044715 * (x * x * x))))


def _inproj_kernel(x_ref, g_ref, w_ref, cw_ref, cb_ref, qkv_ref, zg_ref, xbc_ref, xl_ref, gl_ref,
                   dt_ref, win_s, *, tiles_per_seq):
    pad = CONV_PAD

    @pl.when(pl.program_id(0) % tiles_per_seq == 0)
    def _new_sequence():
        win_s[0:pad, :] = jnp.zeros((pad, CONV_CH), F32)

    h = _rmsnorm(x_ref[...], g_ref[...]).astype(BF16)

    def seg(a, b):
        return jnp.dot(h, w_ref[:, a:b], preferred_element_type=F32)

    win_s[pad:, :] = seg(_C_Z, _C_XL)
    k_w = CONV_WIDTH
    conv = cb_ref[...] + win_s[pad:pad + ROW_TILE, :] * cw_ref[k_w - 1:k_w, :]
    for k in range(k_w - 1):
        off = pad - (k_w - 1) + k
        conv = conv + win_s[off:off + ROW_TILE, :] * cw_ref[k:k + 1, :]
    win_s[0:pad, :] = win_s[ROW_TILE:ROW_TILE + pad, :]
    xbc_ref[...] = _silu(conv[:, :SSD_CONV_CH])
    xl_ref[...] = conv[:, SSD_CONV_CH:]

    zg_ref[...] = _silu(seg(_C_QKV, _C_Z))
    gl_ref[...] = _gelu_tanh(seg(_C_XL, _C_GL))
    qkv_ref[:, 0:ATT_WIDTH] = seg(0, ATT_WIDTH) * (ATT_HEAD_DIM ** -0.5)
    qkv_ref[:, ATT_WIDTH:_C_QKV] = seg(ATT_WIDTH, _C_QKV)
    dt_ref[...] = seg(_C_GL, _C_DT)


def _inproj(x2, g, w, conv_w, conv_b, seq):
    t = x2.shape[0]
    row = lambda width: pl.BlockSpec((ROW_TILE, width), lambda i: (i, 0))
    widths = (_C_QKV, SSD_WIDTH, SSD_CONV_CH, LRU_WIDTH, LRU_WIDTH, DT_PAD)
    return pl.pallas_call(
        functools.partial(_inproj_kernel, tiles_per_seq=seq // ROW_TILE),
        grid=(t // ROW_TILE,),
        in_specs=[row(D_MODEL), _const_spec((1, D_MODEL)), _const_spec((D_MODEL, _C_DT)),
                  _const_spec((CONV_WIDTH, CONV_CH)), _const_spec((1, CONV_CH))],
        out_specs=[row(wd) for wd in widths],
        out_shape=[jax.ShapeDtypeStruct((t, wd), F32) for wd in widths],
        scratch_shapes=[pltpu.VMEM((CONV_PAD + ROW_TILE, CONV_CH), F32)],
        compiler_params=_params(1),
        name="inproj",
    )(x2, g, w, conv_w, conv_b)


def _pair_block(q_lo, q_hi, kk, vv, bias_lo, bias_hi, low_half):
    parts = []
    for qm, bias in ((q_lo, bias_lo), (q_hi, bias_hi)):
        s = lax.dot_general(qm, kk, (((1,), (1,)), ((), ())), preferred_element_type=F32)
        s = s + bias
        m = jnp.max(s, axis=-1, keepdims=True)
        e = jnp.exp(s - m)
        den = jnp.sum(e, axis=-1, keepdims=True)
        o = jnp.dot(e.astype(BF16), vv, preferred_element_type=F32)
        parts.append((o, m, den))
    (o0, m0, d0), (o1, m1, d1) = parts
    return (jnp.where(low_half, o0, o1), jnp.where(low_half, m0, m1),
            jnp.where(low_half, d0, d1))


def _attn_kernel(q_ref, k_ref, v_ref, att_ref,
                 qn_s, kn_s, vn_s, tmp_s, qm_s, km_s, vm_s, qx_s, kx_s, vx_s,
                 num_s, m_s, den_s, bias_s):
    pair = pl.program_id(1)
    tile = pl.program_id(2)
    blk = ATT_BLOCK
    first_tile = tile == 0

    @pl.when(first_tile)
    def _start_sequence():
        qi = lax.broadcasted_iota(jnp.int32, (blk, 2 * blk), 0)
        ki = lax.broadcasted_iota(jnp.int32, (blk, 2 * blk), 1)
        dist = blk + qi - ki
        band = (dist >= 0) & (dist <= blk)
        band_first = band & (ki >= blk)
        for hh in range(HEADS_PER_PAIR):
            expo = (126 - HEADS_PER_PAIR * pair - hh) << 23
            slope = lax.bitcast_convert_type(jnp.full((blk, 2 * blk), expo, jnp.int32), F32)
            for br, dil in enumerate((DIL_MAX, DIL_MID, 1)):
                alibi = -slope * (dil * dist).astype(F32)
                bias_s[br * 4 + hh] = jnp.where(band, alibi, NEG_INF)
                bias_s[br * 4 + 2 + hh] = jnp.where(band_first, alibi, NEG_INF)
        kn_s[0:blk, :] = jnp.zeros((blk, LANES), BF16)
        vn_s[0:blk, :] = jnp.zeros((blk, LANES), BF16)
        km_s[:, 0:blk, :] = jnp.zeros((DIL_MID, blk, LANES), BF16)
        vm_s[:, 0:blk, :] = jnp.zeros((DIL_MID, blk, LANES), BF16)
        kx_s[:, 0:blk, :] = jnp.zeros((DIL_MAX, blk, LANES), BF16)
        vx_s[:, 0:blk, :] = jnp.zeros((DIL_MAX, blk, LANES), BF16)

    @pl.when(tile > 0)
    def _carry_history():
        for nat, mid, big in ((kn_s, km_s, kx_s), (vn_s, vm_s, vx_s)):
            nat[0:blk, :] = nat[ATT_TILE:ATT_TILE + blk, :]
            mid[:, 0:blk, :] = mid[:, MID_LEN:MID_LEN + blk, :]
            big[:, 0:blk, :] = big[:, blk:2 * blk, :]

    lane_m = lax.broadcasted_iota(jnp.int32, (MID_LEN, LANES), 1) < ATT_HEAD_DIM
    lane_b = lax.broadcasted_iota(jnp.int32, (blk, LANES), 1) < ATT_HEAD_DIM

    def put_q(dst, idx, x, mask):
        dst[(0,) + idx] = jnp.where(mask, x, 0.0).astype(BF16)
        dst[(1,) + idx] = jnp.where(mask, 0.0, x).astype(BF16)

    def put_kv(dst, idx, x, mask):
        del mask
        dst[idx] = x.astype(BF16)

    for src, nat, mid, big, put, hist in (
            (q_ref, qn_s, qm_s, qx_s, put_q, 0),
            (k_ref, kn_s, km_s, kx_s, put_kv, blk),
            (v_ref, vn_s, vm_s, vx_s, put_kv, blk)):
        for c in range(DIL_MID):
            rows = slice(c * MID_LEN, (c + 1) * MID_LEN)
            put(nat, (slice(hist + c * MID_LEN, hist + (c + 1) * MID_LEN), slice(None)),
                src[rows, :], lane_m)
        for r in range(DIL_MID):
            x = src[pl.ds(r, MID_LEN, stride=DIL_MID), :]
            tmp_s[r] = x
            put(mid, (r, slice(hist, hist + MID_LEN), slice(None)), x, lane_m)
        for r in range(DIL_MID):
            for c in range(DIL_STEP):
                y = tmp_s.at[r][pl.ds(c, blk, stride=DIL_STEP), :]
                put(big, (r + DIL_MID * c, slice(hist, hist + blk), slice(None)), y, lane_b)

    def bias_pair(branch, is_first):
        base = branch * 4 + jnp.where(is_first, 2, 0)
        return bias_s[base], bias_s[base + 1]

    def merge(rows, o_b, m_b, d_b):
        m_old = m_s[rows, :]
        m_new = jnp.maximum(m_old, m_b)
        a_old = jnp.exp(m_old - m_new)
        a_b = jnp.exp(m_b - m_new)
        num = num_s[rows, :] * a_old + o_b * a_b
        den = den_s[rows, :] * a_old + d_b * a_b
        return num, m_new, den

    def big_block(r, carry):
        b_lo, b_hi = bias_pair(0, first_tile)
        o_b, m_b, d_b = _pair_block(qx_s[0, r], qx_s[1, r], kx_s[r], vx_s[r], b_lo, b_hi, lane_b)
        rows = pl.ds(r, blk, stride=DIL_MAX)
        num_s[rows, :] = o_b
        m_s[rows, :] = m_b
        den_s[rows, :] = d_b
        return carry

    lax.fori_loop(0, DIL_MAX, big_block, 0, unroll=ATT_UNROLL)

    mid_blocks = MID_LEN // blk

    def mid_block(i, carry):
        r = i // mid_blocks
        j = i % mid_blocks
        q0 = pl.multiple_of(j * blk, blk)
        b_lo, b_hi = bias_pair(1, first_tile & (j == 0))
        o_b, m_b, d_b = _pair_block(
            qm_s[0, r, pl.ds(q0, blk), :], qm_s[1, r, pl.ds(q0, blk), :],
            km_s[r, pl.ds(q0, 2 * blk), :], vm_s[r, pl.ds(q0, 2 * blk), :], b_lo, b_hi, lane_b)
        rows = pl.ds(j * (blk * DIL_MID) + r, blk, stride=DIL_MID)
        num, m_new, den = merge(rows, o_b, m_b, d_b)
        num_s[rows, :] = num
        m_s[rows, :] = m_new
        den_s[rows, :] = den
        return carry

    lax.fori_loop(0, DIL_MID * mid_blocks, mid_block, 0, unroll=ATT_UNROLL)

    def nat_block(j, carry):
        q0 = pl.multiple_of(j * blk, blk)
        rows = pl.ds(q0, blk)
        b_lo, b_hi = bias_pair(2, first_tile & (j == 0))
        o_b, m_b, d_b = _pair_block(
            qn_s[0, rows, :], qn_s[1, rows, :],
            kn_s[pl.ds(q0, 2 * blk), :], vn_s[pl.ds(q0, 2 * blk), :], b_lo, b_hi, lane_b)
        num, _, den = merge(rows, o_b, m_b, d_b)
        att_ref[rows, :] = (num / den).astype(BF16)
        return carry

    lax.fori_loop(0, ATT_TILE // blk, nat_block, 0, unroll=ATT_UNROLL)


def _dilated_attention(qkv, batch):
    t = qkv.shape[0]
    tiles = t // batch // ATT_TILE
    blk = ATT_BLOCK

    def spec(col0):
        return pl.BlockSpec((ATT_TILE, LANES), lambda b, p, i: (b * tiles + i, col0 + p))

    bf = lambda *shape: pltpu.VMEM(shape, BF16)
    return pl.pallas_call(
        _attn_kernel,
        grid=(batch, ATT_PAIRS, tiles),
        in_specs=[spec(0), spec(ATT_PAIRS), spec(2 * ATT_PAIRS)],
        out_specs=spec(0),
        out_shape=jax.ShapeDtypeStruct((t, ATT_WIDTH), BF16),
        scratch_shapes=[
            bf(2, ATT_TILE, LANES), bf(blk + ATT_TILE, LANES), bf(blk + ATT_TILE, LANES),
            pltpu.VMEM((DIL_MID, MID_LEN, LANES), F32),
            bf(2, DIL_MID, MID_LEN, LANES), bf(DIL_MID, blk + MID_LEN, LANES),
            bf(DIL_MID, blk + MID_LEN, LANES),
            bf(2, DIL_MAX, blk, LANES), bf(DIL_MAX, 2 * blk, LANES), bf(DIL_MAX, 2 * blk, LANES),
            pltpu.VMEM((ATT_TILE, LANES), F32), pltpu.VMEM((ATT_TILE, LANES), F32),
            pltpu.VMEM((ATT_TILE, LANES), F32),
            pltpu.VMEM((3 * 2 * HEADS_PER_PAIR, blk, 2 * blk), F32),
        ],
        compiler_params=_params(3),
        name="attn",
    )(qkv, qkv, qkv)


def _split3(x):
    hi = x.astype(BF16)
    r1 = x - hi.astype(F32)
    mid = r1.astype(BF16)
    lo = (r1 - mid.astype(F32)).astype(BF16)
    return hi, mid, lo


def _dot_exact_rhs(x, w):
    return sum(jnp.dot(part, w, preferred_element_type=F32) for part in _split3(x))


def _ssd_kernel(zg_ref, xbc_ref, dt_ref, dtb_ref, alog_ref, dsk_ref, nw_ref,
                triu_ref, exp_ref, y_ref, state_s):
    c = pl.program_id(1)
    q = SSD_CHUNK
    heads_per_group = SSD_HEADS // SSD_GROUPS
    gw = heads_per_group * SSD_HEAD_DIM

    @pl.when(c == 0)
    def _reset():
        state_s[...] = jnp.zeros_like(state_s)

    ti = lax.broadcasted_iota(jnp.int32, (q, q), 0)
    tj = lax.broadcasted_iota(jnp.int32, (q, q), 1)
    causal = ti >= tj
    low_half = lax.broadcasted_iota(jnp.int32, (q, LANES), 1) < SSD_HEAD_DIM
    a_t = -jnp.exp(alog_ref[...])

    for ci in range(SSD_STEP_CHUNKS):
        r0 = ci * q
        xs = xbc_ref[r0:r0 + q, :SSD_WIDTH]
        bm = xbc_ref[r0:r0 + q, SSD_WIDTH:SSD_WIDTH + SSD_GROUPS * SSD_STATE]
        cm = xbc_ref[r0:r0 + q, SSD_WIDTH + SSD_GROUPS * SSD_STATE:]

        dt_t = _softplus(dt_ref[r0:r0 + q, :].T[0:SSD_HEADS, :] + dtb_ref[...])
        acs_t = _dot_exact_rhs(dt_t * a_t, triu_ref[...])
        e_t = jnp.exp(acs_t)
        w_t = jnp.exp(acs_t[:, q - 1:q] - acs_t) * dt_t
        chunk_decay = jnp.sum(e_t[:, q - 1:q] * exp_ref[...], axis=0, keepdims=True)
        cols_t = jnp.concatenate(
            [acs_t, e_t, jnp.zeros((q - 2 * SSD_HEADS, q), F32)], axis=0).T

        y_parts = []
        for g in range(SSD_GROUPS):
            bm_g = bm[:, g * SSD_STATE:(g + 1) * SSD_STATE]
            cm_g = cm[:, g * SSD_STATE:(g + 1) * SSD_STATE]
            gmat = lax.dot_general(cm_g.astype(BF16), bm_g.astype(BF16),
                                   (((1,), (1,)), ((), ())), preferred_element_type=F32)
            bm_gt = bm_g.T
            for pp in range(heads_per_group // HEADS_PER_PAIR):
                p = g * (heads_per_group // HEADS_PER_PAIR) + pp
                cols = slice(p * LANES, (p + 1) * LANES)
                x_p = xs[:, cols]
                s_p = state_s[:, cols]
                lhs_y, lhs_s, rhs_y, rhs_s = [], [], [], []
                for hh in range(HEADS_PER_PAIR):
                    h = HEADS_PER_PAIR * p + hh
                    keep = low_half if hh == 0 else ~low_half
                    seg = cols_t[:, h:h + 1] - acs_t[h:h + 1, :]
                    lmat = jnp.exp(jnp.where(causal, seg, NEG_INF))
                    scores = gmat * lmat * dt_t[h:h + 1, :]
                    c_dec = cm_g * cols_t[:, SSD_HEADS + h:SSD_HEADS + h + 1]
                    x_h = jnp.where(keep, x_p, 0.0).astype(BF16)
                    s_h = jnp.where(keep, s_p, 0.0).astype(BF16)
                    lhs_y += [scores.astype(BF16), c_dec.astype(BF16)]
                    rhs_y += [x_h, s_h]
                    lhs_s.append((bm_gt * w_t[h:h + 1, :]).astype(BF16))
                    rhs_s.append(x_h)
                y_parts.append(jnp.dot(jnp.concatenate(lhs_y, axis=1),
                                       jnp.concatenate(rhs_y, axis=0),
                                       preferred_element_type=F32))
                state_s[:, cols] = s_p * chunk_decay[:, cols] + jnp.dot(
                    jnp.concatenate(lhs_s, axis=1), jnp.concatenate(rhs_s, axis=0),
                    preferred_element_type=F32)
        y = jnp.concatenate(y_parts, axis=-1) + dsk_ref[...] * xs
        y = y * zg_ref[r0:r0 + q, :]
        outs = []
        for g in range(SSD_GROUPS):
            yg = y[:, g * gw:(g + 1) * gw]
            outs.append(yg * lax.rsqrt(jnp.mean(yg * yg, axis=-1, keepdims=True) + SSD_NORM_EPS))
        y_ref[r0:r0 + q, :] = (jnp.concatenate(outs, axis=-1) * nw_ref[...]).astype(BF16)


def _ssd(zg, xbc, dt, dt_bias, a_log, d_skip, norm_w, batch):
    t = zg.shape[0]
    rows_step = SSD_STEP_CHUNKS * SSD_CHUNK
    ns = t // batch // rows_step
    row = lambda width: pl.BlockSpec((rows_step, width), lambda b, c: (b * ns + c, 0))
    per_time = lambda v: jnp.broadcast_to(v.astype(F32)[:, None], (SSD_HEADS, SSD_CHUNK))
    triu = jnp.triu(jnp.ones((SSD_CHUNK, SSD_CHUNK), BF16))
    expand = (jnp.arange(SSD_HEADS)[:, None] == (jnp.arange(SSD_WIDTH)[None, :] // SSD_HEAD_DIM)
              ).astype(F32)
    d_exp = jnp.repeat(d_skip.astype(F32), SSD_HEAD_DIM).reshape(1, SSD_WIDTH)
    return pl.pallas_call(
        _ssd_kernel,
        grid=(batch, ns),
        in_specs=[row(SSD_WIDTH), row(SSD_CONV_CH), row(DT_PAD),
                  _const_spec((SSD_HEADS, SSD_CHUNK)), _const_spec((SSD_HEADS, SSD_CHUNK)),
                  _const_spec((1, SSD_WIDTH)), _const_spec((1, SSD_WIDTH)),
                  _const_spec((SSD_CHUNK, SSD_CHUNK)), _const_spec((SSD_HEADS, SSD_WIDTH))],
        out_specs=row(SSD_WIDTH),
        out_shape=jax.ShapeDtypeStruct((t, SSD_WIDTH), BF16),
        scratch_shapes=[pltpu.VMEM((SSD_STATE, SSD_WIDTH), F32)],
        compiler_params=_params(2),
        name="ssd",
    )(zg, xbc, dt, per_time(dt_bias), per_time(a_log), d_exp, norm_w.reshape(1, -1), triu,
      expand)


def _lru_kernel(gate_ref, xc_ref, wa_ref, ba_ref, wx_ref, bx_ref, lam_ref, y_ref, h_s):
    c = pl.program_id(1)
    tt = LRU_TILE

    @pl.when(c == 0)
    def _reset():
        h_s[...] = jnp.zeros_like(h_s)

    xc = xc_ref[...]
    xb = xc.astype(BF16)
    r = jax.nn.sigmoid(jnp.dot(xb, wa_ref[...], preferred_element_type=F32) + ba_ref[...])
    i = jax.nn.sigmoid(jnp.dot(xb, wx_ref[...], preferred_element_type=F32) + bx_ref[...])
    log_a = -LRU_C * r * _softplus(-lam_ref[...])
    a = jnp.exp(log_a)
    u = jnp.sqrt(-jnp.tanh(log_a) * (a * a + 1.0)) * (i * xc)

    groups = tt // SUBLANES
    a = a.reshape(groups, SUBLANES, LRU_WIDTH)
    u = u.reshape(groups, SUBLANES, LRU_WIDTH)
    sub = lax.broadcasted_iota(jnp.int32, (groups, SUBLANES, LRU_WIDTH), 1)
    step = 1
    while step < SUBLANES:
        keep = sub >= step
        a_sh = jnp.where(keep, pltpu.roll(a, step, 1), 1.0)
        u_sh = jnp.where(keep, pltpu.roll(u, step, 1), 0.0)
        u = a * u_sh + u
        a = a * a_sh
        step *= 2
    carry = h_s[0:1, :]
    hs = []
    for gi in range(groups):
        h_g = a[gi] * carry + u[gi]
        hs.append(h_g)
        carry = h_g[SUBLANES - 1:SUBLANES, :]
    h_s[...] = jnp.broadcast_to(carry, h_s.shape)
    h = jnp.concatenate(hs, axis=0)
    y_ref[...] = (h * gate_ref[...]).astype(BF16)


def _block_diag(w):
    nb, c, d = w.shape
    eye = jnp.eye(nb, dtype=w.dtype)
    return (eye[:, None, :, None] * w[:, :, None, :]).reshape(nb * c, nb * d)


def _lru(gate, xc, wa, ba, wx, bx, lam, batch):
    t = gate.shape[0]
    s = t // batch
    nt = s // LRU_TILE
    row = pl.BlockSpec((LRU_TILE, LRU_WIDTH), lambda b, c: (b * nt + c, 0))
    vec = lambda v: v.astype(F32).reshape(1, LRU_WIDTH)
    mat = _const_spec((LRU_WIDTH, LRU_WIDTH))
    one = _const_spec((1, LRU_WIDTH))
    return pl.pallas_call(
        _lru_kernel,
        grid=(batch, nt),
        in_specs=[row, row, mat, one, mat, one, one],
        out_specs=row,
        out_shape=jax.ShapeDtypeStruct((t, LRU_WIDTH), BF16),
        scratch_shapes=[pltpu.VMEM((SUBLANES, LRU_WIDTH), F32)],
        compiler_params=_params(2),
        name="rglru",
    )(gate, xc, _block_diag(wa).astype(BF16), vec(ba), _block_diag(wx).astype(BF16), vec(bx),
      vec(lam))


def _outffn_kernel(x_ref, att_ref, ssd_ref, lru_ref, wo_ref, gf_ref, wg_ref, wu_ref, wd_ref,
                   nfin_ref, o_ref, hn_s, *, final_norm):
    x1 = x_ref[...]
    for j, m_ref in enumerate((att_ref, ssd_ref, lru_ref)):
        x1 = x1 + jnp.dot(m_ref[...], wo_ref[j * ATT_WIDTH:(j + 1) * ATT_WIDTH, :],
                          preferred_element_type=F32)
    hn_s[...] = _rmsnorm(x1, gf_ref[...]).astype(BF16)
    o_ref[...] = x1
    for c0, c1 in FF_CHUNKS:
        hn = hn_s[...]
        gate = jnp.dot(hn, wg_ref[:, c0:c1], preferred_element_type=F32)
        up = jnp.dot(hn, wu_ref[:, c0:c1], preferred_element_type=F32)
        act = (_silu(gate) * up).astype(BF16)
        o_ref[...] += jnp.dot(act, wd_ref[c0:c1, :], preferred_element_type=F32)
    if final_norm:
        o_ref[...] = _rmsnorm(o_ref[...], nfin_ref[...])


def _outffn(x2, att, ssd, lru, w_out, norm_ffn, w_gate, w_up, w_down, norm_final, final_norm):
    t = x2.shape[0]
    row = lambda width: pl.BlockSpec((ROW_TILE, width), lambda i: (i, 0))
    return pl.pallas_call(
        functools.partial(_outffn_kernel, final_norm=final_norm),
        grid=(t // ROW_TILE,),
        in_specs=[row(D_MODEL), row(ATT_WIDTH), row(SSD_WIDTH), row(LRU_WIDTH),
                  _const_spec((D_MIX, D_MODEL)), _const_spec((1, D_MODEL)),
                  _const_spec((D_MODEL, D_FF)), _const_spec((D_MODEL, D_FF)),
                  _const_spec((D_FF, D_MODEL)), _const_spec((1, D_MODEL))],
        out_specs=row(D_MODEL),
        out_shape=jax.ShapeDtypeStruct((t, D_MODEL), F32),
        scratch_shapes=[pltpu.VMEM((ROW_TILE, D_MODEL), BF16)],
        compiler_params=_params(1),
        name="outffn",
    )(x2, att, ssd, lru, w_out.astype(BF16), norm_ffn.reshape(1, -1), w_gate.astype(BF16),
      w_up.astype(BF16), w_down.astype(BF16), norm_final.reshape(1, -1))


def _prep_w_in(w):
    o_dt = 3 * ATT_WIDTH + SSD_WIDTH + SSD_CONV_CH
    o_gl = o_dt + SSD_HEADS
    o_xl = o_gl + LRU_WIDTH
    dt_cols = jnp.pad(w[:, o_dt:o_gl], ((0, 0), (0, DT_PAD - SSD_HEADS)))
    return jnp.concatenate([w[:, :o_dt], w[:, o_xl:], w[:, o_gl:o_xl], dt_cols],
                           axis=1).astype(BF16)


def _layer(x2, batch, p, norm_final, final_norm):
    conv_w = jnp.concatenate([p["ssd_conv_w"], p["lru_conv_w"]], axis=1)
    conv_b = jnp.concatenate([p["ssd_conv_b"], p["lru_conv_b"]]).reshape(1, CONV_CH)
    qkv, zg, xbc, xc_lru, gate_lru, dt = _inproj(
        x2, p["norm_mix"].reshape(1, -1), _prep_w_in(p["w_in"]), conv_w, conv_b,
        x2.shape[0] // batch)
    att = _dilated_attention(qkv, batch)
    ssd = _ssd(zg, xbc, dt, p["ssd_dt_bias"], p["ssd_a_log"], p["ssd_d"], p["ssd_norm"], batch)
    lru = _lru(gate_lru, xc_lru, p["lru_wa"], p["lru_ba"], p["lru_wx"], p["lru_bx"],
               p["lru_lambda"], batch)
    return _outffn(x2, att, ssd, lru, p["w_out"], p["norm_ffn"], p["w_gate"], p["w_up"],
                   p["w_down"], norm_final, final_norm)


def kernel(x, norm_mix, w_in, ssd_conv_w, ssd_conv_b, ssd_dt_bias, ssd_a_log, ssd_d, ssd_norm,
           lru_conv_w, lru_conv_b, lru_wa, lru_ba, lru_wx, lru_bx, lru_lambda, w_out,
           norm_ffn, w_gate, w_up, w_down, norm_final):
    batch, seq, _ = x.shape
    stacked = dict(norm_mix=norm_mix, w_in=w_in, ssd_conv_w=ssd_conv_w, ssd_conv_b=ssd_conv_b,
                   ssd_dt_bias=ssd_dt_bias, ssd_a_log=ssd_a_log, ssd_d=ssd_d, ssd_norm=ssd_norm,
                   lru_conv_w=lru_conv_w, lru_conv_b=lru_conv_b, lru_wa=lru_wa, lru_ba=lru_ba,
                   lru_wx=lru_wx, lru_bx=lru_bx, lru_lambda=lru_lambda, w_out=w_out,
                   norm_ffn=norm_ffn, w_gate=w_gate, w_up=w_up, w_down=w_down)
    depth = w_in.shape[0]
    x2 = x.reshape(batch * seq, D_MODEL)
    for l in range(depth):
        p = {k: v[l] for k, v in stacked.items()}
        x2 = _layer(x2, batch, p, norm_final, l == depth - 1)
    return x2.reshape(batch, seq, D_MODEL)
```

```python
import functools
import math

import jax
import jax.numpy as jnp
from jax import lax
from jax.experimental import pallas as pl
from jax.experimental.pallas import tpu as pltpu

F32 = jnp.float32
BF16 = jnp.bfloat16

D_MODEL = 1024
ATT_HEADS = 8
ATT_HEAD_DIM = 64
ATT_WIDTH = ATT_HEADS * ATT_HEAD_DIM
ATT_BLOCK = 128
ATT_DILATIONS = (1, 4, 16)
SSD_HEADS = 8
SSD_HEAD_DIM = 64
SSD_WIDTH = SSD_HEADS * SSD_HEAD_DIM
SSD_GROUPS = 2
SSD_STATE = 128
SSD_CHUNK = 128
SSD_CONV_CH = SSD_WIDTH + 2 * SSD_GROUPS * SSD_STATE
LRU_WIDTH = 512
LRU_BLOCKS = 8
LRU_BLOCK_W = LRU_WIDTH // LRU_BLOCKS
LRU_C = 8.0
CONV_WIDTH = 4
D_MIX = ATT_WIDTH + SSD_WIDTH + LRU_WIDTH
D_FF = 2816
NORM_EPS = 1e-6
SSD_NORM_EPS = 1e-5

LANES = 128
SUBLANES = 8
VMEM_LIMIT_BYTES = 56 * 1024 * 1024

ROW_TILE = 512
LRU_TILE = 512
SSD_STEP_CHUNKS = 4
DT_PAD = LANES
FF_CHUNKS = ((0, 768), (768, 1536), (1536, 2304), (2304, 2816))
CONV_PAD = SUBLANES

DIL_MID, DIL_MAX = ATT_DILATIONS[1], ATT_DILATIONS[2]
DIL_STEP = DIL_MAX // DIL_MID
ATT_TILE = ATT_BLOCK * DIL_MAX
MID_LEN = ATT_TILE // DIL_MID
HEADS_PER_PAIR = LANES // ATT_HEAD_DIM
ATT_PAIRS = ATT_WIDTH // LANES
ATT_UNROLL = 8
assert ATT_HEADS == 8 and HEADS_PER_PAIR == 2 and DIL_STEP == DIL_MID and ATT_DILATIONS[0] == 1

LOG2E = 1.4426950408889634
NEG_INF = float("-inf")


def _params(n_axes):
    return pltpu.CompilerParams(
        dimension_semantics=("arbitrary",) * n_axes,
        vmem_limit_bytes=VMEM_LIMIT_BYTES)


def _const_spec(shape):
    nd = len(shape)
    return pl.BlockSpec(shape, lambda *_: (0,) * nd, pipeline_mode=pl.Buffered(1))


def _rmsnorm(x, g):
    return x * lax.rsqrt(jnp.mean(x * x, axis=-1, keepdims=True) + NORM_EPS) * g


def _softplus(x):
    return jnp.maximum(x, 0.0) + jnp.log1p(jnp.exp(-jnp.abs(x)))


def _silu(x):
    return x * jax.nn.sigmoid(x)


_C_QKV = 3 * ATT_WIDTH
_C_Z = _C_QKV + SSD_WIDTH
_C_XBC = _C_Z + SSD_CONV_CH
_C_GL = _C_XBC + LRU_WIDTH
_C_XL = _C_GL + LRU_WIDTH
_C_DT = _C_XL + DT_PAD


def _inproj_kernel(x_ref, g_ref, w_ref, qkv_ref, z_ref, xbc_ref, gl_ref, xl_ref, dt_ref):
    h = _rmsnorm(x_ref[...], g_ref[...]).astype(BF16)

    def seg(a, b):
        return jnp.dot(h, w_ref[:, a:b], preferred_element_type=F32)

    qkv_ref[:, 0:ATT_WIDTH] = seg(0, ATT_WIDTH) * (ATT_HEAD_DIM ** -0.5 * LOG2E)
    qkv_ref[:, ATT_WIDTH:_C_QKV] = seg(ATT_WIDTH, _C_QKV)
    z_ref[...] = seg(_C_QKV, _C_Z)
    xbc_ref[...] = seg(_C_Z, _C_XBC)
    gl_ref[...] = seg(_C_XBC, _C_GL)
    xl_ref[...] = seg(_C_GL, _C_XL)
    dt_ref[...] = seg(_C_XL, _C_DT)


def _inproj(x2, g, w):
    t = x2.shape[0]
    row = lambda width: pl.BlockSpec((ROW_TILE, width), lambda i: (i, 0))
    widths = (_C_QKV, SSD_WIDTH, SSD_CONV_CH, LRU_WIDTH, LRU_WIDTH, DT_PAD)
    return pl.pallas_call(
        _inproj_kernel,
        grid=(t // ROW_TILE,),
        in_specs=[row(D_MODEL), _const_spec((1, D_MODEL)), _const_spec((D_MODEL, _C_DT))],
        out_specs=[row(wd) for wd in widths],
        out_shape=[jax.ShapeDtypeStruct((t, wd), F32) for wd in widths],
        compiler_params=_params(1),
        name="inproj",
    )(x2, g, w)


def _scores(q_lo, q_hi, kk, bias):
    q_both = jnp.concatenate([q_lo, q_hi], axis=0)
    return lax.dot_general(q_both, kk, (((1,), (1,)), ((), ())),
                           preferred_element_type=F32) + bias


def _attn_kernel(q_ref, k_ref, v_ref, att_ref,
                 qn_s, kn_s, vn_s, tmp_s, qm_s, km_s, vm_s, qx_s, kx_s, vx_s,
                 mn_s, mm_s, mx_s, acc_s, bias_s, sc_s):
    pair = pl.program_id(1)
    tile = pl.program_id(2)
    blk = ATT_BLOCK
    first_tile = tile == 0
    mid_blocks = MID_LEN // blk

    @pl.when(first_tile)
    def _start_sequence():
        qi = lax.broadcasted_iota(jnp.int32, (blk, 2 * blk), 0)
        ki = lax.broadcasted_iota(jnp.int32, (blk, 2 * blk), 1)
        dist = blk + qi - ki
        band = (dist >= 0) & (dist <= blk)
        band_first = band & (ki >= blk)
        for hh in range(HEADS_PER_PAIR):
            expo = (126 - HEADS_PER_PAIR * pair - hh) << 23
            slope = lax.bitcast_convert_type(jnp.full((blk, 2 * blk), expo, jnp.int32), F32)
            rows = slice(hh * blk, (hh + 1) * blk)
            for br, dil in enumerate((DIL_MAX, DIL_MID, 1)):
                alibi = (-slope * (dil * dist).astype(F32)) * LOG2E
                bias_s[2 * br, rows, :] = jnp.where(band, alibi, NEG_INF)
                bias_s[2 * br + 1, rows, :] = jnp.where(band_first, alibi, NEG_INF)
        kn_s[0:blk, :] = jnp.zeros((blk, LANES), BF16)
        vn_s[:, 0:blk, :] = jnp.zeros((2, blk, LANES), BF16)
        km_s[:, 0:blk, :] = jnp.zeros((DIL_MID, blk, LANES), BF16)
        vm_s[:, :, 0:blk, :] = jnp.zeros((2, DIL_MID, blk, LANES), BF16)
        kx_s[:, 0:blk, :] = jnp.zeros((DIL_MAX, blk, LANES), BF16)
        vx_s[:, :, 0:blk, :] = jnp.zeros((2, DIL_MAX, blk, LANES), BF16)

    @pl.when(tile > 0)
    def _carry_history():
        kn_s[0:blk, :] = kn_s[ATT_TILE:ATT_TILE + blk, :]
        vn_s[:, 0:blk, :] = vn_s[:, ATT_TILE:ATT_TILE + blk, :]
        km_s[:, 0:blk, :] = km_s[:, MID_LEN:MID_LEN + blk, :]
        vm_s[:, :, 0:blk, :] = vm_s[:, :, MID_LEN:MID_LEN + blk, :]
        kx_s[:, 0:blk, :] = kx_s[:, blk:2 * blk, :]
        vx_s[:, :, 0:blk, :] = vx_s[:, :, blk:2 * blk, :]

    lane_m = lax.broadcasted_iota(jnp.int32, (MID_LEN, LANES), 1) < ATT_HEAD_DIM
    lane_b = lax.broadcasted_iota(jnp.int32, (blk, LANES), 1) < ATT_HEAD_DIM

    def put_q(dst, idx, x, mask):
        dst[(0,) + idx] = jnp.where(mask, x, 0.0).astype(BF16)
        dst[(1,) + idx] = jnp.where(mask, 0.0, x).astype(BF16)

    def put_k(dst, idx, x, mask):
        del mask
        dst[idx] = x.astype(BF16)

    def put_v(dst, idx, x, mask):
        dst[(0,) + idx] = jnp.where(mask, x, 1.0).astype(BF16)
        dst[(1,) + idx] = jnp.where(mask, 1.0, x).astype(BF16)

    for src, nat, mid, big, put, hist in (
            (q_ref, qn_s, qm_s, qx_s, put_q, 0),
            (k_ref, kn_s, km_s, kx_s, put_k, blk),
            (v_ref, vn_s, vm_s, vx_s, put_v, blk)):
        for c in range(DIL_MID):
            put(nat, (slice(hist + c * MID_LEN, hist + (c + 1) * MID_LEN), slice(None)),
                src[c * MID_LEN:(c + 1) * MID_LEN, :], lane_m)
        for r in range(DIL_MID):
            x = src[pl.ds(r, MID_LEN, stride=DIL_MID), :]
            tmp_s[r] = x
            put(mid, (r, slice(hist, hist + MID_LEN), slice(None)), x, lane_m)
        for r in range(DIL_MID):
            for c in range(DIL_STEP):
                y = tmp_s.at[r][pl.ds(c, blk, stride=DIL_STEP), :]
                put(big, (r + DIL_MID * c, slice(hist, hist + blk), slice(None)), y, lane_b)

    def bias_of(branch, is_first):
        return bias_s[2 * branch + jnp.where(is_first, 1, 0)]

    def big_ops(r):
        return (qx_s[0, r], qx_s[1, r], kx_s[r], bias_of(0, first_tile),
                pl.ds(r, blk, stride=DIL_MAX))

    def mid_ops(i):
        r = i // mid_blocks
        j = i % mid_blocks
        q0 = pl.multiple_of(j * blk, blk)
        return (qm_s[0, r, pl.ds(q0, blk), :], qm_s[1, r, pl.ds(q0, blk), :],
                km_s[r, pl.ds(q0, 2 * blk), :], bias_of(1, first_tile & (j == 0)),
                pl.ds(j * (blk * DIL_MID) + r, blk, stride=DIL_MID))

    def nat_ops(j):
        q0 = pl.multiple_of(j * blk, blk)
        return (qn_s[0, pl.ds(q0, blk), :], qn_s[1, pl.ds(q0, blk), :],
                kn_s[pl.ds(q0, 2 * blk), :], bias_of(2, first_tile & (j == 0)),
                pl.ds(q0, blk))

    def loop(n, body):
        lax.fori_loop(0, n, lambda i, c: (body(i), c)[1], 0, unroll=ATT_UNROLL)

    n_big, n_mid, n_nat = DIL_MAX, DIL_MID * mid_blocks, ATT_TILE // blk

    def row_max(ops, slot):
        q_lo, q_hi, kk, bias, rows = ops
        s = _scores(q_lo, q_hi, kk, bias)
        sc_s[slot] = s
        m = jnp.max(s, axis=-1, keepdims=True)
        return jnp.broadcast_to(m, (2 * blk, LANES)), rows

    def max_big(r):
        m, rows = row_max(big_ops(r), r)
        mn_s[0, rows, :] = m[:blk]
        mn_s[1, rows, :] = m[blk:]

    def max_more(ops, slot):
        m, rows = row_max(ops, slot)
        mn_s[0, rows, :] = jnp.maximum(mn_s[0, rows, :], m[:blk])
        mn_s[1, rows, :] = jnp.maximum(mn_s[1, rows, :], m[blk:])

    loop(n_big, max_big)
    loop(n_mid, lambda i: max_more(mid_ops(i), n_big + i))
    loop(n_nat, lambda j: max_more(nat_ops(j), n_big + n_mid + j))

    for hh in range(HEADS_PER_PAIR):
        for r in range(DIL_MID):
            mm_s[hh, r] = mn_s.at[hh][pl.ds(r, MID_LEN, stride=DIL_MID), :]
        for r in range(DIL_MID):
            for c in range(DIL_STEP):
                mx_s[hh, r + DIL_MID * c] = mm_s.at[hh, r][pl.ds(c, blk, stride=DIL_STEP), :]

    def num_den(slot, m_lo, m_hi, v_lo, v_hi):
        m = jnp.concatenate([m_lo, m_hi], axis=0)
        e = jnp.exp2(sc_s[slot] - jnp.concatenate([m, m], axis=1)).astype(BF16)
        return (jnp.dot(e[:blk], v_lo, preferred_element_type=F32),
                jnp.dot(e[blk:], v_hi, preferred_element_type=F32))

    def acc_big(r):
        o_lo, o_hi = num_den(r, mx_s[0, r], mx_s[1, r], vx_s[0, r], vx_s[1, r])
        rows = pl.ds(r, blk, stride=DIL_MAX)
        acc_s[0, rows, :] = o_lo
        acc_s[1, rows, :] = o_hi

    def acc_mid(i):
        r = i // mid_blocks
        j = i % mid_blocks
        q0 = pl.multiple_of(j * blk, blk)
        o_lo, o_hi = num_den(
            n_big + i, mm_s[0, r, pl.ds(q0, blk), :], mm_s[1, r, pl.ds(q0, blk), :],
            vm_s[0, r, pl.ds(q0, 2 * blk), :], vm_s[1, r, pl.ds(q0, 2 * blk), :])
        rows = pl.ds(j * (blk * DIL_MID) + r, blk, stride=DIL_MID)
        acc_s[0, rows, :] += o_lo
        acc_s[1, rows, :] += o_hi

    def acc_nat(j):
        q0 = pl.multiple_of(j * blk, blk)
        rows = pl.ds(q0, blk)
        o_lo, o_hi = num_den(
            n_big + n_mid + j, mn_s[0, rows, :], mn_s[1, rows, :],
            vn_s[0, pl.ds(q0, 2 * blk), :], vn_s[1, pl.ds(q0, 2 * blk), :])
        a_lo = acc_s[0, rows, :] + o_lo
        a_hi = acc_s[1, rows, :] + o_hi
        num = jnp.where(lane_b, a_lo, a_hi)
        den = pltpu.roll(jnp.where(lane_b, a_hi, a_lo), ATT_HEAD_DIM, 1)
        att_ref[rows, :] = (num / den).astype(BF16)

    loop(n_big, acc_big)
    loop(n_mid, acc_mid)
    loop(n_nat, acc_nat)


def _dilated_attention(qkv, batch):
    t = qkv.shape[0]
    tiles = t // batch // ATT_TILE
    blk = ATT_BLOCK

    def spec(col0):
        return pl.BlockSpec((ATT_TILE, LANES), lambda b, p, i: (b * tiles + i, col0 + p))

    bf = lambda *shape: pltpu.VMEM(shape, BF16)
    f32 = lambda *shape: pltpu.VMEM(shape, F32)
    return pl.pallas_call(
        _attn_kernel,
        grid=(batch, ATT_PAIRS, tiles),
        in_specs=[spec(0), spec(ATT_PAIRS), spec(2 * ATT_PAIRS)],
        out_specs=spec(0),
        out_shape=jax.ShapeDtypeStruct((t, ATT_WIDTH), BF16),
        scratch_shapes=[
            bf(2, ATT_TILE, LANES), bf(blk + ATT_TILE, LANES), bf(2, blk + ATT_TILE, LANES),
            f32(DIL_MID, MID_LEN, LANES),
            bf(2, DIL_MID, MID_LEN, LANES), bf(DIL_MID, blk + MID_LEN, LANES),
            bf(2, DIL_MID, blk + MID_LEN, LANES),
            bf(2, DIL_MAX, blk, LANES), bf(DIL_MAX, 2 * blk, LANES),
            bf(2, DIL_MAX, 2 * blk, LANES),
            f32(2, ATT_TILE, LANES), f32(2, DIL_MID, MID_LEN, LANES),
            f32(2, DIL_MAX, blk, LANES), f32(2, ATT_TILE, LANES),
            f32(3 * 2, 2 * blk, 2 * blk),
            f32(3 * DIL_MAX, 2 * blk, 2 * blk),
        ],
        compiler_params=_params(3),
        name="attn",
    )(qkv, qkv, qkv)


def _split3(x):
    hi = x.astype(BF16)
    r1 = x - hi.astype(F32)
    mid = r1.astype(BF16)
    lo = (r1 - mid.astype(F32)).astype(BF16)
    return hi, mid, lo


def _dot_exact_rhs(x, w):
    return sum(jnp.dot(part, w, preferred_element_type=F32) for part in _split3(x))


def _ssd_kernel(z_ref, xbc_ref, dt_ref, cw_ref, cb_ref, dtb_ref, alog_ref, dsk_ref, nw_ref,
                triu_ref, exp_ref, y_ref, win_s, state_s):
    c = pl.program_id(1)
    q = SSD_CHUNK
    pad = CONV_PAD
    rows_step = SSD_STEP_CHUNKS * q
    heads_per_group = SSD_HEADS // SSD_GROUPS
    gw = heads_per_group * SSD_HEAD_DIM

    @pl.when(c == 0)
    def _reset():
        win_s[0:pad, :] = jnp.zeros((pad, SSD_CONV_CH), F32)
        state_s[...] = jnp.zeros_like(state_s)

    win_s[pad:, :] = xbc_ref[...]

    ti = lax.broadcasted_iota(jnp.int32, (q, q), 0)
    tj = lax.broadcasted_iota(jnp.int32, (q, q), 1)
    causal = ti >= tj
    low_half = lax.broadcasted_iota(jnp.int32, (q, LANES), 1) < SSD_HEAD_DIM
    a_t = -jnp.exp(alog_ref[...])
    k_w = CONV_WIDTH

    for ci in range(SSD_STEP_CHUNKS):
        r0 = ci * q
        conv = cb_ref[...] + win_s[pad + r0:pad + r0 + q, :] * cw_ref[k_w - 1:k_w, :]
        for k in range(k_w - 1):
            off = pad + r0 - (k_w - 1) + k
            conv = conv + win_s[off:off + q, :] * cw_ref[k:k + 1, :]
        xbc = _silu(conv)
        xs = xbc[:, :SSD_WIDTH]
        bm = xbc[:, SSD_WIDTH:SSD_WIDTH + SSD_GROUPS * SSD_STATE]
        cm = xbc[:, SSD_WIDTH + SSD_GROUPS * SSD_STATE:]

        dt_t = _softplus(dt_ref[r0:r0 + q, :].T[0:SSD_HEADS, :] + dtb_ref[...])
        acs_t = _dot_exact_rhs(dt_t * a_t, triu_ref[...])
        e_t = jnp.exp(acs_t)
        w_t = jnp.exp(acs_t[:, q - 1:q] - acs_t) * dt_t
        chunk_decay = jnp.sum(e_t[:, q - 1:q] * exp_ref[...], axis=0, keepdims=True)
        cols_t = jnp.concatenate(
            [acs_t, e_t, jnp.zeros((q - 2 * SSD_HEADS, q), F32)], axis=0).T

        y_parts = []
        for g in range(SSD_GROUPS):
            bm_g = bm[:, g * SSD_STATE:(g + 1) * SSD_STATE]
            cm_g = cm[:, g * SSD_STATE:(g + 1) * SSD_STATE]
            gmat = lax.dot_general(cm_g.astype(BF16), bm_g.astype(BF16),
                                   (((1,), (1,)), ((), ())), preferred_element_type=F32)
            bm_gt = bm_g.T
            for pp in range(heads_per_group // HEADS_PER_PAIR):
                p = g * (heads_per_group // HEADS_PER_PAIR) + pp
                cols = slice(p * LANES, (p + 1) * LANES)
                x_p = xs[:, cols]
                s_p = state_s[:, cols]
                lhs_y, lhs_s, rhs_y, rhs_s = [], [], [], []
                for hh in range(HEADS_PER_PAIR):
                    h = HEADS_PER_PAIR * p + hh
                    keep = low_half if hh == 0 else ~low_half
                    seg = cols_t[:, h:h + 1] - acs_t[h:h + 1, :]
                    lmat = jnp.exp(jnp.where(causal, seg, NEG_INF))
                    scores = gmat * lmat * dt_t[h:h + 1, :]
                    c_dec = cm_g * cols_t[:, SSD_HEADS + h:SSD_HEADS + h + 1]
                    x_h = jnp.where(keep, x_p, 0.0).astype(BF16)
                    s_h = jnp.where(keep, s_p, 0.0).astype(BF16)
                    lhs_y += [scores.astype(BF16), c_dec.astype(BF16)]
                    rhs_y += [x_h, s_h]
                    lhs_s.append((bm_gt * w_t[h:h + 1, :]).astype(BF16))
                    rhs_s.append(x_h)
                y_parts.append(jnp.dot(jnp.concatenate(lhs_y, axis=1),
                                       jnp.concatenate(rhs_y, axis=0),
                                       preferred_element_type=F32))
                state_s[:, cols] = s_p * chunk_decay[:, cols] + jnp.dot(
                    jnp.concatenate(lhs_s, axis=1), jnp.concatenate(rhs_s, axis=0),
                    preferred_element_type=F32)
        y = jnp.concatenate(y_parts, axis=-1) + dsk_ref[...] * xs
        y = y * _silu(z_ref[r0:r0 + q, :])
        outs = []
        for g in range(SSD_GROUPS):
            yg = y[:, g * gw:(g + 1) * gw]
            outs.append(yg * lax.rsqrt(jnp.mean(yg * yg, axis=-1, keepdims=True) + SSD_NORM_EPS))
        y_ref[r0:r0 + q, :] = (jnp.concatenate(outs, axis=-1) * nw_ref[...]).astype(BF16)

    win_s[0:pad, :] = win_s[rows_step:rows_step + pad, :]


def _ssd(z, xbc, dt, conv_w, conv_b, dt_bias, a_log, d_skip, norm_w, batch):
    t = z.shape[0]
    rows_step = SSD_STEP_CHUNKS * SSD_CHUNK
    ns = t // batch // rows_step
    row = lambda width: pl.BlockSpec((rows_step, width), lambda b, c: (b * ns + c, 0))
    per_time = lambda v: jnp.broadcast_to(v.astype(F32)[:, None], (SSD_HEADS, SSD_CHUNK))
    triu = jnp.triu(jnp.ones((SSD_CHUNK, SSD_CHUNK), BF16))
    expand = (jnp.arange(SSD_HEADS)[:, None] == (jnp.arange(SSD_WIDTH)[None, :] // SSD_HEAD_DIM)
              ).astype(F32)
    d_exp = jnp.repeat(d_skip.astype(F32), SSD_HEAD_DIM).reshape(1, SSD_WIDTH)
    return pl.pallas_call(
        _ssd_kernel,
        grid=(batch, ns),
        in_specs=[row(SSD_WIDTH), row(SSD_CONV_CH), row(DT_PAD),
                  _const_spec((CONV_WIDTH, SSD_CONV_CH)), _const_spec((1, SSD_CONV_CH)),
                  _const_spec((SSD_HEADS, SSD_CHUNK)), _const_spec((SSD_HEADS, SSD_CHUNK)),
                  _const_spec((1, SSD_WIDTH)), _const_spec((1, SSD_WIDTH)),
                  _const_spec((SSD_CHUNK, SSD_CHUNK)), _const_spec((SSD_HEADS, SSD_WIDTH))],
        out_specs=row(SSD_WIDTH),
        out_shape=jax.ShapeDtypeStruct((t, SSD_WIDTH), BF16),
        scratch_shapes=[pltpu.VMEM((CONV_PAD + rows_step, SSD_CONV_CH), F32),
                        pltpu.VMEM((SSD_STATE, SSD_WIDTH), F32)],
        compiler_params=_params(2),
        name="ssd",
    )(z, xbc, dt, conv_w, conv_b.reshape(1, -1), per_time(dt_bias), per_time(a_log),
      d_exp, norm_w.reshape(1, -1), triu, expand)


def _gelu_tanh(x):
    c = math.sqrt(2.0 / math.pi)
    return 0.5 * x * (1.0 + jnp.tanh(c * (x + 0.044715 * (x * x * x))))


def _lru_kernel(g_ref, x_ref, cw_ref, cb_ref, wa_ref, ba_ref, wx_ref, bx_ref, lam_ref,
                y_ref, win_s, h_s):
    c = pl.program_id(1)
    tt = LRU_TILE
    pad = CONV_PAD

    @pl.when(c == 0)
    def _reset():
        win_s[0:pad, :] = jnp.zeros((pad, LRU_WIDTH), F32)
        h_s[...] = jnp.zeros_like(h_s)

    win_s[pad:, :] = x_ref[...]
    k_w = CONV_WIDTH
    xc = cb_ref[...] + win_s[pad:pad + tt, :] * cw_ref[k_w - 1:k_w, :]
    for k in range(k_w - 1):
        off = pad - (k_w - 1) + k
        xc = xc + win_s[off:off + tt, :] * cw_ref[k:k + 1, :]
    win_s[0:pad, :] = win_s[tt:tt + pad, :]

    xb = xc.astype(BF16)
    r = jax.nn.sigmoid(jnp.dot(xb, wa_ref[...], preferred_element_type=F32) + ba_ref[...])
    i = jax.nn.sigmoid(jnp.dot(xb, wx_ref[...], preferred_element_type=F32) + bx_ref[...])
    log_a = -LRU_C * r * _softplus(-lam_ref[...])
    a = jnp.exp(log_a)
    u = jnp.sqrt(-jnp.tanh(log_a) * (a * a + 1.0)) * (i * xc)

    groups = tt // SUBLANES
    a = a.reshape(groups, SUBLANES, LRU_WIDTH)
    u = u.reshape(groups, SUBLANES, LRU_WIDTH)
    sub = lax.broadcasted_iota(jnp.int32, (groups, SUBLANES, LRU_WIDTH), 1)
    step = 1
    while step < SUBLANES:
        keep = sub >= step
        a_sh = jnp.where(keep, pltpu.roll(a, step, 1), 1.0)
        u_sh = jnp.where(keep, pltpu.roll(u, step, 1), 0.0)
        u = a * u_sh + u
        a = a * a_sh
        step *= 2
    carry = h_s[0:1, :]
    hs = []
    for gi in range(groups):
        h_g = a[gi] * carry + u[gi]
        hs.append(h_g)
        carry = h_g[SUBLANES - 1:SUBLANES, :]
    h_s[...] = jnp.broadcast_to(carry, h_s.shape)
    h = jnp.concatenate(hs, axis=0)
    y_ref[...] = (h * _gelu_tanh(g_ref[...])).astype(BF16)


def _block_diag(w):
    nb, c, d = w.shape
    eye = jnp.eye(nb, dtype=w.dtype)
    return (eye[:, None, :, None] * w[:, :, None, :]).reshape(nb * c, nb * d)


def _lru(g_in, x_in, conv_w, conv_b, wa, ba, wx, bx, lam, batch):
    t = g_in.shape[0]
    s = t // batch
    nt = s // LRU_TILE
    row = pl.BlockSpec((LRU_TILE, LRU_WIDTH), lambda b, c: (b * nt + c, 0))
    vec = lambda v: v.astype(F32).reshape(1, LRU_WIDTH)
    mat = _const_spec((LRU_WIDTH, LRU_WIDTH))
    one = _const_spec((1, LRU_WIDTH))
    return pl.pallas_call(
        _lru_kernel,
        grid=(batch, nt),
        in_specs=[row, row, _const_spec((CONV_WIDTH, LRU_WIDTH)), one, mat, one, mat, one, one],
        out_specs=row,
        out_shape=jax.ShapeDtypeStruct((t, LRU_WIDTH), BF16),
        scratch_shapes=[pltpu.VMEM((CONV_PAD + LRU_TILE, LRU_WIDTH), F32),
                        pltpu.VMEM((SUBLANES, LRU_WIDTH), F32)],
        compiler_params=_params(2),
        name="rglru",
    )(g_in, x_in, conv_w, vec(conv_b), _block_diag(wa).astype(BF16), vec(ba),
      _block_diag(wx).astype(BF16), vec(bx), vec(lam))


def _outffn_kernel(x_ref, att_ref, ssd_ref, lru_ref, wo_ref, gf_ref, wg_ref, wu_ref, wd_ref,
                   nfin_ref, o_ref, hn_s, *, final_norm):
    x1 = x_ref[...]
    for j, m_ref in enumerate((att_ref, ssd_ref, lru_ref)):
        x1 = x1 + jnp.dot(m_ref[...], wo_ref[j * ATT_WIDTH:(j + 1) * ATT_WIDTH, :],
                          preferred_element_type=F32)
    hn_s[...] = _rmsnorm(x1, gf_ref[...]).astype(BF16)
    o_ref[...] = x1
    for c0, c1 in FF_CHUNKS:
        hn = hn_s[...]
        gate = jnp.dot(hn, wg_ref[:, c0:c1], preferred_element_type=F32)
        up = jnp.dot(hn, wu_ref[:, c0:c1], preferred_element_type=F32)
        act = (_silu(gate) * up).astype(BF16)
        o_ref[...] += jnp.dot(act, wd_ref[c0:c1, :], preferred_element_type=F32)
    if final_norm:
        o_ref[...] = _rmsnorm(o_ref[...], nfin_ref[...])


def _outffn(x2, att, ssd, lru, w_out, norm_ffn, w_gate, w_up, w_down, norm_final, final_norm):
    t = x2.shape[0]
    row = lambda width: pl.BlockSpec((ROW_TILE, width), lambda i: (i, 0))
    return pl.pallas_call(
        functools.partial(_outffn_kernel, final_norm=final_norm),
        grid=(t // ROW_TILE,),
        in_specs=[row(D_MODEL), row(ATT_WIDTH), row(SSD_WIDTH), row(LRU_WIDTH),
                  _const_spec((D_MIX, D_MODEL)), _const_spec((1, D_MODEL)),
                  _const_spec((D_MODEL, D_FF)), _const_spec((D_MODEL, D_FF)),
                  _const_spec((D_FF, D_MODEL)), _const_spec((1, D_MODEL))],
        out_specs=row(D_MODEL),
        out_shape=jax.ShapeDtypeStruct((t, D_MODEL), F32),
        scratch_shapes=[pltpu.VMEM((ROW_TILE, D_MODEL), BF16)],
        compiler_params=_params(1),
        name="outffn",
    )(x2, att, ssd, lru, w_out.astype(BF16), norm_ffn.reshape(1, -1), w_gate.astype(BF16),
      w_up.astype(BF16), w_down.astype(BF16), norm_final.reshape(1, -1))


def _prep_w_in(w):
    o_dt = 3 * ATT_WIDTH + SSD_WIDTH + SSD_CONV_CH
    o_gl = o_dt + SSD_HEADS
    dt_cols = jnp.pad(w[:, o_dt:o_gl], ((0, 0), (0, DT_PAD - SSD_HEADS)))
    return jnp.concatenate([w[:, :o_dt], w[:, o_gl:], dt_cols], axis=1).astype(BF16)


def _layer(x2, batch, p, norm_final, final_norm):
    qkv, z, xbc, g_lru, x_lru, dt = _inproj(x2, p["norm_mix"].reshape(1, -1), _prep_w_in(p["w_in"]))
    att = _dilated_attention(qkv, batch)
    ssd = _ssd(z, xbc, dt, p["ssd_conv_w"], p["ssd_conv_b"], p["ssd_dt_bias"], p["ssd_a_log"],
               p["ssd_d"], p["ssd_norm"], batch)
    lru = _lru(g_lru, x_lru, p["lru_conv_w"], p["lru_conv_b"], p["lru_wa"], p["lru_ba"],
               p["lru_wx"], p["lru_bx"], p["lru_lambda"], batch)
    return _outffn(x2, att, ssd, lru, p["w_out"], p["norm_ffn"], p["w_gate"], p["w_up"],
                   p["w_down"], norm_final, final_norm)


def kernel(x, norm_mix, w_in, ssd_conv_w, ssd_conv_b, ssd_dt_bias, ssd_a_log, ssd_d, ssd_norm,
           lru_conv_w, lru_conv_b, lru_wa, lru_ba, lru_wx, lru_bx, lru_lambda, w_out,
           norm_ffn, w_gate, w_up, w_down, norm_final):
    batch, seq, _ = x.shape
    stacked = dict(norm_mix=norm_mix, w_in=w_in, ssd_conv_w=ssd_conv_w, ssd_conv_b=ssd_conv_b,
                   ssd_dt_bias=ssd_dt_bias, ssd_a_log=ssd_a_log, ssd_d=ssd_d, ssd_norm=ssd_norm,
                   lru_conv_w=lru_conv_w, lru_conv_b=lru_conv_b, lru_wa=lru_wa, lru_ba=lru_ba,
                   lru_wx=lru_wx, lru_bx=lru_bx, lru_lambda=lru_lambda, w_out=w_out,
                   norm_ffn=norm_ffn, w_gate=w_gate, w_up=w_up, w_down=w_down)
    depth = w_in.shape[0]
    x2 = x.reshape(batch * seq, D_MODEL)
    for l in range(depth):
        p = {k: v[l] for k, v in stacked.items()}
        x2 = _layer(x2, batch, p, norm_final, l == depth - 1)
    return x2.reshape(batch, seq, D_MODEL)
```

```python
import functools
import math

import jax
import jax.numpy as jnp
from jax import lax
from jax.experimental import pallas as pl
from jax.experimental.pallas import tpu as pltpu

F32 = jnp.float32
BF16 = jnp.bfloat16

D_MODEL = 1024
ATT_HEADS = 8
ATT_HEAD_DIM = 64
ATT_WIDTH = ATT_HEADS * ATT_HEAD_DIM
ATT_BLOCK = 128
ATT_DILATIONS = (1, 4, 16)
SSD_HEADS = 8
SSD_HEAD_DIM = 64
SSD_WIDTH = SSD_HEADS * SSD_HEAD_DIM
SSD_GROUPS = 2
SSD_STATE = 128
SSD_CHUNK = 128
SSD_CONV_CH = SSD_WIDTH + 2 * SSD_GROUPS * SSD_STATE
LRU_WIDTH = 512
LRU_BLOCKS = 8
LRU_BLOCK_W = LRU_WIDTH // LRU_BLOCKS
LRU_C = 8.0
CONV_WIDTH = 4
D_MIX = ATT_WIDTH + SSD_WIDTH + LRU_WIDTH
D_FF = 2816
NORM_EPS = 1e-6
SSD_NORM_EPS = 1e-5

LANES = 128
SUBLANES = 8
VMEM_LIMIT_BYTES = 56 * 1024 * 1024

ROW_TILE = 512
LRU_TILE = 512
SSD_STEP_CHUNKS = 4
DT_PAD = LANES
FF_CHUNKS = ((0, 768), (768, 1536), (1536, 2304), (2304, 2816))
CONV_PAD = SUBLANES

DIL_MID, DIL_MAX = ATT_DILATIONS[1], ATT_DILATIONS[2]
DIL_STEP = DIL_MAX // DIL_MID
ATT_TILE = ATT_BLOCK * DIL_MAX
MID_LEN = ATT_TILE // DIL_MID
HEADS_PER_PAIR = LANES // ATT_HEAD_DIM
ATT_PAIRS = ATT_WIDTH // LANES
ATT_UNROLL = 16
assert ATT_HEADS == 8 and HEADS_PER_PAIR == 2 and DIL_STEP == DIL_MID and ATT_DILATIONS[0] == 1

LOG2E = 1.4426950408889634
NEG_INF = float("-inf")


def _params(n_axes):
    return pltpu.CompilerParams(
        dimension_semantics=("arbitrary",) * n_axes,
        vmem_limit_bytes=VMEM_LIMIT_BYTES)


def _const_spec(shape):
    nd = len(shape)
    return pl.BlockSpec(shape, lambda *_: (0,) * nd, pipeline_mode=pl.Buffered(1))


def _rmsnorm(x, g):
    return x * lax.rsqrt(jnp.mean(x * x, axis=-1, keepdims=True) + NORM_EPS) * g


def _softplus(x):
    return jnp.maximum(x, 0.0) + jnp.log1p(jnp.exp(-jnp.abs(x)))


def _silu(x):
    return x * jax.nn.sigmoid(x)


_C_QKV = 3 * ATT_WIDTH
_C_Z = _C_QKV + SSD_WIDTH
_C_XBC = _C_Z + SSD_CONV_CH
_C_GL = _C_XBC + LRU_WIDTH
_C_XL = _C_GL + LRU_WIDTH
_C_DT = _C_XL + DT_PAD


def _inproj_kernel(x_ref, g_ref, w_ref, qkv_ref, z_ref, xbc_ref, gl_ref, xl_ref, dt_ref):
    h = _rmsnorm(x_ref[...], g_ref[...]).astype(BF16)

    def seg(a, b):
        return jnp.dot(h, w_ref[:, a:b], preferred_element_type=F32)

    qkv_ref[:, 0:ATT_WIDTH] = seg(0, ATT_WIDTH) * (ATT_HEAD_DIM ** -0.5 * LOG2E)
    qkv_ref[:, ATT_WIDTH:_C_QKV] = seg(ATT_WIDTH, _C_QKV)
    z_ref[...] = seg(_C_QKV, _C_Z)
    xbc_ref[...] = seg(_C_Z, _C_XBC)
    gl_ref[...] = seg(_C_XBC, _C_GL)
    xl_ref[...] = seg(_C_GL, _C_XL)
    dt_ref[...] = seg(_C_XL, _C_DT)


def _inproj(x2, g, w):
    t = x2.shape[0]
    row = lambda width: pl.BlockSpec((ROW_TILE, width), lambda i: (i, 0))
    widths = (_C_QKV, SSD_WIDTH, SSD_CONV_CH, LRU_WIDTH, LRU_WIDTH, DT_PAD)
    return pl.pallas_call(
        _inproj_kernel,
        grid=(t // ROW_TILE,),
        in_specs=[row(D_MODEL), _const_spec((1, D_MODEL)), _const_spec((D_MODEL, _C_DT))],
        out_specs=[row(wd) for wd in widths],
        out_shape=[jax.ShapeDtypeStruct((t, wd), F32) for wd in widths],
        compiler_params=_params(1),
        name="inproj",
    )(x2, g, w)


def _scores(q_lo, q_hi, kk, bias):
    q_both = jnp.concatenate([q_lo, q_hi], axis=0)
    return lax.dot_general(q_both, kk, (((1,), (1,)), ((), ())),
                           preferred_element_type=F32) + bias


def _attn_kernel(q_ref, k_ref, v_ref, att_ref,
                 qn_s, kn_s, vn_s, tmp_s, qm_s, km_s, vm_s, qx_s, kx_s, vx_s,
                 mn_s, mm_s, mx_s, acc_s, bias_s, sc_s):
    pair = pl.program_id(1)
    tile = pl.program_id(2)
    blk = ATT_BLOCK
    first_tile = tile == 0
    mid_blocks = MID_LEN // blk

    @pl.when(first_tile)
    def _start_sequence():
        qi = lax.broadcasted_iota(jnp.int32, (blk, 2 * blk), 0)
        ki = lax.broadcasted_iota(jnp.int32, (blk, 2 * blk), 1)
        dist = blk + qi - ki
        band = (dist >= 0) & (dist <= blk)
        band_first = band & (ki >= blk)
        for hh in range(HEADS_PER_PAIR):
            expo = (126 - HEADS_PER_PAIR * pair - hh) << 23
            slope = lax.bitcast_convert_type(jnp.full((blk, 2 * blk), expo, jnp.int32), F32)
            rows = slice(hh * blk, (hh + 1) * blk)
            for br, dil in enumerate((DIL_MAX, DIL_MID, 1)):
                alibi = (-slope * (dil * dist).astype(F32)) * LOG2E
                bias_s[2 * br, rows, :] = jnp.where(band, alibi, NEG_INF)
                bias_s[2 * br + 1, rows, :] = jnp.where(band_first, alibi, NEG_INF)
        kn_s[0:blk, :] = jnp.zeros((blk, LANES), BF16)
        vn_s[:, 0:blk, :] = jnp.zeros((2, blk, LANES), BF16)
        km_s[:, 0:blk, :] = jnp.zeros((DIL_MID, blk, LANES), BF16)
        vm_s[:, :, 0:blk, :] = jnp.zeros((2, DIL_MID, blk, LANES), BF16)
        kx_s[:, 0:blk, :] = jnp.zeros((DIL_MAX, blk, LANES), BF16)
        vx_s[:, :, 0:blk, :] = jnp.zeros((2, DIL_MAX, blk, LANES), BF16)

    @pl.when(tile > 0)
    def _carry_history():
        kn_s[0:blk, :] = kn_s[ATT_TILE:ATT_TILE + blk, :]
        vn_s[:, 0:blk, :] = vn_s[:, ATT_TILE:ATT_TILE + blk, :]
        km_s[:, 0:blk, :] = km_s[:, MID_LEN:MID_LEN + blk, :]
        vm_s[:, :, 0:blk, :] = vm_s[:, :, MID_LEN:MID_LEN + blk, :]
        kx_s[:, 0:blk, :] = kx_s[:, blk:2 * blk, :]
        vx_s[:, :, 0:blk, :] = vx_s[:, :, blk:2 * blk, :]

    lane_m = lax.broadcasted_iota(jnp.int32, (MID_LEN, LANES), 1) < ATT_HEAD_DIM
    lane_b = lax.broadcasted_iota(jnp.int32, (blk, LANES), 1) < ATT_HEAD_DIM

    def put_q(dst, idx, x, mask):
        dst[(0,) + idx] = jnp.where(mask, x, 0.0).astype(BF16)
        dst[(1,) + idx] = jnp.where(mask, 0.0, x).astype(BF16)

    def put_k(dst, idx, x, mask):
        del mask
        dst[idx] = x.astype(BF16)

    def put_v(dst, idx, x, mask):
        dst[(0,) + idx] = jnp.where(mask, x, 1.0).astype(BF16)
        dst[(1,) + idx] = jnp.where(mask, 1.0, x).astype(BF16)

    for src, nat, mid, big, put, hist in (
            (q_ref, qn_s, qm_s, qx_s, put_q, 0),
            (k_ref, kn_s, km_s, kx_s, put_k, blk),
            (v_ref, vn_s, vm_s, vx_s, put_v, blk)):
        for c in range(DIL_MID):
            put(nat, (slice(hist + c * MID_LEN, hist + (c + 1) * MID_LEN), slice(None)),
                src[c * MID_LEN:(c + 1) * MID_LEN, :], lane_m)
        for r in range(DIL_MID):
            x = src[pl.ds(r, MID_LEN, stride=DIL_MID), :]
            tmp_s[r] = x
            put(mid, (r, slice(hist, hist + MID_LEN), slice(None)), x, lane_m)
        for r in range(DIL_MID):
            for c in range(DIL_STEP):
                y = tmp_s.at[r][pl.ds(c, blk, stride=DIL_STEP), :]
                put(big, (r + DIL_MID * c, slice(hist, hist + blk), slice(None)), y, lane_b)

    def bias_of(branch, is_first):
        return bias_s[2 * branch + jnp.where(is_first, 1, 0)]

    def big_ops(r):
        return (qx_s[0, r], qx_s[1, r], kx_s[r], bias_of(0, first_tile),
                pl.ds(r, blk, stride=DIL_MAX))

    def mid_ops(i):
        r = i // mid_blocks
        j = i % mid_blocks
        q0 = pl.multiple_of(j * blk, blk)
        return (qm_s[0, r, pl.ds(q0, blk), :], qm_s[1, r, pl.ds(q0, blk), :],
                km_s[r, pl.ds(q0, 2 * blk), :], bias_of(1, first_tile & (j == 0)),
                pl.ds(j * (blk * DIL_MID) + r, blk, stride=DIL_MID))

    def nat_ops(j):
        q0 = pl.multiple_of(j * blk, blk)
        return (qn_s[0, pl.ds(q0, blk), :], qn_s[1, pl.ds(q0, blk), :],
                kn_s[pl.ds(q0, 2 * blk), :], bias_of(2, first_tile & (j == 0)),
                pl.ds(q0, blk))

    def loop(n, body):
        lax.fori_loop(0, n, lambda i, c: (body(i), c)[1], 0, unroll=ATT_UNROLL)

    n_big, n_mid, n_nat = DIL_MAX, DIL_MID * mid_blocks, ATT_TILE // blk

    def row_max(ops, slot):
        q_lo, q_hi, kk, bias, rows = ops
        s = _scores(q_lo, q_hi, kk, bias)
        sc_s[slot] = s
        m = jnp.max(s, axis=-1, keepdims=True)
        return jnp.broadcast_to(m, (2 * blk, LANES)), rows

    def max_big(r):
        m, rows = row_max(big_ops(r), r)
        mn_s[0, rows, :] = m[:blk]
        mn_s[1, rows, :] = m[blk:]

    def max_more(ops, slot):
        m, rows = row_max(ops, slot)
        mn_s[0, rows, :] = jnp.maximum(mn_s[0, rows, :], m[:blk])
        mn_s[1, rows, :] = jnp.maximum(mn_s[1, rows, :], m[blk:])

    loop(n_big, max_big)
    loop(n_mid, lambda i: max_more(mid_ops(i), n_big + i))
    loop(n_nat, lambda j: max_more(nat_ops(j), n_big + n_mid + j))

    for hh in range(HEADS_PER_PAIR):
        for r in range(DIL_MID):
            mm_s[hh, r] = mn_s.at[hh][pl.ds(r, MID_LEN, stride=DIL_MID), :]
        for r in range(DIL_MID):
            for c in range(DIL_STEP):
                mx_s[hh, r + DIL_MID * c] = mm_s.at[hh, r][pl.ds(c, blk, stride=DIL_STEP), :]

    def num_den(slot, m_lo, m_hi, v_lo, v_hi):
        m = jnp.concatenate([m_lo, m_hi], axis=0)
        e = jnp.exp2(sc_s[slot] - jnp.concatenate([m, m], axis=1)).astype(BF16)
        return (jnp.dot(e[:blk], v_lo, preferred_element_type=F32),
                jnp.dot(e[blk:], v_hi, preferred_element_type=F32))

    def acc_big(r):
        o_lo, o_hi = num_den(r, mx_s[0, r], mx_s[1, r], vx_s[0, r], vx_s[1, r])
        rows = pl.ds(r, blk, stride=DIL_MAX)
        acc_s[0, rows, :] = o_lo
        acc_s[1, rows, :] = o_hi

    def acc_mid(i):
        r = i // mid_blocks
        j = i % mid_blocks
        q0 = pl.multiple_of(j * blk, blk)
        o_lo, o_hi = num_den(
            n_big + i, mm_s[0, r, pl.ds(q0, blk), :], mm_s[1, r, pl.ds(q0, blk), :],
            vm_s[0, r, pl.ds(q0, 2 * blk), :], vm_s[1, r, pl.ds(q0, 2 * blk), :])
        rows = pl.ds(j * (blk * DIL_MID) + r, blk, stride=DIL_MID)
        acc_s[0, rows, :] += o_lo
        acc_s[1, rows, :] += o_hi

    def acc_nat(j):
        q0 = pl.multiple_of(j * blk, blk)
        rows = pl.ds(q0, blk)
        o_lo, o_hi = num_den(
            n_big + n_mid + j, mn_s[0, rows, :], mn_s[1, rows, :],
            vn_s[0, pl.ds(q0, 2 * blk), :], vn_s[1, pl.ds(q0, 2 * blk), :])
        a_lo = acc_s[0, rows, :] + o_lo
        a_hi = acc_s[1, rows, :] + o_hi
        num = jnp.where(lane_b, a_lo, a_hi)
        den = pltpu.roll(jnp.where(lane_b, a_hi, a_lo), ATT_HEAD_DIM, 1)
        att_ref[rows, :] = (num / den).astype(BF16)

    loop(n_big, acc_big)
    loop(n_mid, acc_mid)
    loop(n_nat, acc_nat)


def _dilated_attention(qkv, batch):
    t = qkv.shape[0]
    tiles = t // batch // ATT_TILE
    blk = ATT_BLOCK

    def spec(col0):
        return pl.BlockSpec((ATT_TILE, LANES), lambda b, p, i: (b * tiles + i, col0 + p))

    bf = lambda *shape: pltpu.VMEM(shape, BF16)
    f32 = lambda *shape: pltpu.VMEM(shape, F32)
    return pl.pallas_call(
        _attn_kernel,
        grid=(batch, ATT_PAIRS, tiles),
        in_specs=[spec(0), spec(ATT_PAIRS), spec(2 * ATT_PAIRS)],
        out_specs=spec(0),
        out_shape=jax.ShapeDtypeStruct((t, ATT_WIDTH), BF16),
        scratch_shapes=[
            bf(2, ATT_TILE, LANES), bf(blk + ATT_TILE, LANES), bf(2, blk + ATT_TILE, LANES),
            f32(DIL_MID, MID_LEN, LANES),
            bf(2, DIL_MID, MID_LEN, LANES), bf(DIL_MID, blk + MID_LEN, LANES),
            bf(2, DIL_MID, blk + MID_LEN, LANES),
            bf(2, DIL_MAX, blk, LANES), bf(DIL_MAX, 2 * blk, LANES),
            bf(2, DIL_MAX, 2 * blk, LANES),
            f32(2, ATT_TILE, LANES), f32(2, DIL_MID, MID_LEN, LANES),
            f32(2, DIL_MAX, blk, LANES), f32(2, ATT_TILE, LANES),
            f32(3 * 2, 2 * blk, 2 * blk),
            f32(3 * DIL_MAX, 2 * blk, 2 * blk),
        ],
        compiler_params=_params(3),
        name="attn",
    )(qkv, qkv, qkv)


def _split3(x):
    hi = x.astype(BF16)
    r1 = x - hi.astype(F32)
    mid = r1.astype(BF16)
    lo = (r1 - mid.astype(F32)).astype(BF16)
    return hi, mid, lo


def _dot_exact_rhs(x, w):
    return sum(jnp.dot(part, w, preferred_element_type=F32) for part in _split3(x))


def _ssd_kernel(z_ref, xbc_ref, dt_ref, cw_ref, cb_ref, dtb_ref, alog_ref, dsk_ref, nw_ref,
                triu_ref, exp_ref, y_ref, win_s, state_s):
    c = pl.program_id(1)
    q = SSD_CHUNK
    pad = CONV_PAD
    rows_step = SSD_STEP_CHUNKS * q
    heads_per_group = SSD_HEADS // SSD_GROUPS
    gw = heads_per_group * SSD_HEAD_DIM

    @pl.when(c == 0)
    def _reset():
        win_s[0:pad, :] = jnp.zeros((pad, SSD_CONV_CH), F32)
        state_s[...] = jnp.zeros_like(state_s)

    win_s[pad:, :] = xbc_ref[...]

    ti = lax.broadcasted_iota(jnp.int32, (q, q), 0)
    tj = lax.broadcasted_iota(jnp.int32, (q, q), 1)
    causal = ti >= tj
    low_half = lax.broadcasted_iota(jnp.int32, (q, LANES), 1) < SSD_HEAD_DIM
    a_t = -jnp.exp(alog_ref[...])
    k_w = CONV_WIDTH

    for ci in range(SSD_STEP_CHUNKS):
        r0 = ci * q
        conv = cb_ref[...] + win_s[pad + r0:pad + r0 + q, :] * cw_ref[k_w - 1:k_w, :]
        for k in range(k_w - 1):
            off = pad + r0 - (k_w - 1) + k
            conv = conv + win_s[off:off + q, :] * cw_ref[k:k + 1, :]
        xbc = _silu(conv)
        xs = xbc[:, :SSD_WIDTH]
        bm = xbc[:, SSD_WIDTH:SSD_WIDTH + SSD_GROUPS * SSD_STATE]
        cm = xbc[:, SSD_WIDTH + SSD_GROUPS * SSD_STATE:]

        dt_t = _softplus(dt_ref[r0:r0 + q, :].T[0:SSD_HEADS, :] + dtb_ref[...])
        acs_t = _dot_exact_rhs(dt_t * a_t, triu_ref[...])
        e_t = jnp.exp(acs_t)
        w_t = jnp.exp(acs_t[:, q - 1:q] - acs_t) * dt_t
        chunk_decay = jnp.sum(e_t[:, q - 1:q] * exp_ref[...], axis=0, keepdims=True)
        cols_t = jnp.concatenate(
            [acs_t, e_t, jnp.zeros((q - 2 * SSD_HEADS, q), F32)], axis=0).T

        y_parts = []
        for g in range(SSD_GROUPS):
            bm_g = bm[:, g * SSD_STATE:(g + 1) * SSD_STATE]
            cm_g = cm[:, g * SSD_STATE:(g + 1) * SSD_STATE]
            gmat = lax.dot_general(cm_g.astype(BF16), bm_g.astype(BF16),
                                   (((1,), (1,)), ((), ())), preferred_element_type=F32)
            bm_gt = bm_g.T
            for pp in range(heads_per_group // HEADS_PER_PAIR):
                p = g * (heads_per_group // HEADS_PER_PAIR) + pp
                cols = slice(p * LANES, (p + 1) * LANES)
                x_p = xs[:, cols]
                s_p = state_s[:, cols]
                lhs_y, lhs_s, rhs_y, rhs_s = [], [], [], []
                for hh in range(HEADS_PER_PAIR):
                    h = HEADS_PER_PAIR * p + hh
                    keep = low_half if hh == 0 else ~low_half
                    seg = cols_t[:, h:h + 1] - acs_t[h:h + 1, :]
                    lmat = jnp.exp(jnp.where(causal, seg, NEG_INF))
                    scores = gmat * lmat * dt_t[h:h + 1, :]
                    c_dec = cm_g * cols_t[:, SSD_HEADS + h:SSD_HEADS + h + 1]
                    x_h = jnp.where(keep, x_p, 0.0).astype(BF16)
                    s_h = jnp.where(keep, s_p, 0.0).astype(BF16)
                    lhs_y += [scores.astype(BF16), c_dec.astype(BF16)]
                    rhs_y += [x_h, s_h]
                    lhs_s.append((bm_gt * w_t[h:h + 1, :]).astype(BF16))
                    rhs_s.append(x_h)
                y_parts.append(jnp.dot(jnp.concatenate(lhs_y, axis=1),
                                       jnp.concatenate(rhs_y, axis=0),
                                       preferred_element_type=F32))
                state_s[:, cols] = s_p * chunk_decay[:, cols] + jnp.dot(
                    jnp.concatenate(lhs_s, axis=1), jnp.concatenate(rhs_s, axis=0),
                    preferred_element_type=F32)
        y = jnp.concatenate(y_parts, axis=-1) + dsk_ref[...] * xs
        y = y * _silu(z_ref[r0:r0 + q, :])
        outs = []
        for g in range(SSD_GROUPS):
            yg = y[:, g * gw:(g + 1) * gw]
            outs.append(yg * lax.rsqrt(jnp.mean(yg * yg, axis=-1, keepdims=True) + SSD_NORM_EPS))
        y_ref[r0:r0 + q, :] = (jnp.concatenate(outs, axis=-1) * nw_ref[...]).astype(BF16)

    win_s[0:pad, :] = win_s[rows_step:rows_step + pad, :]


def _ssd(z, xbc, dt, conv_w, conv_b, dt_bias, a_log, d_skip, norm_w, batch):
    t = z.shape[0]
    rows_step = SSD_STEP_CHUNKS * SSD_CHUNK
    ns = t // batch // rows_step
    row = lambda width: pl.BlockSpec((rows_step, width), lambda b, c: (b * ns + c, 0))
    per_time = lambda v: jnp.broadcast_to(v.astype(F32)[:, None], (SSD_HEADS, SSD_CHUNK))
    triu = jnp.triu(jnp.ones((SSD_CHUNK, SSD_CHUNK), BF16))
    expand = (jnp.arange(SSD_HEADS)[:, None] == (jnp.arange(SSD_WIDTH)[None, :] // SSD_HEAD_DIM)
              ).astype(F32)
    d_exp = jnp.repeat(d_skip.astype(F32), SSD_HEAD_DIM).reshape(1, SSD_WIDTH)
    return pl.pallas_call(
        _ssd_kernel,
        grid=(batch, ns),
        in_specs=[row(SSD_WIDTH), row(SSD_CONV_CH), row(DT_PAD),
                  _const_spec((CONV_WIDTH, SSD_CONV_CH)), _const_spec((1, SSD_CONV_CH)),
                  _const_spec((SSD_HEADS, SSD_CHUNK)), _const_spec((SSD_HEADS, SSD_CHUNK)),
                  _const_spec((1, SSD_WIDTH)), _const_spec((1, SSD_WIDTH)),
                  _const_spec((SSD_CHUNK, SSD_CHUNK)), _const_spec((SSD_HEADS, SSD_WIDTH))],
        out_specs=row(SSD_WIDTH),
        out_shape=jax.ShapeDtypeStruct((t, SSD_WIDTH), BF16),
        scratch_shapes=[pltpu.VMEM((CONV_PAD + rows_step, SSD_CONV_CH), F32),
                        pltpu.VMEM((SSD_STATE, SSD_WIDTH), F32)],
        compiler_params=_params(2),
        name="ssd",
    )(z, xbc, dt, conv_w, conv_b.reshape(1, -1), per_time(dt_bias), per_time(a_log),
      d_exp, norm_w.reshape(1, -1), triu, expand)


def _gelu_tanh(x):
    c = math.sqrt(2.0 / math.pi)
    return 0.5 * x * (1.0 + jnp.tanh(c * (x + 0.044715 * (x * x * x))))


def _lru_kernel(g_ref, x_ref, cw_ref, cb_ref, wa_ref, ba_ref, wx_ref, bx_ref, lam_ref,
                y_ref, win_s, h_s):
    c = pl.program_id(1)
    tt = LRU_TILE
    pad = CONV_PAD

    @pl.when(c == 0)
    def _reset():
        win_s[0:pad, :] = jnp.zeros((pad, LRU_WIDTH), F32)
        h_s[...] = jnp.zeros_like(h_s)

    win_s[pad:, :] = x_ref[...]
    k_w = CONV_WIDTH
    xc = cb_ref[...] + win_s[pad:pad + tt, :] * cw_ref[k_w - 1:k_w, :]
    for k in range(k_w - 1):
        off = pad - (k_w - 1) + k
        xc = xc + win_s[off:off + tt, :] * cw_ref[k:k + 1, :]
    win_s[0:pad, :] = win_s[tt:tt + pad, :]

    xb = xc.astype(BF16)
    r = jax.nn.sigmoid(jnp.dot(xb, wa_ref[...], preferred_element_type=F32) + ba_ref[...])
    i = jax.nn.sigmoid(jnp.dot(xb, wx_ref[...], preferred_element_type=F32) + bx_ref[...])
    log_a = -LRU_C * r * _softplus(-lam_ref[...])
    a = jnp.exp(log_a)
    u = jnp.sqrt(-jnp.tanh(log_a) * (a * a + 1.0)) * (i * xc)

    groups = tt // SUBLANES
    a = a.reshape(groups, SUBLANES, LRU_WIDTH)
    u = u.reshape(groups, SUBLANES, LRU_WIDTH)
    sub = lax.broadcasted_iota(jnp.int32, (groups, SUBLANES, LRU_WIDTH), 1)
    step = 1
    while step < SUBLANES:
        keep = sub >= step
        a_sh = jnp.where(keep, pltpu.roll(a, step, 1), 1.0)
        u_sh = jnp.where(keep, pltpu.roll(u, step, 1), 0.0)
        u = a * u_sh + u
        a = a * a_sh
        step *= 2
    carry = h_s[0:1, :]
    hs = []
    for gi in range(groups):
        h_g = a[gi] * carry + u[gi]
        hs.append(h_g)
        carry = h_g[SUBLANES - 1:SUBLANES, :]
    h_s[...] = jnp.broadcast_to(carry, h_s.shape)
    h = jnp.concatenate(hs, axis=0)
    y_ref[...] = (h * _gelu_tanh(g_ref[...])).astype(BF16)


def _block_diag(w):
    nb, c, d = w.shape
    eye = jnp.eye(nb, dtype=w.dtype)
    return (eye[:, None, :, None] * w[:, :, None, :]).reshape(nb * c, nb * d)


def _lru(g_in, x_in, conv_w, conv_b, wa, ba, wx, bx, lam, batch):
    t = g_in.shape[0]
    s = t // batch
    nt = s // LRU_TILE
    row = pl.BlockSpec((LRU_TILE, LRU_WIDTH), lambda b, c: (b * nt + c, 0))
    vec = lambda v: v.astype(F32).reshape(1, LRU_WIDTH)
    mat = _const_spec((LRU_WIDTH, LRU_WIDTH))
    one = _const_spec((1, LRU_WIDTH))
    return pl.pallas_call(
        _lru_kernel,
        grid=(batch, nt),
        in_specs=[row, row, _const_spec((CONV_WIDTH, LRU_WIDTH)), one, mat, one, mat, one, one],
        out_specs=row,
        out_shape=jax.ShapeDtypeStruct((t, LRU_WIDTH), BF16),
        scratch_shapes=[pltpu.VMEM((CONV_PAD + LRU_TILE, LRU_WIDTH), F32),
                        pltpu.VMEM((SUBLANES, LRU_WIDTH), F32)],
        compiler_params=_params(2),
        name="rglru",
    )(g_in, x_in, conv_w, vec(conv_b), _block_diag(wa).astype(BF16), vec(ba),
      _block_diag(wx).astype(BF16), vec(bx), vec(lam))


def _outffn_kernel(x_ref, att_ref, ssd_ref, lru_ref, wo_ref, gf_ref, wg_ref, wu_ref, wd_ref,
                   nfin_ref, o_ref, hn_s, *, final_norm):
    x1 = x_ref[...]
    for j, m_ref in enumerate((att_ref, ssd_ref, lru_ref)):
        x1 = x1 + jnp.dot(m_ref[...], wo_ref[j * ATT_WIDTH:(j + 1) * ATT_WIDTH, :],
                          preferred_element_type=F32)
    hn_s[...] = _rmsnorm(x1, gf_ref[...]).astype(BF16)
    o_ref[...] = x1
    for c0, c1 in FF_CHUNKS:
        hn = hn_s[...]
        gate = jnp.dot(hn, wg_ref[:, c0:c1], preferred_element_type=F32)
        up = jnp.dot(hn, wu_ref[:, c0:c1], preferred_element_type=F32)
        act = (_silu(gate) * up).astype(BF16)
        o_ref[...] += jnp.dot(act, wd_ref[c0:c1, :], preferred_element_type=F32)
    if final_norm:
        o_ref[...] = _rmsnorm(o_ref[...], nfin_ref[...])


def _outffn(x2, att, ssd, lru, w_out, norm_ffn, w_gate, w_up, w_down, norm_final, final_norm):
    t = x2.shape[0]
    row = lambda width: pl.BlockSpec((ROW_TILE, width), lambda i: (i, 0))
    return pl.pallas_call(
        functools.partial(_outffn_kernel, final_norm=final_norm),
        grid=(t // ROW_TILE,),
        in_specs=[row(D_MODEL), row(ATT_WIDTH), row(SSD_WIDTH), row(LRU_WIDTH),
                  _const_spec((D_MIX, D_MODEL)), _const_spec((1, D_MODEL)),
                  _const_spec((D_MODEL, D_FF)), _const_spec((D_MODEL, D_FF)),
                  _const_spec((D_FF, D_MODEL)), _const_spec((1, D_MODEL))],
        out_specs=row(D_MODEL),
        out_shape=jax.ShapeDtypeStruct((t, D_MODEL), F32),
        scratch_shapes=[pltpu.VMEM((ROW_TILE, D_MODEL), BF16)],
        compiler_params=_params(1),
        name="outffn",
    )(x2, att, ssd, lru, w_out.astype(BF16), norm_ffn.reshape(1, -1), w_gate.astype(BF16),
      w_up.astype(BF16), w_down.astype(BF16), norm_final.reshape(1, -1))


def _prep_w_in(w):
    o_dt = 3 * ATT_WIDTH + SSD_WIDTH + SSD_CONV_CH
    o_gl = o_dt + SSD_HEADS
    dt_cols = jnp.pad(w[:, o_dt:o_gl], ((0, 0), (0, DT_PAD - SSD_HEADS)))
    return jnp.concatenate([w[:, :o_dt], w[:, o_gl:], dt_cols], axis=1).astype(BF16)


def _layer(x2, batch, p, norm_final, final_norm):
    qkv, z, xbc, g_lru, x_lru, dt = _inproj(x2, p["norm_mix"].reshape(1, -1), _prep_w_in(p["w_in"]))
    att = _dilated_attention(qkv, batch)
    ssd = _ssd(z, xbc, dt, p["ssd_conv_w"], p["ssd_conv_b"], p["ssd_dt_bias"], p["ssd_a_log"],
               p["ssd_d"], p["ssd_norm"], batch)
    lru = _lru(g_lru, x_lru, p["lru_conv_w"], p["lru_conv_b"], p["lru_wa"], p["lru_ba"],
               p["lru_wx"], p["lru_bx"], p["lru_lambda"], batch)
    return _outffn(x2, att, ssd, lru, p["w_out"], p["norm_ffn"], p["w_gate"], p["w_up"],
                   p["w_down"], norm_final, final_norm)


def kernel(x, norm_mix, w_in, ssd_conv_w, ssd_conv_b, ssd_dt_bias, ssd_a_log, ssd_d, ssd_norm,
           lru_conv_w, lru_conv_b, lru_wa, lru_ba, lru_wx, lru_bx, lru_lambda, w_out,
           norm_ffn, w_gate, w_up, w_down, norm_final):
    batch, seq, _ = x.shape
    stacked = dict(norm_mix=norm_mix, w_in=w_in, ssd_conv_w=ssd_conv_w, ssd_conv_b=ssd_conv_b,
                   ssd_dt_bias=ssd_dt_bias, ssd_a_log=ssd_a_log, ssd_d=ssd_d, ssd_norm=ssd_norm,
                   lru_conv_w=lru_conv_w, lru_conv_b=lru_conv_b, lru_wa=lru_wa, lru_ba=lru_ba,
                   lru_wx=lru_wx, lru_bx=lru_bx, lru_lambda=lru_lambda, w_out=w_out,
                   norm_ffn=norm_ffn, w_gate=w_gate, w_up=w_up, w_down=w_down)
    depth = w_in.shape[0]
    x2 = x.reshape(batch * seq, D_MODEL)
    for l in range(depth):
        p = {k: v[l] for k, v in stacked.items()}
        x2 = _layer(x2, batch, p, norm_final, l == depth - 1)
    return x2.reshape(batch, seq, D_MODEL)
```

```python
import functools
import math

import jax
import jax.numpy as jnp
from jax import lax
from jax.experimental import pallas as pl
from jax.experimental.pallas import tpu as pltpu

F32 = jnp.float32
BF16 = jnp.bfloat16

D_MODEL = 1024
ATT_HEADS = 8
ATT_HEAD_DIM = 64
ATT_WIDTH = ATT_HEADS * ATT_HEAD_DIM
ATT_BLOCK = 128
ATT_DILATIONS = (1, 4, 16)
SSD_HEADS = 8
SSD_HEAD_DIM = 64
SSD_WIDTH = SSD_HEADS * SSD_HEAD_DIM
SSD_GROUPS = 2
SSD_STATE = 128
SSD_CHUNK = 128
SSD_CONV_CH = SSD_WIDTH + 2 * SSD_GROUPS * SSD_STATE
LRU_WIDTH = 512
LRU_BLOCKS = 8
LRU_BLOCK_W = LRU_WIDTH // LRU_BLOCKS
LRU_C = 8.0
CONV_WIDTH = 4
D_MIX = ATT_WIDTH + SSD_WIDTH + LRU_WIDTH
D_FF = 2816
NORM_EPS = 1e-6
SSD_NORM_EPS = 1e-5

LANES = 128
SUBLANES = 8
VMEM_LIMIT_BYTES = 56 * 1024 * 1024

ROW_TILE = 512
LRU_TILE = 512
SSD_STEP_CHUNKS = 4
DT_PAD = LANES
FF_CHUNKS = ((0, 768), (768, 1536), (1536, 2304), (2304, 2816))
CONV_PAD = SUBLANES

DIL_MID, DIL_MAX = ATT_DILATIONS[1], ATT_DILATIONS[2]
DIL_STEP = DIL_MAX // DIL_MID
ATT_TILE = ATT_BLOCK * DIL_MAX
MID_LEN = ATT_TILE // DIL_MID
HEADS_PER_PAIR = LANES // ATT_HEAD_DIM
ATT_PAIRS = ATT_WIDTH // LANES
ATT_UNROLL = 16
assert ATT_HEADS == 8 and HEADS_PER_PAIR == 2 and DIL_STEP == DIL_MID and ATT_DILATIONS[0] == 1

LOG2E = 1.4426950408889634
NEG_INF = float("-inf")


def _params(n_axes):
    return pltpu.CompilerParams(
        dimension_semantics=("arbitrary",) * n_axes,
        vmem_limit_bytes=VMEM_LIMIT_BYTES)


def _const_spec(shape):
    nd = len(shape)
    return pl.BlockSpec(shape, lambda *_: (0,) * nd, pipeline_mode=pl.Buffered(1))


def _layer_spec(shape, layer):
    nd = len(shape)
    return pl.BlockSpec((None,) + tuple(shape), lambda *_: (layer,) + (0,) * nd,
                        pipeline_mode=pl.Buffered(1))


def _rmsnorm(x, g):
    return x * lax.rsqrt(jnp.mean(x * x, axis=-1, keepdims=True) + NORM_EPS) * g


def _softplus(x):
    return jnp.maximum(x, 0.0) + jnp.log1p(jnp.exp(-jnp.abs(x)))


def _silu(x):
    return x * jax.nn.sigmoid(x)


_C_QKV = 3 * ATT_WIDTH
_C_Z = _C_QKV + SSD_WIDTH
_C_XBC = _C_Z + SSD_CONV_CH
_C_GL = _C_XBC + LRU_WIDTH
_C_XL = _C_GL + LRU_WIDTH
_C_DT = _C_XL + DT_PAD


def _inproj_kernel(x_ref, g_ref, w_ref, qkv_ref, z_ref, xbc_ref, gl_ref, xl_ref, dt_ref):
    h = _rmsnorm(x_ref[...], g_ref[...]).astype(BF16)

    def seg(a, b):
        return jnp.dot(h, w_ref[:, a:b], preferred_element_type=F32)

    qkv_ref[:, 0:ATT_WIDTH] = seg(0, ATT_WIDTH) * (ATT_HEAD_DIM ** -0.5 * LOG2E)
    qkv_ref[:, ATT_WIDTH:_C_QKV] = seg(ATT_WIDTH, _C_QKV)
    z_ref[...] = seg(_C_QKV, _C_Z)
    xbc_ref[...] = seg(_C_Z, _C_XBC)
    gl_ref[...] = seg(_C_XBC, _C_GL)
    xl_ref[...] = seg(_C_GL, _C_XL)
    dt_ref[...] = seg(_C_XL, _C_DT)


def _inproj(x2, g, w, layer):
    t = x2.shape[0]
    row = lambda width: pl.BlockSpec((ROW_TILE, width), lambda i: (i, 0))
    widths = (_C_QKV, SSD_WIDTH, SSD_CONV_CH, LRU_WIDTH, LRU_WIDTH, DT_PAD)
    return pl.pallas_call(
        _inproj_kernel,
        grid=(t // ROW_TILE,),
        in_specs=[row(D_MODEL), _layer_spec((1, D_MODEL), layer),
                  _layer_spec((D_MODEL, _C_DT), layer)],
        out_specs=[row(wd) for wd in widths],
        out_shape=[jax.ShapeDtypeStruct((t, wd), F32) for wd in widths],
        compiler_params=_params(1),
        name="inproj",
    )(x2, g, w)


def _scores(q_lo, q_hi, kk, bias):
    q_both = jnp.concatenate([q_lo, q_hi], axis=0)
    return lax.dot_general(q_both, kk, (((1,), (1,)), ((), ())),
                           preferred_element_type=F32) + bias


def _attn_kernel(q_ref, k_ref, v_ref, att_ref,
                 qn_s, kn_s, vn_s, tmp_s, qm_s, km_s, vm_s, qx_s, kx_s, vx_s,
                 mn_s, mm_s, mx_s, acc_s, bias_s, sc_s):
    pair = pl.program_id(1)
    tile = pl.program_id(2)
    blk = ATT_BLOCK
    first_tile = tile == 0
    mid_blocks = MID_LEN // blk

    @pl.when(first_tile)
    def _start_sequence():
        qi = lax.broadcasted_iota(jnp.int32, (blk, 2 * blk), 0)
        ki = lax.broadcasted_iota(jnp.int32, (blk, 2 * blk), 1)
        dist = blk + qi - ki
        band = (dist >= 0) & (dist <= blk)
        band_first = band & (ki >= blk)
        for hh in range(HEADS_PER_PAIR):
            expo = (126 - HEADS_PER_PAIR * pair - hh) << 23
            slope = lax.bitcast_convert_type(jnp.full((blk, 2 * blk), expo, jnp.int32), F32)
            rows = slice(hh * blk, (hh + 1) * blk)
            for br, dil in enumerate((DIL_MAX, DIL_MID, 1)):
                alibi = (-slope * (dil * dist).astype(F32)) * LOG2E
                bias_s[2 * br, rows, :] = jnp.where(band, alibi, NEG_INF)
                bias_s[2 * br + 1, rows, :] = jnp.where(band_first, alibi, NEG_INF)
        kn_s[0:blk, :] = jnp.zeros((blk, LANES), BF16)
        vn_s[:, 0:blk, :] = jnp.zeros((2, blk, LANES), BF16)
        km_s[:, 0:blk, :] = jnp.zeros((DIL_MID, blk, LANES), BF16)
        vm_s[:, :, 0:blk, :] = jnp.zeros((2, DIL_MID, blk, LANES), BF16)
        kx_s[:, 0:blk, :] = jnp.zeros((DIL_MAX, blk, LANES), BF16)
        vx_s[:, :, 0:blk, :] = jnp.zeros((2, DIL_MAX, blk, LANES), BF16)

    @pl.when(tile > 0)
    def _carry_history():
        kn_s[0:blk, :] = kn_s[ATT_TILE:ATT_TILE + blk, :]
        vn_s[:, 0:blk, :] = vn_s[:, ATT_TILE:ATT_TILE + blk, :]
        km_s[:, 0:blk, :] = km_s[:, MID_LEN:MID_LEN + blk, :]
        vm_s[:, :, 0:blk, :] = vm_s[:, :, MID_LEN:MID_LEN + blk, :]
        kx_s[:, 0:blk, :] = kx_s[:, blk:2 * blk, :]
        vx_s[:, :, 0:blk, :] = vx_s[:, :, blk:2 * blk, :]

    lane_m = lax.broadcasted_iota(jnp.int32, (MID_LEN, LANES), 1) < ATT_HEAD_DIM
    lane_b = lax.broadcasted_iota(jnp.int32, (blk, LANES), 1) < ATT_HEAD_DIM

    def put_q(dst, idx, x, mask):
        dst[(0,) + idx] = jnp.where(mask, x, 0.0).astype(BF16)
        dst[(1,) + idx] = jnp.where(mask, 0.0, x).astype(BF16)

    def put_k(dst, idx, x, mask):
        del mask
        dst[idx] = x.astype(BF16)

    def put_v(dst, idx, x, mask):
        dst[(0,) + idx] = jnp.where(mask, x, 1.0).astype(BF16)
        dst[(1,) + idx] = jnp.where(mask, 1.0, x).astype(BF16)

    for src, nat, mid, big, put, hist in (
            (q_ref, qn_s, qm_s, qx_s, put_q, 0),
            (k_ref, kn_s, km_s, kx_s, put_k, blk),
            (v_ref, vn_s, vm_s, vx_s, put_v, blk)):
        for c in range(DIL_MID):
            put(nat, (slice(hist + c * MID_LEN, hist + (c + 1) * MID_LEN), slice(None)),
                src[c * MID_LEN:(c + 1) * MID_LEN, :], lane_m)
        for r in range(DIL_MID):
            x = src[pl.ds(r, MID_LEN, stride=DIL_MID), :]
            tmp_s[r] = x
            put(mid, (r, slice(hist, hist + MID_LEN), slice(None)), x, lane_m)
        for r in range(DIL_MID):
            for c in range(DIL_STEP):
                y = tmp_s.at[r][pl.ds(c, blk, stride=DIL_STEP), :]
                put(big, (r + DIL_MID * c, slice(hist, hist + blk), slice(None)), y, lane_b)

    def bias_of(branch, is_first):
        return bias_s[2 * branch + jnp.where(is_first, 1, 0)]

    def big_ops(r):
        return (qx_s[0, r], qx_s[1, r], kx_s[r], bias_of(0, first_tile),
                pl.ds(r, blk, stride=DIL_MAX))

    def mid_ops(i):
        r = i // mid_blocks
        j = i % mid_blocks
        q0 = pl.multiple_of(j * blk, blk)
        return (qm_s[0, r, pl.ds(q0, blk), :], qm_s[1, r, pl.ds(q0, blk), :],
                km_s[r, pl.ds(q0, 2 * blk), :], bias_of(1, first_tile & (j == 0)),
                pl.ds(j * (blk * DIL_MID) + r, blk, stride=DIL_MID))

    def nat_ops(j):
        q0 = pl.multiple_of(j * blk, blk)
        return (qn_s[0, pl.ds(q0, blk), :], qn_s[1, pl.ds(q0, blk), :],
                kn_s[pl.ds(q0, 2 * blk), :], bias_of(2, first_tile & (j == 0)),
                pl.ds(q0, blk))

    def loop(n, body):
        lax.fori_loop(0, n, lambda i, c: (body(i), c)[1], 0, unroll=ATT_UNROLL)

    n_big, n_mid, n_nat = DIL_MAX, DIL_MID * mid_blocks, ATT_TILE // blk

    def row_max(ops, slot):
        q_lo, q_hi, kk, bias, rows = ops
        s = _scores(q_lo, q_hi, kk, bias)
        sc_s[slot] = s
        m = jnp.max(s, axis=-1, keepdims=True)
        return jnp.broadcast_to(m, (2 * blk, LANES)), rows

    def max_big(r):
        m, rows = row_max(big_ops(r), r)
        mn_s[0, rows, :] = m[:blk]
        mn_s[1, rows, :] = m[blk:]

    def max_more(ops, slot):
        m, rows = row_max(ops, slot)
        mn_s[0, rows, :] = jnp.maximum(mn_s[0, rows, :], m[:blk])
        mn_s[1, rows, :] = jnp.maximum(mn_s[1, rows, :], m[blk:])

    loop(n_big, max_big)
    loop(n_mid, lambda i: max_more(mid_ops(i), n_big + i))
    loop(n_nat, lambda j: max_more(nat_ops(j), n_big + n_mid + j))

    for hh in range(HEADS_PER_PAIR):
        for r in range(DIL_MID):
            mm_s[hh, r] = mn_s.at[hh][pl.ds(r, MID_LEN, stride=DIL_MID), :]
        for r in range(DIL_MID):
            for c in range(DIL_STEP):
                mx_s[hh, r + DIL_MID * c] = mm_s.at[hh, r][pl.ds(c, blk, stride=DIL_STEP), :]

    def num_den(slot, m_lo, m_hi, v_lo, v_hi):
        m = jnp.concatenate([m_lo, m_hi], axis=0)
        e = jnp.exp2(sc_s[slot] - jnp.concatenate([m, m], axis=1)).astype(BF16)
        return (jnp.dot(e[:blk], v_lo, preferred_element_type=F32),
                jnp.dot(e[blk:], v_hi, preferred_element_type=F32))

    def acc_big(r):
        o_lo, o_hi = num_den(r, mx_s[0, r], mx_s[1, r], vx_s[0, r], vx_s[1, r])
        rows = pl.ds(r, blk, stride=DIL_MAX)
        acc_s[0, rows, :] = o_lo
        acc_s[1, rows, :] = o_hi

    def acc_mid(i):
        r = i // mid_blocks
        j = i % mid_blocks
        q0 = pl.multiple_of(j * blk, blk)
        o_lo, o_hi = num_den(
            n_big + i, mm_s[0, r, pl.ds(q0, blk), :], mm_s[1, r, pl.ds(q0, blk), :],
            vm_s[0, r, pl.ds(q0, 2 * blk), :], vm_s[1, r, pl.ds(q0, 2 * blk), :])
        rows = pl.ds(j * (blk * DIL_MID) + r, blk, stride=DIL_MID)
        acc_s[0, rows, :] += o_lo
        acc_s[1, rows, :] += o_hi

    def acc_nat(j):
        q0 = pl.multiple_of(j * blk, blk)
        rows = pl.ds(q0, blk)
        o_lo, o_hi = num_den(
            n_big + n_mid + j, mn_s[0, rows, :], mn_s[1, rows, :],
            vn_s[0, pl.ds(q0, 2 * blk), :], vn_s[1, pl.ds(q0, 2 * blk), :])
        a_lo = acc_s[0, rows, :] + o_lo
        a_hi = acc_s[1, rows, :] + o_hi
        num = jnp.where(lane_b, a_lo, a_hi)
        den = pltpu.roll(jnp.where(lane_b, a_hi, a_lo), ATT_HEAD_DIM, 1)
        att_ref[rows, :] = (num / den).astype(BF16)

    loop(n_big, acc_big)
    loop(n_mid, acc_mid)
    loop(n_nat, acc_nat)


def _dilated_attention(qkv, batch):
    t = qkv.shape[0]
    tiles = t // batch // ATT_TILE
    blk = ATT_BLOCK

    def spec(col0):
        return pl.BlockSpec((ATT_TILE, LANES), lambda b, p, i: (b * tiles + i, col0 + p))

    bf = lambda *shape: pltpu.VMEM(shape, BF16)
    f32 = lambda *shape: pltpu.VMEM(shape, F32)
    return pl.pallas_call(
        _attn_kernel,
        grid=(batch, ATT_PAIRS, tiles),
        in_specs=[spec(0), spec(ATT_PAIRS), spec(2 * ATT_PAIRS)],
        out_specs=spec(0),
        out_shape=jax.ShapeDtypeStruct((t, ATT_WIDTH), BF16),
        scratch_shapes=[
            bf(2, ATT_TILE, LANES), bf(blk + ATT_TILE, LANES), bf(2, blk + ATT_TILE, LANES),
            f32(DIL_MID, MID_LEN, LANES),
            bf(2, DIL_MID, MID_LEN, LANES), bf(DIL_MID, blk + MID_LEN, LANES),
            bf(2, DIL_MID, blk + MID_LEN, LANES),
            bf(2, DIL_MAX, blk, LANES), bf(DIL_MAX, 2 * blk, LANES),
            bf(2, DIL_MAX, 2 * blk, LANES),
            f32(2, ATT_TILE, LANES), f32(2, DIL_MID, MID_LEN, LANES),
            f32(2, DIL_MAX, blk, LANES), f32(2, ATT_TILE, LANES),
            f32(3 * 2, 2 * blk, 2 * blk),
            f32(3 * DIL_MAX, 2 * blk, 2 * blk),
        ],
        compiler_params=_params(3),
        name="attn",
    )(qkv, qkv, qkv)


def _split3(x):
    hi = x.astype(BF16)
    r1 = x - hi.astype(F32)
    mid = r1.astype(BF16)
    lo = (r1 - mid.astype(F32)).astype(BF16)
    return hi, mid, lo


def _dot_exact_rhs(x, w):
    return sum(jnp.dot(part, w, preferred_element_type=F32) for part in _split3(x))


def _ssd_kernel(z_ref, xbc_ref, dt_ref, cw_ref, cb_ref, dtb_ref, alog_ref, dsk_ref, nw_ref,
                triu_ref, exp_ref, y_ref, win_s, state_s):
    c = pl.program_id(1)
    q = SSD_CHUNK
    pad = CONV_PAD
    rows_step = SSD_STEP_CHUNKS * q
    heads_per_group = SSD_HEADS // SSD_GROUPS
    gw = heads_per_group * SSD_HEAD_DIM

    @pl.when(c == 0)
    def _reset():
        win_s[0:pad, :] = jnp.zeros((pad, SSD_CONV_CH), F32)
        state_s[...] = jnp.zeros_like(state_s)

    win_s[pad:, :] = xbc_ref[...]

    ti = lax.broadcasted_iota(jnp.int32, (q, q), 0)
    tj = lax.broadcasted_iota(jnp.int32, (q, q), 1)
    causal = ti >= tj
    low_half = lax.broadcasted_iota(jnp.int32, (q, LANES), 1) < SSD_HEAD_DIM
    a_t = -jnp.exp(alog_ref[...])
    k_w = CONV_WIDTH

    for ci in range(SSD_STEP_CHUNKS):
        r0 = ci * q
        conv = cb_ref[...] + win_s[pad + r0:pad + r0 + q, :] * cw_ref[k_w - 1:k_w, :]
        for k in range(k_w - 1):
            off = pad + r0 - (k_w - 1) + k
            conv = conv + win_s[off:off + q, :] * cw_ref[k:k + 1, :]
        xbc = _silu(conv)
        xs = xbc[:, :SSD_WIDTH]
        bm = xbc[:, SSD_WIDTH:SSD_WIDTH + SSD_GROUPS * SSD_STATE]
        cm = xbc[:, SSD_WIDTH + SSD_GROUPS * SSD_STATE:]

        dt_t = _softplus(dt_ref[r0:r0 + q, :].T[0:SSD_HEADS, :] + dtb_ref[...])
        acs_t = _dot_exact_rhs(dt_t * a_t, triu_ref[...])
        e_t = jnp.exp(acs_t)
        w_t = jnp.exp(acs_t[:, q - 1:q] - acs_t) * dt_t
        chunk_decay = jnp.sum(e_t[:, q - 1:q] * exp_ref[...], axis=0, keepdims=True)
        cols_t = jnp.concatenate(
            [acs_t, e_t, jnp.zeros((q - 2 * SSD_HEADS, q), F32)], axis=0).T

        y_parts = []
        for g in range(SSD_GROUPS):
            bm_g = bm[:, g * SSD_STATE:(g + 1) * SSD_STATE]
            cm_g = cm[:, g * SSD_STATE:(g + 1) * SSD_STATE]
            gmat = lax.dot_general(cm_g.astype(BF16), bm_g.astype(BF16),
                                   (((1,), (1,)), ((), ())), preferred_element_type=F32)
            bm_gt = bm_g.T
            for pp in range(heads_per_group // HEADS_PER_PAIR):
                p = g * (heads_per_group // HEADS_PER_PAIR) + pp
                cols = slice(p * LANES, (p + 1) * LANES)
                x_p = xs[:, cols]
                s_p = state_s[:, cols]
                lhs_y, lhs_s, rhs_y, rhs_s = [], [], [], []
                for hh in range(HEADS_PER_PAIR):
                    h = HEADS_PER_PAIR * p + hh
                    keep = low_half if hh == 0 else ~low_half
                    seg = cols_t[:, h:h + 1] - acs_t[h:h + 1, :]
                    lmat = jnp.exp(jnp.where(causal, seg, NEG_INF))
                    scores = gmat * lmat * dt_t[h:h + 1, :]
                    c_dec = cm_g * cols_t[:, SSD_HEADS + h:SSD_HEADS + h + 1]
                    x_h = jnp.where(keep, x_p, 0.0).astype(BF16)
                    s_h = jnp.where(keep, s_p, 0.0).astype(BF16)
                    lhs_y += [scores.astype(BF16), c_dec.astype(BF16)]
                    rhs_y += [x_h, s_h]
                    lhs_s.append((bm_gt * w_t[h:h + 1, :]).astype(BF16))
                    rhs_s.append(x_h)
                y_parts.append(jnp.dot(jnp.concatenate(lhs_y, axis=1),
                                       jnp.concatenate(rhs_y, axis=0),
                                       preferred_element_type=F32))
                state_s[:, cols] = s_p * chunk_decay[:, cols] + jnp.dot(
                    jnp.concatenate(lhs_s, axis=1), jnp.concatenate(rhs_s, axis=0),
                    preferred_element_type=F32)
        y = jnp.concatenate(y_parts, axis=-1) + dsk_ref[...] * xs
        y = y * _silu(z_ref[r0:r0 + q, :])
        outs = []
        for g in range(SSD_GROUPS):
            yg = y[:, g * gw:(g + 1) * gw]
            outs.append(yg * lax.rsqrt(jnp.mean(yg * yg, axis=-1, keepdims=True) + SSD_NORM_EPS))
        y_ref[r0:r0 + q, :] = (jnp.concatenate(outs, axis=-1) * nw_ref[...]).astype(BF16)

    win_s[0:pad, :] = win_s[rows_step:rows_step + pad, :]


def _prep_ssd(conv_w, conv_b, dt_bias, a_log, d_skip, norm_w):
    depth = conv_w.shape[0]
    per_time = lambda v: jnp.broadcast_to(v.astype(F32)[:, :, None],
                                          (depth, SSD_HEADS, SSD_CHUNK))
    triu = jnp.triu(jnp.ones((SSD_CHUNK, SSD_CHUNK), BF16))
    expand = (jnp.arange(SSD_HEADS)[:, None] == (jnp.arange(SSD_WIDTH)[None, :] // SSD_HEAD_DIM)
              ).astype(F32)
    d_exp = jnp.repeat(d_skip.astype(F32), SSD_HEAD_DIM, axis=1).reshape(depth, 1, SSD_WIDTH)
    return (conv_w, conv_b.reshape(depth, 1, SSD_CONV_CH), per_time(dt_bias), per_time(a_log),
            d_exp, norm_w.reshape(depth, 1, SSD_WIDTH), triu, expand)


def _ssd(z, xbc, dt, prepared, layer, batch):
    t = z.shape[0]
    rows_step = SSD_STEP_CHUNKS * SSD_CHUNK
    ns = t // batch // rows_step
    row = lambda width: pl.BlockSpec((rows_step, width), lambda b, c: (b * ns + c, 0))
    lspec = lambda *shape: _layer_spec(shape, layer)
    return pl.pallas_call(
        _ssd_kernel,
        grid=(batch, ns),
        in_specs=[row(SSD_WIDTH), row(SSD_CONV_CH), row(DT_PAD),
                  lspec(CONV_WIDTH, SSD_CONV_CH), lspec(1, SSD_CONV_CH),
                  lspec(SSD_HEADS, SSD_CHUNK), lspec(SSD_HEADS, SSD_CHUNK),
                  lspec(1, SSD_WIDTH), lspec(1, SSD_WIDTH),
                  _const_spec((SSD_CHUNK, SSD_CHUNK)), _const_spec((SSD_HEADS, SSD_WIDTH))],
        out_specs=row(SSD_WIDTH),
        out_shape=jax.ShapeDtypeStruct((t, SSD_WIDTH), BF16),
        scratch_shapes=[pltpu.VMEM((CONV_PAD + rows_step, SSD_CONV_CH), F32),
                        pltpu.VMEM((SSD_STATE, SSD_WIDTH), F32)],
        compiler_params=_params(2),
        name="ssd",
    )(z, xbc, dt, *prepared)


def _gelu_tanh(x):
    c = math.sqrt(2.0 / math.pi)
    return 0.5 * x * (1.0 + jnp.tanh(c * (x + 0.044715 * (x * x * x))))


def _lru_kernel(g_ref, x_ref, cw_ref, cb_ref, wa_ref, ba_ref, wx_ref, bx_ref, lam_ref,
                y_ref, win_s, h_s):
    c = pl.program_id(1)
    tt = LRU_TILE
    pad = CONV_PAD

    @pl.when(c == 0)
    def _reset():
        win_s[0:pad, :] = jnp.zeros((pad, LRU_WIDTH), F32)
        h_s[...] = jnp.zeros_like(h_s)

    win_s[pad:, :] = x_ref[...]
    k_w = CONV_WIDTH
    xc = cb_ref[...] + win_s[pad:pad + tt, :] * cw_ref[k_w - 1:k_w, :]
    for k in range(k_w - 1):
        off = pad - (k_w - 1) + k
        xc = xc + win_s[off:off + tt, :] * cw_ref[k:k + 1, :]
    win_s[0:pad, :] = win_s[tt:tt + pad, :]

    xb = xc.astype(BF16)
    r = jax.nn.sigmoid(jnp.dot(xb, wa_ref[...], preferred_element_type=F32) + ba_ref[...])
    i = jax.nn.sigmoid(jnp.dot(xb, wx_ref[...], preferred_element_type=F32) + bx_ref[...])
    log_a = -LRU_C * r * _softplus(-lam_ref[...])
    a = jnp.exp(log_a)
    u = jnp.sqrt(-jnp.tanh(log_a) * (a * a + 1.0)) * (i * xc)

    groups = tt // SUBLANES
    a = a.reshape(groups, SUBLANES, LRU_WIDTH)
    u = u.reshape(groups, SUBLANES, LRU_WIDTH)
    sub = lax.broadcasted_iota(jnp.int32, (groups, SUBLANES, LRU_WIDTH), 1)
    step = 1
    while step < SUBLANES:
        keep = sub >= step
        a_sh = jnp.where(keep, pltpu.roll(a, step, 1), 1.0)
        u_sh = jnp.where(keep, pltpu.roll(u, step, 1), 0.0)
        u = a * u_sh + u
        a = a * a_sh
        step *= 2
    carry = h_s[0:1, :]
    hs = []
    for gi in range(groups):
        h_g = a[gi] * carry + u[gi]
        hs.append(h_g)
        carry = h_g[SUBLANES - 1:SUBLANES, :]
    h_s[...] = jnp.broadcast_to(carry, h_s.shape)
    h = jnp.concatenate(hs, axis=0)
    y_ref[...] = (h * _gelu_tanh(g_ref[...])).astype(BF16)


def _block_diag(w):
    depth, nb, c, d = w.shape
    eye = jnp.eye(nb, dtype=w.dtype)
    return (eye[None, :, None, :, None] * w[:, :, :, None, :]).reshape(depth, nb * c, nb * d)


def _prep_lru(conv_w, conv_b, wa, ba, wx, bx, lam):
    depth = conv_w.shape[0]
    vec = lambda v: v.astype(F32).reshape(depth, 1, LRU_WIDTH)
    return (conv_w, vec(conv_b), _block_diag(wa).astype(BF16), vec(ba),
            _block_diag(wx).astype(BF16), vec(bx), vec(lam))


def _lru(g_in, x_in, prepared, layer, batch):
    t = g_in.shape[0]
    s = t // batch
    nt = s // LRU_TILE
    row = pl.BlockSpec((LRU_TILE, LRU_WIDTH), lambda b, c: (b * nt + c, 0))
    mat = _layer_spec((LRU_WIDTH, LRU_WIDTH), layer)
    one = _layer_spec((1, LRU_WIDTH), layer)
    return pl.pallas_call(
        _lru_kernel,
        grid=(batch, nt),
        in_specs=[row, row, _layer_spec((CONV_WIDTH, LRU_WIDTH), layer), one, mat, one, mat,
                  one, one],
        out_specs=row,
        out_shape=jax.ShapeDtypeStruct((t, LRU_WIDTH), BF16),
        scratch_shapes=[pltpu.VMEM((CONV_PAD + LRU_TILE, LRU_WIDTH), F32),
                        pltpu.VMEM((SUBLANES, LRU_WIDTH), F32)],
        compiler_params=_params(2),
        name="rglru",
    )(g_in, x_in, *prepared)


def _outffn_kernel(x_ref, att_ref, ssd_ref, lru_ref, wo_ref, gf_ref, wg_ref, wu_ref, wd_ref,
                   nfin_ref, o_ref, hn_s, *, final_norm):
    x1 = x_ref[...]
    for j, m_ref in enumerate((att_ref, ssd_ref, lru_ref)):
        x1 = x1 + jnp.dot(m_ref[...], wo_ref[j * ATT_WIDTH:(j + 1) * ATT_WIDTH, :],
                          preferred_element_type=F32)
    hn_s[...] = _rmsnorm(x1, gf_ref[...]).astype(BF16)
    o_ref[...] = x1
    for c0, c1 in FF_CHUNKS:
        hn = hn_s[...]
        gate = jnp.dot(hn, wg_ref[:, c0:c1], preferred_element_type=F32)
        up = jnp.dot(hn, wu_ref[:, c0:c1], preferred_element_type=F32)
        act = (_silu(gate) * up).astype(BF16)
        o_ref[...] += jnp.dot(act, wd_ref[c0:c1, :], preferred_element_type=F32)
    if final_norm:
        o_ref[...] = _rmsnorm(o_ref[...], nfin_ref[...])


def _prep_ffn(w_out, norm_ffn, w_gate, w_up, w_down, norm_final):
    depth = w_out.shape[0]
    return (w_out.astype(BF16), norm_ffn.reshape(depth, 1, D_MODEL), w_gate.astype(BF16),
            w_up.astype(BF16), w_down.astype(BF16), norm_final.reshape(1, D_MODEL))


def _outffn(x2, att, ssd, lru, prepared, layer, final_norm):
    t = x2.shape[0]
    row = lambda width: pl.BlockSpec((ROW_TILE, width), lambda i: (i, 0))
    lspec = lambda *shape: _layer_spec(shape, layer)
    return pl.pallas_call(
        functools.partial(_outffn_kernel, final_norm=final_norm),
        grid=(t // ROW_TILE,),
        in_specs=[row(D_MODEL), row(ATT_WIDTH), row(SSD_WIDTH), row(LRU_WIDTH),
                  lspec(D_MIX, D_MODEL), lspec(1, D_MODEL), lspec(D_MODEL, D_FF),
                  lspec(D_MODEL, D_FF), lspec(D_FF, D_MODEL), _const_spec((1, D_MODEL))],
        out_specs=row(D_MODEL),
        out_shape=jax.ShapeDtypeStruct((t, D_MODEL), F32),
        scratch_shapes=[pltpu.VMEM((ROW_TILE, D_MODEL), BF16)],
        compiler_params=_params(1),
        name="outffn",
    )(x2, att, ssd, lru, *prepared)


def _prep_w_in(w):
    o_dt = 3 * ATT_WIDTH + SSD_WIDTH + SSD_CONV_CH
    o_gl = o_dt + SSD_HEADS
    dt_cols = jnp.pad(w[:, :, o_dt:o_gl], ((0, 0), (0, 0), (0, DT_PAD - SSD_HEADS)))
    return jnp.concatenate([w[:, :, :o_dt], w[:, :, o_gl:], dt_cols], axis=2).astype(BF16)


def kernel(x, norm_mix, w_in, ssd_conv_w, ssd_conv_b, ssd_dt_bias, ssd_a_log, ssd_d, ssd_norm,
           lru_conv_w, lru_conv_b, lru_wa, lru_ba, lru_wx, lru_bx, lru_lambda, w_out,
           norm_ffn, w_gate, w_up, w_down, norm_final):
    batch, seq, _ = x.shape
    depth = w_in.shape[0]
    g_mix = norm_mix.reshape(depth, 1, D_MODEL)
    w_proj = _prep_w_in(w_in)
    ssd_p = _prep_ssd(ssd_conv_w, ssd_conv_b, ssd_dt_bias, ssd_a_log, ssd_d, ssd_norm)
    lru_p = _prep_lru(lru_conv_w, lru_conv_b, lru_wa, lru_ba, lru_wx, lru_bx, lru_lambda)
    ffn_p = _prep_ffn(w_out, norm_ffn, w_gate, w_up, w_down, norm_final)
    x2 = x.reshape(batch * seq, D_MODEL)
    for l in range(depth):
        qkv, z, xbc, g_lru, x_lru, dt = _inproj(x2, g_mix, w_proj, l)
        att = _dilated_attention(qkv, batch)
        ssd = _ssd(z, xbc, dt, ssd_p, l, batch)
        lru = _lru(g_lru, x_lru, lru_p, l, batch)
        x2 = _outffn(x2, att, ssd, lru, ffn_p, l, l == depth - 1)
    return x2.reshape(batch, seq, D_MODEL)
```

```python
import functools
import math

import jax
import jax.numpy as jnp
from jax import lax
from jax.experimental import pallas as pl
from jax.experimental.pallas import tpu as pltpu

F32 = jnp.float32
BF16 = jnp.bfloat16

D_MODEL = 1024
ATT_HEADS = 8
ATT_HEAD_DIM = 64
ATT_WIDTH = ATT_HEADS * ATT_HEAD_DIM
ATT_BLOCK = 128
ATT_DILATIONS = (1, 4, 16)
SSD_HEADS = 8
SSD_HEAD_DIM = 64
SSD_WIDTH = SSD_HEADS * SSD_HEAD_DIM
SSD_GROUPS = 2
SSD_STATE = 128
SSD_CHUNK = 128
SSD_CONV_CH = SSD_WIDTH + 2 * SSD_GROUPS * SSD_STATE
LRU_WIDTH = 512
LRU_BLOCKS = 8
LRU_BLOCK_W = LRU_WIDTH // LRU_BLOCKS
LRU_C = 8.0
CONV_WIDTH = 4
D_MIX = ATT_WIDTH + SSD_WIDTH + LRU_WIDTH
D_FF = 2816
NORM_EPS = 1e-6
SSD_NORM_EPS = 1e-5

LANES = 128
SUBLANES = 8
VMEM_LIMIT_BYTES = 56 * 1024 * 1024

ROW_TILE = 512
LRU_TILE = 512
SSD_STEP_CHUNKS = 4
DT_PAD = LANES
FF_CHUNKS = ((0, 768), (768, 1536), (1536, 2304), (2304, 2816))
CONV_PAD = SUBLANES

DIL_MID, DIL_MAX = ATT_DILATIONS[1], ATT_DILATIONS[2]
DIL_STEP = DIL_MAX // DIL_MID
ATT_TILE = ATT_BLOCK * DIL_MAX
MID_LEN = ATT_TILE // DIL_MID
HEADS_PER_PAIR = LANES // ATT_HEAD_DIM
ATT_PAIRS = ATT_WIDTH // LANES
ATT_UNROLL = 16
assert ATT_HEADS == 8 and HEADS_PER_PAIR == 2 and DIL_STEP == DIL_MID and ATT_DILATIONS[0] == 1

LOG2E = 1.4426950408889634
NEG_INF = float("-inf")


def _params(n_axes):
    return pltpu.CompilerParams(
        dimension_semantics=("arbitrary",) * n_axes,
        vmem_limit_bytes=VMEM_LIMIT_BYTES)


def _const_spec(shape):
    nd = len(shape)
    return pl.BlockSpec(shape, lambda *_: (0,) * nd, pipeline_mode=pl.Buffered(1))


def _layer_spec(shape, layer):
    nd = len(shape)
    return pl.BlockSpec((None,) + tuple(shape), lambda *_: (layer,) + (0,) * nd,
                        pipeline_mode=pl.Buffered(1))


def _rmsnorm(x, g):
    return x * lax.rsqrt(jnp.mean(x * x, axis=-1, keepdims=True) + NORM_EPS) * g


def _softplus(x):
    return jnp.maximum(x, 0.0) + jnp.log1p(jnp.exp(-jnp.abs(x)))


def _silu(x):
    return x * jax.nn.sigmoid(x)


_C_QKV = 3 * ATT_WIDTH
_C_Z = _C_QKV + SSD_WIDTH
_C_XBC = _C_Z + SSD_CONV_CH
_C_GL = _C_XBC + LRU_WIDTH
_C_XL = _C_GL + LRU_WIDTH
_C_DT = _C_XL + DT_PAD


def _inproj_kernel(x_ref, g_ref, w_ref, qkv_ref, z_ref, xbc_ref, gl_ref, xl_ref, dt_ref):
    h = _rmsnorm(x_ref[...], g_ref[...]).astype(BF16)

    def seg(a, b):
        return jnp.dot(h, w_ref[:, a:b], preferred_element_type=F32)

    qkv_ref[:, 0:ATT_WIDTH] = seg(0, ATT_WIDTH) * (ATT_HEAD_DIM ** -0.5 * LOG2E)
    qkv_ref[:, ATT_WIDTH:_C_QKV] = seg(ATT_WIDTH, _C_QKV)
    z_ref[...] = seg(_C_QKV, _C_Z)
    xbc_ref[...] = seg(_C_Z, _C_XBC)
    gl_ref[...] = seg(_C_XBC, _C_GL)
    xl_ref[...] = seg(_C_GL, _C_XL)
    dt_ref[...] = seg(_C_XL, _C_DT)


def _inproj(x2, g, w, layer):
    t = x2.shape[0]
    row = lambda width: pl.BlockSpec((ROW_TILE, width), lambda i: (i, 0))
    widths = (_C_QKV, SSD_WIDTH, SSD_CONV_CH, LRU_WIDTH, LRU_WIDTH, DT_PAD)
    return pl.pallas_call(
        _inproj_kernel,
        grid=(t // ROW_TILE,),
        in_specs=[row(D_MODEL), _layer_spec((1, D_MODEL), layer),
                  _layer_spec((D_MODEL, _C_DT), layer)],
        out_specs=[row(wd) for wd in widths],
        out_shape=[jax.ShapeDtypeStruct((t, wd), F32) for wd in widths],
        compiler_params=_params(1),
        name="inproj",
    )(x2, g, w)


def _scores(q_lo, q_hi, kk, bias):
    q_both = jnp.concatenate([q_lo, q_hi], axis=0)
    return lax.dot_general(q_both, kk, (((1,), (1,)), ((), ())),
                           preferred_element_type=F32) + bias


def _attn_kernel(q_ref, k_ref, v_ref, att_ref,
                 qn_s, kn_s, vn_s, tmp_s, qm_s, km_s, vm_s, qx_s, kx_s, vx_s,
                 mn_s, mm_s, mx_s, acc_s, bias_s, sc_s):
    pair = pl.program_id(1)
    tile = pl.program_id(2)
    blk = ATT_BLOCK
    first_tile = tile == 0
    mid_blocks = MID_LEN // blk

    @pl.when(first_tile)
    def _start_sequence():
        qi = lax.broadcasted_iota(jnp.int32, (blk, 2 * blk), 0)
        ki = lax.broadcasted_iota(jnp.int32, (blk, 2 * blk), 1)
        dist = blk + qi - ki
        band = (dist >= 0) & (dist <= blk)
        band_first = band & (ki >= blk)
        for hh in range(HEADS_PER_PAIR):
            expo = (126 - HEADS_PER_PAIR * pair - hh) << 23
            slope = lax.bitcast_convert_type(jnp.full((blk, 2 * blk), expo, jnp.int32), F32)
            rows = slice(hh * blk, (hh + 1) * blk)
            for br, dil in enumerate((DIL_MAX, DIL_MID, 1)):
                alibi = (-slope * (dil * dist).astype(F32)) * LOG2E
                bias_s[2 * br, rows, :] = jnp.where(band, alibi, NEG_INF)
                bias_s[2 * br + 1, rows, :] = jnp.where(band_first, alibi, NEG_INF)
        kn_s[0:blk, :] = jnp.zeros((blk, LANES), BF16)
        vn_s[:, 0:blk, :] = jnp.zeros((2, blk, LANES), BF16)
        km_s[:, 0:blk, :] = jnp.zeros((DIL_MID, blk, LANES), BF16)
        vm_s[:, :, 0:blk, :] = jnp.zeros((2, DIL_MID, blk, LANES), BF16)
        kx_s[:, 0:blk, :] = jnp.zeros((DIL_MAX, blk, LANES), BF16)
        vx_s[:, :, 0:blk, :] = jnp.zeros((2, DIL_MAX, blk, LANES), BF16)

    @pl.when(tile > 0)
    def _carry_history():
        kn_s[0:blk, :] = kn_s[ATT_TILE:ATT_TILE + blk, :]
        vn_s[:, 0:blk, :] = vn_s[:, ATT_TILE:ATT_TILE + blk, :]
        km_s[:, 0:blk, :] = km_s[:, MID_LEN:MID_LEN + blk, :]
        vm_s[:, :, 0:blk, :] = vm_s[:, :, MID_LEN:MID_LEN + blk, :]
        kx_s[:, 0:blk, :] = kx_s[:, blk:2 * blk, :]
        vx_s[:, :, 0:blk, :] = vx_s[:, :, blk:2 * blk, :]

    lane_m = lax.broadcasted_iota(jnp.int32, (MID_LEN, LANES), 1) < ATT_HEAD_DIM
    lane_b = lax.broadcasted_iota(jnp.int32, (blk, LANES), 1) < ATT_HEAD_DIM

    def put_q(dst, idx, x, mask):
        dst[(0,) + idx] = jnp.where(mask, x, 0.0).astype(BF16)
        dst[(1,) + idx] = jnp.where(mask, 0.0, x).astype(BF16)

    def put_k(dst, idx, x, mask):
        del mask
        dst[idx] = x.astype(BF16)

    def put_v(dst, idx, x, mask):
        dst[(0,) + idx] = jnp.where(mask, x, 1.0).astype(BF16)
        dst[(1,) + idx] = jnp.where(mask, 1.0, x).astype(BF16)

    for src, nat, mid, big, put, hist in (
            (q_ref, qn_s, qm_s, qx_s, put_q, 0),
            (k_ref, kn_s, km_s, kx_s, put_k, blk),
            (v_ref, vn_s, vm_s, vx_s, put_v, blk)):
        for c in range(DIL_MID):
            put(nat, (slice(hist + c * MID_LEN, hist + (c + 1) * MID_LEN), slice(None)),
                src[c * MID_LEN:(c + 1) * MID_LEN, :], lane_m)
        for r in range(DIL_MID):
            x = src[pl.ds(r, MID_LEN, stride=DIL_MID), :]
            tmp_s[r] = x
            put(mid, (r, slice(hist, hist + MID_LEN), slice(None)), x, lane_m)
        for r in range(DIL_MID):
            for c in range(DIL_STEP):
                y = tmp_s.at[r][pl.ds(c, blk, stride=DIL_STEP), :]
                put(big, (r + DIL_MID * c, slice(hist, hist + blk), slice(None)), y, lane_b)

    def bias_of(branch, is_first):
        return bias_s[2 * branch + jnp.where(is_first, 1, 0)]

    def big_ops(r):
        return (qx_s[0, r], qx_s[1, r], kx_s[r], bias_of(0, first_tile),
                pl.ds(r, blk, stride=DIL_MAX))

    def mid_ops(i):
        r = i // mid_blocks
        j = i % mid_blocks
        q0 = pl.multiple_of(j * blk, blk)
        return (qm_s[0, r, pl.ds(q0, blk), :], qm_s[1, r, pl.ds(q0, blk), :],
                km_s[r, pl.ds(q0, 2 * blk), :], bias_of(1, first_tile & (j == 0)),
                pl.ds(j * (blk * DIL_MID) + r, blk, stride=DIL_MID))

    def nat_ops(j):
        q0 = pl.multiple_of(j * blk, blk)
        return (qn_s[0, pl.ds(q0, blk), :], qn_s[1, pl.ds(q0, blk), :],
                kn_s[pl.ds(q0, 2 * blk), :], bias_of(2, first_tile & (j == 0)),
                pl.ds(q0, blk))

    def loop(n, body):
        lax.fori_loop(0, n, lambda i, c: (body(i), c)[1], 0, unroll=ATT_UNROLL)

    n_big, n_mid, n_nat = DIL_MAX, DIL_MID * mid_blocks, ATT_TILE // blk

    def row_max(ops, slot):
        q_lo, q_hi, kk, bias, rows = ops
        s = _scores(q_lo, q_hi, kk, bias)
        sc_s[slot] = s
        m = jnp.max(s, axis=-1, keepdims=True)
        return jnp.broadcast_to(m, (2 * blk, LANES)), rows

    def max_big(r):
        m, rows = row_max(big_ops(r), r)
        mn_s[0, rows, :] = m[:blk]
        mn_s[1, rows, :] = m[blk:]

    def max_more(ops, slot):
        m, rows = row_max(ops, slot)
        mn_s[0, rows, :] = jnp.maximum(mn_s[0, rows, :], m[:blk])
        mn_s[1, rows, :] = jnp.maximum(mn_s[1, rows, :], m[blk:])

    loop(n_big, max_big)
    loop(n_mid, lambda i: max_more(mid_ops(i), n_big + i))
    loop(n_nat, lambda j: max_more(nat_ops(j), n_big + n_mid + j))

    for hh in range(HEADS_PER_PAIR):
        for r in range(DIL_MID):
            mm_s[hh, r] = mn_s.at[hh][pl.ds(r, MID_LEN, stride=DIL_MID), :]
        for r in range(DIL_MID):
            for c in range(DIL_STEP):
                mx_s[hh, r + DIL_MID * c] = mm_s.at[hh, r][pl.ds(c, blk, stride=DIL_STEP), :]

    def num_den(slot, m_lo, m_hi, v_lo, v_hi):
        m = jnp.concatenate([m_lo, m_hi], axis=0)
        e = jnp.exp2(sc_s[slot] - jnp.concatenate([m, m], axis=1)).astype(BF16)
        return (jnp.dot(e[:blk], v_lo, preferred_element_type=F32),
                jnp.dot(e[blk:], v_hi, preferred_element_type=F32))

    def acc_big(r):
        o_lo, o_hi = num_den(r, mx_s[0, r], mx_s[1, r], vx_s[0, r], vx_s[1, r])
        rows = pl.ds(r, blk, stride=DIL_MAX)
        acc_s[0, rows, :] = o_lo
        acc_s[1, rows, :] = o_hi

    def acc_mid(i):
        r = i // mid_blocks
        j = i % mid_blocks
        q0 = pl.multiple_of(j * blk, blk)
        o_lo, o_hi = num_den(
            n_big + i, mm_s[0, r, pl.ds(q0, blk), :], mm_s[1, r, pl.ds(q0, blk), :],
            vm_s[0, r, pl.ds(q0, 2 * blk), :], vm_s[1, r, pl.ds(q0, 2 * blk), :])
        rows = pl.ds(j * (blk * DIL_MID) + r, blk, stride=DIL_MID)
        acc_s[0, rows, :] += o_lo
        acc_s[1, rows, :] += o_hi

    def acc_nat(j):
        q0 = pl.multiple_of(j * blk, blk)
        rows = pl.ds(q0, blk)
        o_lo, o_hi = num_den(
            n_big + n_mid + j, mn_s[0, rows, :], mn_s[1, rows, :],
            vn_s[0, pl.ds(q0, 2 * blk), :], vn_s[1, pl.ds(q0, 2 * blk), :])
        a_lo = acc_s[0, rows, :] + o_lo
        a_hi = acc_s[1, rows, :] + o_hi
        num = jnp.where(lane_b, a_lo, a_hi)
        den = pltpu.roll(jnp.where(lane_b, a_hi, a_lo), ATT_HEAD_DIM, 1)
        att_ref[rows, :] = (num / den).astype(BF16)

    loop(n_big, acc_big)
    loop(n_mid, acc_mid)
    loop(n_nat, acc_nat)


def _dilated_attention(qkv, batch):
    t = qkv.shape[0]
    tiles = t // batch // ATT_TILE
    blk = ATT_BLOCK

    def spec(col0):
        return pl.BlockSpec((ATT_TILE, LANES), lambda b, p, i: (b * tiles + i, col0 + p))

    bf = lambda *shape: pltpu.VMEM(shape, BF16)
    f32 = lambda *shape: pltpu.VMEM(shape, F32)
    return pl.pallas_call(
        _attn_kernel,
        grid=(batch, ATT_PAIRS, tiles),
        in_specs=[spec(0), spec(ATT_PAIRS), spec(2 * ATT_PAIRS)],
        out_specs=spec(0),
        out_shape=jax.ShapeDtypeStruct((t, ATT_WIDTH), BF16),
        scratch_shapes=[
            bf(2, ATT_TILE, LANES), bf(blk + ATT_TILE, LANES), bf(2, blk + ATT_TILE, LANES),
            f32(DIL_MID, MID_LEN, LANES),
            bf(2, DIL_MID, MID_LEN, LANES), bf(DIL_MID, blk + MID_LEN, LANES),
            bf(2, DIL_MID, blk + MID_LEN, LANES),
            bf(2, DIL_MAX, blk, LANES), bf(DIL_MAX, 2 * blk, LANES),
            bf(2, DIL_MAX, 2 * blk, LANES),
            f32(2, ATT_TILE, LANES), f32(2, DIL_MID, MID_LEN, LANES),
            f32(2, DIL_MAX, blk, LANES), f32(2, ATT_TILE, LANES),
            f32(3 * 2, 2 * blk, 2 * blk),
            f32(3 * DIL_MAX, 2 * blk, 2 * blk),
        ],
        compiler_params=_params(3),
        name="attn",
    )(qkv, qkv, qkv)


def _split3(x):
    hi = x.astype(BF16)
    r1 = x - hi.astype(F32)
    mid = r1.astype(BF16)
    lo = (r1 - mid.astype(F32)).astype(BF16)
    return hi, mid, lo


def _dot_exact_rhs(x, w):
    return sum(jnp.dot(part, w, preferred_element_type=F32) for part in _split3(x))


def _ssd_kernel(z_ref, xbc_ref, dt_ref, cw_ref, cb_ref, dtb_ref, alog_ref, dsk_ref, nw_ref,
                triu_ref, exp_ref, y_ref, win_s, state_s):
    c = pl.program_id(0)
    q = SSD_CHUNK
    pad = CONV_PAD
    rows_step = SSD_STEP_CHUNKS * q
    heads_per_group = SSD_HEADS // SSD_GROUPS
    gw = heads_per_group * SSD_HEAD_DIM
    n_seq = z_ref.shape[0]

    @pl.when(c == 0)
    def _reset():
        win_s[:, 0:pad, :] = jnp.zeros((n_seq, pad, SSD_CONV_CH), F32)
        state_s[...] = jnp.zeros_like(state_s)

    win_s[:, pad:, :] = xbc_ref[...]

    ti = lax.broadcasted_iota(jnp.int32, (q, q), 0)
    tj = lax.broadcasted_iota(jnp.int32, (q, q), 1)
    causal = ti >= tj
    low_half = lax.broadcasted_iota(jnp.int32, (q, LANES), 1) < SSD_HEAD_DIM
    a_t = -jnp.exp(alog_ref[...])
    k_w = CONV_WIDTH

    for ci, b in ((ci, b) for ci in range(SSD_STEP_CHUNKS) for b in range(n_seq)):
        r0 = ci * q
        win_b, state_b = win_s.at[b], state_s.at[b]
        z_b, dt_b, y_b = z_ref.at[b], dt_ref.at[b], y_ref.at[b]
        conv = cb_ref[...] + win_b[pad + r0:pad + r0 + q, :] * cw_ref[k_w - 1:k_w, :]
        for k in range(k_w - 1):
            off = pad + r0 - (k_w - 1) + k
            conv = conv + win_b[off:off + q, :] * cw_ref[k:k + 1, :]
        xbc = _silu(conv)
        xs = xbc[:, :SSD_WIDTH]
        bm = xbc[:, SSD_WIDTH:SSD_WIDTH + SSD_GROUPS * SSD_STATE]
        cm = xbc[:, SSD_WIDTH + SSD_GROUPS * SSD_STATE:]

        dt_t = _softplus(dt_b[r0:r0 + q, :].T[0:SSD_HEADS, :] + dtb_ref[...])
        acs_t = _dot_exact_rhs(dt_t * a_t, triu_ref[...])
        e_t = jnp.exp(acs_t)
        w_t = jnp.exp(acs_t[:, q - 1:q] - acs_t) * dt_t
        chunk_decay = jnp.sum(e_t[:, q - 1:q] * exp_ref[...], axis=0, keepdims=True)
        cols_t = jnp.concatenate(
            [acs_t, e_t, jnp.zeros((q - 2 * SSD_HEADS, q), F32)], axis=0).T

        y_parts = []
        for g in range(SSD_GROUPS):
            bm_g = bm[:, g * SSD_STATE:(g + 1) * SSD_STATE]
            cm_g = cm[:, g * SSD_STATE:(g + 1) * SSD_STATE]
            gmat = lax.dot_general(cm_g.astype(BF16), bm_g.astype(BF16),
                                   (((1,), (1,)), ((), ())), preferred_element_type=F32)
            bm_gt = bm_g.T
            for pp in range(heads_per_group // HEADS_PER_PAIR):
                p = g * (heads_per_group // HEADS_PER_PAIR) + pp
                cols = slice(p * LANES, (p + 1) * LANES)
                x_p = xs[:, cols]
                s_p = state_b[:, cols]
                lhs_y, lhs_s, rhs_y, rhs_s = [], [], [], []
                for hh in range(HEADS_PER_PAIR):
                    h = HEADS_PER_PAIR * p + hh
                    keep = low_half if hh == 0 else ~low_half
                    seg = cols_t[:, h:h + 1] - acs_t[h:h + 1, :]
                    lmat = jnp.exp(jnp.where(causal, seg, NEG_INF))
                    scores = gmat * lmat * dt_t[h:h + 1, :]
                    c_dec = cm_g * cols_t[:, SSD_HEADS + h:SSD_HEADS + h + 1]
                    x_h = jnp.where(keep, x_p, 0.0).astype(BF16)
                    s_h = jnp.where(keep, s_p, 0.0).astype(BF16)
                    lhs_y += [scores.astype(BF16), c_dec.astype(BF16)]
                    rhs_y += [x_h, s_h]
                    lhs_s.append((bm_gt * w_t[h:h + 1, :]).astype(BF16))
                    rhs_s.append(x_h)
                y_parts.append(jnp.dot(jnp.concatenate(lhs_y, axis=1),
                                       jnp.concatenate(rhs_y, axis=0),
                                       preferred_element_type=F32))
                state_b[:, cols] = s_p * chunk_decay[:, cols] + jnp.dot(
                    jnp.concatenate(lhs_s, axis=1), jnp.concatenate(rhs_s, axis=0),
                    preferred_element_type=F32)
        y = jnp.concatenate(y_parts, axis=-1) + dsk_ref[...] * xs
        y = y * _silu(z_b[r0:r0 + q, :])
        outs = []
        for g in range(SSD_GROUPS):
            yg = y[:, g * gw:(g + 1) * gw]
            outs.append(yg * lax.rsqrt(jnp.mean(yg * yg, axis=-1, keepdims=True) + SSD_NORM_EPS))
        y_b[r0:r0 + q, :] = (jnp.concatenate(outs, axis=-1) * nw_ref[...]).astype(BF16)

    win_s[:, 0:pad, :] = win_s[:, rows_step:rows_step + pad, :]


def _prep_ssd(conv_w, conv_b, dt_bias, a_log, d_skip, norm_w):
    depth = conv_w.shape[0]
    per_time = lambda v: jnp.broadcast_to(v.astype(F32)[:, :, None],
                                          (depth, SSD_HEADS, SSD_CHUNK))
    triu = jnp.triu(jnp.ones((SSD_CHUNK, SSD_CHUNK), BF16))
    expand = (jnp.arange(SSD_HEADS)[:, None] == (jnp.arange(SSD_WIDTH)[None, :] // SSD_HEAD_DIM)
              ).astype(F32)
    d_exp = jnp.repeat(d_skip.astype(F32), SSD_HEAD_DIM, axis=1).reshape(depth, 1, SSD_WIDTH)
    return (conv_w, conv_b.reshape(depth, 1, SSD_CONV_CH), per_time(dt_bias), per_time(a_log),
            d_exp, norm_w.reshape(depth, 1, SSD_WIDTH), triu, expand)


def _ssd(z, xbc, dt, prepared, layer, batch):
    t = z.shape[0]
    seq = t // batch
    rows_step = SSD_STEP_CHUNKS * SSD_CHUNK
    row = lambda width: pl.BlockSpec((batch, rows_step, width), lambda c: (0, c, 0))
    by_seq = lambda a: a.reshape(batch, seq, a.shape[-1])
    lspec = lambda *shape: _layer_spec(shape, layer)
    out = pl.pallas_call(
        _ssd_kernel,
        grid=(seq // rows_step,),
        in_specs=[row(SSD_WIDTH), row(SSD_CONV_CH), row(DT_PAD),
                  lspec(CONV_WIDTH, SSD_CONV_CH), lspec(1, SSD_CONV_CH),
                  lspec(SSD_HEADS, SSD_CHUNK), lspec(SSD_HEADS, SSD_CHUNK),
                  lspec(1, SSD_WIDTH), lspec(1, SSD_WIDTH),
                  _const_spec((SSD_CHUNK, SSD_CHUNK)), _const_spec((SSD_HEADS, SSD_WIDTH))],
        out_specs=row(SSD_WIDTH),
        out_shape=jax.ShapeDtypeStruct((batch, seq, SSD_WIDTH), BF16),
        scratch_shapes=[pltpu.VMEM((batch, CONV_PAD + rows_step, SSD_CONV_CH), F32),
                        pltpu.VMEM((batch, SSD_STATE, SSD_WIDTH), F32)],
        compiler_params=_params(1),
        name="ssd",
    )(by_seq(z), by_seq(xbc), by_seq(dt), *prepared)
    return out.reshape(t, SSD_WIDTH)


def _gelu_tanh(x):
    c = math.sqrt(2.0 / math.pi)
    return 0.5 * x * (1.0 + jnp.tanh(c * (x + 0.044715 * (x * x * x))))


def _lru_kernel(g_ref, x_ref, cw_ref, cb_ref, wa_ref, ba_ref, wx_ref, bx_ref, lam_ref,
                y_ref, win_s, h_s):
    c = pl.program_id(1)
    tt = LRU_TILE
    pad = CONV_PAD

    @pl.when(c == 0)
    def _reset():
        win_s[0:pad, :] = jnp.zeros((pad, LRU_WIDTH), F32)
        h_s[...] = jnp.zeros_like(h_s)

    win_s[pad:, :] = x_ref[...]
    k_w = CONV_WIDTH
    xc = cb_ref[...] + win_s[pad:pad + tt, :] * cw_ref[k_w - 1:k_w, :]
    for k in range(k_w - 1):
        off = pad - (k_w - 1) + k
        xc = xc + win_s[off:off + tt, :] * cw_ref[k:k + 1, :]
    win_s[0:pad, :] = win_s[tt:tt + pad, :]

    xb = xc.astype(BF16)
    r = jax.nn.sigmoid(jnp.dot(xb, wa_ref[...], preferred_element_type=F32) + ba_ref[...])
    i = jax.nn.sigmoid(jnp.dot(xb, wx_ref[...], preferred_element_type=F32) + bx_ref[...])
    log_a = -LRU_C * r * _softplus(-lam_ref[...])
    a = jnp.exp(log_a)
    u = jnp.sqrt(-jnp.tanh(log_a) * (a * a + 1.0)) * (i * xc)

    groups = tt // SUBLANES
    a = a.reshape(groups, SUBLANES, LRU_WIDTH)
    u = u.reshape(groups, SUBLANES, LRU_WIDTH)
    sub = lax.broadcasted_iota(jnp.int32, (groups, SUBLANES, LRU_WIDTH), 1)
    step = 1
    while step < SUBLANES:
        keep = sub >= step
        a_sh = jnp.where(keep, pltpu.roll(a, step, 1), 1.0)
        u_sh = jnp.where(keep, pltpu.roll(u, step, 1), 0.0)
        u = a * u_sh + u
        a = a * a_sh
        step *= 2
    carry = h_s[0:1, :]
    hs = []
    for gi in range(groups):
        h_g = a[gi] * carry + u[gi]
        hs.append(h_g)
        carry = h_g[SUBLANES - 1:SUBLANES, :]
    h_s[...] = jnp.broadcast_to(carry, h_s.shape)
    h = jnp.concatenate(hs, axis=0)
    y_ref[...] = (h * _gelu_tanh(g_ref[...])).astype(BF16)


def _block_diag(w):
    depth, nb, c, d = w.shape
    eye = jnp.eye(nb, dtype=w.dtype)
    return (eye[None, :, None, :, None] * w[:, :, :, None, :]).reshape(depth, nb * c, nb * d)


def _prep_lru(conv_w, conv_b, wa, ba, wx, bx, lam):
    depth = conv_w.shape[0]
    vec = lambda v: v.astype(F32).reshape(depth, 1, LRU_WIDTH)
    return (conv_w, vec(conv_b), _block_diag(wa).astype(BF16), vec(ba),
            _block_diag(wx).astype(BF16), vec(bx), vec(lam))


def _lru(g_in, x_in, prepared, layer, batch):
    t = g_in.shape[0]
    s = t // batch
    nt = s // LRU_TILE
    row = pl.BlockSpec((LRU_TILE, LRU_WIDTH), lambda b, c: (b * nt + c, 0))
    mat = _layer_spec((LRU_WIDTH, LRU_WIDTH), layer)
    one = _layer_spec((1, LRU_WIDTH), layer)
    return pl.pallas_call(
        _lru_kernel,
        grid=(batch, nt),
        in_specs=[row, row, _layer_spec((CONV_WIDTH, LRU_WIDTH), layer), one, mat, one, mat,
                  one, one],
        out_specs=row,
        out_shape=jax.ShapeDtypeStruct((t, LRU_WIDTH), BF16),
        scratch_shapes=[pltpu.VMEM((CONV_PAD + LRU_TILE, LRU_WIDTH), F32),
                        pltpu.VMEM((SUBLANES, LRU_WIDTH), F32)],
        compiler_params=_params(2),
        name="rglru",
    )(g_in, x_in, *prepared)


def _outffn_kernel(x_ref, att_ref, ssd_ref, lru_ref, wo_ref, gf_ref, wg_ref, wu_ref, wd_ref,
                   nfin_ref, o_ref, hn_s, *, final_norm):
    x1 = x_ref[...]
    for j, m_ref in enumerate((att_ref, ssd_ref, lru_ref)):
        x1 = x1 + jnp.dot(m_ref[...], wo_ref[j * ATT_WIDTH:(j + 1) * ATT_WIDTH, :],
                          preferred_element_type=F32)
    hn_s[...] = _rmsnorm(x1, gf_ref[...]).astype(BF16)
    o_ref[...] = x1
    for c0, c1 in FF_CHUNKS:
        hn = hn_s[...]
        gate = jnp.dot(hn, wg_ref[:, c0:c1], preferred_element_type=F32)
        up = jnp.dot(hn, wu_ref[:, c0:c1], preferred_element_type=F32)
        act = (_silu(gate) * up).astype(BF16)
        o_ref[...] += jnp.dot(act, wd_ref[c0:c1, :], preferred_element_type=F32)
    if final_norm:
        o_ref[...] = _rmsnorm(o_ref[...], nfin_ref[...])


def _prep_ffn(w_out, norm_ffn, w_gate, w_up, w_down, norm_final):
    depth = w_out.shape[0]
    return (w_out.astype(BF16), norm_ffn.reshape(depth, 1, D_MODEL), w_gate.astype(BF16),
            w_up.astype(BF16), w_down.astype(BF16), norm_final.reshape(1, D_MODEL))


def _outffn(x2, att, ssd, lru, prepared, layer, final_norm):
    t = x2.shape[0]
    row = lambda width: pl.BlockSpec((ROW_TILE, width), lambda i: (i, 0))
    lspec = lambda *shape: _layer_spec(shape, layer)
    return pl.pallas_call(
        functools.partial(_outffn_kernel, final_norm=final_norm),
        grid=(t // ROW_TILE,),
        in_specs=[row(D_MODEL), row(ATT_WIDTH), row(SSD_WIDTH), row(LRU_WIDTH),
                  lspec(D_MIX, D_MODEL), lspec(1, D_MODEL), lspec(D_MODEL, D_FF),
                  lspec(D_MODEL, D_FF), lspec(D_FF, D_MODEL), _const_spec((1, D_MODEL))],
        out_specs=row(D_MODEL),
        out_shape=jax.ShapeDtypeStruct((t, D_MODEL), F32),
        scratch_shapes=[pltpu.VMEM((ROW_TILE, D_MODEL), BF16)],
        compiler_params=_params(1),
        name="outffn",
    )(x2, att, ssd, lru, *prepared)


def _prep_w_in(w):
    o_dt = 3 * ATT_WIDTH + SSD_WIDTH + SSD_CONV_CH
    o_gl = o_dt + SSD_HEADS
    dt_cols = jnp.pad(w[:, :, o_dt:o_gl], ((0, 0), (0, 0), (0, DT_PAD - SSD_HEADS)))
    return jnp.concatenate([w[:, :, :o_dt], w[:, :, o_gl:], dt_cols], axis=2).astype(BF16)


def kernel(x, norm_mix, w_in, ssd_conv_w, ssd_conv_b, ssd_dt_bias, ssd_a_log, ssd_d, ssd_norm,
           lru_conv_w, lru_conv_b, lru_wa, lru_ba, lru_wx, lru_bx, lru_lambda, w_out,
           norm_ffn, w_gate, w_up, w_down, norm_final):
    batch, seq, _ = x.shape
    depth = w_in.shape[0]
    g_mix = norm_mix.reshape(depth, 1, D_MODEL)
    w_proj = _prep_w_in(w_in)
    ssd_p = _prep_ssd(ssd_conv_w, ssd_conv_b, ssd_dt_bias, ssd_a_log, ssd_d, ssd_norm)
    lru_p = _prep_lru(lru_conv_w, lru_conv_b, lru_wa, lru_ba, lru_wx, lru_bx, lru_lambda)
    ffn_p = _prep_ffn(w_out, norm_ffn, w_gate, w_up, w_down, norm_final)
    x2 = x.reshape(batch * seq, D_MODEL)
    for l in range(depth):
        qkv, z, xbc, g_lru, x_lru, dt = _inproj(x2, g_mix, w_proj, l)
        att = _dilated_attention(qkv, batch)
        ssd = _ssd(z, xbc, dt, ssd_p, l, batch)
        lru = _lru(g_lru, x_lru, lru_p, l, batch)
        x2 = _outffn(x2, att, ssd, lru, ffn_p, l, l == depth - 1)
    return x2.reshape(batch, seq, D_MODEL)
```

```python
import functools
import math

import jax
import jax.numpy as jnp
from jax import lax
from jax.experimental import pallas as pl
from jax.experimental.pallas import tpu as pltpu

F32 = jnp.float32
BF16 = jnp.bfloat16

D_MODEL = 1024
ATT_HEADS = 8
ATT_HEAD_DIM = 64
ATT_WIDTH = ATT_HEADS * ATT_HEAD_DIM
ATT_BLOCK = 128
ATT_DILATIONS = (1, 4, 16)
SSD_HEADS = 8
SSD_HEAD_DIM = 64
SSD_WIDTH = SSD_HEADS * SSD_HEAD_DIM
SSD_GROUPS = 2
SSD_STATE = 128
SSD_CHUNK = 128
SSD_CONV_CH = SSD_WIDTH + 2 * SSD_GROUPS * SSD_STATE
LRU_WIDTH = 512
LRU_BLOCKS = 8
LRU_BLOCK_W = LRU_WIDTH // LRU_BLOCKS
LRU_C = 8.0
CONV_WIDTH = 4
D_MIX = ATT_WIDTH + SSD_WIDTH + LRU_WIDTH
D_FF = 2816
NORM_EPS = 1e-6
SSD_NORM_EPS = 1e-5

LANES = 128
SUBLANES = 8
VMEM_LIMIT_BYTES = 56 * 1024 * 1024

ROW_TILE = 512
LRU_TILE = 512
SSD_STEP_CHUNKS = 4
DT_PAD = LANES
FF_CHUNKS = ((0, 768), (768, 1536), (1536, 2304), (2304, 2816))
CONV_PAD = SUBLANES

DIL_MID, DIL_MAX = ATT_DILATIONS[1], ATT_DILATIONS[2]
DIL_STEP = DIL_MAX // DIL_MID
ATT_TILE = ATT_BLOCK * DIL_MAX
MID_LEN = ATT_TILE // DIL_MID
HEADS_PER_PAIR = LANES // ATT_HEAD_DIM
ATT_PAIRS = ATT_WIDTH // LANES
ATT_UNROLL = 16
assert ATT_HEADS == 8 and HEADS_PER_PAIR == 2 and DIL_STEP == DIL_MID and ATT_DILATIONS[0] == 1

LOG2E = 1.4426950408889634
NEG_INF = float("-inf")


def _params(n_axes):
    return pltpu.CompilerParams(
        dimension_semantics=("arbitrary",) * n_axes,
        vmem_limit_bytes=VMEM_LIMIT_BYTES)


def _const_spec(shape):
    nd = len(shape)
    return pl.BlockSpec(shape, lambda *_: (0,) * nd, pipeline_mode=pl.Buffered(1))


def _layer_spec(shape, layer):
    nd = len(shape)
    return pl.BlockSpec((None,) + tuple(shape), lambda *_: (layer,) + (0,) * nd,
                        pipeline_mode=pl.Buffered(1))


def _rmsnorm(x, g):
    return x * lax.rsqrt(jnp.mean(x * x, axis=-1, keepdims=True) + NORM_EPS) * g


def _softplus(x):
    return jnp.maximum(x, 0.0) + jnp.log1p(jnp.exp(-jnp.abs(x)))


def _silu(x):
    return x * jax.nn.sigmoid(x)


_C_QKV = 3 * ATT_WIDTH
_C_Z = _C_QKV + SSD_WIDTH
_C_XBC = _C_Z + SSD_CONV_CH
_C_GL = _C_XBC + LRU_WIDTH
_C_XL = _C_GL + LRU_WIDTH
_C_DT = _C_XL + DT_PAD


def _inproj_kernel(x_ref, g_ref, w_ref, qkv_ref, z_ref, xbc_ref, gl_ref, xl_ref, dt_ref):
    h = _rmsnorm(x_ref[...], g_ref[...]).astype(BF16)

    def seg(a, b):
        return jnp.dot(h, w_ref[:, a:b], preferred_element_type=F32)

    qkv_ref[:, 0:ATT_WIDTH] = seg(0, ATT_WIDTH) * (ATT_HEAD_DIM ** -0.5 * LOG2E)
    qkv_ref[:, ATT_WIDTH:_C_QKV] = seg(ATT_WIDTH, _C_QKV)
    z_ref[...] = seg(_C_QKV, _C_Z)
    xbc_ref[...] = seg(_C_Z, _C_XBC)
    gl_ref[...] = seg(_C_XBC, _C_GL)
    xl_ref[...] = seg(_C_GL, _C_XL)
    dt_ref[...] = seg(_C_XL, _C_DT)


def _inproj(x2, g, w, layer):
    t = x2.shape[0]
    row = lambda width: pl.BlockSpec((ROW_TILE, width), lambda i: (i, 0))
    widths = (_C_QKV, SSD_WIDTH, SSD_CONV_CH, LRU_WIDTH, LRU_WIDTH, DT_PAD)
    return pl.pallas_call(
        _inproj_kernel,
        grid=(t // ROW_TILE,),
        in_specs=[row(D_MODEL), _layer_spec((1, D_MODEL), layer),
                  _layer_spec((D_MODEL, _C_DT), layer)],
        out_specs=[row(wd) for wd in widths],
        out_shape=[jax.ShapeDtypeStruct((t, wd), F32) for wd in widths],
        compiler_params=_params(1),
        name="inproj",
    )(x2, g, w)


def _scores(q_lo, q_hi, kk, bias):
    q_both = jnp.concatenate([q_lo, q_hi], axis=0)
    return lax.dot_general(q_both, kk, (((1,), (1,)), ((), ())),
                           preferred_element_type=F32) + bias


def _attn_kernel(q_ref, k_ref, v_ref, att_ref,
                 qn_s, kn_s, vn_s, tmp_s, qm_s, km_s, vm_s, qx_s, kx_s, vx_s,
                 mn_s, mm_s, mx_s, acc_s, bias_s, sc_s):
    pair = pl.program_id(1)
    tile = pl.program_id(2)
    blk = ATT_BLOCK
    first_tile = tile == 0
    mid_blocks = MID_LEN // blk

    @pl.when(first_tile)
    def _start_sequence():
        qi = lax.broadcasted_iota(jnp.int32, (blk, 2 * blk), 0)
        ki = lax.broadcasted_iota(jnp.int32, (blk, 2 * blk), 1)
        dist = blk + qi - ki
        band = (dist >= 0) & (dist <= blk)
        band_first = band & (ki >= blk)
        for hh in range(HEADS_PER_PAIR):
            expo = (126 - HEADS_PER_PAIR * pair - hh) << 23
            slope = lax.bitcast_convert_type(jnp.full((blk, 2 * blk), expo, jnp.int32), F32)
            rows = slice(hh * blk, (hh + 1) * blk)
            for br, dil in enumerate((DIL_MAX, DIL_MID, 1)):
                alibi = (-slope * (dil * dist).astype(F32)) * LOG2E
                bias_s[2 * br, rows, :] = jnp.where(band, alibi, NEG_INF)
                bias_s[2 * br + 1, rows, :] = jnp.where(band_first, alibi, NEG_INF)
        kn_s[0:blk, :] = jnp.zeros((blk, LANES), BF16)
        vn_s[:, 0:blk, :] = jnp.zeros((2, blk, LANES), BF16)
        km_s[:, 0:blk, :] = jnp.zeros((DIL_MID, blk, LANES), BF16)
        vm_s[:, :, 0:blk, :] = jnp.zeros((2, DIL_MID, blk, LANES), BF16)
        kx_s[:, 0:blk, :] = jnp.zeros((DIL_MAX, blk, LANES), BF16)
        vx_s[:, :, 0:blk, :] = jnp.zeros((2, DIL_MAX, blk, LANES), BF16)

    @pl.when(tile > 0)
    def _carry_history():
        kn_s[0:blk, :] = kn_s[ATT_TILE:ATT_TILE + blk, :]
        vn_s[:, 0:blk, :] = vn_s[:, ATT_TILE:ATT_TILE + blk, :]
        km_s[:, 0:blk, :] = km_s[:, MID_LEN:MID_LEN + blk, :]
        vm_s[:, :, 0:blk, :] = vm_s[:, :, MID_LEN:MID_LEN + blk, :]
        kx_s[:, 0:blk, :] = kx_s[:, blk:2 * blk, :]
        vx_s[:, :, 0:blk, :] = vx_s[:, :, blk:2 * blk, :]

    lane_m = lax.broadcasted_iota(jnp.int32, (MID_LEN, LANES), 1) < ATT_HEAD_DIM
    lane_b = lax.broadcasted_iota(jnp.int32, (blk, LANES), 1) < ATT_HEAD_DIM

    def put_q(dst, idx, x, mask):
        dst[(0,) + idx] = jnp.where(mask, x, 0.0).astype(BF16)
        dst[(1,) + idx] = jnp.where(mask, 0.0, x).astype(BF16)

    def put_k(dst, idx, x, mask):
        del mask
        dst[idx] = x.astype(BF16)

    def put_v(dst, idx, x, mask):
        dst[(0,) + idx] = jnp.where(mask, x, 1.0).astype(BF16)
        dst[(1,) + idx] = jnp.where(mask, 1.0, x).astype(BF16)

    for src, nat, mid, big, put, hist in (
            (q_ref, qn_s, qm_s, qx_s, put_q, 0),
            (k_ref, kn_s, km_s, kx_s, put_k, blk),
            (v_ref, vn_s, vm_s, vx_s, put_v, blk)):
        for c in range(DIL_MID):
            put(nat, (slice(hist + c * MID_LEN, hist + (c + 1) * MID_LEN), slice(None)),
                src[c * MID_LEN:(c + 1) * MID_LEN, :], lane_m)
        for r in range(DIL_MID):
            x = src[pl.ds(r, MID_LEN, stride=DIL_MID), :]
            tmp_s[r] = x
            put(mid, (r, slice(hist, hist + MID_LEN), slice(None)), x, lane_m)
        for r in range(DIL_MID):
            for c in range(DIL_STEP):
                y = tmp_s.at[r][pl.ds(c, blk, stride=DIL_STEP), :]
                put(big, (r + DIL_MID * c, slice(hist, hist + blk), slice(None)), y, lane_b)

    def bias_of(branch, is_first):
        return bias_s[2 * branch + jnp.where(is_first, 1, 0)]

    def big_ops(r):
        return (qx_s[0, r], qx_s[1, r], kx_s[r], bias_of(0, first_tile),
                pl.ds(r, blk, stride=DIL_MAX))

    def mid_ops(i):
        r = i // mid_blocks
        j = i % mid_blocks
        q0 = pl.multiple_of(j * blk, blk)
        return (qm_s[0, r, pl.ds(q0, blk), :], qm_s[1, r, pl.ds(q0, blk), :],
                km_s[r, pl.ds(q0, 2 * blk), :], bias_of(1, first_tile & (j == 0)),
                pl.ds(j * (blk * DIL_MID) + r, blk, stride=DIL_MID))

    def nat_ops(j):
        q0 = pl.multiple_of(j * blk, blk)
        return (qn_s[0, pl.ds(q0, blk), :], qn_s[1, pl.ds(q0, blk), :],
                kn_s[pl.ds(q0, 2 * blk), :], bias_of(2, first_tile & (j == 0)),
                pl.ds(q0, blk))

    def loop(n, body):
        lax.fori_loop(0, n, lambda i, c: (body(i), c)[1], 0, unroll=ATT_UNROLL)

    n_big, n_mid, n_nat = DIL_MAX, DIL_MID * mid_blocks, ATT_TILE // blk

    def row_max(ops, slot):
        q_lo, q_hi, kk, bias, rows = ops
        s = _scores(q_lo, q_hi, kk, bias)
        sc_s[slot] = s
        m = jnp.max(s, axis=-1, keepdims=True)
        return jnp.broadcast_to(m, (2 * blk, LANES)), rows

    def max_big(r):
        m, rows = row_max(big_ops(r), r)
        mn_s[0, rows, :] = m[:blk]
        mn_s[1, rows, :] = m[blk:]

    def max_more(ops, slot):
        m, rows = row_max(ops, slot)
        mn_s[0, rows, :] = jnp.maximum(mn_s[0, rows, :], m[:blk])
        mn_s[1, rows, :] = jnp.maximum(mn_s[1, rows, :], m[blk:])

    loop(n_big, max_big)
    loop(n_mid, lambda i: max_more(mid_ops(i), n_big + i))
    loop(n_nat, lambda j: max_more(nat_ops(j), n_big + n_mid + j))

    for hh in range(HEADS_PER_PAIR):
        for r in range(DIL_MID):
            mm_s[hh, r] = mn_s.at[hh][pl.ds(r, MID_LEN, stride=DIL_MID), :]
        for r in range(DIL_MID):
            for c in range(DIL_STEP):
                mx_s[hh, r + DIL_MID * c] = mm_s.at[hh, r][pl.ds(c, blk, stride=DIL_STEP), :]

    def num_den(slot, m_lo, m_hi, v_lo, v_hi):
        m = jnp.concatenate([m_lo, m_hi], axis=0)
        e = jnp.exp2(sc_s[slot] - jnp.concatenate([m, m], axis=1)).astype(BF16)
        return (jnp.dot(e[:blk], v_lo, preferred_element_type=F32),
                jnp.dot(e[blk:], v_hi, preferred_element_type=F32))

    def acc_big(r):
        o_lo, o_hi = num_den(r, mx_s[0, r], mx_s[1, r], vx_s[0, r], vx_s[1, r])
        rows = pl.ds(r, blk, stride=DIL_MAX)
        acc_s[0, rows, :] = o_lo
        acc_s[1, rows, :] = o_hi

    def acc_mid(i):
        r = i // mid_blocks
        j = i % mid_blocks
        q0 = pl.multiple_of(j * blk, blk)
        o_lo, o_hi = num_den(
            n_big + i, mm_s[0, r, pl.ds(q0, blk), :], mm_s[1, r, pl.ds(q0, blk), :],
            vm_s[0, r, pl.ds(q0, 2 * blk), :], vm_s[1, r, pl.ds(q0, 2 * blk), :])
        rows = pl.ds(j * (blk * DIL_MID) + r, blk, stride=DIL_MID)
        acc_s[0, rows, :] += o_lo
        acc_s[1, rows, :] += o_hi

    def acc_nat(j):
        q0 = pl.multiple_of(j * blk, blk)
        rows = pl.ds(q0, blk)
        o_lo, o_hi = num_den(
            n_big + n_mid + j, mn_s[0, rows, :], mn_s[1, rows, :],
            vn_s[0, pl.ds(q0, 2 * blk), :], vn_s[1, pl.ds(q0, 2 * blk), :])
        a_lo = acc_s[0, rows, :] + o_lo
        a_hi = acc_s[1, rows, :] + o_hi
        num = jnp.where(lane_b, a_lo, a_hi)
        den = pltpu.roll(jnp.where(lane_b, a_hi, a_lo), ATT_HEAD_DIM, 1)
        att_ref[rows, :] = (num / den).astype(BF16)

    loop(n_big, acc_big)
    loop(n_mid, acc_mid)
    loop(n_nat, acc_nat)


def _dilated_attention(qkv, batch):
    t = qkv.shape[0]
    tiles = t // batch // ATT_TILE
    blk = ATT_BLOCK

    def spec(col0):
        return pl.BlockSpec((ATT_TILE, LANES), lambda b, p, i: (b * tiles + i, col0 + p))

    bf = lambda *shape: pltpu.VMEM(shape, BF16)
    f32 = lambda *shape: pltpu.VMEM(shape, F32)
    return pl.pallas_call(
        _attn_kernel,
        grid=(batch, ATT_PAIRS, tiles),
        in_specs=[spec(0), spec(ATT_PAIRS), spec(2 * ATT_PAIRS)],
        out_specs=spec(0),
        out_shape=jax.ShapeDtypeStruct((t, ATT_WIDTH), BF16),
        scratch_shapes=[
            bf(2, ATT_TILE, LANES), bf(blk + ATT_TILE, LANES), bf(2, blk + ATT_TILE, LANES),
            f32(DIL_MID, MID_LEN, LANES),
            bf(2, DIL_MID, MID_LEN, LANES), bf(DIL_MID, blk + MID_LEN, LANES),
            bf(2, DIL_MID, blk + MID_LEN, LANES),
            bf(2, DIL_MAX, blk, LANES), bf(DIL_MAX, 2 * blk, LANES),
            bf(2, DIL_MAX, 2 * blk, LANES),
            f32(2, ATT_TILE, LANES), f32(2, DIL_MID, MID_LEN, LANES),
            f32(2, DIL_MAX, blk, LANES), f32(2, ATT_TILE, LANES),
            f32(3 * 2, 2 * blk, 2 * blk),
            f32(3 * DIL_MAX, 2 * blk, 2 * blk),
        ],
        compiler_params=_params(3),
        name="attn",
    )(qkv, qkv, qkv)


def _split3(x):
    hi = x.astype(BF16)
    r1 = x - hi.astype(F32)
    mid = r1.astype(BF16)
    lo = (r1 - mid.astype(F32)).astype(BF16)
    return hi, mid, lo


def _dot_exact_rhs(x, w):
    return sum(jnp.dot(part, w, preferred_element_type=F32) for part in _split3(x))


TM_CHUNK = 128
TM_STEPS = TM_CHUNK // SUBLANES
TM_HIST = (CONV_WIDTH - 1) * SUBLANES
assert TM_CHUNK == SSD_CHUNK


def _to_time_major(src, mid, dst, base):
    quarter = TM_CHUNK // 4
    for r in range(4):
        mid[base + quarter * r:base + quarter * (r + 1), :] = (
            src[pl.ds(base + r, quarter, stride=4), :])
    for r in range(4):
        for c in range(4):
            v = r + 4 * c
            dst[base + SUBLANES * v:base + SUBLANES * (v + 1), :] = (
                mid[pl.ds(base + quarter * r + c, SUBLANES, stride=4), :])


def _tm_time(idx):
    return TM_STEPS * (idx % SUBLANES) + idx // SUBLANES


def _time_major_perm():
    n = jnp.arange(TM_CHUNK)
    return (_tm_time(n)[None, :] == n[:, None]).astype(BF16)


def _conv_time_major(x, hist, cw_ref, cb_ref, first_row):
    width = x.shape[-1]
    groups = TM_HIST // SUBLANES
    tail = x[TM_CHUNK - TM_HIST:, :]
    down = lambda a: pltpu.roll(a.reshape(groups, SUBLANES, width), 1, 1)
    wrapped = jnp.where(first_row, down(hist), down(tail)).reshape(TM_HIST, width)
    ext = jnp.concatenate([wrapped, x], axis=0)
    k_w = CONV_WIDTH
    conv = cb_ref[...] + x * cw_ref[k_w - 1:k_w, :]
    for back in range(1, k_w):
        off = TM_HIST - SUBLANES * back
        conv = conv + ext[off:off + TM_CHUNK, :] * cw_ref[k_w - 1 - back:k_w - back, :]
    return conv, tail


def _ssd_kernel(z_ref, xbc_ref, dt_ref, cw_ref, cb_ref, dtb_ref, alog_ref, dsk_ref, nw_ref,
                triu_ref, exp_ref, y_ref, win_s, state_s):
    c = pl.program_id(0)
    q = SSD_CHUNK
    pad = CONV_PAD
    rows_step = SSD_STEP_CHUNKS * q
    heads_per_group = SSD_HEADS // SSD_GROUPS
    gw = heads_per_group * SSD_HEAD_DIM
    n_seq = z_ref.shape[0]

    @pl.when(c == 0)
    def _reset():
        win_s[:, 0:pad, :] = jnp.zeros((n_seq, pad, SSD_CONV_CH), F32)
        state_s[...] = jnp.zeros_like(state_s)

    win_s[:, pad:, :] = xbc_ref[...]

    ti = lax.broadcasted_iota(jnp.int32, (q, q), 0)
    tj = lax.broadcasted_iota(jnp.int32, (q, q), 1)
    causal = ti >= tj
    low_half = lax.broadcasted_iota(jnp.int32, (q, LANES), 1) < SSD_HEAD_DIM
    a_t = -jnp.exp(alog_ref[...])
    k_w = CONV_WIDTH

    for ci, b in ((ci, b) for ci in range(SSD_STEP_CHUNKS) for b in range(n_seq)):
        r0 = ci * q
        win_b, state_b = win_s.at[b], state_s.at[b]
        z_b, dt_b, y_b = z_ref.at[b], dt_ref.at[b], y_ref.at[b]
        conv = cb_ref[...] + win_b[pad + r0:pad + r0 + q, :] * cw_ref[k_w - 1:k_w, :]
        for k in range(k_w - 1):
            off = pad + r0 - (k_w - 1) + k
            conv = conv + win_b[off:off + q, :] * cw_ref[k:k + 1, :]
        xbc = _silu(conv)
        xs = xbc[:, :SSD_WIDTH]
        bm = xbc[:, SSD_WIDTH:SSD_WIDTH + SSD_GROUPS * SSD_STATE]
        cm = xbc[:, SSD_WIDTH + SSD_GROUPS * SSD_STATE:]

        dt_t = _softplus(dt_b[r0:r0 + q, :].T[0:SSD_HEADS, :] + dtb_ref[...])
        acs_t = _dot_exact_rhs(dt_t * a_t, triu_ref[...])
        e_t = jnp.exp(acs_t)
        w_t = jnp.exp(acs_t[:, q - 1:q] - acs_t) * dt_t
        chunk_decay = jnp.sum(e_t[:, q - 1:q] * exp_ref[...], axis=0, keepdims=True)
        cols_t = jnp.concatenate(
            [acs_t, e_t, jnp.zeros((q - 2 * SSD_HEADS, q), F32)], axis=0).T

        y_parts = []
        for g in range(SSD_GROUPS):
            bm_g = bm[:, g * SSD_STATE:(g + 1) * SSD_STATE]
            cm_g = cm[:, g * SSD_STATE:(g + 1) * SSD_STATE]
            gmat = lax.dot_general(cm_g.astype(BF16), bm_g.astype(BF16),
                                   (((1,), (1,)), ((), ())), preferred_element_type=F32)
            bm_gt = bm_g.T
            for pp in range(heads_per_group // HEADS_PER_PAIR):
                p = g * (heads_per_group // HEADS_PER_PAIR) + pp
                cols = slice(p * LANES, (p + 1) * LANES)
                x_p = xs[:, cols]
                s_p = state_b[:, cols]
                lhs_y, lhs_s, rhs_y, rhs_s = [], [], [], []
                for hh in range(HEADS_PER_PAIR):
                    h = HEADS_PER_PAIR * p + hh
                    keep = low_half if hh == 0 else ~low_half
                    seg = cols_t[:, h:h + 1] - acs_t[h:h + 1, :]
                    lmat = jnp.exp(jnp.where(causal, seg, NEG_INF))
                    scores = gmat * lmat * dt_t[h:h + 1, :]
                    c_dec = cm_g * cols_t[:, SSD_HEADS + h:SSD_HEADS + h + 1]
                    x_h = jnp.where(keep, x_p, 0.0).astype(BF16)
                    s_h = jnp.where(keep, s_p, 0.0).astype(BF16)
                    lhs_y += [scores.astype(BF16), c_dec.astype(BF16)]
                    rhs_y += [x_h, s_h]
                    lhs_s.append((bm_gt * w_t[h:h + 1, :]).astype(BF16))
                    rhs_s.append(x_h)
                y_parts.append(jnp.dot(jnp.concatenate(lhs_y, axis=1),
                                       jnp.concatenate(rhs_y, axis=0),
                                       preferred_element_type=F32))
                state_b[:, cols] = s_p * chunk_decay[:, cols] + jnp.dot(
                    jnp.concatenate(lhs_s, axis=1), jnp.concatenate(rhs_s, axis=0),
                    preferred_element_type=F32)
        y = jnp.concatenate(y_parts, axis=-1) + dsk_ref[...] * xs
        y = y * _silu(z_b[r0:r0 + q, :])
        outs = []
        for g in range(SSD_GROUPS):
            yg = y[:, g * gw:(g + 1) * gw]
            outs.append(yg * lax.rsqrt(jnp.mean(yg * yg, axis=-1, keepdims=True) + SSD_NORM_EPS))
        y_b[r0:r0 + q, :] = (jnp.concatenate(outs, axis=-1) * nw_ref[...]).astype(BF16)

    win_s[:, 0:pad, :] = win_s[:, rows_step:rows_step + pad, :]


def _prep_ssd(conv_w, conv_b, dt_bias, a_log, d_skip, norm_w):
    depth = conv_w.shape[0]
    per_time = lambda v: jnp.broadcast_to(v.astype(F32)[:, :, None],
                                          (depth, SSD_HEADS, SSD_CHUNK))
    triu = jnp.triu(jnp.ones((SSD_CHUNK, SSD_CHUNK), BF16))
    expand = (jnp.arange(SSD_HEADS)[:, None] == (jnp.arange(SSD_WIDTH)[None, :] // SSD_HEAD_DIM)
              ).astype(F32)
    d_exp = jnp.repeat(d_skip.astype(F32), SSD_HEAD_DIM, axis=1).reshape(depth, 1, SSD_WIDTH)
    return (conv_w, conv_b.reshape(depth, 1, SSD_CONV_CH), per_time(dt_bias), per_time(a_log),
            d_exp, norm_w.reshape(depth, 1, SSD_WIDTH), triu, expand)


def _ssd(z, xbc, dt, prepared, layer, batch):
    t = z.shape[0]
    seq = t // batch
    rows_step = SSD_STEP_CHUNKS * SSD_CHUNK
    row = lambda width: pl.BlockSpec((batch, rows_step, width), lambda c: (0, c, 0))
    by_seq = lambda a: a.reshape(batch, seq, a.shape[-1])
    lspec = lambda *shape: _layer_spec(shape, layer)
    out = pl.pallas_call(
        _ssd_kernel,
        grid=(seq // rows_step,),
        in_specs=[row(SSD_WIDTH), row(SSD_CONV_CH), row(DT_PAD),
                  lspec(CONV_WIDTH, SSD_CONV_CH), lspec(1, SSD_CONV_CH),
                  lspec(SSD_HEADS, SSD_CHUNK), lspec(SSD_HEADS, SSD_CHUNK),
                  lspec(1, SSD_WIDTH), lspec(1, SSD_WIDTH),
                  _const_spec((SSD_CHUNK, SSD_CHUNK)), _const_spec((SSD_HEADS, SSD_WIDTH))],
        out_specs=row(SSD_WIDTH),
        out_shape=jax.ShapeDtypeStruct((batch, seq, SSD_WIDTH), BF16),
        scratch_shapes=[pltpu.VMEM((batch, CONV_PAD + rows_step, SSD_CONV_CH), F32),
                        pltpu.VMEM((batch, SSD_STATE, SSD_WIDTH), F32)],
        compiler_params=_params(1),
        name="ssd",
    )(by_seq(z), by_seq(xbc), by_seq(dt), *prepared)
    return out.reshape(t, SSD_WIDTH)


def _gelu_tanh(x):
    c = math.sqrt(2.0 / math.pi)
    return 0.5 * x * (1.0 + jnp.tanh(c * (x + 0.044715 * (x * x * x))))


def _lru_kernel(*refs):
    ncol = LRU_WIDTH // LANES
    g_refs, x_refs = refs[:ncol], refs[ncol:2 * ncol]
    (cw_ref, cb_ref, wa_ref, ba_ref, wx_ref, bx_ref, lam_ref, perm_ref, y_ref,
     mid_s, gt_s, xt_s, hist_s, h_s) = refs[2 * ncol:]
    c = pl.program_id(1)
    tt = LRU_TILE
    n_chunks = tt // TM_CHUNK

    @pl.when(c == 0)
    def _reset():
        hist_s[...] = jnp.zeros_like(hist_s)
        h_s[...] = jnp.zeros_like(h_s)

    for j in range(ncol):
        for k in range(n_chunks):
            _to_time_major(g_refs[j], mid_s.at[j], gt_s.at[j], k * TM_CHUNK)
        for k in range(n_chunks):
            _to_time_major(x_refs[j], mid_s.at[j], xt_s.at[j], k * TM_CHUNK)

    def chunk_of(buf, k):
        rows = slice(k * TM_CHUNK, (k + 1) * TM_CHUNK)
        return jnp.concatenate([buf[j, rows, :] for j in range(ncol)], axis=1)

    first_row = lax.broadcasted_iota(
        jnp.int32, (TM_HIST // SUBLANES, SUBLANES, LRU_WIDTH), 1) == 0
    hist = hist_s[...]
    convs = []
    for k in range(n_chunks):
        conv, hist = _conv_time_major(chunk_of(xt_s, k), hist, cw_ref, cb_ref, first_row)
        convs.append(conv)
    hist_s[...] = hist
    xc = jnp.concatenate(convs, axis=0)

    xb = xc.astype(BF16)
    r = jax.nn.sigmoid(jnp.dot(xb, wa_ref[...], preferred_element_type=F32) + ba_ref[...])
    i = jax.nn.sigmoid(jnp.dot(xb, wx_ref[...], preferred_element_type=F32) + bx_ref[...])
    log_a = -LRU_C * r * _softplus(-lam_ref[...])
    a = jnp.exp(log_a)
    u = jnp.sqrt(-jnp.tanh(log_a) * (a * a + 1.0)) * (i * xc)

    sub = lax.broadcasted_iota(jnp.int32, (SUBLANES, LRU_WIDTH), 0)
    h_prev = h_s[0:1, :]
    for k in range(n_chunks):
        rows = slice(k * TM_CHUNK, (k + 1) * TM_CHUNK)
        a_k = a[rows].reshape(TM_STEPS, SUBLANES, LRU_WIDTH)
        u_k = u[rows].reshape(TM_STEPS, SUBLANES, LRU_WIDTH)
        hh, aa = [u_k[0]], [a_k[0]]
        for v in range(1, TM_STEPS):
            hh.append(a_k[v] * hh[-1] + u_k[v])
            aa.append(a_k[v] * aa[-1])
        pa, ph = aa[-1], hh[-1]
        step = 1
        while step < SUBLANES:
            keep = sub >= step
            pa_sh = jnp.where(keep, pltpu.roll(pa, step, 0), 1.0)
            ph_sh = jnp.where(keep, pltpu.roll(ph, step, 0), 0.0)
            ph = pa * ph_sh + ph
            pa = pa * pa_sh
            step *= 2
        c_in = jnp.where(sub == 0, h_prev, pltpu.roll(pa, 1, 0) * h_prev + pltpu.roll(ph, 1, 0))
        h_k = [hh[v] + aa[v] * c_in for v in range(TM_STEPS)]
        h_prev = h_k[-1][SUBLANES - 1:SUBLANES, :]
        y_k = (jnp.concatenate(h_k, axis=0) * _gelu_tanh(chunk_of(gt_s, k))).astype(BF16)
        y_ref[rows, :] = jnp.dot(perm_ref[...], y_k, preferred_element_type=F32).astype(BF16)
    h_s[...] = jnp.broadcast_to(h_prev, h_s.shape)


def _block_diag(w):
    depth, nb, c, d = w.shape
    eye = jnp.eye(nb, dtype=w.dtype)
    return (eye[None, :, None, :, None] * w[:, :, :, None, :]).reshape(depth, nb * c, nb * d)


def _prep_lru(conv_w, conv_b, wa, ba, wx, bx, lam):
    depth = conv_w.shape[0]
    vec = lambda v: v.astype(F32).reshape(depth, 1, LRU_WIDTH)
    return (conv_w, vec(conv_b), _block_diag(wa).astype(BF16), vec(ba),
            _block_diag(wx).astype(BF16), vec(bx), vec(lam))


def _lru(g_in, x_in, prepared, layer, batch):
    t = g_in.shape[0]
    s = t // batch
    nt = s // LRU_TILE
    ncol = LRU_WIDTH // LANES
    row = pl.BlockSpec((LRU_TILE, LRU_WIDTH), lambda b, c: (b * nt + c, 0))
    cols = [pl.BlockSpec((LRU_TILE, LANES), lambda b, c, j=j: (b * nt + c, j))
            for j in range(ncol)]
    mat = _layer_spec((LRU_WIDTH, LRU_WIDTH), layer)
    one = _layer_spec((1, LRU_WIDTH), layer)
    tile = lambda: pltpu.VMEM((ncol, LRU_TILE, LANES), F32)
    return pl.pallas_call(
        _lru_kernel,
        grid=(batch, nt),
        in_specs=cols + cols + [_layer_spec((CONV_WIDTH, LRU_WIDTH), layer), one, mat, one,
                                mat, one, one, _const_spec((TM_CHUNK, TM_CHUNK))],
        out_specs=row,
        out_shape=jax.ShapeDtypeStruct((t, LRU_WIDTH), BF16),
        scratch_shapes=[tile(), tile(), tile(),
                        pltpu.VMEM((TM_HIST, LRU_WIDTH), F32),
                        pltpu.VMEM((SUBLANES, LRU_WIDTH), F32)],
        compiler_params=_params(2),
        name="rglru",
    )(*([g_in] * ncol), *([x_in] * ncol), *prepared, _time_major_perm())


def _outffn_kernel(x_ref, att_ref, ssd_ref, lru_ref, wo_ref, gf_ref, wg_ref, wu_ref, wd_ref,
                   nfin_ref, o_ref, hn_s, *, final_norm):
    x1 = x_ref[...]
    for j, m_ref in enumerate((att_ref, ssd_ref, lru_ref)):
        x1 = x1 + jnp.dot(m_ref[...], wo_ref[j * ATT_WIDTH:(j + 1) * ATT_WIDTH, :],
                          preferred_element_type=F32)
    hn_s[...] = _rmsnorm(x1, gf_ref[...]).astype(BF16)
    o_ref[...] = x1
    for c0, c1 in FF_CHUNKS:
        hn = hn_s[...]
        gate = jnp.dot(hn, wg_ref[:, c0:c1], preferred_element_type=F32)
        up = jnp.dot(hn, wu_ref[:, c0:c1], preferred_element_type=F32)
        act = (_silu(gate) * up).astype(BF16)
        o_ref[...] += jnp.dot(act, wd_ref[c0:c1, :], preferred_element_type=F32)
    if final_norm:
        o_ref[...] = _rmsnorm(o_ref[...], nfin_ref[...])


def _prep_ffn(w_out, norm_ffn, w_gate, w_up, w_down, norm_final):
    depth = w_out.shape[0]
    return (w_out.astype(BF16), norm_ffn.reshape(depth, 1, D_MODEL), w_gate.astype(BF16),
            w_up.astype(BF16), w_down.astype(BF16), norm_final.reshape(1, D_MODEL))


def _outffn(x2, att, ssd, lru, prepared, layer, final_norm):
    t = x2.shape[0]
    row = lambda width: pl.BlockSpec((ROW_TILE, width), lambda i: (i, 0))
    lspec = lambda *shape: _layer_spec(shape, layer)
    return pl.pallas_call(
        functools.partial(_outffn_kernel, final_norm=final_norm),
        grid=(t // ROW_TILE,),
        in_specs=[row(D_MODEL), row(ATT_WIDTH), row(SSD_WIDTH), row(LRU_WIDTH),
                  lspec(D_MIX, D_MODEL), lspec(1, D_MODEL), lspec(D_MODEL, D_FF),
                  lspec(D_MODEL, D_FF), lspec(D_FF, D_MODEL), _const_spec((1, D_MODEL))],
        out_specs=row(D_MODEL),
        out_shape=jax.ShapeDtypeStruct((t, D_MODEL), F32),
        scratch_shapes=[pltpu.VMEM((ROW_TILE, D_MODEL), BF16)],
        compiler_params=_params(1),
        name="outffn",
    )(x2, att, ssd, lru, *prepared)


def _prep_w_in(w):
    o_dt = 3 * ATT_WIDTH + SSD_WIDTH + SSD_CONV_CH
    o_gl = o_dt + SSD_HEADS
    dt_cols = jnp.pad(w[:, :, o_dt:o_gl], ((0, 0), (0, 0), (0, DT_PAD - SSD_HEADS)))
    return jnp.concatenate([w[:, :, :o_dt], w[:, :, o_gl:], dt_cols], axis=2).astype(BF16)


def kernel(x, norm_mix, w_in, ssd_conv_w, ssd_conv_b, ssd_dt_bias, ssd_a_log, ssd_d, ssd_norm,
           lru_conv_w, lru_conv_b, lru_wa, lru_ba, lru_wx, lru_bx, lru_lambda, w_out,
           norm_ffn, w_gate, w_up, w_down, norm_final):
    batch, seq, _ = x.shape
    depth = w_in.shape[0]
    g_mix = norm_mix.reshape(depth, 1, D_MODEL)
    w_proj = _prep_w_in(w_in)
    ssd_p = _prep_ssd(ssd_conv_w, ssd_conv_b, ssd_dt_bias, ssd_a_log, ssd_d, ssd_norm)
    lru_p = _prep_lru(lru_conv_w, lru_conv_b, lru_wa, lru_ba, lru_wx, lru_bx, lru_lambda)
    ffn_p = _prep_ffn(w_out, norm_ffn, w_gate, w_up, w_down, norm_final)
    x2 = x.reshape(batch * seq, D_MODEL)
    for l in range(depth):
        qkv, z, xbc, g_lru, x_lru, dt = _inproj(x2, g_mix, w_proj, l)
        att = _dilated_attention(qkv, batch)
        ssd = _ssd(z, xbc, dt, ssd_p, l, batch)
        lru = _lru(g_lru, x_lru, lru_p, l, batch)
        x2 = _outffn(x2, att, ssd, lru, ffn_p, l, l == depth - 1)
    return x2.reshape(batch, seq, D_MODEL)
```

```python
import functools
import math

import jax
import jax.numpy as jnp
from jax import lax
from jax.experimental import pallas as pl
from jax.experimental.pallas import tpu as pltpu

F32 = jnp.float32
BF16 = jnp.bfloat16

D_MODEL = 1024
ATT_HEADS = 8
ATT_HEAD_DIM = 64
ATT_WIDTH = ATT_HEADS * ATT_HEAD_DIM
ATT_BLOCK = 128
ATT_DILATIONS = (1, 4, 16)
SSD_HEADS = 8
SSD_HEAD_DIM = 64
SSD_WIDTH = SSD_HEADS * SSD_HEAD_DIM
SSD_GROUPS = 2
SSD_STATE = 128
SSD_CHUNK = 128
SSD_CONV_CH = SSD_WIDTH + 2 * SSD_GROUPS * SSD_STATE
LRU_WIDTH = 512
LRU_BLOCKS = 8
LRU_BLOCK_W = LRU_WIDTH // LRU_BLOCKS
LRU_C = 8.0
CONV_WIDTH = 4
D_MIX = ATT_WIDTH + SSD_WIDTH + LRU_WIDTH
D_FF = 2816
NORM_EPS = 1e-6
SSD_NORM_EPS = 1e-5

LANES = 128
SUBLANES = 8
VMEM_LIMIT_BYTES = 56 * 1024 * 1024

ROW_TILE = 512
LRU_TILE = 512
SSD_STEP_CHUNKS = 4
DT_PAD = LANES
FF_CHUNKS = ((0, 768), (768, 1536), (1536, 2304), (2304, 2816))
CONV_PAD = SUBLANES

DIL_MID, DIL_MAX = ATT_DILATIONS[1], ATT_DILATIONS[2]
DIL_STEP = DIL_MAX // DIL_MID
ATT_TILE = ATT_BLOCK * DIL_MAX
MID_LEN = ATT_TILE // DIL_MID
HEADS_PER_PAIR = LANES // ATT_HEAD_DIM
ATT_PAIRS = ATT_WIDTH // LANES
ATT_UNROLL = 16
assert ATT_HEADS == 8 and HEADS_PER_PAIR == 2 and DIL_STEP == DIL_MID and ATT_DILATIONS[0] == 1

LOG2E = 1.4426950408889634
NEG_INF = float("-inf")


def _params(n_axes):
    return pltpu.CompilerParams(
        dimension_semantics=("arbitrary",) * n_axes,
        vmem_limit_bytes=VMEM_LIMIT_BYTES)


def _const_spec(shape):
    nd = len(shape)
    return pl.BlockSpec(shape, lambda *_: (0,) * nd, pipeline_mode=pl.Buffered(1))


def _layer_spec(shape, layer):
    nd = len(shape)
    return pl.BlockSpec((None,) + tuple(shape), lambda *_: (layer,) + (0,) * nd,
                        pipeline_mode=pl.Buffered(1))


def _rmsnorm(x, g):
    return x * lax.rsqrt(jnp.mean(x * x, axis=-1, keepdims=True) + NORM_EPS) * g


def _softplus(x):
    return jnp.maximum(x, 0.0) + jnp.log1p(jnp.exp(-jnp.abs(x)))


def _silu(x):
    return x * jax.nn.sigmoid(x)


_C_QKV = 3 * ATT_WIDTH
_C_Z = _C_QKV + SSD_WIDTH
_C_XBC = _C_Z + SSD_CONV_CH
_C_GL = _C_XBC + LRU_WIDTH
_C_XL = _C_GL + LRU_WIDTH
_C_DT = _C_XL + DT_PAD


def _inproj_kernel(x_ref, g_ref, w_ref, qkv_ref, z_ref, xbc_ref, gl_ref, xl_ref, dt_ref):
    h = _rmsnorm(x_ref[...], g_ref[...]).astype(BF16)

    def seg(a, b):
        return jnp.dot(h, w_ref[:, a:b], preferred_element_type=F32)

    qkv_ref[:, 0:ATT_WIDTH] = seg(0, ATT_WIDTH) * (ATT_HEAD_DIM ** -0.5 * LOG2E)
    qkv_ref[:, ATT_WIDTH:_C_QKV] = seg(ATT_WIDTH, _C_QKV)
    z_ref[...] = seg(_C_QKV, _C_Z)
    xbc_ref[...] = seg(_C_Z, _C_XBC)
    gl_ref[...] = seg(_C_XBC, _C_GL)
    xl_ref[...] = seg(_C_GL, _C_XL)
    dt_ref[...] = seg(_C_XL, _C_DT)


def _inproj(x2, g, w, layer):
    t = x2.shape[0]
    row = lambda width: pl.BlockSpec((ROW_TILE, width), lambda i: (i, 0))
    widths = (_C_QKV, SSD_WIDTH, SSD_CONV_CH, LRU_WIDTH, LRU_WIDTH, DT_PAD)
    return pl.pallas_call(
        _inproj_kernel,
        grid=(t // ROW_TILE,),
        in_specs=[row(D_MODEL), _layer_spec((1, D_MODEL), layer),
                  _layer_spec((D_MODEL, _C_DT), layer)],
        out_specs=[row(wd) for wd in widths],
        out_shape=[jax.ShapeDtypeStruct((t, wd), F32) for wd in widths],
        compiler_params=_params(1),
        name="inproj",
    )(x2, g, w)


def _scores(q_lo, q_hi, kk, bias):
    q_both = jnp.concatenate([q_lo, q_hi], axis=0)
    return lax.dot_general(q_both, kk, (((1,), (1,)), ((), ())),
                           preferred_element_type=F32) + bias


def _attn_kernel(q_ref, k_ref, v_ref, att_ref,
                 qn_s, kn_s, vn_s, tmp_s, qm_s, km_s, vm_s, qx_s, kx_s, vx_s,
                 mn_s, mm_s, mx_s, acc_s, bias_s, sc_s):
    pair = pl.program_id(1)
    tile = pl.program_id(2)
    blk = ATT_BLOCK
    first_tile = tile == 0
    mid_blocks = MID_LEN // blk

    @pl.when(first_tile)
    def _start_sequence():
        qi = lax.broadcasted_iota(jnp.int32, (blk, 2 * blk), 0)
        ki = lax.broadcasted_iota(jnp.int32, (blk, 2 * blk), 1)
        dist = blk + qi - ki
        band = (dist >= 0) & (dist <= blk)
        band_first = band & (ki >= blk)
        for hh in range(HEADS_PER_PAIR):
            expo = (126 - HEADS_PER_PAIR * pair - hh) << 23
            slope = lax.bitcast_convert_type(jnp.full((blk, 2 * blk), expo, jnp.int32), F32)
            rows = slice(hh * blk, (hh + 1) * blk)
            for br, dil in enumerate((DIL_MAX, DIL_MID, 1)):
                alibi = (-slope * (dil * dist).astype(F32)) * LOG2E
                bias_s[2 * br, rows, :] = jnp.where(band, alibi, NEG_INF)
                bias_s[2 * br + 1, rows, :] = jnp.where(band_first, alibi, NEG_INF)
        kn_s[0:blk, :] = jnp.zeros((blk, LANES), BF16)
        vn_s[:, 0:blk, :] = jnp.zeros((2, blk, LANES), BF16)
        km_s[:, 0:blk, :] = jnp.zeros((DIL_MID, blk, LANES), BF16)
        vm_s[:, :, 0:blk, :] = jnp.zeros((2, DIL_MID, blk, LANES), BF16)
        kx_s[:, 0:blk, :] = jnp.zeros((DIL_MAX, blk, LANES), BF16)
        vx_s[:, :, 0:blk, :] = jnp.zeros((2, DIL_MAX, blk, LANES), BF16)

    @pl.when(tile > 0)
    def _carry_history():
        kn_s[0:blk, :] = kn_s[ATT_TILE:ATT_TILE + blk, :]
        vn_s[:, 0:blk, :] = vn_s[:, ATT_TILE:ATT_TILE + blk, :]
        km_s[:, 0:blk, :] = km_s[:, MID_LEN:MID_LEN + blk, :]
        vm_s[:, :, 0:blk, :] = vm_s[:, :, MID_LEN:MID_LEN + blk, :]
        kx_s[:, 0:blk, :] = kx_s[:, blk:2 * blk, :]
        vx_s[:, :, 0:blk, :] = vx_s[:, :, blk:2 * blk, :]

    lane_m = lax.broadcasted_iota(jnp.int32, (MID_LEN, LANES), 1) < ATT_HEAD_DIM
    lane_b = lax.broadcasted_iota(jnp.int32, (blk, LANES), 1) < ATT_HEAD_DIM

    def put_q(dst, idx, x, mask):
        dst[(0,) + idx] = jnp.where(mask, x, 0.0).astype(BF16)
        dst[(1,) + idx] = jnp.where(mask, 0.0, x).astype(BF16)

    def put_k(dst, idx, x, mask):
        del mask
        dst[idx] = x.astype(BF16)

    def put_v(dst, idx, x, mask):
        dst[(0,) + idx] = jnp.where(mask, x, 1.0).astype(BF16)
        dst[(1,) + idx] = jnp.where(mask, 1.0, x).astype(BF16)

    for src, nat, mid, big, put, hist in (
            (q_ref, qn_s, qm_s, qx_s, put_q, 0),
            (k_ref, kn_s, km_s, kx_s, put_k, blk),
            (v_ref, vn_s, vm_s, vx_s, put_v, blk)):
        for c in range(DIL_MID):
            put(nat, (slice(hist + c * MID_LEN, hist + (c + 1) * MID_LEN), slice(None)),
                src[c * MID_LEN:(c + 1) * MID_LEN, :], lane_m)
        for r in range(DIL_MID):
            x = src[pl.ds(r, MID_LEN, stride=DIL_MID), :]
            tmp_s[r] = x
            put(mid, (r, slice(hist, hist + MID_LEN), slice(None)), x, lane_m)
        for r in range(DIL_MID):
            for c in range(DIL_STEP):
                y = tmp_s.at[r][pl.ds(c, blk, stride=DIL_STEP), :]
                put(big, (r + DIL_MID * c, slice(hist, hist + blk), slice(None)), y, lane_b)

    def bias_of(branch, is_first):
        return bias_s[2 * branch + jnp.where(is_first, 1, 0)]

    def big_ops(r):
        return (qx_s[0, r], qx_s[1, r], kx_s[r], bias_of(0, first_tile),
                pl.ds(r, blk, stride=DIL_MAX))

    def mid_ops(i):
        r = i // mid_blocks
        j = i % mid_blocks
        q0 = pl.multiple_of(j * blk, blk)
        return (qm_s[0, r, pl.ds(q0, blk), :], qm_s[1, r, pl.ds(q0, blk), :],
                km_s[r, pl.ds(q0, 2 * blk), :], bias_of(1, first_tile & (j == 0)),
                pl.ds(j * (blk * DIL_MID) + r, blk, stride=DIL_MID))

    def nat_ops(j):
        q0 = pl.multiple_of(j * blk, blk)
        return (qn_s[0, pl.ds(q0, blk), :], qn_s[1, pl.ds(q0, blk), :],
                kn_s[pl.ds(q0, 2 * blk), :], bias_of(2, first_tile & (j == 0)),
                pl.ds(q0, blk))

    def loop(n, body):
        lax.fori_loop(0, n, lambda i, c: (body(i), c)[1], 0, unroll=ATT_UNROLL)

    n_big, n_mid, n_nat = DIL_MAX, DIL_MID * mid_blocks, ATT_TILE // blk

    def row_max(ops, slot):
        q_lo, q_hi, kk, bias, rows = ops
        s = _scores(q_lo, q_hi, kk, bias)
        sc_s[slot] = s
        m = jnp.broadcast_to(jnp.max(s, axis=-1, keepdims=True), (2 * blk, LANES))
        return jnp.where(lane_b, m[:blk], m[blk:]), rows

    def max_big(r):
        m, rows = row_max(big_ops(r), r)
        mn_s[rows, :] = m

    def max_more(ops, slot):
        m, rows = row_max(ops, slot)
        mn_s[rows, :] = jnp.maximum(mn_s[rows, :], m)

    loop(n_big, max_big)
    loop(n_mid, lambda i: max_more(mid_ops(i), n_big + i))
    loop(n_nat, lambda j: max_more(nat_ops(j), n_big + n_mid + j))

    for r in range(DIL_MID):
        mm_s[r] = mn_s[pl.ds(r, MID_LEN, stride=DIL_MID), :]
    for r in range(DIL_MID):
        for c in range(DIL_STEP):
            mx_s[r + DIL_MID * c] = mm_s.at[r][pl.ds(c, blk, stride=DIL_STEP), :]

    def num_den(slot, m_pair, v_lo, v_hi):
        swapped = pltpu.roll(m_pair, ATT_HEAD_DIM, 1)
        m = jnp.concatenate([jnp.where(lane_b, m_pair, swapped),
                             jnp.where(lane_b, swapped, m_pair)], axis=0)
        e = jnp.exp2(sc_s[slot] - jnp.concatenate([m, m], axis=1)).astype(BF16)
        return (jnp.dot(e[:blk], v_lo, preferred_element_type=F32),
                jnp.dot(e[blk:], v_hi, preferred_element_type=F32))

    def acc_big(r):
        o_lo, o_hi = num_den(r, mx_s[r], vx_s[0, r], vx_s[1, r])
        rows = pl.ds(r, blk, stride=DIL_MAX)
        acc_s[0, rows, :] = o_lo
        acc_s[1, rows, :] = o_hi

    def acc_mid(i):
        r = i // mid_blocks
        j = i % mid_blocks
        q0 = pl.multiple_of(j * blk, blk)
        o_lo, o_hi = num_den(
            n_big + i, mm_s[r, pl.ds(q0, blk), :],
            vm_s[0, r, pl.ds(q0, 2 * blk), :], vm_s[1, r, pl.ds(q0, 2 * blk), :])
        rows = pl.ds(j * (blk * DIL_MID) + r, blk, stride=DIL_MID)
        acc_s[0, rows, :] += o_lo
        acc_s[1, rows, :] += o_hi

    def acc_nat(j):
        q0 = pl.multiple_of(j * blk, blk)
        rows = pl.ds(q0, blk)
        o_lo, o_hi = num_den(
            n_big + n_mid + j, mn_s[rows, :],
            vn_s[0, pl.ds(q0, 2 * blk), :], vn_s[1, pl.ds(q0, 2 * blk), :])
        a_lo = acc_s[0, rows, :] + o_lo
        a_hi = acc_s[1, rows, :] + o_hi
        num = jnp.where(lane_b, a_lo, a_hi)
        den = pltpu.roll(jnp.where(lane_b, a_hi, a_lo), ATT_HEAD_DIM, 1)
        att_ref[rows, :] = (num / den).astype(BF16)

    @pl.when(tile >= 0)
    def _pass_two():
        loop(n_big, acc_big)
        loop(n_mid, acc_mid)
        loop(n_nat, acc_nat)


def _dilated_attention(qkv, batch):
    t = qkv.shape[0]
    tiles = t // batch // ATT_TILE
    blk = ATT_BLOCK

    def spec(col0):
        return pl.BlockSpec((ATT_TILE, LANES), lambda b, p, i: (b * tiles + i, col0 + p))

    bf = lambda *shape: pltpu.VMEM(shape, BF16)
    f32 = lambda *shape: pltpu.VMEM(shape, F32)
    return pl.pallas_call(
        _attn_kernel,
        grid=(batch, ATT_PAIRS, tiles),
        in_specs=[spec(0), spec(ATT_PAIRS), spec(2 * ATT_PAIRS)],
        out_specs=spec(0),
        out_shape=jax.ShapeDtypeStruct((t, ATT_WIDTH), BF16),
        scratch_shapes=[
            bf(2, ATT_TILE, LANES), bf(blk + ATT_TILE, LANES), bf(2, blk + ATT_TILE, LANES),
            f32(DIL_MID, MID_LEN, LANES),
            bf(2, DIL_MID, MID_LEN, LANES), bf(DIL_MID, blk + MID_LEN, LANES),
            bf(2, DIL_MID, blk + MID_LEN, LANES),
            bf(2, DIL_MAX, blk, LANES), bf(DIL_MAX, 2 * blk, LANES),
            bf(2, DIL_MAX, 2 * blk, LANES),
            f32(ATT_TILE, LANES), f32(DIL_MID, MID_LEN, LANES),
            f32(DIL_MAX, blk, LANES), f32(2, ATT_TILE, LANES),
            f32(3 * 2, 2 * blk, 2 * blk),
            f32(3 * DIL_MAX, 2 * blk, 2 * blk),
        ],
        compiler_params=_params(3),
        name="attn",
    )(qkv, qkv, qkv)


def _split3(x):
    hi = x.astype(BF16)
    r1 = x - hi.astype(F32)
    mid = r1.astype(BF16)
    lo = (r1 - mid.astype(F32)).astype(BF16)
    return hi, mid, lo


def _dot_exact_rhs(x, w):
    return sum(jnp.dot(part, w, preferred_element_type=F32) for part in _split3(x))


TM_CHUNK = 128
TM_STEPS = TM_CHUNK // SUBLANES
TM_HIST = (CONV_WIDTH - 1) * SUBLANES
assert TM_CHUNK == SSD_CHUNK


def _to_time_major(src, mid, dst, base):
    quarter = TM_CHUNK // 4
    for r in range(4):
        mid[base + quarter * r:base + quarter * (r + 1), :] = (
            src[pl.ds(base + r, quarter, stride=4), :])
    for r in range(4):
        for c in range(4):
            v = r + 4 * c
            dst[base + SUBLANES * v:base + SUBLANES * (v + 1), :] = (
                mid[pl.ds(base + quarter * r + c, SUBLANES, stride=4), :])


def _tm_time(idx):
    return TM_STEPS * (idx % SUBLANES) + idx // SUBLANES


def _time_major_perm():
    n = jnp.arange(TM_CHUNK)
    return (_tm_time(n)[None, :] == n[:, None]).astype(BF16)


def _conv_time_major(x, hist, cw_ref, cb_ref, first_row):
    width = x.shape[-1]
    groups = TM_HIST // SUBLANES
    tail = x[TM_CHUNK - TM_HIST:, :]
    down = lambda a: pltpu.roll(a.reshape(groups, SUBLANES, width), 1, 1)
    wrapped = jnp.where(first_row, down(hist), down(tail)).reshape(TM_HIST, width)
    ext = jnp.concatenate([wrapped, x], axis=0)
    k_w = CONV_WIDTH
    conv = cb_ref[...] + x * cw_ref[k_w - 1:k_w, :]
    for back in range(1, k_w):
        off = TM_HIST - SUBLANES * back
        conv = conv + ext[off:off + TM_CHUNK, :] * cw_ref[k_w - 1 - back:k_w - back, :]
    return conv, tail


def _ssd_kernel(z_ref, xbc_ref, dt_ref, cw_ref, cb_ref, dtb_ref, alog_ref, dsk_ref, nw_ref,
                triu_ref, exp_ref, y_ref, win_s, state_s):
    c = pl.program_id(0)
    q = SSD_CHUNK
    pad = CONV_PAD
    rows_step = SSD_STEP_CHUNKS * q
    heads_per_group = SSD_HEADS // SSD_GROUPS
    gw = heads_per_group * SSD_HEAD_DIM
    n_seq = z_ref.shape[0]

    @pl.when(c == 0)
    def _reset():
        win_s[:, 0:pad, :] = jnp.zeros((n_seq, pad, SSD_CONV_CH), F32)
        state_s[...] = jnp.zeros_like(state_s)

    win_s[:, pad:, :] = xbc_ref[...]

    ti = lax.broadcasted_iota(jnp.int32, (q, q), 0)
    tj = lax.broadcasted_iota(jnp.int32, (q, q), 1)
    causal = ti >= tj
    low_half = lax.broadcasted_iota(jnp.int32, (q, LANES), 1) < SSD_HEAD_DIM
    a_t = -jnp.exp(alog_ref[...])
    k_w = CONV_WIDTH

    for ci, b in ((ci, b) for ci in range(SSD_STEP_CHUNKS) for b in range(n_seq)):
        r0 = ci * q
        win_b, state_b = win_s.at[b], state_s.at[b]
        z_b, dt_b, y_b = z_ref.at[b], dt_ref.at[b], y_ref.at[b]
        conv = cb_ref[...] + win_b[pad + r0:pad + r0 + q, :] * cw_ref[k_w - 1:k_w, :]
        for k in range(k_w - 1):
            off = pad + r0 - (k_w - 1) + k
            conv = conv + win_b[off:off + q, :] * cw_ref[k:k + 1, :]
        xbc = _silu(conv)
        xs = xbc[:, :SSD_WIDTH]
        bm = xbc[:, SSD_WIDTH:SSD_WIDTH + SSD_GROUPS * SSD_STATE]
        cm = xbc[:, SSD_WIDTH + SSD_GROUPS * SSD_STATE:]

        dt_t = _softplus(dt_b[r0:r0 + q, :].T[0:SSD_HEADS, :] + dtb_ref[...])
        acs_t = _dot_exact_rhs(dt_t * a_t, triu_ref[...])
        e_t = jnp.exp(acs_t)
        w_t = jnp.exp(acs_t[:, q - 1:q] - acs_t) * dt_t
        chunk_decay = jnp.sum(e_t[:, q - 1:q] * exp_ref[...], axis=0, keepdims=True)
        cols_t = jnp.concatenate(
            [acs_t, e_t, jnp.zeros((q - 2 * SSD_HEADS, q), F32)], axis=0).T

        y_parts = []
        for g in range(SSD_GROUPS):
            bm_g = bm[:, g * SSD_STATE:(g + 1) * SSD_STATE]
            cm_g = cm[:, g * SSD_STATE:(g + 1) * SSD_STATE]
            gmat = lax.dot_general(cm_g.astype(BF16), bm_g.astype(BF16),
                                   (((1,), (1,)), ((), ())), preferred_element_type=F32)
            bm_gt = bm_g.T
            for pp in range(heads_per_group // HEADS_PER_PAIR):
                p = g * (heads_per_group // HEADS_PER_PAIR) + pp
                cols = slice(p * LANES, (p + 1) * LANES)
                x_p = xs[:, cols]
                s_p = state_b[:, cols]
                lhs_y, lhs_s, rhs_y, rhs_s = [], [], [], []
                for hh in range(HEADS_PER_PAIR):
                    h = HEADS_PER_PAIR * p + hh
                    keep = low_half if hh == 0 else ~low_half
                    seg = cols_t[:, h:h + 1] - acs_t[h:h + 1, :]
                    lmat = jnp.exp(jnp.where(causal, seg, NEG_INF))
                    scores = gmat * lmat * dt_t[h:h + 1, :]
                    c_dec = cm_g * cols_t[:, SSD_HEADS + h:SSD_HEADS + h + 1]
                    x_h = jnp.where(keep, x_p, 0.0).astype(BF16)
                    s_h = jnp.where(keep, s_p, 0.0).astype(BF16)
                    lhs_y += [scores.astype(BF16), c_dec.astype(BF16)]
                    rhs_y += [x_h, s_h]
                    lhs_s.append((bm_gt * w_t[h:h + 1, :]).astype(BF16))
                    rhs_s.append(x_h)
                y_parts.append(jnp.dot(jnp.concatenate(lhs_y, axis=1),
                                       jnp.concatenate(rhs_y, axis=0),
                                       preferred_element_type=F32))
                state_b[:, cols] = s_p * chunk_decay[:, cols] + jnp.dot(
                    jnp.concatenate(lhs_s, axis=1), jnp.concatenate(rhs_s, axis=0),
                    preferred_element_type=F32)
        y = jnp.concatenate(y_parts, axis=-1) + dsk_ref[...] * xs
        y = y * _silu(z_b[r0:r0 + q, :])
        outs = []
        for g in range(SSD_GROUPS):
            yg = y[:, g * gw:(g + 1) * gw]
            outs.append(yg * lax.rsqrt(jnp.mean(yg * yg, axis=-1, keepdims=True) + SSD_NORM_EPS))
        y_b[r0:r0 + q, :] = (jnp.concatenate(outs, axis=-1) * nw_ref[...]).astype(BF16)

    win_s[:, 0:pad, :] = win_s[:, rows_step:rows_step + pad, :]


def _prep_ssd(conv_w, conv_b, dt_bias, a_log, d_skip, norm_w):
    depth = conv_w.shape[0]
    per_time = lambda v: jnp.broadcast_to(v.astype(F32)[:, :, None],
                                          (depth, SSD_HEADS, SSD_CHUNK))
    triu = jnp.triu(jnp.ones((SSD_CHUNK, SSD_CHUNK), BF16))
    expand = (jnp.arange(SSD_HEADS)[:, None] == (jnp.arange(SSD_WIDTH)[None, :] // SSD_HEAD_DIM)
              ).astype(F32)
    d_exp = jnp.repeat(d_skip.astype(F32), SSD_HEAD_DIM, axis=1).reshape(depth, 1, SSD_WIDTH)
    return (conv_w, conv_b.reshape(depth, 1, SSD_CONV_CH), per_time(dt_bias), per_time(a_log),
            d_exp, norm_w.reshape(depth, 1, SSD_WIDTH), triu, expand)


def _ssd(z, xbc, dt, prepared, layer, batch):
    t = z.shape[0]
    seq = t // batch
    rows_step = SSD_STEP_CHUNKS * SSD_CHUNK
    row = lambda width: pl.BlockSpec((batch, rows_step, width), lambda c: (0, c, 0))
    by_seq = lambda a: a.reshape(batch, seq, a.shape[-1])
    lspec = lambda *shape: _layer_spec(shape, layer)
    out = pl.pallas_call(
        _ssd_kernel,
        grid=(seq // rows_step,),
        in_specs=[row(SSD_WIDTH), row(SSD_CONV_CH), row(DT_PAD),
                  lspec(CONV_WIDTH, SSD_CONV_CH), lspec(1, SSD_CONV_CH),
                  lspec(SSD_HEADS, SSD_CHUNK), lspec(SSD_HEADS, SSD_CHUNK),
                  lspec(1, SSD_WIDTH), lspec(1, SSD_WIDTH),
                  _const_spec((SSD_CHUNK, SSD_CHUNK)), _const_spec((SSD_HEADS, SSD_WIDTH))],
        out_specs=row(SSD_WIDTH),
        out_shape=jax.ShapeDtypeStruct((batch, seq, SSD_WIDTH), BF16),
        scratch_shapes=[pltpu.VMEM((batch, CONV_PAD + rows_step, SSD_CONV_CH), F32),
                        pltpu.VMEM((batch, SSD_STATE, SSD_WIDTH), F32)],
        compiler_params=_params(1),
        name="ssd",
    )(by_seq(z), by_seq(xbc), by_seq(dt), *prepared)
    return out.reshape(t, SSD_WIDTH)


def _gelu_tanh(x):
    c = math.sqrt(2.0 / math.pi)
    return 0.5 * x * (1.0 + jnp.tanh(c * (x + 0.044715 * (x * x * x))))


def _lru_kernel(*refs):
    ncol = LRU_WIDTH // LANES
    g_refs, x_refs = refs[:ncol], refs[ncol:2 * ncol]
    (cw_ref, cb_ref, wa_ref, ba_ref, wx_ref, bx_ref, lam_ref, perm_ref, y_ref,
     mid_s, gt_s, xt_s, hist_s, h_s) = refs[2 * ncol:]
    c = pl.program_id(1)
    tt = LRU_TILE
    n_chunks = tt // TM_CHUNK

    @pl.when(c == 0)
    def _reset():
        hist_s[...] = jnp.zeros_like(hist_s)
        h_s[...] = jnp.zeros_like(h_s)

    for j in range(ncol):
        for k in range(n_chunks):
            _to_time_major(g_refs[j], mid_s.at[j], gt_s.at[j], k * TM_CHUNK)
        for k in range(n_chunks):
            _to_time_major(x_refs[j], mid_s.at[j], xt_s.at[j], k * TM_CHUNK)

    def chunk_of(buf, k):
        rows = slice(k * TM_CHUNK, (k + 1) * TM_CHUNK)
        return jnp.concatenate([buf[j, rows, :] for j in range(ncol)], axis=1)

    first_row = lax.broadcasted_iota(
        jnp.int32, (TM_HIST // SUBLANES, SUBLANES, LRU_WIDTH), 1) == 0
    hist = hist_s[...]
    convs = []
    for k in range(n_chunks):
        conv, hist = _conv_time_major(chunk_of(xt_s, k), hist, cw_ref, cb_ref, first_row)
        convs.append(conv)
    hist_s[...] = hist
    xc = jnp.concatenate(convs, axis=0)

    xb = xc.astype(BF16)
    r = jax.nn.sigmoid(jnp.dot(xb, wa_ref[...], preferred_element_type=F32) + ba_ref[...])
    i = jax.nn.sigmoid(jnp.dot(xb, wx_ref[...], preferred_element_type=F32) + bx_ref[...])
    log_a = -LRU_C * r * _softplus(-lam_ref[...])
    a = jnp.exp(log_a)
    u = jnp.sqrt(-jnp.tanh(log_a) * (a * a + 1.0)) * (i * xc)

    sub = lax.broadcasted_iota(jnp.int32, (SUBLANES, LRU_WIDTH), 0)
    h_prev = h_s[0:1, :]
    for k in range(n_chunks):
        rows = slice(k * TM_CHUNK, (k + 1) * TM_CHUNK)
        a_k = a[rows].reshape(TM_STEPS, SUBLANES, LRU_WIDTH)
        u_k = u[rows].reshape(TM_STEPS, SUBLANES, LRU_WIDTH)
        hh, aa = [u_k[0]], [a_k[0]]
        for v in range(1, TM_STEPS):
            hh.append(a_k[v] * hh[-1] + u_k[v])
            aa.append(a_k[v] * aa[-1])
        pa, ph = aa[-1], hh[-1]
        step = 1
        while step < SUBLANES:
            keep = sub >= step
            pa_sh = jnp.where(keep, pltpu.roll(pa, step, 0), 1.0)
            ph_sh = jnp.where(keep, pltpu.roll(ph, step, 0), 0.0)
            ph = pa * ph_sh + ph
            pa = pa * pa_sh
            step *= 2
        c_in = jnp.where(sub == 0, h_prev, pltpu.roll(pa, 1, 0) * h_prev + pltpu.roll(ph, 1, 0))
        h_k = [hh[v] + aa[v] * c_in for v in range(TM_STEPS)]
        h_prev = h_k[-1][SUBLANES - 1:SUBLANES, :]
        y_k = (jnp.concatenate(h_k, axis=0) * _gelu_tanh(chunk_of(gt_s, k))).astype(BF16)
        y_ref[rows, :] = jnp.dot(perm_ref[...], y_k, preferred_element_type=F32).astype(BF16)
    h_s[...] = jnp.broadcast_to(h_prev, h_s.shape)


def _block_diag(w):
    depth, nb, c, d = w.shape
    eye = jnp.eye(nb, dtype=w.dtype)
    return (eye[None, :, None, :, None] * w[:, :, :, None, :]).reshape(depth, nb * c, nb * d)


def _prep_lru(conv_w, conv_b, wa, ba, wx, bx, lam):
    depth = conv_w.shape[0]
    vec = lambda v: v.astype(F32).reshape(depth, 1, LRU_WIDTH)
    return (conv_w, vec(conv_b), _block_diag(wa).astype(BF16), vec(ba),
            _block_diag(wx).astype(BF16), vec(bx), vec(lam))


def _lru(g_in, x_in, prepared, layer, batch):
    t = g_in.shape[0]
    s = t // batch
    nt = s // LRU_TILE
    ncol = LRU_WIDTH // LANES
    row = pl.BlockSpec((LRU_TILE, LRU_WIDTH), lambda b, c: (b * nt + c, 0))
    cols = [pl.BlockSpec((LRU_TILE, LANES), lambda b, c, j=j: (b * nt + c, j))
            for j in range(ncol)]
    mat = _layer_spec((LRU_WIDTH, LRU_WIDTH), layer)
    one = _layer_spec((1, LRU_WIDTH), layer)
    tile = lambda: pltpu.VMEM((ncol, LRU_TILE, LANES), F32)
    return pl.pallas_call(
        _lru_kernel,
        grid=(batch, nt),
        in_specs=cols + cols + [_layer_spec((CONV_WIDTH, LRU_WIDTH), layer), one, mat, one,
                                mat, one, one, _const_spec((TM_CHUNK, TM_CHUNK))],
        out_specs=row,
        out_shape=jax.ShapeDtypeStruct((t, LRU_WIDTH), BF16),
        scratch_shapes=[tile(), tile(), tile(),
                        pltpu.VMEM((TM_HIST, LRU_WIDTH), F32),
                        pltpu.VMEM((SUBLANES, LRU_WIDTH), F32)],
        compiler_params=_params(2),
        name="rglru",
    )(*([g_in] * ncol), *([x_in] * ncol), *prepared, _time_major_perm())


def _outffn_kernel(x_ref, att_ref, ssd_ref, lru_ref, wo_ref, gf_ref, wg_ref, wu_ref, wd_ref,
                   nfin_ref, o_ref, hn_s, *, final_norm):
    x1 = x_ref[...]
    for j, m_ref in enumerate((att_ref, ssd_ref, lru_ref)):
        x1 = x1 + jnp.dot(m_ref[...], wo_ref[j * ATT_WIDTH:(j + 1) * ATT_WIDTH, :],
                          preferred_element_type=F32)
    hn_s[...] = _rmsnorm(x1, gf_ref[...]).astype(BF16)
    o_ref[...] = x1
    for c0, c1 in FF_CHUNKS:
        hn = hn_s[...]
        gate = jnp.dot(hn, wg_ref[:, c0:c1], preferred_element_type=F32)
        up = jnp.dot(hn, wu_ref[:, c0:c1], preferred_element_type=F32)
        act = (_silu(gate) * up).astype(BF16)
        o_ref[...] += jnp.dot(act, wd_ref[c0:c1, :], preferred_element_type=F32)
    if final_norm:
        o_ref[...] = _rmsnorm(o_ref[...], nfin_ref[...])


def _prep_ffn(w_out, norm_ffn, w_gate, w_up, w_down, norm_final):
    depth = w_out.shape[0]
    return (w_out.astype(BF16), norm_ffn.reshape(depth, 1, D_MODEL), w_gate.astype(BF16),
            w_up.astype(BF16), w_down.astype(BF16), norm_final.reshape(1, D_MODEL))


def _outffn(x2, att, ssd, lru, prepared, layer, final_norm):
    t = x2.shape[0]
    row = lambda width: pl.BlockSpec((ROW_TILE, width), lambda i: (i, 0))
    lspec = lambda *shape: _layer_spec(shape, layer)
    return pl.pallas_call(
        functools.partial(_outffn_kernel, final_norm=final_norm),
        grid=(t // ROW_TILE,),
        in_specs=[row(D_MODEL), row(ATT_WIDTH), row(SSD_WIDTH), row(LRU_WIDTH),
                  lspec(D_MIX, D_MODEL), lspec(1, D_MODEL), lspec(D_MODEL, D_FF),
                  lspec(D_MODEL, D_FF), lspec(D_FF, D_MODEL), _const_spec((1, D_MODEL))],
        out_specs=row(D_MODEL),
        out_shape=jax.ShapeDtypeStruct((t, D_MODEL), F32),
        scratch_shapes=[pltpu.VMEM((ROW_TILE, D_MODEL), BF16)],
        compiler_params=_params(1),
        name="outffn",
    )(x2, att, ssd, lru, *prepared)


def _prep_w_in(w):
    o_dt = 3 * ATT_WIDTH + SSD_WIDTH + SSD_CONV_CH
    o_gl = o_dt + SSD_HEADS
    dt_cols = jnp.pad(w[:, :, o_dt:o_gl], ((0, 0), (0, 0), (0, DT_PAD - SSD_HEADS)))
    return jnp.concatenate([w[:, :, :o_dt], w[:, :, o_gl:], dt_cols], axis=2).astype(BF16)


def kernel(x, norm_mix, w_in, ssd_conv_w, ssd_conv_b, ssd_dt_bias, ssd_a_log, ssd_d, ssd_norm,
           lru_conv_w, lru_conv_b, lru_wa, lru_ba, lru_wx, lru_bx, lru_lambda, w_out,
           norm_ffn, w_gate, w_up, w_down, norm_final):
    batch, seq, _ = x.shape
    depth = w_in.shape[0]
    g_mix = norm_mix.reshape(depth, 1, D_MODEL)
    w_proj = _prep_w_in(w_in)
    ssd_p = _prep_ssd(ssd_conv_w, ssd_conv_b, ssd_dt_bias, ssd_a_log, ssd_d, ssd_norm)
    lru_p = _prep_lru(lru_conv_w, lru_conv_b, lru_wa, lru_ba, lru_wx, lru_bx, lru_lambda)
    ffn_p = _prep_ffn(w_out, norm_ffn, w_gate, w_up, w_down, norm_final)
    x2 = x.reshape(batch * seq, D_MODEL)
    for l in range(depth):
        qkv, z, xbc, g_lru, x_lru, dt = _inproj(x2, g_mix, w_proj, l)
        att = _dilated_attention(qkv, batch)
        ssd = _ssd(z, xbc, dt, ssd_p, l, batch)
        lru = _lru(g_lru, x_lru, lru_p, l, batch)
        x2 = _outffn(x2, att, ssd, lru, ffn_p, l, l == depth - 1)
    return x2.reshape(batch, seq, D_MODEL)
```

```python
import functools
import math

import jax
import jax.numpy as jnp
from jax import lax
from jax.experimental import pallas as pl
from jax.experimental.pallas import tpu as pltpu

F32 = jnp.float32
BF16 = jnp.bfloat16

D_MODEL = 1024
ATT_HEADS = 8
ATT_HEAD_DIM = 64
ATT_WIDTH = ATT_HEADS * ATT_HEAD_DIM
ATT_BLOCK = 128
ATT_DILATIONS = (1, 4, 16)
SSD_HEADS = 8
SSD_HEAD_DIM = 64
SSD_WIDTH = SSD_HEADS * SSD_HEAD_DIM
SSD_GROUPS = 2
SSD_STATE = 128
SSD_CHUNK = 128
SSD_CONV_CH = SSD_WIDTH + 2 * SSD_GROUPS * SSD_STATE
LRU_WIDTH = 512
LRU_BLOCKS = 8
LRU_BLOCK_W = LRU_WIDTH // LRU_BLOCKS
LRU_C = 8.0
CONV_WIDTH = 4
D_MIX = ATT_WIDTH + SSD_WIDTH + LRU_WIDTH
D_FF = 2816
NORM_EPS = 1e-6
SSD_NORM_EPS = 1e-5

LANES = 128
SUBLANES = 8
VMEM_LIMIT_BYTES = 56 * 1024 * 1024

ROW_TILE = 1024
LRU_TILE = 1024
SSD_STEP_CHUNKS = 4
DT_PAD = LANES
FF_CHUNKS = ((0, 768), (768, 1536), (1536, 2304), (2304, 2816))
CONV_PAD = SUBLANES

DIL_MID, DIL_MAX = ATT_DILATIONS[1], ATT_DILATIONS[2]
DIL_STEP = DIL_MAX // DIL_MID
ATT_TILE = ATT_BLOCK * DIL_MAX
MID_LEN = ATT_TILE // DIL_MID
HEADS_PER_PAIR = LANES // ATT_HEAD_DIM
ATT_PAIRS = ATT_WIDTH // LANES
ATT_UNROLL = 16
assert ATT_HEADS == 8 and HEADS_PER_PAIR == 2 and DIL_STEP == DIL_MID and ATT_DILATIONS[0] == 1

LOG2E = 1.4426950408889634
NEG_INF = float("-inf")


def _params(n_axes):
    return pltpu.CompilerParams(
        dimension_semantics=("arbitrary",) * n_axes,
        vmem_limit_bytes=VMEM_LIMIT_BYTES)


def _const_spec(shape):
    nd = len(shape)
    return pl.BlockSpec(shape, lambda *_: (0,) * nd, pipeline_mode=pl.Buffered(1))


def _layer_spec(shape, layer):
    nd = len(shape)
    return pl.BlockSpec((None,) + tuple(shape), lambda *_: (layer,) + (0,) * nd,
                        pipeline_mode=pl.Buffered(1))


def _rmsnorm(x, g):
    return x * lax.rsqrt(jnp.mean(x * x, axis=-1, keepdims=True) + NORM_EPS) * g


def _softplus(x):
    return jnp.maximum(x, 0.0) + jnp.log1p(jnp.exp(-jnp.abs(x)))


def _silu(x):
    return x * jax.nn.sigmoid(x)


_C_QKV = 3 * ATT_WIDTH
_C_Z = _C_QKV + SSD_WIDTH
_C_XBC = _C_Z + SSD_CONV_CH
_C_GL = _C_XBC + LRU_WIDTH
_C_XL = _C_GL + LRU_WIDTH
_C_DT = _C_XL + DT_PAD


def _inproj_kernel(x_ref, g_ref, w_ref, qkv_ref, z_ref, xbc_ref, gl_ref, xl_ref, dt_ref):
    h = _rmsnorm(x_ref[...], g_ref[...]).astype(BF16)

    def seg(a, b):
        return jnp.dot(h, w_ref[:, a:b], preferred_element_type=F32)

    qkv_ref[:, 0:ATT_WIDTH] = seg(0, ATT_WIDTH) * (ATT_HEAD_DIM ** -0.5 * LOG2E)
    qkv_ref[:, ATT_WIDTH:_C_QKV] = seg(ATT_WIDTH, _C_QKV)
    z_ref[...] = seg(_C_QKV, _C_Z)
    xbc_ref[...] = seg(_C_Z, _C_XBC)
    gl_ref[...] = seg(_C_XBC, _C_GL)
    xl_ref[...] = seg(_C_GL, _C_XL)
    dt_ref[...] = seg(_C_XL, _C_DT)


def _inproj(x2, g, w, layer):
    t = x2.shape[0]
    row = lambda width: pl.BlockSpec((ROW_TILE, width), lambda i: (i, 0))
    widths = (_C_QKV, SSD_WIDTH, SSD_CONV_CH, LRU_WIDTH, LRU_WIDTH, DT_PAD)
    return pl.pallas_call(
        _inproj_kernel,
        grid=(t // ROW_TILE,),
        in_specs=[row(D_MODEL), _layer_spec((1, D_MODEL), layer),
                  _layer_spec((D_MODEL, _C_DT), layer)],
        out_specs=[row(wd) for wd in widths],
        out_shape=[jax.ShapeDtypeStruct((t, wd), F32) for wd in widths],
        compiler_params=_params(1),
        name="inproj",
    )(x2, g, w)


def _scores(q_lo, q_hi, kk, bias):
    q_both = jnp.concatenate([q_lo, q_hi], axis=0)
    return lax.dot_general(q_both, kk, (((1,), (1,)), ((), ())),
                           preferred_element_type=F32) + bias


def _attn_kernel(q_ref, k_ref, v_ref, att_ref,
                 qn_s, kn_s, vn_s, tmp_s, qm_s, km_s, vm_s, qx_s, kx_s, vx_s,
                 mn_s, mm_s, mx_s, acc_s, bias_s, sc_s):
    pair = pl.program_id(1)
    tile = pl.program_id(2)
    blk = ATT_BLOCK
    first_tile = tile == 0
    mid_blocks = MID_LEN // blk

    @pl.when(first_tile)
    def _start_sequence():
        qi = lax.broadcasted_iota(jnp.int32, (blk, 2 * blk), 0)
        ki = lax.broadcasted_iota(jnp.int32, (blk, 2 * blk), 1)
        dist = blk + qi - ki
        band = (dist >= 0) & (dist <= blk)
        band_first = band & (ki >= blk)
        for hh in range(HEADS_PER_PAIR):
            expo = (126 - HEADS_PER_PAIR * pair - hh) << 23
            slope = lax.bitcast_convert_type(jnp.full((blk, 2 * blk), expo, jnp.int32), F32)
            rows = slice(hh * blk, (hh + 1) * blk)
            for br, dil in enumerate((DIL_MAX, DIL_MID, 1)):
                alibi = (-slope * (dil * dist).astype(F32)) * LOG2E
                bias_s[2 * br, rows, :] = jnp.where(band, alibi, NEG_INF)
                bias_s[2 * br + 1, rows, :] = jnp.where(band_first, alibi, NEG_INF)
        kn_s[0:blk, :] = jnp.zeros((blk, LANES), BF16)
        vn_s[:, 0:blk, :] = jnp.zeros((2, blk, LANES), BF16)
        km_s[:, 0:blk, :] = jnp.zeros((DIL_MID, blk, LANES), BF16)
        vm_s[:, :, 0:blk, :] = jnp.zeros((2, DIL_MID, blk, LANES), BF16)
        kx_s[:, 0:blk, :] = jnp.zeros((DIL_MAX, blk, LANES), BF16)
        vx_s[:, :, 0:blk, :] = jnp.zeros((2, DIL_MAX, blk, LANES), BF16)

    @pl.when(tile > 0)
    def _carry_history():
        kn_s[0:blk, :] = kn_s[ATT_TILE:ATT_TILE + blk, :]
        vn_s[:, 0:blk, :] = vn_s[:, ATT_TILE:ATT_TILE + blk, :]
        km_s[:, 0:blk, :] = km_s[:, MID_LEN:MID_LEN + blk, :]
        vm_s[:, :, 0:blk, :] = vm_s[:, :, MID_LEN:MID_LEN + blk, :]
        kx_s[:, 0:blk, :] = kx_s[:, blk:2 * blk, :]
        vx_s[:, :, 0:blk, :] = vx_s[:, :, blk:2 * blk, :]

    lane_m = lax.broadcasted_iota(jnp.int32, (MID_LEN, LANES), 1) < ATT_HEAD_DIM
    lane_b = lax.broadcasted_iota(jnp.int32, (blk, LANES), 1) < ATT_HEAD_DIM

    def put_q(dst, idx, x, mask):
        dst[(0,) + idx] = jnp.where(mask, x, 0.0).astype(BF16)
        dst[(1,) + idx] = jnp.where(mask, 0.0, x).astype(BF16)

    def put_k(dst, idx, x, mask):
        del mask
        dst[idx] = x.astype(BF16)

    def put_v(dst, idx, x, mask):
        dst[(0,) + idx] = jnp.where(mask, x, 1.0).astype(BF16)
        dst[(1,) + idx] = jnp.where(mask, 1.0, x).astype(BF16)

    for src, nat, mid, big, put, hist in (
            (q_ref, qn_s, qm_s, qx_s, put_q, 0),
            (k_ref, kn_s, km_s, kx_s, put_k, blk),
            (v_ref, vn_s, vm_s, vx_s, put_v, blk)):
        for c in range(DIL_MID):
            put(nat, (slice(hist + c * MID_LEN, hist + (c + 1) * MID_LEN), slice(None)),
                src[c * MID_LEN:(c + 1) * MID_LEN, :], lane_m)
        for r in range(DIL_MID):
            x = src[pl.ds(r, MID_LEN, stride=DIL_MID), :]
            tmp_s[r] = x
            put(mid, (r, slice(hist, hist + MID_LEN), slice(None)), x, lane_m)
        for r in range(DIL_MID):
            for c in range(DIL_STEP):
                y = tmp_s.at[r][pl.ds(c, blk, stride=DIL_STEP), :]
                put(big, (r + DIL_MID * c, slice(hist, hist + blk), slice(None)), y, lane_b)

    def bias_of(branch, is_first):
        return bias_s[2 * branch + jnp.where(is_first, 1, 0)]

    def big_ops(r):
        return (qx_s[0, r], qx_s[1, r], kx_s[r], bias_of(0, first_tile),
                pl.ds(r, blk, stride=DIL_MAX))

    def mid_ops(i):
        r = i // mid_blocks
        j = i % mid_blocks
        q0 = pl.multiple_of(j * blk, blk)
        return (qm_s[0, r, pl.ds(q0, blk), :], qm_s[1, r, pl.ds(q0, blk), :],
                km_s[r, pl.ds(q0, 2 * blk), :], bias_of(1, first_tile & (j == 0)),
                pl.ds(j * (blk * DIL_MID) + r, blk, stride=DIL_MID))

    def nat_ops(j):
        q0 = pl.multiple_of(j * blk, blk)
        return (qn_s[0, pl.ds(q0, blk), :], qn_s[1, pl.ds(q0, blk), :],
                kn_s[pl.ds(q0, 2 * blk), :], bias_of(2, first_tile & (j == 0)),
                pl.ds(q0, blk))

    def loop(n, body):
        lax.fori_loop(0, n, lambda i, c: (body(i), c)[1], 0, unroll=ATT_UNROLL)

    n_big, n_mid, n_nat = DIL_MAX, DIL_MID * mid_blocks, ATT_TILE // blk

    def row_max(ops, slot):
        q_lo, q_hi, kk, bias, rows = ops
        s = _scores(q_lo, q_hi, kk, bias)
        sc_s[slot] = s
        m = jnp.broadcast_to(jnp.max(s, axis=-1, keepdims=True), (2 * blk, LANES))
        return jnp.where(lane_b, m[:blk], m[blk:]), rows

    def max_big(r):
        m, rows = row_max(big_ops(r), r)
        mn_s[rows, :] = m

    def max_more(ops, slot):
        m, rows = row_max(ops, slot)
        mn_s[rows, :] = jnp.maximum(mn_s[rows, :], m)

    loop(n_big, max_big)
    loop(n_mid, lambda i: max_more(mid_ops(i), n_big + i))
    loop(n_nat, lambda j: max_more(nat_ops(j), n_big + n_mid + j))

    for r in range(DIL_MID):
        mm_s[r] = mn_s[pl.ds(r, MID_LEN, stride=DIL_MID), :]
    for r in range(DIL_MID):
        for c in range(DIL_STEP):
            mx_s[r + DIL_MID * c] = mm_s.at[r][pl.ds(c, blk, stride=DIL_STEP), :]

    def num_den(slot, m_pair, v_lo, v_hi):
        swapped = pltpu.roll(m_pair, ATT_HEAD_DIM, 1)
        m = jnp.concatenate([jnp.where(lane_b, m_pair, swapped),
                             jnp.where(lane_b, swapped, m_pair)], axis=0)
        e = jnp.exp2(sc_s[slot] - jnp.concatenate([m, m], axis=1)).astype(BF16)
        return (jnp.dot(e[:blk], v_lo, preferred_element_type=F32),
                jnp.dot(e[blk:], v_hi, preferred_element_type=F32))

    def acc_big(r):
        o_lo, o_hi = num_den(r, mx_s[r], vx_s[0, r], vx_s[1, r])
        rows = pl.ds(r, blk, stride=DIL_MAX)
        acc_s[0, rows, :] = o_lo
        acc_s[1, rows, :] = o_hi

    def acc_mid(i):
        r = i // mid_blocks
        j = i % mid_blocks
        q0 = pl.multiple_of(j * blk, blk)
        o_lo, o_hi = num_den(
            n_big + i, mm_s[r, pl.ds(q0, blk), :],
            vm_s[0, r, pl.ds(q0, 2 * blk), :], vm_s[1, r, pl.ds(q0, 2 * blk), :])
        rows = pl.ds(j * (blk * DIL_MID) + r, blk, stride=DIL_MID)
        acc_s[0, rows, :] += o_lo
        acc_s[1, rows, :] += o_hi

    def acc_nat(j):
        q0 = pl.multiple_of(j * blk, blk)
        rows = pl.ds(q0, blk)
        o_lo, o_hi = num_den(
            n_big + n_mid + j, mn_s[rows, :],
            vn_s[0, pl.ds(q0, 2 * blk), :], vn_s[1, pl.ds(q0, 2 * blk), :])
        a_lo = acc_s[0, rows, :] + o_lo
        a_hi = acc_s[1, rows, :] + o_hi
        num = jnp.where(lane_b, a_lo, a_hi)
        den = pltpu.roll(jnp.where(lane_b, a_hi, a_lo), ATT_HEAD_DIM, 1)
        att_ref[rows, :] = (num / den).astype(BF16)

    @pl.when(tile >= 0)
    def _pass_two():
        loop(n_big, acc_big)
        loop(n_mid, acc_mid)
        loop(n_nat, acc_nat)


def _dilated_attention(qkv, batch):
    t = qkv.shape[0]
    tiles = t // batch // ATT_TILE
    blk = ATT_BLOCK

    def spec(col0):
        return pl.BlockSpec((ATT_TILE, LANES), lambda b, p, i: (b * tiles + i, col0 + p))

    bf = lambda *shape: pltpu.VMEM(shape, BF16)
    f32 = lambda *shape: pltpu.VMEM(shape, F32)
    return pl.pallas_call(
        _attn_kernel,
        grid=(batch, ATT_PAIRS, tiles),
        in_specs=[spec(0), spec(ATT_PAIRS), spec(2 * ATT_PAIRS)],
        out_specs=spec(0),
        out_shape=jax.ShapeDtypeStruct((t, ATT_WIDTH), BF16),
        scratch_shapes=[
            bf(2, ATT_TILE, LANES), bf(blk + ATT_TILE, LANES), bf(2, blk + ATT_TILE, LANES),
            f32(DIL_MID, MID_LEN, LANES),
            bf(2, DIL_MID, MID_LEN, LANES), bf(DIL_MID, blk + MID_LEN, LANES),
            bf(2, DIL_MID, blk + MID_LEN, LANES),
            bf(2, DIL_MAX, blk, LANES), bf(DIL_MAX, 2 * blk, LANES),
            bf(2, DIL_MAX, 2 * blk, LANES),
            f32(ATT_TILE, LANES), f32(DIL_MID, MID_LEN, LANES),
            f32(DIL_MAX, blk, LANES), f32(2, ATT_TILE, LANES),
            f32(3 * 2, 2 * blk, 2 * blk),
            f32(3 * DIL_MAX, 2 * blk, 2 * blk),
        ],
        compiler_params=_params(3),
        name="attn",
    )(qkv, qkv, qkv)


def _split3(x):
    hi = x.astype(BF16)
    r1 = x - hi.astype(F32)
    mid = r1.astype(BF16)
    lo = (r1 - mid.astype(F32)).astype(BF16)
    return hi, mid, lo


def _dot_exact_rhs(x, w):
    return sum(jnp.dot(part, w, preferred_element_type=F32) for part in _split3(x))


TM_CHUNK = 128
TM_STEPS = TM_CHUNK // SUBLANES
TM_HIST = (CONV_WIDTH - 1) * SUBLANES
assert TM_CHUNK == SSD_CHUNK


def _to_time_major(src, mid, dst, base):
    quarter = TM_CHUNK // 4
    for r in range(4):
        mid[base + quarter * r:base + quarter * (r + 1), :] = (
            src[pl.ds(base + r, quarter, stride=4), :])
    for r in range(4):
        for c in range(4):
            v = r + 4 * c
            dst[base + SUBLANES * v:base + SUBLANES * (v + 1), :] = (
                mid[pl.ds(base + quarter * r + c, SUBLANES, stride=4), :])


def _tm_time(idx):
    return TM_STEPS * (idx % SUBLANES) + idx // SUBLANES


def _time_major_perm():
    n = jnp.arange(TM_CHUNK)
    return (_tm_time(n)[None, :] == n[:, None]).astype(BF16)


def _conv_time_major(x, hist, cw_ref, cb_ref, first_row):
    width = x.shape[-1]
    groups = TM_HIST // SUBLANES
    tail = x[TM_CHUNK - TM_HIST:, :]
    down = lambda a: pltpu.roll(a.reshape(groups, SUBLANES, width), 1, 1)
    wrapped = jnp.where(first_row, down(hist), down(tail)).reshape(TM_HIST, width)
    ext = jnp.concatenate([wrapped, x], axis=0)
    k_w = CONV_WIDTH
    conv = cb_ref[...] + x * cw_ref[k_w - 1:k_w, :]
    for back in range(1, k_w):
        off = TM_HIST - SUBLANES * back
        conv = conv + ext[off:off + TM_CHUNK, :] * cw_ref[k_w - 1 - back:k_w - back, :]
    return conv, tail


def _ssd_kernel(z_ref, xbc_ref, dt_ref, cw_ref, cb_ref, dtb_ref, alog_ref, dsk_ref, nw_ref,
                triu_ref, exp_ref, y_ref, win_s, state_s):
    c = pl.program_id(0)
    q = SSD_CHUNK
    pad = CONV_PAD
    rows_step = SSD_STEP_CHUNKS * q
    heads_per_group = SSD_HEADS // SSD_GROUPS
    gw = heads_per_group * SSD_HEAD_DIM
    n_seq = z_ref.shape[0]

    @pl.when(c == 0)
    def _reset():
        win_s[:, 0:pad, :] = jnp.zeros((n_seq, pad, SSD_CONV_CH), F32)
        state_s[...] = jnp.zeros_like(state_s)

    win_s[:, pad:, :] = xbc_ref[...]

    ti = lax.broadcasted_iota(jnp.int32, (q, q), 0)
    tj = lax.broadcasted_iota(jnp.int32, (q, q), 1)
    causal = ti >= tj
    low_half = lax.broadcasted_iota(jnp.int32, (q, LANES), 1) < SSD_HEAD_DIM
    a_t = -jnp.exp(alog_ref[...])
    k_w = CONV_WIDTH

    for ci, b in ((ci, b) for ci in range(SSD_STEP_CHUNKS) for b in range(n_seq)):
        r0 = ci * q
        win_b, state_b = win_s.at[b], state_s.at[b]
        z_b, dt_b, y_b = z_ref.at[b], dt_ref.at[b], y_ref.at[b]
        conv = cb_ref[...] + win_b[pad + r0:pad + r0 + q, :] * cw_ref[k_w - 1:k_w, :]
        for k in range(k_w - 1):
            off = pad + r0 - (k_w - 1) + k
            conv = conv + win_b[off:off + q, :] * cw_ref[k:k + 1, :]
        xbc = _silu(conv)
        xs = xbc[:, :SSD_WIDTH]
        bm = xbc[:, SSD_WIDTH:SSD_WIDTH + SSD_GROUPS * SSD_STATE]
        cm = xbc[:, SSD_WIDTH + SSD_GROUPS * SSD_STATE:]

        dt_t = _softplus(dt_b[r0:r0 + q, :].T[0:SSD_HEADS, :] + dtb_ref[...])
        acs_t = _dot_exact_rhs(dt_t * a_t, triu_ref[...])
        e_t = jnp.exp(acs_t)
        w_t = jnp.exp(acs_t[:, q - 1:q] - acs_t) * dt_t
        chunk_decay = jnp.sum(e_t[:, q - 1:q] * exp_ref[...], axis=0, keepdims=True)
        cols_t = jnp.concatenate(
            [acs_t, e_t, jnp.zeros((q - 2 * SSD_HEADS, q), F32)], axis=0).T

        y_parts = []
        for g in range(SSD_GROUPS):
            bm_g = bm[:, g * SSD_STATE:(g + 1) * SSD_STATE]
            cm_g = cm[:, g * SSD_STATE:(g + 1) * SSD_STATE]
            gmat = lax.dot_general(cm_g.astype(BF16), bm_g.astype(BF16),
                                   (((1,), (1,)), ((), ())), preferred_element_type=F32)
            bm_gt = bm_g.T
            for pp in range(heads_per_group // HEADS_PER_PAIR):
                p = g * (heads_per_group // HEADS_PER_PAIR) + pp
                cols = slice(p * LANES, (p + 1) * LANES)
                x_p = xs[:, cols]
                s_p = state_b[:, cols]
                lhs_y, lhs_s, rhs_y, rhs_s = [], [], [], []
                for hh in range(HEADS_PER_PAIR):
                    h = HEADS_PER_PAIR * p + hh
                    keep = low_half if hh == 0 else ~low_half
                    seg = cols_t[:, h:h + 1] - acs_t[h:h + 1, :]
                    lmat = jnp.exp(jnp.where(causal, seg, NEG_INF))
                    scores = gmat * lmat * dt_t[h:h + 1, :]
                    c_dec = cm_g * cols_t[:, SSD_HEADS + h:SSD_HEADS + h + 1]
                    x_h = jnp.where(keep, x_p, 0.0).astype(BF16)
                    s_h = jnp.where(keep, s_p, 0.0).astype(BF16)
                    lhs_y += [scores.astype(BF16), c_dec.astype(BF16)]
                    rhs_y += [x_h, s_h]
                    lhs_s.append((bm_gt * w_t[h:h + 1, :]).astype(BF16))
                    rhs_s.append(x_h)
                y_parts.append(jnp.dot(jnp.concatenate(lhs_y, axis=1),
                                       jnp.concatenate(rhs_y, axis=0),
                                       preferred_element_type=F32))
                state_b[:, cols] = s_p * chunk_decay[:, cols] + jnp.dot(
                    jnp.concatenate(lhs_s, axis=1), jnp.concatenate(rhs_s, axis=0),
                    preferred_element_type=F32)
        y = jnp.concatenate(y_parts, axis=-1) + dsk_ref[...] * xs
        y = y * _silu(z_b[r0:r0 + q, :])
        outs = []
        for g in range(SSD_GROUPS):
            yg = y[:, g * gw:(g + 1) * gw]
            outs.append(yg * lax.rsqrt(jnp.mean(yg * yg, axis=-1, keepdims=True) + SSD_NORM_EPS))
        y_b[r0:r0 + q, :] = (jnp.concatenate(outs, axis=-1) * nw_ref[...]).astype(BF16)

    win_s[:, 0:pad, :] = win_s[:, rows_step:rows_step + pad, :]


def _prep_ssd(conv_w, conv_b, dt_bias, a_log, d_skip, norm_w):
    depth = conv_w.shape[0]
    per_time = lambda v: jnp.broadcast_to(v.astype(F32)[:, :, None],
                                          (depth, SSD_HEADS, SSD_CHUNK))
    triu = jnp.triu(jnp.ones((SSD_CHUNK, SSD_CHUNK), BF16))
    expand = (jnp.arange(SSD_HEADS)[:, None] == (jnp.arange(SSD_WIDTH)[None, :] // SSD_HEAD_DIM)
              ).astype(F32)
    d_exp = jnp.repeat(d_skip.astype(F32), SSD_HEAD_DIM, axis=1).reshape(depth, 1, SSD_WIDTH)
    return (conv_w, conv_b.reshape(depth, 1, SSD_CONV_CH), per_time(dt_bias), per_time(a_log),
            d_exp, norm_w.reshape(depth, 1, SSD_WIDTH), triu, expand)


def _ssd(z, xbc, dt, prepared, layer, batch):
    t = z.shape[0]
    seq = t // batch
    rows_step = SSD_STEP_CHUNKS * SSD_CHUNK
    row = lambda width: pl.BlockSpec((batch, rows_step, width), lambda c: (0, c, 0))
    by_seq = lambda a: a.reshape(batch, seq, a.shape[-1])
    lspec = lambda *shape: _layer_spec(shape, layer)
    out = pl.pallas_call(
        _ssd_kernel,
        grid=(seq // rows_step,),
        in_specs=[row(SSD_WIDTH), row(SSD_CONV_CH), row(DT_PAD),
                  lspec(CONV_WIDTH, SSD_CONV_CH), lspec(1, SSD_CONV_CH),
                  lspec(SSD_HEADS, SSD_CHUNK), lspec(SSD_HEADS, SSD_CHUNK),
                  lspec(1, SSD_WIDTH), lspec(1, SSD_WIDTH),
                  _const_spec((SSD_CHUNK, SSD_CHUNK)), _const_spec((SSD_HEADS, SSD_WIDTH))],
        out_specs=row(SSD_WIDTH),
        out_shape=jax.ShapeDtypeStruct((batch, seq, SSD_WIDTH), BF16),
        scratch_shapes=[pltpu.VMEM((batch, CONV_PAD + rows_step, SSD_CONV_CH), F32),
                        pltpu.VMEM((batch, SSD_STATE, SSD_WIDTH), F32)],
        compiler_params=_params(1),
        name="ssd",
    )(by_seq(z), by_seq(xbc), by_seq(dt), *prepared)
    return out.reshape(t, SSD_WIDTH)


def _gelu_tanh(x):
    c = math.sqrt(2.0 / math.pi)
    return 0.5 * x * (1.0 + jnp.tanh(c * (x + 0.044715 * (x * x * x))))


def _lru_kernel(*refs):
    ncol = LRU_WIDTH // LANES
    g_refs, x_refs = refs[:ncol], refs[ncol:2 * ncol]
    (cw_ref, cb_ref, wa_ref, ba_ref, wx_ref, bx_ref, lam_ref, perm_ref, y_ref,
     mid_s, gt_s, xt_s, hist_s, h_s) = refs[2 * ncol:]
    c = pl.program_id(1)
    tt = LRU_TILE
    n_chunks = tt // TM_CHUNK

    @pl.when(c == 0)
    def _reset():
        hist_s[...] = jnp.zeros_like(hist_s)
        h_s[...] = jnp.zeros_like(h_s)

    for j in range(ncol):
        for k in range(n_chunks):
            _to_time_major(g_refs[j], mid_s.at[j], gt_s.at[j], k * TM_CHUNK)
        for k in range(n_chunks):
            _to_time_major(x_refs[j], mid_s.at[j], xt_s.at[j], k * TM_CHUNK)

    def chunk_of(buf, k):
        rows = slice(k * TM_CHUNK, (k + 1) * TM_CHUNK)
        return jnp.concatenate([buf[j, rows, :] for j in range(ncol)], axis=1)

    first_row = lax.broadcasted_iota(
        jnp.int32, (TM_HIST // SUBLANES, SUBLANES, LRU_WIDTH), 1) == 0
    hist = hist_s[...]
    convs = []
    for k in range(n_chunks):
        conv, hist = _conv_time_major(chunk_of(xt_s, k), hist, cw_ref, cb_ref, first_row)
        convs.append(conv)
    hist_s[...] = hist
    xc = jnp.concatenate(convs, axis=0)

    xb = xc.astype(BF16)
    r = jax.nn.sigmoid(jnp.dot(xb, wa_ref[...], preferred_element_type=F32) + ba_ref[...])
    i = jax.nn.sigmoid(jnp.dot(xb, wx_ref[...], preferred_element_type=F32) + bx_ref[...])
    log_a = -LRU_C * r * _softplus(-lam_ref[...])
    a = jnp.exp(log_a)
    u = jnp.sqrt(-jnp.tanh(log_a) * (a * a + 1.0)) * (i * xc)

    sub = lax.broadcasted_iota(jnp.int32, (SUBLANES, LRU_WIDTH), 0)
    h_prev = h_s[0:1, :]
    for k in range(n_chunks):
        rows = slice(k * TM_CHUNK, (k + 1) * TM_CHUNK)
        a_k = a[rows].reshape(TM_STEPS, SUBLANES, LRU_WIDTH)
        u_k = u[rows].reshape(TM_STEPS, SUBLANES, LRU_WIDTH)
        hh, aa = [u_k[0]], [a_k[0]]
        for v in range(1, TM_STEPS):
            hh.append(a_k[v] * hh[-1] + u_k[v])
            aa.append(a_k[v] * aa[-1])
        pa, ph = aa[-1], hh[-1]
        step = 1
        while step < SUBLANES:
            keep = sub >= step
            pa_sh = jnp.where(keep, pltpu.roll(pa, step, 0), 1.0)
            ph_sh = jnp.where(keep, pltpu.roll(ph, step, 0), 0.0)
            ph = pa * ph_sh + ph
            pa = pa * pa_sh
            step *= 2
        c_in = jnp.where(sub == 0, h_prev, pltpu.roll(pa, 1, 0) * h_prev + pltpu.roll(ph, 1, 0))
        h_k = [hh[v] + aa[v] * c_in for v in range(TM_STEPS)]
        h_prev = h_k[-1][SUBLANES - 1:SUBLANES, :]
        y_k = (jnp.concatenate(h_k, axis=0) * _gelu_tanh(chunk_of(gt_s, k))).astype(BF16)
        y_ref[rows, :] = jnp.dot(perm_ref[...], y_k, preferred_element_type=F32).astype(BF16)
    h_s[...] = jnp.broadcast_to(h_prev, h_s.shape)


def _block_diag(w):
    depth, nb, c, d = w.shape
    eye = jnp.eye(nb, dtype=w.dtype)
    return (eye[None, :, None, :, None] * w[:, :, :, None, :]).reshape(depth, nb * c, nb * d)


def _prep_lru(conv_w, conv_b, wa, ba, wx, bx, lam):
    depth = conv_w.shape[0]
    vec = lambda v: v.astype(F32).reshape(depth, 1, LRU_WIDTH)
    return (conv_w, vec(conv_b), _block_diag(wa).astype(BF16), vec(ba),
            _block_diag(wx).astype(BF16), vec(bx), vec(lam))


def _lru(g_in, x_in, prepared, layer, batch):
    t = g_in.shape[0]
    s = t // batch
    nt = s // LRU_TILE
    ncol = LRU_WIDTH // LANES
    row = pl.BlockSpec((LRU_TILE, LRU_WIDTH), lambda b, c: (b * nt + c, 0))
    cols = [pl.BlockSpec((LRU_TILE, LANES), lambda b, c, j=j: (b * nt + c, j))
            for j in range(ncol)]
    mat = _layer_spec((LRU_WIDTH, LRU_WIDTH), layer)
    one = _layer_spec((1, LRU_WIDTH), layer)
    tile = lambda: pltpu.VMEM((ncol, LRU_TILE, LANES), F32)
    return pl.pallas_call(
        _lru_kernel,
        grid=(batch, nt),
        in_specs=cols + cols + [_layer_spec((CONV_WIDTH, LRU_WIDTH), layer), one, mat, one,
                                mat, one, one, _const_spec((TM_CHUNK, TM_CHUNK))],
        out_specs=row,
        out_shape=jax.ShapeDtypeStruct((t, LRU_WIDTH), BF16),
        scratch_shapes=[tile(), tile(), tile(),
                        pltpu.VMEM((TM_HIST, LRU_WIDTH), F32),
                        pltpu.VMEM((SUBLANES, LRU_WIDTH), F32)],
        compiler_params=_params(2),
        name="rglru",
    )(*([g_in] * ncol), *([x_in] * ncol), *prepared, _time_major_perm())


def _outffn_kernel(x_ref, att_ref, ssd_ref, lru_ref, wo_ref, gf_ref, wg_ref, wu_ref, wd_ref,
                   nfin_ref, o_ref, hn_s, *, final_norm):
    x1 = x_ref[...]
    for j, m_ref in enumerate((att_ref, ssd_ref, lru_ref)):
        x1 = x1 + jnp.dot(m_ref[...], wo_ref[j * ATT_WIDTH:(j + 1) * ATT_WIDTH, :],
                          preferred_element_type=F32)
    hn_s[...] = _rmsnorm(x1, gf_ref[...]).astype(BF16)
    o_ref[...] = x1
    for c0, c1 in FF_CHUNKS:
        hn = hn_s[...]
        gate = jnp.dot(hn, wg_ref[:, c0:c1], preferred_element_type=F32)
        up = jnp.dot(hn, wu_ref[:, c0:c1], preferred_element_type=F32)
        act = (_silu(gate) * up).astype(BF16)
        o_ref[...] += jnp.dot(act, wd_ref[c0:c1, :], preferred_element_type=F32)
    if final_norm:
        o_ref[...] = _rmsnorm(o_ref[...], nfin_ref[...])


def _prep_ffn(w_out, norm_ffn, w_gate, w_up, w_down, norm_final):
    depth = w_out.shape[0]
    return (w_out.astype(BF16), norm_ffn.reshape(depth, 1, D_MODEL), w_gate.astype(BF16),
            w_up.astype(BF16), w_down.astype(BF16), norm_final.reshape(1, D_MODEL))


def _outffn(x2, att, ssd, lru, prepared, layer, final_norm):
    t = x2.shape[0]
    row = lambda width: pl.BlockSpec((ROW_TILE, width), lambda i: (i, 0))
    lspec = lambda *shape: _layer_spec(shape, layer)
    return pl.pallas_call(
        functools.partial(_outffn_kernel, final_norm=final_norm),
        grid=(t // ROW_TILE,),
        in_specs=[row(D_MODEL), row(ATT_WIDTH), row(SSD_WIDTH), row(LRU_WIDTH),
                  lspec(D_MIX, D_MODEL), lspec(1, D_MODEL), lspec(D_MODEL, D_FF),
                  lspec(D_MODEL, D_FF), lspec(D_FF, D_MODEL), _const_spec((1, D_MODEL))],
        out_specs=row(D_MODEL),
        out_shape=jax.ShapeDtypeStruct((t, D_MODEL), F32),
        scratch_shapes=[pltpu.VMEM((ROW_TILE, D_MODEL), BF16)],
        compiler_params=_params(1),
        name="outffn",
    )(x2, att, ssd, lru, *prepared)


def _prep_w_in(w):
    o_dt = 3 * ATT_WIDTH + SSD_WIDTH + SSD_CONV_CH
    o_gl = o_dt + SSD_HEADS
    w = w.astype(BF16)
    dt_cols = jnp.pad(w[:, :, o_dt:o_gl], ((0, 0), (0, 0), (0, DT_PAD - SSD_HEADS)))
    return jnp.concatenate([w[:, :, :o_dt], w[:, :, o_gl:], dt_cols], axis=2)


def kernel(x, norm_mix, w_in, ssd_conv_w, ssd_conv_b, ssd_dt_bias, ssd_a_log, ssd_d, ssd_norm,
           lru_conv_w, lru_conv_b, lru_wa, lru_ba, lru_wx, lru_bx, lru_lambda, w_out,
           norm_ffn, w_gate, w_up, w_down, norm_final):
    batch, seq, _ = x.shape
    depth = w_in.shape[0]
    g_mix = norm_mix.reshape(depth, 1, D_MODEL)
    w_proj = _prep_w_in(w_in)
    ssd_p = _prep_ssd(ssd_conv_w, ssd_conv_b, ssd_dt_bias, ssd_a_log, ssd_d, ssd_norm)
    lru_p = _prep_lru(lru_conv_w, lru_conv_b, lru_wa, lru_ba, lru_wx, lru_bx, lru_lambda)
    ffn_p = _prep_ffn(w_out, norm_ffn, w_gate, w_up, w_down, norm_final)
    x2 = x.reshape(batch * seq, D_MODEL)
    for l in range(depth):
        qkv, z, xbc, g_lru, x_lru, dt = _inproj(x2, g_mix, w_proj, l)
        att = _dilated_attention(qkv, batch)
        ssd = _ssd(z, xbc, dt, ssd_p, l, batch)
        lru = _lru(g_lru, x_lru, lru_p, l, batch)
        x2 = _outffn(x2, att, ssd, lru, ffn_p, l, l == depth - 1)
    return x2.reshape(batch, seq, D_MODEL)
```

```python
import functools
import math

import jax
import jax.numpy as jnp
from jax import lax
from jax.experimental import pallas as pl
from jax.experimental.pallas import tpu as pltpu

F32 = jnp.float32
BF16 = jnp.bfloat16

D_MODEL = 1024
ATT_HEADS = 8
ATT_HEAD_DIM = 64
ATT_WIDTH = ATT_HEADS * ATT_HEAD_DIM
ATT_BLOCK = 128
ATT_DILATIONS = (1, 4, 16)
SSD_HEADS = 8
SSD_HEAD_DIM = 64
SSD_WIDTH = SSD_HEADS * SSD_HEAD_DIM
SSD_GROUPS = 2
SSD_STATE = 128
SSD_CHUNK = 128
SSD_CONV_CH = SSD_WIDTH + 2 * SSD_GROUPS * SSD_STATE
LRU_WIDTH = 512
LRU_BLOCKS = 8
LRU_BLOCK_W = LRU_WIDTH // LRU_BLOCKS
LRU_C = 8.0
CONV_WIDTH = 4
D_MIX = ATT_WIDTH + SSD_WIDTH + LRU_WIDTH
D_FF = 2816
NORM_EPS = 1e-6
SSD_NORM_EPS = 1e-5

LANES = 128
SUBLANES = 8
VMEM_LIMIT_BYTES = 56 * 1024 * 1024

ROW_TILE = 1024
LRU_TILE = 1024
SSD_STEP_CHUNKS = 4
DT_PAD = LANES
FF_CHUNKS = ((0, 768), (768, 1536), (1536, 2304), (2304, 2816))
CONV_PAD = SUBLANES

DIL_MID, DIL_MAX = ATT_DILATIONS[1], ATT_DILATIONS[2]
DIL_STEP = DIL_MAX // DIL_MID
ATT_TILE = ATT_BLOCK * DIL_MAX
MID_LEN = ATT_TILE // DIL_MID
HEADS_PER_PAIR = LANES // ATT_HEAD_DIM
ATT_PAIRS = ATT_WIDTH // LANES
ATT_UNROLL = 16
assert ATT_HEADS == 8 and HEADS_PER_PAIR == 2 and DIL_STEP == DIL_MID and ATT_DILATIONS[0] == 1

LOG2E = 1.4426950408889634
NEG_INF = float("-inf")


def _params(n_axes):
    return pltpu.CompilerParams(
        dimension_semantics=("arbitrary",) * n_axes,
        vmem_limit_bytes=VMEM_LIMIT_BYTES)


def _const_spec(shape):
    nd = len(shape)
    return pl.BlockSpec(shape, lambda *_: (0,) * nd, pipeline_mode=pl.Buffered(1))


def _layer_spec(shape, layer):
    nd = len(shape)
    return pl.BlockSpec((None,) + tuple(shape), lambda *_: (layer,) + (0,) * nd,
                        pipeline_mode=pl.Buffered(1))


def _rmsnorm(x, g):
    return x * lax.rsqrt(jnp.mean(x * x, axis=-1, keepdims=True) + NORM_EPS) * g


def _softplus(x):
    return jnp.maximum(x, 0.0) + jnp.log1p(jnp.exp(-jnp.abs(x)))


def _silu(x):
    return x * jax.nn.sigmoid(x)


_C_QKV = 3 * ATT_WIDTH
_C_Z = _C_QKV + SSD_WIDTH
_C_XBC = _C_Z + SSD_CONV_CH
_T_XL = LRU_WIDTH
_T_DT = 2 * LRU_WIDTH
_T_END = _T_DT + DT_PAD


def _inproj_kernel(x_ref, g_ref, w_ref, wt_ref, qkv_ref, z_ref, xbc_ref, gl_ref, xl_ref, dt_ref):
    h = _rmsnorm(x_ref[...], g_ref[...]).astype(BF16)

    def seg(ref, a, b):
        return jnp.dot(h, ref[:, a:b], preferred_element_type=F32)

    qkv_ref[:, 0:ATT_WIDTH] = seg(w_ref, 0, ATT_WIDTH) * (ATT_HEAD_DIM ** -0.5 * LOG2E)
    qkv_ref[:, ATT_WIDTH:_C_QKV] = seg(w_ref, ATT_WIDTH, _C_QKV)
    z_ref[...] = seg(w_ref, _C_QKV, _C_Z)
    xbc_ref[...] = seg(w_ref, _C_Z, _C_XBC)
    gl_ref[...] = seg(wt_ref, 0, _T_XL)
    xl_ref[...] = seg(wt_ref, _T_XL, _T_DT)
    dt_ref[...] = seg(wt_ref, _T_DT, _T_END)


def _inproj(x2, g, w, w_tail, layer):
    t = x2.shape[0]
    row = lambda width: pl.BlockSpec((ROW_TILE, width), lambda i: (i, 0))
    widths = (_C_QKV, SSD_WIDTH, SSD_CONV_CH, LRU_WIDTH, LRU_WIDTH, DT_PAD)
    return pl.pallas_call(
        _inproj_kernel,
        grid=(t // ROW_TILE,),
        in_specs=[row(D_MODEL), _layer_spec((1, D_MODEL), layer),
                  _layer_spec((D_MODEL, _C_XBC), layer),
                  _layer_spec((D_MODEL, _T_END), layer)],
        out_specs=[row(wd) for wd in widths],
        out_shape=[jax.ShapeDtypeStruct((t, wd), F32) for wd in widths],
        compiler_params=_params(1),
        name="inproj",
    )(x2, g, w, w_tail)


def _scores(q_lo, q_hi, kk, bias):
    q_both = jnp.concatenate([q_lo, q_hi], axis=0)
    return lax.dot_general(q_both, kk, (((1,), (1,)), ((), ())),
                           preferred_element_type=F32) + bias


def _attn_kernel(q_ref, k_ref, v_ref, att_ref,
                 qn_s, kn_s, vn_s, tmp_s, qm_s, km_s, vm_s, qx_s, kx_s, vx_s,
                 mn_s, mm_s, mx_s, acc_s, bias_s, sc_s):
    pair = pl.program_id(1)
    tile = pl.program_id(2)
    blk = ATT_BLOCK
    first_tile = tile == 0
    mid_blocks = MID_LEN // blk

    @pl.when(first_tile)
    def _start_sequence():
        qi = lax.broadcasted_iota(jnp.int32, (blk, 2 * blk), 0)
        ki = lax.broadcasted_iota(jnp.int32, (blk, 2 * blk), 1)
        dist = blk + qi - ki
        band = (dist >= 0) & (dist <= blk)
        band_first = band & (ki >= blk)
        for hh in range(HEADS_PER_PAIR):
            expo = (126 - HEADS_PER_PAIR * pair - hh) << 23
            slope = lax.bitcast_convert_type(jnp.full((blk, 2 * blk), expo, jnp.int32), F32)
            rows = slice(hh * blk, (hh + 1) * blk)
            for br, dil in enumerate((DIL_MAX, DIL_MID, 1)):
                alibi = (-slope * (dil * dist).astype(F32)) * LOG2E
                bias_s[2 * br, rows, :] = jnp.where(band, alibi, NEG_INF)
                bias_s[2 * br + 1, rows, :] = jnp.where(band_first, alibi, NEG_INF)
        kn_s[0:blk, :] = jnp.zeros((blk, LANES), BF16)
        vn_s[:, 0:blk, :] = jnp.zeros((2, blk, LANES), BF16)
        km_s[:, 0:blk, :] = jnp.zeros((DIL_MID, blk, LANES), BF16)
        vm_s[:, :, 0:blk, :] = jnp.zeros((2, DIL_MID, blk, LANES), BF16)
        kx_s[:, 0:blk, :] = jnp.zeros((DIL_MAX, blk, LANES), BF16)
        vx_s[:, :, 0:blk, :] = jnp.zeros((2, DIL_MAX, blk, LANES), BF16)

    @pl.when(tile > 0)
    def _carry_history():
        kn_s[0:blk, :] = kn_s[ATT_TILE:ATT_TILE + blk, :]
        vn_s[:, 0:blk, :] = vn_s[:, ATT_TILE:ATT_TILE + blk, :]
        km_s[:, 0:blk, :] = km_s[:, MID_LEN:MID_LEN + blk, :]
        vm_s[:, :, 0:blk, :] = vm_s[:, :, MID_LEN:MID_LEN + blk, :]
        kx_s[:, 0:blk, :] = kx_s[:, blk:2 * blk, :]
        vx_s[:, :, 0:blk, :] = vx_s[:, :, blk:2 * blk, :]

    lane_m = lax.broadcasted_iota(jnp.int32, (MID_LEN, LANES), 1) < ATT_HEAD_DIM
    lane_b = lax.broadcasted_iota(jnp.int32, (blk, LANES), 1) < ATT_HEAD_DIM

    def put_q(dst, idx, x, mask):
        dst[(0,) + idx] = jnp.where(mask, x, 0.0).astype(BF16)
        dst[(1,) + idx] = jnp.where(mask, 0.0, x).astype(BF16)

    def put_k(dst, idx, x, mask):
        del mask
        dst[idx] = x.astype(BF16)

    def put_v(dst, idx, x, mask):
        dst[(0,) + idx] = jnp.where(mask, x, 1.0).astype(BF16)
        dst[(1,) + idx] = jnp.where(mask, 1.0, x).astype(BF16)

    for src, nat, mid, big, put, hist in (
            (q_ref, qn_s, qm_s, qx_s, put_q, 0),
            (k_ref, kn_s, km_s, kx_s, put_k, blk),
            (v_ref, vn_s, vm_s, vx_s, put_v, blk)):
        for c in range(DIL_MID):
            put(nat, (slice(hist + c * MID_LEN, hist + (c + 1) * MID_LEN), slice(None)),
                src[c * MID_LEN:(c + 1) * MID_LEN, :], lane_m)
        for r in range(DIL_MID):
            x = src[pl.ds(r, MID_LEN, stride=DIL_MID), :]
            tmp_s[r] = x
            put(mid, (r, slice(hist, hist + MID_LEN), slice(None)), x, lane_m)
        for r in range(DIL_MID):
            for c in range(DIL_STEP):
                y = tmp_s.at[r][pl.ds(c, blk, stride=DIL_STEP), :]
                put(big, (r + DIL_MID * c, slice(hist, hist + blk), slice(None)), y, lane_b)

    def bias_of(branch, is_first):
        return bias_s[2 * branch + jnp.where(is_first, 1, 0)]

    def big_ops(r):
        return (qx_s[0, r], qx_s[1, r], kx_s[r], bias_of(0, first_tile),
                pl.ds(r, blk, stride=DIL_MAX))

    def mid_ops(i):
        r = i // mid_blocks
        j = i % mid_blocks
        q0 = pl.multiple_of(j * blk, blk)
        return (qm_s[0, r, pl.ds(q0, blk), :], qm_s[1, r, pl.ds(q0, blk), :],
                km_s[r, pl.ds(q0, 2 * blk), :], bias_of(1, first_tile & (j == 0)),
                pl.ds(j * (blk * DIL_MID) + r, blk, stride=DIL_MID))

    def nat_ops(j):
        q0 = pl.multiple_of(j * blk, blk)
        return (qn_s[0, pl.ds(q0, blk), :], qn_s[1, pl.ds(q0, blk), :],
                kn_s[pl.ds(q0, 2 * blk), :], bias_of(2, first_tile & (j == 0)),
                pl.ds(q0, blk))

    def loop(n, body):
        lax.fori_loop(0, n, lambda i, c: (body(i), c)[1], 0, unroll=ATT_UNROLL)

    n_big, n_mid, n_nat = DIL_MAX, DIL_MID * mid_blocks, ATT_TILE // blk

    def row_max(ops, slot):
        q_lo, q_hi, kk, bias, rows = ops
        s = _scores(q_lo, q_hi, kk, bias)
        sc_s[slot] = s
        m = jnp.broadcast_to(jnp.max(s, axis=-1, keepdims=True), (2 * blk, LANES))
        return jnp.where(lane_b, m[:blk], m[blk:]), rows

    def max_big(r):
        m, rows = row_max(big_ops(r), r)
        mn_s[rows, :] = m

    def max_more(ops, slot):
        m, rows = row_max(ops, slot)
        mn_s[rows, :] = jnp.maximum(mn_s[rows, :], m)

    loop(n_big, max_big)
    loop(n_mid, lambda i: max_more(mid_ops(i), n_big + i))
    loop(n_nat, lambda j: max_more(nat_ops(j), n_big + n_mid + j))

    for r in range(DIL_MID):
        mm_s[r] = mn_s[pl.ds(r, MID_LEN, stride=DIL_MID), :]
    for r in range(DIL_MID):
        for c in range(DIL_STEP):
            mx_s[r + DIL_MID * c] = mm_s.at[r][pl.ds(c, blk, stride=DIL_STEP), :]

    def num_den(slot, m_pair, v_lo, v_hi):
        swapped = pltpu.roll(m_pair, ATT_HEAD_DIM, 1)
        m = jnp.concatenate([jnp.where(lane_b, m_pair, swapped),
                             jnp.where(lane_b, swapped, m_pair)], axis=0)
        e = jnp.exp2(sc_s[slot] - jnp.concatenate([m, m], axis=1)).astype(BF16)
        return (jnp.dot(e[:blk], v_lo, preferred_element_type=F32),
                jnp.dot(e[blk:], v_hi, preferred_element_type=F32))

    def acc_big(r):
        o_lo, o_hi = num_den(r, mx_s[r], vx_s[0, r], vx_s[1, r])
        rows = pl.ds(r, blk, stride=DIL_MAX)
        acc_s[0, rows, :] = o_lo
        acc_s[1, rows, :] = o_hi

    def acc_mid(i):
        r = i // mid_blocks
        j = i % mid_blocks
        q0 = pl.multiple_of(j * blk, blk)
        o_lo, o_hi = num_den(
            n_big + i, mm_s[r, pl.ds(q0, blk), :],
            vm_s[0, r, pl.ds(q0, 2 * blk), :], vm_s[1, r, pl.ds(q0, 2 * blk), :])
        rows = pl.ds(j * (blk * DIL_MID) + r, blk, stride=DIL_MID)
        acc_s[0, rows, :] += o_lo
        acc_s[1, rows, :] += o_hi

    def acc_nat(j):
        q0 = pl.multiple_of(j * blk, blk)
        rows = pl.ds(q0, blk)
        o_lo, o_hi = num_den(
            n_big + n_mid + j, mn_s[rows, :],
            vn_s[0, pl.ds(q0, 2 * blk), :], vn_s[1, pl.ds(q0, 2 * blk), :])
        a_lo = acc_s[0, rows, :] + o_lo
        a_hi = acc_s[1, rows, :] + o_hi
        num = jnp.where(lane_b, a_lo, a_hi)
        den = pltpu.roll(jnp.where(lane_b, a_hi, a_lo), ATT_HEAD_DIM, 1)
        att_ref[rows, :] = (num / den).astype(BF16)

    @pl.when(tile >= 0)
    def _pass_two():
        loop(n_big, acc_big)
        loop(n_mid, acc_mid)
        loop(n_nat, acc_nat)


def _dilated_attention(qkv, batch):
    t = qkv.shape[0]
    tiles = t // batch // ATT_TILE
    blk = ATT_BLOCK

    def spec(col0):
        return pl.BlockSpec((ATT_TILE, LANES), lambda b, p, i: (b * tiles + i, col0 + p))

    bf = lambda *shape: pltpu.VMEM(shape, BF16)
    f32 = lambda *shape: pltpu.VMEM(shape, F32)
    return pl.pallas_call(
        _attn_kernel,
        grid=(batch, ATT_PAIRS, tiles),
        in_specs=[spec(0), spec(ATT_PAIRS), spec(2 * ATT_PAIRS)],
        out_specs=spec(0),
        out_shape=jax.ShapeDtypeStruct((t, ATT_WIDTH), BF16),
        scratch_shapes=[
            bf(2, ATT_TILE, LANES), bf(blk + ATT_TILE, LANES), bf(2, blk + ATT_TILE, LANES),
            f32(DIL_MID, MID_LEN, LANES),
            bf(2, DIL_MID, MID_LEN, LANES), bf(DIL_MID, blk + MID_LEN, LANES),
            bf(2, DIL_MID, blk + MID_LEN, LANES),
            bf(2, DIL_MAX, blk, LANES), bf(DIL_MAX, 2 * blk, LANES),
            bf(2, DIL_MAX, 2 * blk, LANES),
            f32(ATT_TILE, LANES), f32(DIL_MID, MID_LEN, LANES),
            f32(DIL_MAX, blk, LANES), f32(2, ATT_TILE, LANES),
            f32(3 * 2, 2 * blk, 2 * blk),
            f32(3 * DIL_MAX, 2 * blk, 2 * blk),
        ],
        compiler_params=_params(3),
        name="attn",
    )(qkv, qkv, qkv)


def _split3(x):
    hi = x.astype(BF16)
    r1 = x - hi.astype(F32)
    mid = r1.astype(BF16)
    lo = (r1 - mid.astype(F32)).astype(BF16)
    return hi, mid, lo


def _dot_exact_rhs(x, w):
    return sum(jnp.dot(part, w, preferred_element_type=F32) for part in _split3(x))


TM_CHUNK = 128
TM_STEPS = TM_CHUNK // SUBLANES
TM_HIST = (CONV_WIDTH - 1) * SUBLANES
assert TM_CHUNK == SSD_CHUNK


def _to_time_major(src, mid, dst, base):
    quarter = TM_CHUNK // 4
    for r in range(4):
        mid[base + quarter * r:base + quarter * (r + 1), :] = (
            src[pl.ds(base + r, quarter, stride=4), :])
    for r in range(4):
        for c in range(4):
            v = r + 4 * c
            dst[base + SUBLANES * v:base + SUBLANES * (v + 1), :] = (
                mid[pl.ds(base + quarter * r + c, SUBLANES, stride=4), :])


def _tm_time(idx):
    return TM_STEPS * (idx % SUBLANES) + idx // SUBLANES


def _time_major_perm():
    n = jnp.arange(TM_CHUNK)
    return (_tm_time(n)[None, :] == n[:, None]).astype(BF16)


def _conv_time_major(x, hist, cw_ref, cb_ref, first_row):
    width = x.shape[-1]
    groups = TM_HIST // SUBLANES
    tail = x[TM_CHUNK - TM_HIST:, :]
    down = lambda a: pltpu.roll(a.reshape(groups, SUBLANES, width), 1, 1)
    wrapped = jnp.where(first_row, down(hist), down(tail)).reshape(TM_HIST, width)
    ext = jnp.concatenate([wrapped, x], axis=0)
    k_w = CONV_WIDTH
    conv = cb_ref[...] + x * cw_ref[k_w - 1:k_w, :]
    for back in range(1, k_w):
        off = TM_HIST - SUBLANES * back
        conv = conv + ext[off:off + TM_CHUNK, :] * cw_ref[k_w - 1 - back:k_w - back, :]
    return conv, tail


def _ssd_kernel(z_ref, xbc_ref, dt_ref, cw_ref, cb_ref, dtb_ref, alog_ref, dsk_ref, nw_ref,
                triu_ref, exp_ref, y_ref, win_s, state_s):
    c = pl.program_id(0)
    q = SSD_CHUNK
    pad = CONV_PAD
    rows_step = SSD_STEP_CHUNKS * q
    heads_per_group = SSD_HEADS // SSD_GROUPS
    gw = heads_per_group * SSD_HEAD_DIM
    n_seq = z_ref.shape[0]

    @pl.when(c == 0)
    def _reset():
        win_s[:, 0:pad, :] = jnp.zeros((n_seq, pad, SSD_CONV_CH), F32)
        state_s[...] = jnp.zeros_like(state_s)

    win_s[:, pad:, :] = xbc_ref[...]

    ti = lax.broadcasted_iota(jnp.int32, (q, q), 0)
    tj = lax.broadcasted_iota(jnp.int32, (q, q), 1)
    causal = ti >= tj
    low_half = lax.broadcasted_iota(jnp.int32, (q, LANES), 1) < SSD_HEAD_DIM
    a_t = -jnp.exp(alog_ref[...])
    k_w = CONV_WIDTH

    for ci, b in ((ci, b) for ci in range(SSD_STEP_CHUNKS) for b in range(n_seq)):
        r0 = ci * q
        win_b, state_b = win_s.at[b], state_s.at[b]
        z_b, dt_b, y_b = z_ref.at[b], dt_ref.at[b], y_ref.at[b]
        conv = cb_ref[...] + win_b[pad + r0:pad + r0 + q, :] * cw_ref[k_w - 1:k_w, :]
        for k in range(k_w - 1):
            off = pad + r0 - (k_w - 1) + k
            conv = conv + win_b[off:off + q, :] * cw_ref[k:k + 1, :]
        xbc = _silu(conv)
        xs = xbc[:, :SSD_WIDTH]
        bm = xbc[:, SSD_WIDTH:SSD_WIDTH + SSD_GROUPS * SSD_STATE]
        cm = xbc[:, SSD_WIDTH + SSD_GROUPS * SSD_STATE:]

        dt_t = _softplus(dt_b[r0:r0 + q, :].T[0:SSD_HEADS, :] + dtb_ref[...])
        acs_t = _dot_exact_rhs(dt_t * a_t, triu_ref[...])
        e_t = jnp.exp(acs_t)
        w_t = jnp.exp(acs_t[:, q - 1:q] - acs_t) * dt_t
        chunk_decay = jnp.sum(e_t[:, q - 1:q] * exp_ref[...], axis=0, keepdims=True)
        cols_t = jnp.concatenate(
            [acs_t, e_t, jnp.zeros((q - 2 * SSD_HEADS, q), F32)], axis=0).T

        y_parts = []
        for g in range(SSD_GROUPS):
            bm_g = bm[:, g * SSD_STATE:(g + 1) * SSD_STATE]
            cm_g = cm[:, g * SSD_STATE:(g + 1) * SSD_STATE]
            gmat = lax.dot_general(cm_g.astype(BF16), bm_g.astype(BF16),
                                   (((1,), (1,)), ((), ())), preferred_element_type=F32)
            bm_gt = bm_g.T
            for pp in range(heads_per_group // HEADS_PER_PAIR):
                p = g * (heads_per_group // HEADS_PER_PAIR) + pp
                cols = slice(p * LANES, (p + 1) * LANES)
                x_p = xs[:, cols]
                s_p = state_b[:, cols]
                lhs_y, lhs_s, rhs_y, rhs_s = [], [], [], []
                for hh in range(HEADS_PER_PAIR):
                    h = HEADS_PER_PAIR * p + hh
                    keep = low_half if hh == 0 else ~low_half
                    seg = cols_t[:, h:h + 1] - acs_t[h:h + 1, :]
                    lmat = jnp.exp(jnp.where(causal, seg, NEG_INF))
                    scores = gmat * lmat * dt_t[h:h + 1, :]
                    c_dec = cm_g * cols_t[:, SSD_HEADS + h:SSD_HEADS + h + 1]
                    x_h = jnp.where(keep, x_p, 0.0).astype(BF16)
                    s_h = jnp.where(keep, s_p, 0.0).astype(BF16)
                    lhs_y += [scores.astype(BF16), c_dec.astype(BF16)]
                    rhs_y += [x_h, s_h]
                    lhs_s.append((bm_gt * w_t[h:h + 1, :]).astype(BF16))
                    rhs_s.append(x_h)
                y_parts.append(jnp.dot(jnp.concatenate(lhs_y, axis=1),
                                       jnp.concatenate(rhs_y, axis=0),
                                       preferred_element_type=F32))
                state_b[:, cols] = s_p * chunk_decay[:, cols] + jnp.dot(
                    jnp.concatenate(lhs_s, axis=1), jnp.concatenate(rhs_s, axis=0),
                    preferred_element_type=F32)
        y = jnp.concatenate(y_parts, axis=-1) + dsk_ref[...] * xs
        y = y * _silu(z_b[r0:r0 + q, :])
        outs = []
        for g in range(SSD_GROUPS):
            yg = y[:, g * gw:(g + 1) * gw]
            outs.append(yg * lax.rsqrt(jnp.mean(yg * yg, axis=-1, keepdims=True) + SSD_NORM_EPS))
        y_b[r0:r0 + q, :] = (jnp.concatenate(outs, axis=-1) * nw_ref[...]).astype(BF16)

    win_s[:, 0:pad, :] = win_s[:, rows_step:rows_step + pad, :]


def _prep_ssd(conv_w, conv_b, dt_bias, a_log, d_skip, norm_w):
    depth = conv_w.shape[0]
    per_time = lambda v: jnp.broadcast_to(v.astype(F32)[:, :, None],
                                          (depth, SSD_HEADS, SSD_CHUNK))
    triu = jnp.triu(jnp.ones((SSD_CHUNK, SSD_CHUNK), BF16))
    expand = (jnp.arange(SSD_HEADS)[:, None] == (jnp.arange(SSD_WIDTH)[None, :] // SSD_HEAD_DIM)
              ).astype(F32)
    d_exp = jnp.repeat(d_skip.astype(F32), SSD_HEAD_DIM, axis=1).reshape(depth, 1, SSD_WIDTH)
    return (conv_w, conv_b.reshape(depth, 1, SSD_CONV_CH), per_time(dt_bias), per_time(a_log),
            d_exp, norm_w.reshape(depth, 1, SSD_WIDTH), triu, expand)


def _ssd(z, xbc, dt, prepared, layer, batch):
    t = z.shape[0]
    seq = t // batch
    rows_step = SSD_STEP_CHUNKS * SSD_CHUNK
    row = lambda width: pl.BlockSpec((batch, rows_step, width), lambda c: (0, c, 0))
    by_seq = lambda a: a.reshape(batch, seq, a.shape[-1])
    lspec = lambda *shape: _layer_spec(shape, layer)
    out = pl.pallas_call(
        _ssd_kernel,
        grid=(seq // rows_step,),
        in_specs=[row(SSD_WIDTH), row(SSD_CONV_CH), row(DT_PAD),
                  lspec(CONV_WIDTH, SSD_CONV_CH), lspec(1, SSD_CONV_CH),
                  lspec(SSD_HEADS, SSD_CHUNK), lspec(SSD_HEADS, SSD_CHUNK),
                  lspec(1, SSD_WIDTH), lspec(1, SSD_WIDTH),
                  _const_spec((SSD_CHUNK, SSD_CHUNK)), _const_spec((SSD_HEADS, SSD_WIDTH))],
        out_specs=row(SSD_WIDTH),
        out_shape=jax.ShapeDtypeStruct((batch, seq, SSD_WIDTH), BF16),
        scratch_shapes=[pltpu.VMEM((batch, CONV_PAD + rows_step, SSD_CONV_CH), F32),
                        pltpu.VMEM((batch, SSD_STATE, SSD_WIDTH), F32)],
        compiler_params=_params(1),
        name="ssd",
    )(by_seq(z), by_seq(xbc), by_seq(dt), *prepared)
    return out.reshape(t, SSD_WIDTH)


def _gelu_tanh(x):
    c = math.sqrt(2.0 / math.pi)
    return 0.5 * x * (1.0 + jnp.tanh(c * (x + 0.044715 * (x * x * x))))


def _lru_kernel(*refs):
    ncol = LRU_WIDTH // LANES
    g_refs, x_refs = refs[:ncol], refs[ncol:2 * ncol]
    (cw_ref, cb_ref, wa_ref, ba_ref, wx_ref, bx_ref, lam_ref, perm_ref, y_ref,
     mid_s, gt_s, xt_s, hist_s, h_s) = refs[2 * ncol:]
    c = pl.program_id(1)
    tt = LRU_TILE
    n_chunks = tt // TM_CHUNK

    @pl.when(c == 0)
    def _reset():
        hist_s[...] = jnp.zeros_like(hist_s)
        h_s[...] = jnp.zeros_like(h_s)

    for j in range(ncol):
        for k in range(n_chunks):
            _to_time_major(g_refs[j], mid_s.at[j], gt_s.at[j], k * TM_CHUNK)
        for k in range(n_chunks):
            _to_time_major(x_refs[j], mid_s.at[j], xt_s.at[j], k * TM_CHUNK)

    def chunk_of(buf, k):
        rows = slice(k * TM_CHUNK, (k + 1) * TM_CHUNK)
        return jnp.concatenate([buf[j, rows, :] for j in range(ncol)], axis=1)

    first_row = lax.broadcasted_iota(
        jnp.int32, (TM_HIST // SUBLANES, SUBLANES, LRU_WIDTH), 1) == 0
    hist = hist_s[...]
    convs = []
    for k in range(n_chunks):
        conv, hist = _conv_time_major(chunk_of(xt_s, k), hist, cw_ref, cb_ref, first_row)
        convs.append(conv)
    hist_s[...] = hist
    xc = jnp.concatenate(convs, axis=0)

    xb = xc.astype(BF16)
    r = jax.nn.sigmoid(jnp.dot(xb, wa_ref[...], preferred_element_type=F32) + ba_ref[...])
    i = jax.nn.sigmoid(jnp.dot(xb, wx_ref[...], preferred_element_type=F32) + bx_ref[...])
    log_a = -LRU_C * r * _softplus(-lam_ref[...])
    a = jnp.exp(log_a)
    u = jnp.sqrt(-jnp.tanh(log_a) * (a * a + 1.0)) * (i * xc)

    sub = lax.broadcasted_iota(jnp.int32, (SUBLANES, LRU_WIDTH), 0)
    h_prev = h_s[0:1, :]
    for k in range(n_chunks):
        rows = slice(k * TM_CHUNK, (k + 1) * TM_CHUNK)
        a_k = a[rows].reshape(TM_STEPS, SUBLANES, LRU_WIDTH)
        u_k = u[rows].reshape(TM_STEPS, SUBLANES, LRU_WIDTH)
        hh, aa = [u_k[0]], [a_k[0]]
        for v in range(1, TM_STEPS):
            hh.append(a_k[v] * hh[-1] + u_k[v])
            aa.append(a_k[v] * aa[-1])
        pa, ph = aa[-1], hh[-1]
        step = 1
        while step < SUBLANES:
            keep = sub >= step
            pa_sh = jnp.where(keep, pltpu.roll(pa, step, 0), 1.0)
            ph_sh = jnp.where(keep, pltpu.roll(ph, step, 0), 0.0)
            ph = pa * ph_sh + ph
            pa = pa * pa_sh
            step *= 2
        c_in = jnp.where(sub == 0, h_prev, pltpu.roll(pa, 1, 0) * h_prev + pltpu.roll(ph, 1, 0))
        h_k = [hh[v] + aa[v] * c_in for v in range(TM_STEPS)]
        h_prev = h_k[-1][SUBLANES - 1:SUBLANES, :]
        y_k = (jnp.concatenate(h_k, axis=0) * _gelu_tanh(chunk_of(gt_s, k))).astype(BF16)
        y_ref[rows, :] = jnp.dot(perm_ref[...], y_k, preferred_element_type=F32).astype(BF16)
    h_s[...] = jnp.broadcast_to(h_prev, h_s.shape)


def _block_diag(w):
    depth, nb, c, d = w.shape
    eye = jnp.eye(nb, dtype=w.dtype)
    return (eye[None, :, None, :, None] * w[:, :, :, None, :]).reshape(depth, nb * c, nb * d)


def _prep_lru(conv_w, conv_b, wa, ba, wx, bx, lam):
    depth = conv_w.shape[0]
    vec = lambda v: v.astype(F32).reshape(depth, 1, LRU_WIDTH)
    return (conv_w, vec(conv_b), _block_diag(wa).astype(BF16), vec(ba),
            _block_diag(wx).astype(BF16), vec(bx), vec(lam))


def _lru(g_in, x_in, prepared, layer, batch):
    t = g_in.shape[0]
    s = t // batch
    nt = s // LRU_TILE
    ncol = LRU_WIDTH // LANES
    row = pl.BlockSpec((LRU_TILE, LRU_WIDTH), lambda b, c: (b * nt + c, 0))
    cols = [pl.BlockSpec((LRU_TILE, LANES), lambda b, c, j=j: (b * nt + c, j))
            for j in range(ncol)]
    mat = _layer_spec((LRU_WIDTH, LRU_WIDTH), layer)
    one = _layer_spec((1, LRU_WIDTH), layer)
    tile = lambda: pltpu.VMEM((ncol, LRU_TILE, LANES), F32)
    return pl.pallas_call(
        _lru_kernel,
        grid=(batch, nt),
        in_specs=cols + cols + [_layer_spec((CONV_WIDTH, LRU_WIDTH), layer), one, mat, one,
                                mat, one, one, _const_spec((TM_CHUNK, TM_CHUNK))],
        out_specs=row,
        out_shape=jax.ShapeDtypeStruct((t, LRU_WIDTH), BF16),
        scratch_shapes=[tile(), tile(), tile(),
                        pltpu.VMEM((TM_HIST, LRU_WIDTH), F32),
                        pltpu.VMEM((SUBLANES, LRU_WIDTH), F32)],
        compiler_params=_params(2),
        name="rglru",
    )(*([g_in] * ncol), *([x_in] * ncol), *prepared, _time_major_perm())


def _outffn_kernel(x_ref, att_ref, ssd_ref, lru_ref, wo_ref, gf_ref, wg_ref, wu_ref, wd_ref,
                   nfin_ref, o_ref, hn_s, *, final_norm):
    x1 = x_ref[...]
    for j, m_ref in enumerate((att_ref, ssd_ref, lru_ref)):
        x1 = x1 + jnp.dot(m_ref[...], wo_ref[j * ATT_WIDTH:(j + 1) * ATT_WIDTH, :],
                          preferred_element_type=F32)
    hn_s[...] = _rmsnorm(x1, gf_ref[...]).astype(BF16)
    o_ref[...] = x1
    for c0, c1 in FF_CHUNKS:
        hn = hn_s[...]
        gate = jnp.dot(hn, wg_ref[:, c0:c1], preferred_element_type=F32)
        up = jnp.dot(hn, wu_ref[:, c0:c1], preferred_element_type=F32)
        act = (_silu(gate) * up).astype(BF16)
        o_ref[...] += jnp.dot(act, wd_ref[c0:c1, :], preferred_element_type=F32)
    if final_norm:
        o_ref[...] = _rmsnorm(o_ref[...], nfin_ref[...])


def _prep_ffn(w_out, norm_ffn, w_gate, w_up, w_down, norm_final):
    depth = w_out.shape[0]
    return (w_out.astype(BF16), norm_ffn.reshape(depth, 1, D_MODEL), w_gate.astype(BF16),
            w_up.astype(BF16), w_down.astype(BF16), norm_final.reshape(1, D_MODEL))


def _outffn(x2, att, ssd, lru, prepared, layer, final_norm):
    t = x2.shape[0]
    row = lambda width: pl.BlockSpec((ROW_TILE, width), lambda i: (i, 0))
    lspec = lambda *shape: _layer_spec(shape, layer)
    return pl.pallas_call(
        functools.partial(_outffn_kernel, final_norm=final_norm),
        grid=(t // ROW_TILE,),
        in_specs=[row(D_MODEL), row(ATT_WIDTH), row(SSD_WIDTH), row(LRU_WIDTH),
                  lspec(D_MIX, D_MODEL), lspec(1, D_MODEL), lspec(D_MODEL, D_FF),
                  lspec(D_MODEL, D_FF), lspec(D_FF, D_MODEL), _const_spec((1, D_MODEL))],
        out_specs=row(D_MODEL),
        out_shape=jax.ShapeDtypeStruct((t, D_MODEL), F32),
        scratch_shapes=[pltpu.VMEM((ROW_TILE, D_MODEL), BF16)],
        compiler_params=_params(1),
        name="outffn",
    )(x2, att, ssd, lru, *prepared)


def _prep_w_in(w):
    tail = w[:, :, _C_XBC:]
    dt_cols = jnp.pad(tail[:, :, :SSD_HEADS], ((0, 0), (0, 0), (0, DT_PAD - SSD_HEADS)))
    w_tail = jnp.concatenate([tail[:, :, SSD_HEADS:], dt_cols], axis=2).astype(BF16)
    return w.astype(BF16), w_tail


def kernel(x, norm_mix, w_in, ssd_conv_w, ssd_conv_b, ssd_dt_bias, ssd_a_log, ssd_d, ssd_norm,
           lru_conv_w, lru_conv_b, lru_wa, lru_ba, lru_wx, lru_bx, lru_lambda, w_out,
           norm_ffn, w_gate, w_up, w_down, norm_final):
    batch, seq, _ = x.shape
    depth = w_in.shape[0]
    g_mix = norm_mix.reshape(depth, 1, D_MODEL)
    w_proj, w_tail = _prep_w_in(w_in)
    ssd_p = _prep_ssd(ssd_conv_w, ssd_conv_b, ssd_dt_bias, ssd_a_log, ssd_d, ssd_norm)
    lru_p = _prep_lru(lru_conv_w, lru_conv_b, lru_wa, lru_ba, lru_wx, lru_bx, lru_lambda)
    ffn_p = _prep_ffn(w_out, norm_ffn, w_gate, w_up, w_down, norm_final)
    x2 = x.reshape(batch * seq, D_MODEL)
    for l in range(depth):
        qkv, z, xbc, g_lru, x_lru, dt = _inproj(x2, g_mix, w_proj, w_tail, l)
        att = _dilated_attention(qkv, batch)
        ssd = _ssd(z, xbc, dt, ssd_p, l, batch)
        lru = _lru(g_lru, x_lru, lru_p, l, batch)
        x2 = _outffn(x2, att, ssd, lru, ffn_p, l, l == depth - 1)
    return x2.reshape(batch, seq, D_MODEL)
```

```python
import functools
import math

import jax
import jax.numpy as jnp
from jax import lax
from jax.experimental import pallas as pl
from jax.experimental.pallas import tpu as pltpu

F32 = jnp.float32
BF16 = jnp.bfloat16

D_MODEL = 1024
ATT_HEADS = 8
ATT_HEAD_DIM = 64
ATT_WIDTH = ATT_HEADS * ATT_HEAD_DIM
ATT_BLOCK = 128
ATT_DILATIONS = (1, 4, 16)
SSD_HEADS = 8
SSD_HEAD_DIM = 64
SSD_WIDTH = SSD_HEADS * SSD_HEAD_DIM
SSD_GROUPS = 2
SSD_STATE = 128
SSD_CHUNK = 128
SSD_CONV_CH = SSD_WIDTH + 2 * SSD_GROUPS * SSD_STATE
LRU_WIDTH = 512
LRU_BLOCKS = 8
LRU_BLOCK_W = LRU_WIDTH // LRU_BLOCKS
LRU_C = 8.0
CONV_WIDTH = 4
D_MIX = ATT_WIDTH + SSD_WIDTH + LRU_WIDTH
D_FF = 2816
NORM_EPS = 1e-6
SSD_NORM_EPS = 1e-5

LANES = 128
SUBLANES = 8
VMEM_LIMIT_BYTES = 56 * 1024 * 1024

ROW_TILE = 1024
LRU_TILE = 1024
SSD_STEP_CHUNKS = 4
DT_PAD = LANES
FF_CHUNKS = ((0, 768), (768, 1536), (1536, 2304), (2304, 2816))
CONV_PAD = SUBLANES

DIL_MID, DIL_MAX = ATT_DILATIONS[1], ATT_DILATIONS[2]
DIL_STEP = DIL_MAX // DIL_MID
ATT_TILE = ATT_BLOCK * DIL_MAX
MID_LEN = ATT_TILE // DIL_MID
HEADS_PER_PAIR = LANES // ATT_HEAD_DIM
ATT_PAIRS = ATT_WIDTH // LANES
ATT_UNROLL = 16
assert ATT_HEADS == 8 and HEADS_PER_PAIR == 2 and DIL_STEP == DIL_MID and ATT_DILATIONS[0] == 1

LOG2E = 1.4426950408889634
NEG_INF = float("-inf")


def _params(n_axes):
    return pltpu.CompilerParams(
        dimension_semantics=("arbitrary",) * n_axes,
        vmem_limit_bytes=VMEM_LIMIT_BYTES)


def _const_spec(shape):
    nd = len(shape)
    return pl.BlockSpec(shape, lambda *_: (0,) * nd, pipeline_mode=pl.Buffered(1))


def _layer_spec(shape, layer):
    nd = len(shape)
    return pl.BlockSpec((None,) + tuple(shape), lambda *_: (layer,) + (0,) * nd,
                        pipeline_mode=pl.Buffered(1))


def _rmsnorm(x, g):
    return x * lax.rsqrt(jnp.mean(x * x, axis=-1, keepdims=True) + NORM_EPS) * g


def _softplus(x):
    return jnp.maximum(x, 0.0) + jnp.log1p(jnp.exp(-jnp.abs(x)))


def _silu(x):
    return x * jax.nn.sigmoid(x)


_C_QKV = 3 * ATT_WIDTH
_C_Z = _C_QKV + SSD_WIDTH
_C_XBC = _C_Z + SSD_CONV_CH
_T_XL = LRU_WIDTH
_T_DT = 2 * LRU_WIDTH
_T_END = _T_DT + DT_PAD


def _inproj_kernel(x_ref, g_ref, w_ref, wt_ref, qkv_ref, z_ref, xbc_ref, gl_ref, xl_ref, dt_ref):
    h = _rmsnorm(x_ref[...], g_ref[...]).astype(BF16)

    def seg(ref, a, b):
        return jnp.dot(h, ref[:, a:b], preferred_element_type=F32)

    qkv_ref[:, 0:ATT_WIDTH] = seg(w_ref, 0, ATT_WIDTH) * (ATT_HEAD_DIM ** -0.5 * LOG2E)
    qkv_ref[:, ATT_WIDTH:_C_QKV] = seg(w_ref, ATT_WIDTH, _C_QKV)
    z_ref[...] = seg(w_ref, _C_QKV, _C_Z)
    xbc_ref[...] = seg(w_ref, _C_Z, _C_XBC)
    gl_ref[...] = seg(wt_ref, 0, _T_XL)
    xl_ref[...] = seg(wt_ref, _T_XL, _T_DT)
    dt_ref[...] = seg(wt_ref, _T_DT, _T_END)


def _inproj(x2, g, w, w_tail, layer):
    t = x2.shape[0]
    row = lambda width: pl.BlockSpec((ROW_TILE, width), lambda i: (i, 0))
    widths = (_C_QKV, SSD_WIDTH, SSD_CONV_CH, LRU_WIDTH, LRU_WIDTH, DT_PAD)
    return pl.pallas_call(
        _inproj_kernel,
        grid=(t // ROW_TILE,),
        in_specs=[row(D_MODEL), _layer_spec((1, D_MODEL), layer),
                  _layer_spec((D_MODEL, _C_XBC), layer),
                  _layer_spec((D_MODEL, _T_END), layer)],
        out_specs=[row(wd) for wd in widths],
        out_shape=[jax.ShapeDtypeStruct((t, wd), F32) for wd in widths],
        compiler_params=_params(1),
        name="inproj",
    )(x2, g, w, w_tail)


def _scores(q_lo, q_hi, kk, bias):
    q_both = jnp.concatenate([q_lo, q_hi], axis=0)
    return lax.dot_general(q_both, kk, (((1,), (1,)), ((), ())),
                           preferred_element_type=F32) + bias


def _attn_kernel(q_ref, k_ref, v_ref, att_ref,
                 qn_s, kn_s, vn_s, tmp_s, qm_s, km_s, vm_s, qx_s, kx_s, vx_s,
                 mn_s, mm_s, mx_s, acc_s, bias_s, sc_s):
    pair = pl.program_id(1)
    tile = pl.program_id(2)
    blk = ATT_BLOCK
    first_tile = tile == 0
    mid_blocks = MID_LEN // blk

    @pl.when(first_tile)
    def _start_sequence():
        qi = lax.broadcasted_iota(jnp.int32, (blk, 2 * blk), 0)
        ki = lax.broadcasted_iota(jnp.int32, (blk, 2 * blk), 1)
        dist = blk + qi - ki
        band = (dist >= 0) & (dist <= blk)
        band_first = band & (ki >= blk)
        for hh in range(HEADS_PER_PAIR):
            expo = (126 - HEADS_PER_PAIR * pair - hh) << 23
            slope = lax.bitcast_convert_type(jnp.full((blk, 2 * blk), expo, jnp.int32), F32)
            rows = slice(hh * blk, (hh + 1) * blk)
            for br, dil in enumerate((DIL_MAX, DIL_MID, 1)):
                alibi = (-slope * (dil * dist).astype(F32)) * LOG2E
                bias_s[2 * br, rows, :] = jnp.where(band, alibi, NEG_INF)
                bias_s[2 * br + 1, rows, :] = jnp.where(band_first, alibi, NEG_INF)
        kn_s[0:blk, :] = jnp.zeros((blk, LANES), BF16)
        vn_s[:, 0:blk, :] = jnp.zeros((2, blk, LANES), BF16)
        km_s[:, 0:blk, :] = jnp.zeros((DIL_MID, blk, LANES), BF16)
        vm_s[:, :, 0:blk, :] = jnp.zeros((2, DIL_MID, blk, LANES), BF16)
        kx_s[:, 0:blk, :] = jnp.zeros((DIL_MAX, blk, LANES), BF16)
        vx_s[:, :, 0:blk, :] = jnp.zeros((2, DIL_MAX, blk, LANES), BF16)

    @pl.when(tile > 0)
    def _carry_history():
        kn_s[0:blk, :] = kn_s[ATT_TILE:ATT_TILE + blk, :]
        vn_s[:, 0:blk, :] = vn_s[:, ATT_TILE:ATT_TILE + blk, :]
        km_s[:, 0:blk, :] = km_s[:, MID_LEN:MID_LEN + blk, :]
        vm_s[:, :, 0:blk, :] = vm_s[:, :, MID_LEN:MID_LEN + blk, :]
        kx_s[:, 0:blk, :] = kx_s[:, blk:2 * blk, :]
        vx_s[:, :, 0:blk, :] = vx_s[:, :, blk:2 * blk, :]

    lane_m = lax.broadcasted_iota(jnp.int32, (MID_LEN, LANES), 1) < ATT_HEAD_DIM
    lane_b = lax.broadcasted_iota(jnp.int32, (blk, LANES), 1) < ATT_HEAD_DIM

    def put_q(dst, idx, x, mask):
        dst[(0,) + idx] = jnp.where(mask, x, 0.0).astype(BF16)
        dst[(1,) + idx] = jnp.where(mask, 0.0, x).astype(BF16)

    def put_k(dst, idx, x, mask):
        del mask
        dst[idx] = x.astype(BF16)

    def put_v(dst, idx, x, mask):
        dst[(0,) + idx] = jnp.where(mask, x, 1.0).astype(BF16)
        dst[(1,) + idx] = jnp.where(mask, 1.0, x).astype(BF16)

    for src, nat, mid, big, put, hist in (
            (q_ref, qn_s, qm_s, qx_s, put_q, 0),
            (k_ref, kn_s, km_s, kx_s, put_k, blk),
            (v_ref, vn_s, vm_s, vx_s, put_v, blk)):
        for c in range(DIL_MID):
            put(nat, (slice(hist + c * MID_LEN, hist + (c + 1) * MID_LEN), slice(None)),
                src[c * MID_LEN:(c + 1) * MID_LEN, :], lane_m)
        for r in range(DIL_MID):
            x = src[pl.ds(r, MID_LEN, stride=DIL_MID), :]
            tmp_s[r] = x
            put(mid, (r, slice(hist, hist + MID_LEN), slice(None)), x, lane_m)
        for r in range(DIL_MID):
            for c in range(DIL_STEP):
                y = tmp_s.at[r][pl.ds(c, blk, stride=DIL_STEP), :]
                put(big, (r + DIL_MID * c, slice(hist, hist + blk), slice(None)), y, lane_b)

    def bias_of(branch, is_first):
        return bias_s[2 * branch + jnp.where(is_first, 1, 0)]

    def big_ops(r):
        return (qx_s[0, r], qx_s[1, r], kx_s[r], bias_of(0, first_tile),
                pl.ds(r, blk, stride=DIL_MAX))

    def mid_ops(i):
        r = i // mid_blocks
        j = i % mid_blocks
        q0 = pl.multiple_of(j * blk, blk)
        return (qm_s[0, r, pl.ds(q0, blk), :], qm_s[1, r, pl.ds(q0, blk), :],
                km_s[r, pl.ds(q0, 2 * blk), :], bias_of(1, first_tile & (j == 0)),
                pl.ds(j * (blk * DIL_MID) + r, blk, stride=DIL_MID))

    def nat_ops(j):
        q0 = pl.multiple_of(j * blk, blk)
        return (qn_s[0, pl.ds(q0, blk), :], qn_s[1, pl.ds(q0, blk), :],
                kn_s[pl.ds(q0, 2 * blk), :], bias_of(2, first_tile & (j == 0)),
                pl.ds(q0, blk))

    def loop(n, body):
        lax.fori_loop(0, n, lambda i, c: (body(i), c)[1], 0, unroll=ATT_UNROLL)

    n_big, n_mid, n_nat = DIL_MAX, DIL_MID * mid_blocks, ATT_TILE // blk

    def row_max(ops, slot):
        q_lo, q_hi, kk, bias, rows = ops
        s = _scores(q_lo, q_hi, kk, bias)
        sc_s[slot] = s
        m = jnp.broadcast_to(jnp.max(s, axis=-1, keepdims=True), (2 * blk, LANES))
        return jnp.where(lane_b, m[:blk], m[blk:]), rows

    def max_big(r):
        m, rows = row_max(big_ops(r), r)
        mn_s[rows, :] = m

    def max_more(ops, slot):
        m, rows = row_max(ops, slot)
        mn_s[rows, :] = jnp.maximum(mn_s[rows, :], m)

    loop(n_big, max_big)
    loop(n_mid, lambda i: max_more(mid_ops(i), n_big + i))
    loop(n_nat, lambda j: max_more(nat_ops(j), n_big + n_mid + j))

    for r in range(DIL_MID):
        mm_s[r] = mn_s[pl.ds(r, MID_LEN, stride=DIL_MID), :]
    for r in range(DIL_MID):
        for c in range(DIL_STEP):
            mx_s[r + DIL_MID * c] = mm_s.at[r][pl.ds(c, blk, stride=DIL_STEP), :]

    def num_den(slot, m_pair, v_lo, v_hi):
        swapped = pltpu.roll(m_pair, ATT_HEAD_DIM, 1)
        m = jnp.concatenate([jnp.where(lane_b, m_pair, swapped),
                             jnp.where(lane_b, swapped, m_pair)], axis=0)
        e = jnp.exp2(sc_s[slot] - jnp.concatenate([m, m], axis=1)).astype(BF16)
        return (jnp.dot(e[:blk], v_lo, preferred_element_type=F32),
                jnp.dot(e[blk:], v_hi, preferred_element_type=F32))

    def acc_big(r):
        o_lo, o_hi = num_den(r, mx_s[r], vx_s[0, r], vx_s[1, r])
        rows = pl.ds(r, blk, stride=DIL_MAX)
        acc_s[0, rows, :] = o_lo
        acc_s[1, rows, :] = o_hi

    def acc_mid(i):
        r = i // mid_blocks
        j = i % mid_blocks
        q0 = pl.multiple_of(j * blk, blk)
        o_lo, o_hi = num_den(
            n_big + i, mm_s[r, pl.ds(q0, blk), :],
            vm_s[0, r, pl.ds(q0, 2 * blk), :], vm_s[1, r, pl.ds(q0, 2 * blk), :])
        rows = pl.ds(j * (blk * DIL_MID) + r, blk, stride=DIL_MID)
        acc_s[0, rows, :] += o_lo
        acc_s[1, rows, :] += o_hi

    def acc_nat(j):
        q0 = pl.multiple_of(j * blk, blk)
        rows = pl.ds(q0, blk)
        o_lo, o_hi = num_den(
            n_big + n_mid + j, mn_s[rows, :],
            vn_s[0, pl.ds(q0, 2 * blk), :], vn_s[1, pl.ds(q0, 2 * blk), :])
        a_lo = acc_s[0, rows, :] + o_lo
        a_hi = acc_s[1, rows, :] + o_hi
        num = jnp.where(lane_b, a_lo, a_hi)
        den = pltpu.roll(jnp.where(lane_b, a_hi, a_lo), ATT_HEAD_DIM, 1)
        att_ref[rows, :] = (num / den).astype(BF16)

    @pl.when(tile >= 0)
    def _pass_two():
        loop(n_big, acc_big)
        loop(n_mid, acc_mid)
        loop(n_nat, acc_nat)


def _dilated_attention(qkv, batch):
    t = qkv.shape[0]
    tiles = t // batch // ATT_TILE
    blk = ATT_BLOCK

    def spec(col0):
        return pl.BlockSpec((ATT_TILE, LANES), lambda b, p, i: (b * tiles + i, col0 + p))

    bf = lambda *shape: pltpu.VMEM(shape, BF16)
    f32 = lambda *shape: pltpu.VMEM(shape, F32)
    return pl.pallas_call(
        _attn_kernel,
        grid=(batch, ATT_PAIRS, tiles),
        in_specs=[spec(0), spec(ATT_PAIRS), spec(2 * ATT_PAIRS)],
        out_specs=spec(0),
        out_shape=jax.ShapeDtypeStruct((t, ATT_WIDTH), BF16),
        scratch_shapes=[
            bf(2, ATT_TILE, LANES), bf(blk + ATT_TILE, LANES), bf(2, blk + ATT_TILE, LANES),
            f32(DIL_MID, MID_LEN, LANES),
            bf(2, DIL_MID, MID_LEN, LANES), bf(DIL_MID, blk + MID_LEN, LANES),
            bf(2, DIL_MID, blk + MID_LEN, LANES),
            bf(2, DIL_MAX, blk, LANES), bf(DIL_MAX, 2 * blk, LANES),
            bf(2, DIL_MAX, 2 * blk, LANES),
            f32(ATT_TILE, LANES), f32(DIL_MID, MID_LEN, LANES),
            f32(DIL_MAX, blk, LANES), f32(2, ATT_TILE, LANES),
            f32(3 * 2, 2 * blk, 2 * blk),
            f32(3 * DIL_MAX, 2 * blk, 2 * blk),
        ],
        compiler_params=_params(3),
        name="attn",
    )(qkv, qkv, qkv)


def _split3(x):
    hi = x.astype(BF16)
    r1 = x - hi.astype(F32)
    mid = r1.astype(BF16)
    lo = (r1 - mid.astype(F32)).astype(BF16)
    return hi, mid, lo


def _dot_exact_rhs(x, w):
    return sum(jnp.dot(part, w, preferred_element_type=F32) for part in _split3(x))


TM_CHUNK = 128
TM_STEPS = TM_CHUNK // SUBLANES
TM_HIST = (CONV_WIDTH - 1) * SUBLANES
assert TM_CHUNK == SSD_CHUNK


def _to_time_major(src, mid, dst, base):
    quarter = TM_CHUNK // 4
    for r in range(4):
        mid[base + quarter * r:base + quarter * (r + 1), :] = (
            src[pl.ds(base + r, quarter, stride=4), :])
    for r in range(4):
        for c in range(4):
            v = r + 4 * c
            dst[base + SUBLANES * v:base + SUBLANES * (v + 1), :] = (
                mid[pl.ds(base + quarter * r + c, SUBLANES, stride=4), :])


def _tm_time(idx):
    return TM_STEPS * (idx % SUBLANES) + idx // SUBLANES


def _time_major_perm():
    n = jnp.arange(TM_CHUNK)
    return (_tm_time(n)[None, :] == n[:, None]).astype(BF16)


def _conv_time_major(x, hist, cw_ref, cb_ref, first_row):
    width = x.shape[-1]
    groups = TM_HIST // SUBLANES
    tail = x[TM_CHUNK - TM_HIST:, :]
    down = lambda a: pltpu.roll(a.reshape(groups, SUBLANES, width), 1, 1)
    wrapped = jnp.where(first_row, down(hist), down(tail)).reshape(TM_HIST, width)
    ext = jnp.concatenate([wrapped, x], axis=0)
    k_w = CONV_WIDTH
    conv = cb_ref[...] + x * cw_ref[k_w - 1:k_w, :]
    for back in range(1, k_w):
        off = TM_HIST - SUBLANES * back
        conv = conv + ext[off:off + TM_CHUNK, :] * cw_ref[k_w - 1 - back:k_w - back, :]
    return conv, tail


def _ssd_kernel(z_ref, xbc_ref, dt_ref, cw_ref, cb_ref, dtb_ref, alog_ref, dsk_ref, nw_ref,
                triu_ref, exp_ref, y_ref, win_s, state_s):
    c = pl.program_id(0)
    q = SSD_CHUNK
    pad = CONV_PAD
    rows_step = SSD_STEP_CHUNKS * q
    heads_per_group = SSD_HEADS // SSD_GROUPS
    gw = heads_per_group * SSD_HEAD_DIM
    n_seq = z_ref.shape[0]

    @pl.when(c == 0)
    def _reset():
        win_s[:, 0:pad, :] = jnp.zeros((n_seq, pad, SSD_CONV_CH), F32)
        state_s[...] = jnp.zeros_like(state_s)

    win_s[:, pad:, :] = xbc_ref[...]

    ti = lax.broadcasted_iota(jnp.int32, (q, q), 0)
    tj = lax.broadcasted_iota(jnp.int32, (q, q), 1)
    causal = ti >= tj
    low_half = lax.broadcasted_iota(jnp.int32, (q, LANES), 1) < SSD_HEAD_DIM
    a_t = -jnp.exp(alog_ref[...])
    k_w = CONV_WIDTH

    pairs_per_group = heads_per_group // HEADS_PER_PAIR

    def prepare(ci, b):
        r0 = ci * q
        win_b, dt_b = win_s.at[b], dt_ref.at[b]
        conv = cb_ref[...] + win_b[pad + r0:pad + r0 + q, :] * cw_ref[k_w - 1:k_w, :]
        for k in range(k_w - 1):
            off = pad + r0 - (k_w - 1) + k
            conv = conv + win_b[off:off + q, :] * cw_ref[k:k + 1, :]
        xbc = _silu(conv)
        xs = xbc[:, :SSD_WIDTH]
        bm = xbc[:, SSD_WIDTH:SSD_WIDTH + SSD_GROUPS * SSD_STATE]
        cm = xbc[:, SSD_WIDTH + SSD_GROUPS * SSD_STATE:]

        dt_t = _softplus(dt_b[r0:r0 + q, :].T[0:SSD_HEADS, :] + dtb_ref[...])
        acs_t = _dot_exact_rhs(dt_t * a_t, triu_ref[...])
        e_t = jnp.exp(acs_t)
        w_t = jnp.exp(acs_t[:, q - 1:q] - acs_t) * dt_t
        chunk_decay = jnp.sum(e_t[:, q - 1:q] * exp_ref[...], axis=0, keepdims=True)
        cols_t = jnp.concatenate(
            [acs_t, e_t, jnp.zeros((q - 2 * SSD_HEADS, q), F32)], axis=0).T

        pairs = []
        for g in range(SSD_GROUPS):
            bm_g = bm[:, g * SSD_STATE:(g + 1) * SSD_STATE]
            cm_g = cm[:, g * SSD_STATE:(g + 1) * SSD_STATE]
            gmat = lax.dot_general(cm_g.astype(BF16), bm_g.astype(BF16),
                                   (((1,), (1,)), ((), ())), preferred_element_type=F32)
            bm_gt = bm_g.T
            for pp in range(pairs_per_group):
                p = g * pairs_per_group + pp
                x_p = xs[:, p * LANES:(p + 1) * LANES]
                lhs_y, lhs_s, x_heads = [], [], []
                for hh in range(HEADS_PER_PAIR):
                    h = HEADS_PER_PAIR * p + hh
                    keep = low_half if hh == 0 else ~low_half
                    seg = cols_t[:, h:h + 1] - acs_t[h:h + 1, :]
                    lmat = jnp.exp(jnp.where(causal, seg, NEG_INF))
                    scores = gmat * lmat * dt_t[h:h + 1, :]
                    c_dec = cm_g * cols_t[:, SSD_HEADS + h:SSD_HEADS + h + 1]
                    lhs_y += [scores.astype(BF16), c_dec.astype(BF16)]
                    lhs_s.append((bm_gt * w_t[h:h + 1, :]).astype(BF16))
                    x_heads.append(jnp.where(keep, x_p, 0.0).astype(BF16))
                pairs.append((jnp.concatenate(lhs_y, axis=1), jnp.concatenate(lhs_s, axis=1),
                              x_heads))
        return xs, chunk_decay, pairs

    def finish(ci, b, prepared):
        xs, chunk_decay, pairs = prepared
        r0 = ci * q
        state_b, z_b, y_b = state_s.at[b], z_ref.at[b], y_ref.at[b]
        y_parts = []
        for p, (lhs_y, lhs_s, x_heads) in enumerate(pairs):
            cols = slice(p * LANES, (p + 1) * LANES)
            s_p = state_b[:, cols]
            rhs_y = []
            for hh in range(HEADS_PER_PAIR):
                keep = low_half if hh == 0 else ~low_half
                rhs_y += [x_heads[hh], jnp.where(keep, s_p, 0.0).astype(BF16)]
            y_parts.append(jnp.dot(lhs_y, jnp.concatenate(rhs_y, axis=0),
                                   preferred_element_type=F32))
            state_b[:, cols] = s_p * chunk_decay[:, cols] + jnp.dot(
                lhs_s, jnp.concatenate(x_heads, axis=0), preferred_element_type=F32)
        y = jnp.concatenate(y_parts, axis=-1) + dsk_ref[...] * xs
        y = y * _silu(z_b[r0:r0 + q, :])
        outs = []
        for g in range(SSD_GROUPS):
            yg = y[:, g * gw:(g + 1) * gw]
            outs.append(yg * lax.rsqrt(jnp.mean(yg * yg, axis=-1, keepdims=True) + SSD_NORM_EPS))
        y_b[r0:r0 + q, :] = (jnp.concatenate(outs, axis=-1) * nw_ref[...]).astype(BF16)

    order = [(ci, b) for ci in range(SSD_STEP_CHUNKS) for b in range(n_seq)]
    prepared = [prepare(ci, b) for ci, b in order]
    for (ci, b), prep in zip(order, prepared):
        finish(ci, b, prep)

    win_s[:, 0:pad, :] = win_s[:, rows_step:rows_step + pad, :]


def _prep_ssd(conv_w, conv_b, dt_bias, a_log, d_skip, norm_w):
    depth = conv_w.shape[0]
    per_time = lambda v: jnp.broadcast_to(v.astype(F32)[:, :, None],
                                          (depth, SSD_HEADS, SSD_CHUNK))
    triu = jnp.triu(jnp.ones((SSD_CHUNK, SSD_CHUNK), BF16))
    expand = (jnp.arange(SSD_HEADS)[:, None] == (jnp.arange(SSD_WIDTH)[None, :] // SSD_HEAD_DIM)
              ).astype(F32)
    d_exp = jnp.repeat(d_skip.astype(F32), SSD_HEAD_DIM, axis=1).reshape(depth, 1, SSD_WIDTH)
    return (conv_w, conv_b.reshape(depth, 1, SSD_CONV_CH), per_time(dt_bias), per_time(a_log),
            d_exp, norm_w.reshape(depth, 1, SSD_WIDTH), triu, expand)


def _ssd(z, xbc, dt, prepared, layer, batch):
    t = z.shape[0]
    seq = t // batch
    rows_step = SSD_STEP_CHUNKS * SSD_CHUNK
    row = lambda width: pl.BlockSpec((batch, rows_step, width), lambda c: (0, c, 0))
    by_seq = lambda a: a.reshape(batch, seq, a.shape[-1])
    lspec = lambda *shape: _layer_spec(shape, layer)
    out = pl.pallas_call(
        _ssd_kernel,
        grid=(seq // rows_step,),
        in_specs=[row(SSD_WIDTH), row(SSD_CONV_CH), row(DT_PAD),
                  lspec(CONV_WIDTH, SSD_CONV_CH), lspec(1, SSD_CONV_CH),
                  lspec(SSD_HEADS, SSD_CHUNK), lspec(SSD_HEADS, SSD_CHUNK),
                  lspec(1, SSD_WIDTH), lspec(1, SSD_WIDTH),
                  _const_spec((SSD_CHUNK, SSD_CHUNK)), _const_spec((SSD_HEADS, SSD_WIDTH))],
        out_specs=row(SSD_WIDTH),
        out_shape=jax.ShapeDtypeStruct((batch, seq, SSD_WIDTH), BF16),
        scratch_shapes=[pltpu.VMEM((batch, CONV_PAD + rows_step, SSD_CONV_CH), F32),
                        pltpu.VMEM((batch, SSD_STATE, SSD_WIDTH), F32)],
        compiler_params=_params(1),
        name="ssd",
    )(by_seq(z), by_seq(xbc), by_seq(dt), *prepared)
    return out.reshape(t, SSD_WIDTH)


def _gelu_tanh(x):
    c = math.sqrt(2.0 / math.pi)
    return 0.5 * x * (1.0 + jnp.tanh(c * (x + 0.044715 * (x * x * x))))


def _lru_kernel(*refs):
    ncol = LRU_WIDTH // LANES
    g_refs, x_refs = refs[:ncol], refs[ncol:2 * ncol]
    (cw_ref, cb_ref, wa_ref, ba_ref, wx_ref, bx_ref, lam_ref, perm_ref, y_ref,
     mid_s, gt_s, xt_s, hist_s, h_s) = refs[2 * ncol:]
    c = pl.program_id(1)
    tt = LRU_TILE
    n_chunks = tt // TM_CHUNK

    @pl.when(c == 0)
    def _reset():
        hist_s[...] = jnp.zeros_like(hist_s)
        h_s[...] = jnp.zeros_like(h_s)

    for j in range(ncol):
        for k in range(n_chunks):
            _to_time_major(g_refs[j], mid_s.at[j], gt_s.at[j], k * TM_CHUNK)
        for k in range(n_chunks):
            _to_time_major(x_refs[j], mid_s.at[j], xt_s.at[j], k * TM_CHUNK)

    def chunk_of(buf, k):
        rows = slice(k * TM_CHUNK, (k + 1) * TM_CHUNK)
        return jnp.concatenate([buf[j, rows, :] for j in range(ncol)], axis=1)

    first_row = lax.broadcasted_iota(
        jnp.int32, (TM_HIST // SUBLANES, SUBLANES, LRU_WIDTH), 1) == 0
    hist = hist_s[...]
    convs = []
    for k in range(n_chunks):
        conv, hist = _conv_time_major(chunk_of(xt_s, k), hist, cw_ref, cb_ref, first_row)
        convs.append(conv)
    hist_s[...] = hist
    xc = jnp.concatenate(convs, axis=0)

    xb = xc.astype(BF16)
    r = jax.nn.sigmoid(jnp.dot(xb, wa_ref[...], preferred_element_type=F32) + ba_ref[...])
    i = jax.nn.sigmoid(jnp.dot(xb, wx_ref[...], preferred_element_type=F32) + bx_ref[...])
    log_a = -LRU_C * r * _softplus(-lam_ref[...])
    a = jnp.exp(log_a)
    u = jnp.sqrt(-jnp.tanh(log_a) * (a * a + 1.0)) * (i * xc)

    sub = lax.broadcasted_iota(jnp.int32, (SUBLANES, LRU_WIDTH), 0)
    h_prev = h_s[0:1, :]
    for k in range(n_chunks):
        rows = slice(k * TM_CHUNK, (k + 1) * TM_CHUNK)
        a_k = a[rows].reshape(TM_STEPS, SUBLANES, LRU_WIDTH)
        u_k = u[rows].reshape(TM_STEPS, SUBLANES, LRU_WIDTH)
        hh, aa = [u_k[0]], [a_k[0]]
        for v in range(1, TM_STEPS):
            hh.append(a_k[v] * hh[-1] + u_k[v])
            aa.append(a_k[v] * aa[-1])
        pa, ph = aa[-1], hh[-1]
        step = 1
        while step < SUBLANES:
            keep = sub >= step
            pa_sh = jnp.where(keep, pltpu.roll(pa, step, 0), 1.0)
            ph_sh = jnp.where(keep, pltpu.roll(ph, step, 0), 0.0)
            ph = pa * ph_sh + ph
            pa = pa * pa_sh
            step *= 2
        c_in = jnp.where(sub == 0, h_prev, pltpu.roll(pa, 1, 0) * h_prev + pltpu.roll(ph, 1, 0))
        h_k = [hh[v] + aa[v] * c_in for v in range(TM_STEPS)]
        h_prev = h_k[-1][SUBLANES - 1:SUBLANES, :]
        y_k = (jnp.concatenate(h_k, axis=0) * _gelu_tanh(chunk_of(gt_s, k))).astype(BF16)
        y_ref[rows, :] = jnp.dot(perm_ref[...], y_k, preferred_element_type=F32).astype(BF16)
    h_s[...] = jnp.broadcast_to(h_prev, h_s.shape)


def _block_diag(w):
    depth, nb, c, d = w.shape
    eye = jnp.eye(nb, dtype=w.dtype)
    return (eye[None, :, None, :, None] * w[:, :, :, None, :]).reshape(depth, nb * c, nb * d)


def _prep_lru(conv_w, conv_b, wa, ba, wx, bx, lam):
    depth = conv_w.shape[0]
    vec = lambda v: v.astype(F32).reshape(depth, 1, LRU_WIDTH)
    return (conv_w, vec(conv_b), _block_diag(wa).astype(BF16), vec(ba),
            _block_diag(wx).astype(BF16), vec(bx), vec(lam))


def _lru(g_in, x_in, prepared, layer, batch):
    t = g_in.shape[0]
    s = t // batch
    nt = s // LRU_TILE
    ncol = LRU_WIDTH // LANES
    row = pl.BlockSpec((LRU_TILE, LRU_WIDTH), lambda b, c: (b * nt + c, 0))
    cols = [pl.BlockSpec((LRU_TILE, LANES), lambda b, c, j=j: (b * nt + c, j))
            for j in range(ncol)]
    mat = _layer_spec((LRU_WIDTH, LRU_WIDTH), layer)
    one = _layer_spec((1, LRU_WIDTH), layer)
    tile = lambda: pltpu.VMEM((ncol, LRU_TILE, LANES), F32)
    return pl.pallas_call(
        _lru_kernel,
        grid=(batch, nt),
        in_specs=cols + cols + [_layer_spec((CONV_WIDTH, LRU_WIDTH), layer), one, mat, one,
                                mat, one, one, _const_spec((TM_CHUNK, TM_CHUNK))],
        out_specs=row,
        out_shape=jax.ShapeDtypeStruct((t, LRU_WIDTH), BF16),
        scratch_shapes=[tile(), tile(), tile(),
                        pltpu.VMEM((TM_HIST, LRU_WIDTH), F32),
                        pltpu.VMEM((SUBLANES, LRU_WIDTH), F32)],
        compiler_params=_params(2),
        name="rglru",
    )(*([g_in] * ncol), *([x_in] * ncol), *prepared, _time_major_perm())


def _outffn_kernel(x_ref, att_ref, ssd_ref, lru_ref, wo_ref, gf_ref, wg_ref, wu_ref, wd_ref,
                   nfin_ref, o_ref, hn_s, *, final_norm):
    x1 = x_ref[...]
    for j, m_ref in enumerate((att_ref, ssd_ref, lru_ref)):
        x1 = x1 + jnp.dot(m_ref[...], wo_ref[j * ATT_WIDTH:(j + 1) * ATT_WIDTH, :],
                          preferred_element_type=F32)
    hn_s[...] = _rmsnorm(x1, gf_ref[...]).astype(BF16)
    o_ref[...] = x1
    for c0, c1 in FF_CHUNKS:
        hn = hn_s[...]
        gate = jnp.dot(hn, wg_ref[:, c0:c1], preferred_element_type=F32)
        up = jnp.dot(hn, wu_ref[:, c0:c1], preferred_element_type=F32)
        act = (_silu(gate) * up).astype(BF16)
        o_ref[...] += jnp.dot(act, wd_ref[c0:c1, :], preferred_element_type=F32)
    if final_norm:
        o_ref[...] = _rmsnorm(o_ref[...], nfin_ref[...])


def _prep_ffn(w_out, norm_ffn, w_gate, w_up, w_down, norm_final):
    depth = w_out.shape[0]
    return (w_out.astype(BF16), norm_ffn.reshape(depth, 1, D_MODEL), w_gate.astype(BF16),
            w_up.astype(BF16), w_down.astype(BF16), norm_final.reshape(1, D_MODEL))


def _outffn(x2, att, ssd, lru, prepared, layer, final_norm):
    t = x2.shape[0]
    row = lambda width: pl.BlockSpec((ROW_TILE, width), lambda i: (i, 0))
    lspec = lambda *shape: _layer_spec(shape, layer)
    return pl.pallas_call(
        functools.partial(_outffn_kernel, final_norm=final_norm),
        grid=(t // ROW_TILE,),
        in_specs=[row(D_MODEL), row(ATT_WIDTH), row(SSD_WIDTH), row(LRU_WIDTH),
                  lspec(D_MIX, D_MODEL), lspec(1, D_MODEL), lspec(D_MODEL, D_FF),
                  lspec(D_MODEL, D_FF), lspec(D_FF, D_MODEL), _const_spec((1, D_MODEL))],
        out_specs=row(D_MODEL),
        out_shape=jax.ShapeDtypeStruct((t, D_MODEL), F32),
        scratch_shapes=[pltpu.VMEM((ROW_TILE, D_MODEL), BF16)],
        compiler_params=_params(1),
        name="outffn",
    )(x2, att, ssd, lru, *prepared)


def _prep_w_in(w):
    tail = w[:, :, _C_XBC:]
    dt_cols = jnp.pad(tail[:, :, :SSD_HEADS], ((0, 0), (0, 0), (0, DT_PAD - SSD_HEADS)))
    w_tail = jnp.concatenate([tail[:, :, SSD_HEADS:], dt_cols], axis=2).astype(BF16)
    return w[:, :, :_C_XBC].astype(BF16), w_tail


def kernel(x, norm_mix, w_in, ssd_conv_w, ssd_conv_b, ssd_dt_bias, ssd_a_log, ssd_d, ssd_norm,
           lru_conv_w, lru_conv_b, lru_wa, lru_ba, lru_wx, lru_bx, lru_lambda, w_out,
           norm_ffn, w_gate, w_up, w_down, norm_final):
    batch, seq, _ = x.shape
    depth = w_in.shape[0]
    g_mix = norm_mix.reshape(depth, 1, D_MODEL)
    w_proj, w_tail = _prep_w_in(w_in)
    ssd_p = _prep_ssd(ssd_conv_w, ssd_conv_b, ssd_dt_bias, ssd_a_log, ssd_d, ssd_norm)
    lru_p = _prep_lru(lru_conv_w, lru_conv_b, lru_wa, lru_ba, lru_wx, lru_bx, lru_lambda)
    ffn_p = _prep_ffn(w_out, norm_ffn, w_gate, w_up, w_down, norm_final)
    x2 = x.reshape(batch * seq, D_MODEL)
    for l in range(depth):
        qkv, z, xbc, g_lru, x_lru, dt = _inproj(x2, g_mix, w_proj, w_tail, l)
        att = _dilated_attention(qkv, batch)
        ssd = _ssd(z, xbc, dt, ssd_p, l, batch)
        lru = _lru(g_lru, x_lru, lru_p, l, batch)
        x2 = _outffn(x2, att, ssd, lru, ffn_p, l, l == depth - 1)
    return x2.reshape(batch, seq, D_MODEL)
```

```python
import functools
import math

import jax
import jax.numpy as jnp
from jax import lax
from jax.experimental import pallas as pl
from jax.experimental.pallas import tpu as pltpu

F32 = jnp.float32
BF16 = jnp.bfloat16

D_MODEL = 1024
ATT_HEADS = 8
ATT_HEAD_DIM = 64
ATT_WIDTH = ATT_HEADS * ATT_HEAD_DIM
ATT_BLOCK = 128
ATT_DILATIONS = (1, 4, 16)
SSD_HEADS = 8
SSD_HEAD_DIM = 64
SSD_WIDTH = SSD_HEADS * SSD_HEAD_DIM
SSD_GROUPS = 2
SSD_STATE = 128
SSD_CHUNK = 128
SSD_CONV_CH = SSD_WIDTH + 2 * SSD_GROUPS * SSD_STATE
LRU_WIDTH = 512
LRU_BLOCKS = 8
LRU_BLOCK_W = LRU_WIDTH // LRU_BLOCKS
LRU_C = 8.0
CONV_WIDTH = 4
D_MIX = ATT_WIDTH + SSD_WIDTH + LRU_WIDTH
D_FF = 2816
NORM_EPS = 1e-6
SSD_NORM_EPS = 1e-5

LANES = 128
SUBLANES = 8
VMEM_LIMIT_BYTES = 56 * 1024 * 1024

ROW_TILE = 1024
LRU_TILE = 1024
SSD_STEP_CHUNKS = 4
DT_PAD = LANES
FF_CHUNKS = ((0, 768), (768, 1536), (1536, 2304), (2304, 2816))
CONV_PAD = SUBLANES

DIL_MID, DIL_MAX = ATT_DILATIONS[1], ATT_DILATIONS[2]
DIL_STEP = DIL_MAX // DIL_MID
ATT_TILE = ATT_BLOCK * DIL_MAX
MID_LEN = ATT_TILE // DIL_MID
HEADS_PER_PAIR = LANES // ATT_HEAD_DIM
ATT_PAIRS = ATT_WIDTH // LANES
ATT_UNROLL = 16
assert ATT_HEADS == 8 and HEADS_PER_PAIR == 2 and DIL_STEP == DIL_MID and ATT_DILATIONS[0] == 1

LOG2E = 1.4426950408889634
NEG_INF = float("-inf")


def _params(n_axes):
    return pltpu.CompilerParams(
        dimension_semantics=("arbitrary",) * n_axes,
        vmem_limit_bytes=VMEM_LIMIT_BYTES)


def _const_spec(shape):
    nd = len(shape)
    return pl.BlockSpec(shape, lambda *_: (0,) * nd, pipeline_mode=pl.Buffered(1))


def _layer_spec(shape, layer):
    nd = len(shape)
    return pl.BlockSpec((None,) + tuple(shape), lambda *_: (layer,) + (0,) * nd,
                        pipeline_mode=pl.Buffered(1))


def _rmsnorm(x, g):
    return x * lax.rsqrt(jnp.mean(x * x, axis=-1, keepdims=True) + NORM_EPS) * g


def _softplus(x):
    return jnp.maximum(x, 0.0) + jnp.log1p(jnp.exp(-jnp.abs(x)))


def _silu(x):
    return x * jax.nn.sigmoid(x)


_C_QKV = 3 * ATT_WIDTH
_C_Z = _C_QKV + SSD_WIDTH
_C_XBC = _C_Z + SSD_CONV_CH
_T_XL = LRU_WIDTH
_T_DT = 2 * LRU_WIDTH
_T_END = _T_DT + DT_PAD


def _inproj_kernel(x_ref, g_ref, w_ref, wt_ref, qkv_ref, z_ref, xbc_ref, gl_ref, xl_ref, dt_ref):
    h = _rmsnorm(x_ref[...], g_ref[...]).astype(BF16)

    def seg(ref, a, b):
        return jnp.dot(h, ref[:, a:b], preferred_element_type=F32)

    qkv_ref[:, 0:ATT_WIDTH] = seg(w_ref, 0, ATT_WIDTH) * (ATT_HEAD_DIM ** -0.5 * LOG2E)
    qkv_ref[:, ATT_WIDTH:_C_QKV] = seg(w_ref, ATT_WIDTH, _C_QKV)
    z_ref[...] = seg(w_ref, _C_QKV, _C_Z)
    xbc_ref[...] = seg(w_ref, _C_Z, _C_XBC)
    gl_ref[...] = seg(wt_ref, 0, _T_XL)
    xl_ref[...] = seg(wt_ref, _T_XL, _T_DT)
    dt_ref[...] = seg(wt_ref, _T_DT, _T_END)


def _inproj(x2, g, w, w_tail, layer):
    t = x2.shape[0]
    row = lambda width: pl.BlockSpec((ROW_TILE, width), lambda i: (i, 0))
    widths = (_C_QKV, SSD_WIDTH, SSD_CONV_CH, LRU_WIDTH, LRU_WIDTH, DT_PAD)
    return pl.pallas_call(
        _inproj_kernel,
        grid=(t // ROW_TILE,),
        in_specs=[row(D_MODEL), _layer_spec((1, D_MODEL), layer),
                  _layer_spec((D_MODEL, _C_XBC), layer),
                  _layer_spec((D_MODEL, _T_END), layer)],
        out_specs=[row(wd) for wd in widths],
        out_shape=[jax.ShapeDtypeStruct((t, wd), F32) for wd in widths],
        compiler_params=_params(1),
        name="inproj",
    )(x2, g, w, w_tail)


def _scores(q, keep_lo, keep_hi, kk, bias):
    q_both = jnp.concatenate([q * keep_lo, q * keep_hi], axis=0)
    return lax.dot_general(q_both, kk, (((1,), (1,)), ((), ())),
                           preferred_element_type=F32) + bias


def _attn_kernel(q_ref, k_ref, v_ref, att_ref,
                 qn_s, kn_s, vn_s, tmp_s, qm_s, km_s, vm_s, qx_s, kx_s, vx_s,
                 mn_s, mm_s, mx_s, acc_s, bias_s, sc_s):
    pair = pl.program_id(1)
    tile = pl.program_id(2)
    blk = ATT_BLOCK
    first_tile = tile == 0
    mid_blocks = MID_LEN // blk

    @pl.when(first_tile)
    def _start_sequence():
        qi = lax.broadcasted_iota(jnp.int32, (blk, 2 * blk), 0)
        ki = lax.broadcasted_iota(jnp.int32, (blk, 2 * blk), 1)
        dist = blk + qi - ki
        band = (dist >= 0) & (dist <= blk)
        band_first = band & (ki >= blk)
        for hh in range(HEADS_PER_PAIR):
            expo = (126 - HEADS_PER_PAIR * pair - hh) << 23
            slope = lax.bitcast_convert_type(jnp.full((blk, 2 * blk), expo, jnp.int32), F32)
            rows = slice(hh * blk, (hh + 1) * blk)
            for br, dil in enumerate((DIL_MAX, DIL_MID, 1)):
                alibi = (-slope * (dil * dist).astype(F32)) * LOG2E
                bias_s[2 * br, rows, :] = jnp.where(band, alibi, NEG_INF)
                bias_s[2 * br + 1, rows, :] = jnp.where(band_first, alibi, NEG_INF)
        for nat, mid, big in ((kn_s, km_s, kx_s), (vn_s, vm_s, vx_s)):
            nat[0:blk, :] = jnp.zeros((blk, LANES), BF16)
            mid[:, 0:blk, :] = jnp.zeros((DIL_MID, blk, LANES), BF16)
            big[:, 0:blk, :] = jnp.zeros((DIL_MAX, blk, LANES), BF16)

    @pl.when(tile > 0)
    def _carry_history():
        for nat, mid, big in ((kn_s, km_s, kx_s), (vn_s, vm_s, vx_s)):
            nat[0:blk, :] = nat[ATT_TILE:ATT_TILE + blk, :]
            mid[:, 0:blk, :] = mid[:, MID_LEN:MID_LEN + blk, :]
            big[:, 0:blk, :] = big[:, blk:2 * blk, :]

    lane_b = lax.broadcasted_iota(jnp.int32, (blk, LANES), 1) < ATT_HEAD_DIM
    keep_lo = jnp.where(lane_b, 1.0, 0.0).astype(BF16)
    keep_hi = jnp.where(lane_b, 0.0, 1.0).astype(BF16)
    keep2_lo = jnp.concatenate([keep_lo, keep_lo], axis=0)
    keep2_hi = jnp.concatenate([keep_hi, keep_hi], axis=0)

    for src, nat, mid, big, hist in ((q_ref, qn_s, qm_s, qx_s, 0),
                                     (k_ref, kn_s, km_s, kx_s, blk),
                                     (v_ref, vn_s, vm_s, vx_s, blk)):
        for c in range(DIL_MID):
            nat[hist + c * MID_LEN:hist + (c + 1) * MID_LEN, :] = (
                src[c * MID_LEN:(c + 1) * MID_LEN, :].astype(BF16))
        for r in range(DIL_MID):
            x = src[pl.ds(r, MID_LEN, stride=DIL_MID), :]
            tmp_s[r] = x
            mid[r, hist:hist + MID_LEN, :] = x.astype(BF16)
        for r in range(DIL_MID):
            for c in range(DIL_STEP):
                big[r + DIL_MID * c, hist:hist + blk, :] = (
                    tmp_s.at[r][pl.ds(c, blk, stride=DIL_STEP), :].astype(BF16))

    def bias_of(branch, is_first):
        return bias_s[2 * branch + jnp.where(is_first, 1, 0)]

    def big_ops(r):
        return (qx_s[r], kx_s[r], bias_of(0, first_tile), pl.ds(r, blk, stride=DIL_MAX))

    def mid_ops(i):
        r = i // mid_blocks
        j = i % mid_blocks
        q0 = pl.multiple_of(j * blk, blk)
        return (qm_s[r, pl.ds(q0, blk), :], km_s[r, pl.ds(q0, 2 * blk), :],
                bias_of(1, first_tile & (j == 0)),
                pl.ds(j * (blk * DIL_MID) + r, blk, stride=DIL_MID))

    def nat_ops(j):
        q0 = pl.multiple_of(j * blk, blk)
        return (qn_s[pl.ds(q0, blk), :], kn_s[pl.ds(q0, 2 * blk), :],
                bias_of(2, first_tile & (j == 0)), pl.ds(q0, blk))

    def loop(n, body):
        lax.fori_loop(0, n, lambda i, c: (body(i), c)[1], 0, unroll=ATT_UNROLL)

    n_big, n_mid, n_nat = DIL_MAX, DIL_MID * mid_blocks, ATT_TILE // blk

    def row_max(ops, slot):
        q, kk, bias, rows = ops
        s = _scores(q, keep_lo, keep_hi, kk, bias)
        sc_s[slot] = s
        m = jnp.broadcast_to(jnp.max(s, axis=-1, keepdims=True), (2 * blk, LANES))
        return jnp.where(lane_b, m[:blk], m[blk:]), rows

    def max_big(r):
        m, rows = row_max(big_ops(r), r)
        mn_s[rows, :] = m

    def max_more(ops, slot):
        m, rows = row_max(ops, slot)
        mn_s[rows, :] = jnp.maximum(mn_s[rows, :], m)

    loop(n_big, max_big)
    loop(n_mid, lambda i: max_more(mid_ops(i), n_big + i))
    loop(n_nat, lambda j: max_more(nat_ops(j), n_big + n_mid + j))

    for r in range(DIL_MID):
        mm_s[r] = mn_s[pl.ds(r, MID_LEN, stride=DIL_MID), :]
    for r in range(DIL_MID):
        for c in range(DIL_STEP):
            mx_s[r + DIL_MID * c] = mm_s.at[r][pl.ds(c, blk, stride=DIL_STEP), :]

    def num_den(slot, m_pair, vv):
        swapped = pltpu.roll(m_pair, ATT_HEAD_DIM, 1)
        m = jnp.concatenate([jnp.where(lane_b, m_pair, swapped),
                             jnp.where(lane_b, swapped, m_pair)], axis=0)
        e = jnp.exp2(sc_s[slot] - jnp.concatenate([m, m], axis=1)).astype(BF16)
        v_lo = vv * keep2_lo + keep2_hi
        v_hi = vv * keep2_hi + keep2_lo
        return (jnp.dot(e[:blk], v_lo, preferred_element_type=F32),
                jnp.dot(e[blk:], v_hi, preferred_element_type=F32))

    def acc_big(r):
        o_lo, o_hi = num_den(r, mx_s[r], vx_s[r])
        rows = pl.ds(r, blk, stride=DIL_MAX)
        acc_s[0, rows, :] = o_lo
        acc_s[1, rows, :] = o_hi

    def acc_mid(i):
        r = i // mid_blocks
        j = i % mid_blocks
        q0 = pl.multiple_of(j * blk, blk)
        o_lo, o_hi = num_den(n_big + i, mm_s[r, pl.ds(q0, blk), :],
                             vm_s[r, pl.ds(q0, 2 * blk), :])
        rows = pl.ds(j * (blk * DIL_MID) + r, blk, stride=DIL_MID)
        acc_s[0, rows, :] += o_lo
        acc_s[1, rows, :] += o_hi

    def acc_nat(j):
        q0 = pl.multiple_of(j * blk, blk)
        rows = pl.ds(q0, blk)
        o_lo, o_hi = num_den(n_big + n_mid + j, mn_s[rows, :], vn_s[pl.ds(q0, 2 * blk), :])
        a_lo = acc_s[0, rows, :] + o_lo
        a_hi = acc_s[1, rows, :] + o_hi
        num = jnp.where(lane_b, a_lo, a_hi)
        den = pltpu.roll(jnp.where(lane_b, a_hi, a_lo), ATT_HEAD_DIM, 1)
        att_ref[rows, :] = (num / den).astype(BF16)

    @pl.when(tile >= 0)
    def _pass_two():
        loop(n_big, acc_big)
        loop(n_mid, acc_mid)
        loop(n_nat, acc_nat)


def _dilated_attention(qkv, batch):
    t = qkv.shape[0]
    tiles = t // batch // ATT_TILE
    blk = ATT_BLOCK

    def spec(col0):
        return pl.BlockSpec((ATT_TILE, LANES), lambda b, p, i: (b * tiles + i, col0 + p))

    bf = lambda *shape: pltpu.VMEM(shape, BF16)
    f32 = lambda *shape: pltpu.VMEM(shape, F32)
    return pl.pallas_call(
        _attn_kernel,
        grid=(batch, ATT_PAIRS, tiles),
        in_specs=[spec(0), spec(ATT_PAIRS), spec(2 * ATT_PAIRS)],
        out_specs=spec(0),
        out_shape=jax.ShapeDtypeStruct((t, ATT_WIDTH), BF16),
        scratch_shapes=[
            bf(ATT_TILE, LANES), bf(blk + ATT_TILE, LANES), bf(blk + ATT_TILE, LANES),
            f32(DIL_MID, MID_LEN, LANES),
            bf(DIL_MID, MID_LEN, LANES), bf(DIL_MID, blk + MID_LEN, LANES),
            bf(DIL_MID, blk + MID_LEN, LANES),
            bf(DIL_MAX, blk, LANES), bf(DIL_MAX, 2 * blk, LANES),
            bf(DIL_MAX, 2 * blk, LANES),
            f32(ATT_TILE, LANES), f32(DIL_MID, MID_LEN, LANES),
            f32(DIL_MAX, blk, LANES), f32(2, ATT_TILE, LANES),
            f32(3 * 2, 2 * blk, 2 * blk),
            f32(3 * DIL_MAX, 2 * blk, 2 * blk),
        ],
        compiler_params=_params(3),
        name="attn",
    )(qkv, qkv, qkv)


def _split3(x):
    hi = x.astype(BF16)
    r1 = x - hi.astype(F32)
    mid = r1.astype(BF16)
    lo = (r1 - mid.astype(F32)).astype(BF16)
    return hi, mid, lo


def _dot_exact_rhs(x, w):
    return sum(jnp.dot(part, w, preferred_element_type=F32) for part in _split3(x))


TM_CHUNK = 128
TM_STEPS = TM_CHUNK // SUBLANES
TM_HIST = (CONV_WIDTH - 1) * SUBLANES
assert TM_CHUNK == SSD_CHUNK


def _to_time_major(src, mid, dst, base):
    quarter = TM_CHUNK // 4
    for r in range(4):
        mid[base + quarter * r:base + quarter * (r + 1), :] = (
            src[pl.ds(base + r, quarter, stride=4), :])
    for r in range(4):
        for c in range(4):
            v = r + 4 * c
            dst[base + SUBLANES * v:base + SUBLANES * (v + 1), :] = (
                mid[pl.ds(base + quarter * r + c, SUBLANES, stride=4), :])


def _tm_time(idx):
    return TM_STEPS * (idx % SUBLANES) + idx // SUBLANES


def _time_major_perm():
    n = jnp.arange(TM_CHUNK)
    return (_tm_time(n)[None, :] == n[:, None]).astype(BF16)


def _conv_time_major(x, hist, cw_ref, cb_ref, first_row):
    width = x.shape[-1]
    groups = TM_HIST // SUBLANES
    tail = x[TM_CHUNK - TM_HIST:, :]
    down = lambda a: pltpu.roll(a.reshape(groups, SUBLANES, width), 1, 1)
    wrapped = jnp.where(first_row, down(hist), down(tail)).reshape(TM_HIST, width)
    ext = jnp.concatenate([wrapped, x], axis=0)
    k_w = CONV_WIDTH
    conv = cb_ref[...] + x * cw_ref[k_w - 1:k_w, :]
    for back in range(1, k_w):
        off = TM_HIST - SUBLANES * back
        conv = conv + ext[off:off + TM_CHUNK, :] * cw_ref[k_w - 1 - back:k_w - back, :]
    return conv, tail


def _ssd_kernel(z_ref, xbc_ref, dt_ref, cw_ref, cb_ref, dtb_ref, alog_ref, dsk_ref, nw_ref,
                triu_ref, exp_ref, y_ref, win_s, state_s):
    c = pl.program_id(0)
    q = SSD_CHUNK
    pad = CONV_PAD
    rows_step = SSD_STEP_CHUNKS * q
    heads_per_group = SSD_HEADS // SSD_GROUPS
    gw = heads_per_group * SSD_HEAD_DIM
    n_seq = z_ref.shape[0]

    @pl.when(c == 0)
    def _reset():
        win_s[:, 0:pad, :] = jnp.zeros((n_seq, pad, SSD_CONV_CH), F32)
        state_s[...] = jnp.zeros_like(state_s)

    win_s[:, pad:, :] = xbc_ref[...]

    ti = lax.broadcasted_iota(jnp.int32, (q, q), 0)
    tj = lax.broadcasted_iota(jnp.int32, (q, q), 1)
    causal = ti >= tj
    low_half = lax.broadcasted_iota(jnp.int32, (q, LANES), 1) < SSD_HEAD_DIM
    a_t = -jnp.exp(alog_ref[...])
    k_w = CONV_WIDTH

    pairs_per_group = heads_per_group // HEADS_PER_PAIR

    def prepare(ci, b):
        r0 = ci * q
        win_b, dt_b = win_s.at[b], dt_ref.at[b]
        conv = cb_ref[...] + win_b[pad + r0:pad + r0 + q, :] * cw_ref[k_w - 1:k_w, :]
        for k in range(k_w - 1):
            off = pad + r0 - (k_w - 1) + k
            conv = conv + win_b[off:off + q, :] * cw_ref[k:k + 1, :]
        xbc = _silu(conv)
        xs = xbc[:, :SSD_WIDTH]
        bm = xbc[:, SSD_WIDTH:SSD_WIDTH + SSD_GROUPS * SSD_STATE]
        cm = xbc[:, SSD_WIDTH + SSD_GROUPS * SSD_STATE:]

        dt_t = _softplus(dt_b[r0:r0 + q, :].T[0:SSD_HEADS, :] + dtb_ref[...])
        acs_t = _dot_exact_rhs(dt_t * a_t, triu_ref[...])
        e_t = jnp.exp(acs_t)
        w_t = jnp.exp(acs_t[:, q - 1:q] - acs_t) * dt_t
        chunk_decay = jnp.sum(e_t[:, q - 1:q] * exp_ref[...], axis=0, keepdims=True)
        cols_t = jnp.concatenate(
            [acs_t, e_t, jnp.zeros((q - 2 * SSD_HEADS, q), F32)], axis=0).T

        pairs = []
        for g in range(SSD_GROUPS):
            bm_g = bm[:, g * SSD_STATE:(g + 1) * SSD_STATE]
            cm_g = cm[:, g * SSD_STATE:(g + 1) * SSD_STATE]
            gmat = lax.dot_general(cm_g.astype(BF16), bm_g.astype(BF16),
                                   (((1,), (1,)), ((), ())), preferred_element_type=F32)
            bm_gt = bm_g.T
            for pp in range(pairs_per_group):
                p = g * pairs_per_group + pp
                x_p = xs[:, p * LANES:(p + 1) * LANES]
                lhs_y, lhs_s, x_heads = [], [], []
                for hh in range(HEADS_PER_PAIR):
                    h = HEADS_PER_PAIR * p + hh
                    keep = low_half if hh == 0 else ~low_half
                    seg = cols_t[:, h:h + 1] - acs_t[h:h + 1, :]
                    lmat = jnp.exp(jnp.where(causal, seg, NEG_INF))
                    scores = gmat * lmat * dt_t[h:h + 1, :]
                    c_dec = cm_g * cols_t[:, SSD_HEADS + h:SSD_HEADS + h + 1]
                    lhs_y += [scores.astype(BF16), c_dec.astype(BF16)]
                    lhs_s.append((bm_gt * w_t[h:h + 1, :]).astype(BF16))
                    x_heads.append(jnp.where(keep, x_p, 0.0).astype(BF16))
                pairs.append((jnp.concatenate(lhs_y, axis=1), jnp.concatenate(lhs_s, axis=1),
                              x_heads))
        return xs, chunk_decay, pairs

    def finish(ci, b, prepared):
        xs, chunk_decay, pairs = prepared
        r0 = ci * q
        state_b, z_b, y_b = state_s.at[b], z_ref.at[b], y_ref.at[b]
        y_parts = []
        for p, (lhs_y, lhs_s, x_heads) in enumerate(pairs):
            cols = slice(p * LANES, (p + 1) * LANES)
            s_p = state_b[:, cols]
            rhs_y = []
            for hh in range(HEADS_PER_PAIR):
                keep = low_half if hh == 0 else ~low_half
                rhs_y += [x_heads[hh], jnp.where(keep, s_p, 0.0).astype(BF16)]
            y_parts.append(jnp.dot(lhs_y, jnp.concatenate(rhs_y, axis=0),
                                   preferred_element_type=F32))
            state_b[:, cols] = s_p * chunk_decay[:, cols] + jnp.dot(
                lhs_s, jnp.concatenate(x_heads, axis=0), preferred_element_type=F32)
        y = jnp.concatenate(y_parts, axis=-1) + dsk_ref[...] * xs
        y = y * _silu(z_b[r0:r0 + q, :])
        outs = []
        for g in range(SSD_GROUPS):
            yg = y[:, g * gw:(g + 1) * gw]
            outs.append(yg * lax.rsqrt(jnp.mean(yg * yg, axis=-1, keepdims=True) + SSD_NORM_EPS))
        y_b[r0:r0 + q, :] = (jnp.concatenate(outs, axis=-1) * nw_ref[...]).astype(BF16)

    order = [(ci, b) for ci in range(SSD_STEP_CHUNKS) for b in range(n_seq)]
    prepared = [prepare(ci, b) for ci, b in order]
    for (ci, b), prep in zip(order, prepared):
        finish(ci, b, prep)

    win_s[:, 0:pad, :] = win_s[:, rows_step:rows_step + pad, :]


def _prep_ssd(conv_w, conv_b, dt_bias, a_log, d_skip, norm_w):
    depth = conv_w.shape[0]
    per_time = lambda v: jnp.broadcast_to(v.astype(F32)[:, :, None],
                                          (depth, SSD_HEADS, SSD_CHUNK))
    triu = jnp.triu(jnp.ones((SSD_CHUNK, SSD_CHUNK), BF16))
    expand = (jnp.arange(SSD_HEADS)[:, None] == (jnp.arange(SSD_WIDTH)[None, :] // SSD_HEAD_DIM)
              ).astype(F32)
    d_exp = jnp.repeat(d_skip.astype(F32), SSD_HEAD_DIM, axis=1).reshape(depth, 1, SSD_WIDTH)
    return (conv_w, conv_b.reshape(depth, 1, SSD_CONV_CH), per_time(dt_bias), per_time(a_log),
            d_exp, norm_w.reshape(depth, 1, SSD_WIDTH), triu, expand)


def _ssd(z, xbc, dt, prepared, layer, batch):
    t = z.shape[0]
    seq = t // batch
    rows_step = SSD_STEP_CHUNKS * SSD_CHUNK
    row = lambda width: pl.BlockSpec((batch, rows_step, width), lambda c: (0, c, 0))
    by_seq = lambda a: a.reshape(batch, seq, a.shape[-1])
    lspec = lambda *shape: _layer_spec(shape, layer)
    out = pl.pallas_call(
        _ssd_kernel,
        grid=(seq // rows_step,),
        in_specs=[row(SSD_WIDTH), row(SSD_CONV_CH), row(DT_PAD),
                  lspec(CONV_WIDTH, SSD_CONV_CH), lspec(1, SSD_CONV_CH),
                  lspec(SSD_HEADS, SSD_CHUNK), lspec(SSD_HEADS, SSD_CHUNK),
                  lspec(1, SSD_WIDTH), lspec(1, SSD_WIDTH),
                  _const_spec((SSD_CHUNK, SSD_CHUNK)), _const_spec((SSD_HEADS, SSD_WIDTH))],
        out_specs=row(SSD_WIDTH),
        out_shape=jax.ShapeDtypeStruct((batch, seq, SSD_WIDTH), BF16),
        scratch_shapes=[pltpu.VMEM((batch, CONV_PAD + rows_step, SSD_CONV_CH), F32),
                        pltpu.VMEM((batch, SSD_STATE, SSD_WIDTH), F32)],
        compiler_params=_params(1),
        name="ssd",
    )(by_seq(z), by_seq(xbc), by_seq(dt), *prepared)
    return out.reshape(t, SSD_WIDTH)


def _gelu_tanh(x):
    c = math.sqrt(2.0 / math.pi)
    return 0.5 * x * (1.0 + jnp.tanh(c * (x + 0.044715 * (x * x * x))))


def _lru_kernel(*refs):
    ncol = LRU_WIDTH // LANES
    g_refs, x_refs = refs[:ncol], refs[ncol:2 * ncol]
    (cw_ref, cb_ref, wa_ref, ba_ref, wx_ref, bx_ref, lam_ref, perm_ref, y_ref,
     mid_s, gt_s, xt_s, hist_s, h_s) = refs[2 * ncol:]
    c = pl.program_id(1)
    tt = LRU_TILE
    n_chunks = tt // TM_CHUNK

    @pl.when(c == 0)
    def _reset():
        hist_s[...] = jnp.zeros_like(hist_s)
        h_s[...] = jnp.zeros_like(h_s)

    for j in range(ncol):
        for k in range(n_chunks):
            _to_time_major(g_refs[j], mid_s.at[j], gt_s.at[j], k * TM_CHUNK)
        for k in range(n_chunks):
            _to_time_major(x_refs[j], mid_s.at[j], xt_s.at[j], k * TM_CHUNK)

    def chunk_of(buf, k):
        rows = slice(k * TM_CHUNK, (k + 1) * TM_CHUNK)
        return jnp.concatenate([buf[j, rows, :] for j in range(ncol)], axis=1)

    first_row = lax.broadcasted_iota(
        jnp.int32, (TM_HIST // SUBLANES, SUBLANES, LRU_WIDTH), 1) == 0
    hist = hist_s[...]
    convs = []
    for k in range(n_chunks):
        conv, hist = _conv_time_major(chunk_of(xt_s, k), hist, cw_ref, cb_ref, first_row)
        convs.append(conv)
    hist_s[...] = hist
    xc = jnp.concatenate(convs, axis=0)

    xb = xc.astype(BF16)
    r = jax.nn.sigmoid(jnp.dot(xb, wa_ref[...], preferred_element_type=F32) + ba_ref[...])
    i = jax.nn.sigmoid(jnp.dot(xb, wx_ref[...], preferred_element_type=F32) + bx_ref[...])
    log_a = -LRU_C * r * _softplus(-lam_ref[...])
    a = jnp.exp(log_a)
    u = jnp.sqrt(-jnp.tanh(log_a) * (a * a + 1.0)) * (i * xc)

    sub = lax.broadcasted_iota(jnp.int32, (SUBLANES, LRU_WIDTH), 0)
    h_prev = h_s[0:1, :]
    for k in range(n_chunks):
        rows = slice(k * TM_CHUNK, (k + 1) * TM_CHUNK)
        a_k = a[rows].reshape(TM_STEPS, SUBLANES, LRU_WIDTH)
        u_k = u[rows].reshape(TM_STEPS, SUBLANES, LRU_WIDTH)
        hh, aa = [u_k[0]], [a_k[0]]
        for v in range(1, TM_STEPS):
            hh.append(a_k[v] * hh[-1] + u_k[v])
            aa.append(a_k[v] * aa[-1])
        pa, ph = aa[-1], hh[-1]
        step = 1
        while step < SUBLANES:
            keep = sub >= step
            pa_sh = jnp.where(keep, pltpu.roll(pa, step, 0), 1.0)
            ph_sh = jnp.where(keep, pltpu.roll(ph, step, 0), 0.0)
            ph = pa * ph_sh + ph
            pa = pa * pa_sh
            step *= 2
        c_in = jnp.where(sub == 0, h_prev, pltpu.roll(pa, 1, 0) * h_prev + pltpu.roll(ph, 1, 0))
        h_k = [hh[v] + aa[v] * c_in for v in range(TM_STEPS)]
        h_prev = h_k[-1][SUBLANES - 1:SUBLANES, :]
        y_k = (jnp.concatenate(h_k, axis=0) * _gelu_tanh(chunk_of(gt_s, k))).astype(BF16)
        y_ref[rows, :] = jnp.dot(perm_ref[...], y_k, preferred_element_type=F32).astype(BF16)
    h_s[...] = jnp.broadcast_to(h_prev, h_s.shape)


def _block_diag(w):
    depth, nb, c, d = w.shape
    eye = jnp.eye(nb, dtype=w.dtype)
    return (eye[None, :, None, :, None] * w[:, :, :, None, :]).reshape(depth, nb * c, nb * d)


def _prep_lru(conv_w, conv_b, wa, ba, wx, bx, lam):
    depth = conv_w.shape[0]
    vec = lambda v: v.astype(F32).reshape(depth, 1, LRU_WIDTH)
    return (conv_w, vec(conv_b), _block_diag(wa).astype(BF16), vec(ba),
            _block_diag(wx).astype(BF16), vec(bx), vec(lam))


def _lru(g_in, x_in, prepared, layer, batch):
    t = g_in.shape[0]
    s = t // batch
    nt = s // LRU_TILE
    ncol = LRU_WIDTH // LANES
    row = pl.BlockSpec((LRU_TILE, LRU_WIDTH), lambda b, c: (b * nt + c, 0))
    cols = [pl.BlockSpec((LRU_TILE, LANES), lambda b, c, j=j: (b * nt + c, j))
            for j in range(ncol)]
    mat = _layer_spec((LRU_WIDTH, LRU_WIDTH), layer)
    one = _layer_spec((1, LRU_WIDTH), layer)
    tile = lambda: pltpu.VMEM((ncol, LRU_TILE, LANES), F32)
    return pl.pallas_call(
        _lru_kernel,
        grid=(batch, nt),
        in_specs=cols + cols + [_layer_spec((CONV_WIDTH, LRU_WIDTH), layer), one, mat, one,
                                mat, one, one, _const_spec((TM_CHUNK, TM_CHUNK))],
        out_specs=row,
        out_shape=jax.ShapeDtypeStruct((t, LRU_WIDTH), BF16),
        scratch_shapes=[tile(), tile(), tile(),
                        pltpu.VMEM((TM_HIST, LRU_WIDTH), F32),
                        pltpu.VMEM((SUBLANES, LRU_WIDTH), F32)],
        compiler_params=_params(2),
        name="rglru",
    )(*([g_in] * ncol), *([x_in] * ncol), *prepared, _time_major_perm())


def _outffn_kernel(x_ref, att_ref, ssd_ref, lru_ref, wo_ref, gf_ref, wg_ref, wu_ref, wd_ref,
                   nfin_ref, o_ref, hn_s, *, final_norm):
    x1 = x_ref[...]
    for j, m_ref in enumerate((att_ref, ssd_ref, lru_ref)):
        x1 = x1 + jnp.dot(m_ref[...], wo_ref[j * ATT_WIDTH:(j + 1) * ATT_WIDTH, :],
                          preferred_element_type=F32)
    hn_s[...] = _rmsnorm(x1, gf_ref[...]).astype(BF16)
    o_ref[...] = x1
    for c0, c1 in FF_CHUNKS:
        hn = hn_s[...]
        gate = jnp.dot(hn, wg_ref[:, c0:c1], preferred_element_type=F32)
        up = jnp.dot(hn, wu_ref[:, c0:c1], preferred_element_type=F32)
        act = (_silu(gate) * up).astype(BF16)
        o_ref[...] += jnp.dot(act, wd_ref[c0:c1, :], preferred_element_type=F32)
    if final_norm:
        o_ref[...] = _rmsnorm(o_ref[...], nfin_ref[...])


def _prep_ffn(w_out, norm_ffn, w_gate, w_up, w_down, norm_final):
    depth = w_out.shape[0]
    return (w_out.astype(BF16), norm_ffn.reshape(depth, 1, D_MODEL), w_gate.astype(BF16),
            w_up.astype(BF16), w_down.astype(BF16), norm_final.reshape(1, D_MODEL))


def _outffn(x2, att, ssd, lru, prepared, layer, final_norm):
    t = x2.shape[0]
    row = lambda width: pl.BlockSpec((ROW_TILE, width), lambda i: (i, 0))
    lspec = lambda *shape: _layer_spec(shape, layer)
    return pl.pallas_call(
        functools.partial(_outffn_kernel, final_norm=final_norm),
        grid=(t // ROW_TILE,),
        in_specs=[row(D_MODEL), row(ATT_WIDTH), row(SSD_WIDTH), row(LRU_WIDTH),
                  lspec(D_MIX, D_MODEL), lspec(1, D_MODEL), lspec(D_MODEL, D_FF),
                  lspec(D_MODEL, D_FF), lspec(D_FF, D_MODEL), _const_spec((1, D_MODEL))],
        out_specs=row(D_MODEL),
        out_shape=jax.ShapeDtypeStruct((t, D_MODEL), F32),
        scratch_shapes=[pltpu.VMEM((ROW_TILE, D_MODEL), BF16)],
        compiler_params=_params(1),
        name="outffn",
    )(x2, att, ssd, lru, *prepared)


def _cast_head_kernel(w_ref, o_ref):
    o_ref[...] = w_ref[...].astype(BF16)


def _prep_w_in(w):
    depth = w.shape[0]
    tail = w[:, :, _C_XBC:]
    dt_cols = jnp.pad(tail[:, :, :SSD_HEADS], ((0, 0), (0, 0), (0, DT_PAD - SSD_HEADS)))
    w_tail = jnp.concatenate([tail[:, :, SSD_HEADS:], dt_cols], axis=2).astype(BF16)
    rows = D_MODEL // 4
    spec = pl.BlockSpec((None, rows, _C_XBC), lambda l, i: (l, i, 0))
    w_head = pl.pallas_call(
        _cast_head_kernel,
        grid=(depth, D_MODEL // rows),
        in_specs=[spec],
        out_specs=spec,
        out_shape=jax.ShapeDtypeStruct((depth, D_MODEL, _C_XBC), BF16),
        compiler_params=_params(2),
        name="cast_w_in",
    )(w)
    return w_head, w_tail


def kernel(x, norm_mix, w_in, ssd_conv_w, ssd_conv_b, ssd_dt_bias, ssd_a_log, ssd_d, ssd_norm,
           lru_conv_w, lru_conv_b, lru_wa, lru_ba, lru_wx, lru_bx, lru_lambda, w_out,
           norm_ffn, w_gate, w_up, w_down, norm_final):
    batch, seq, _ = x.shape
    depth = w_in.shape[0]
    g_mix = norm_mix.reshape(depth, 1, D_MODEL)
    w_proj, w_tail = _prep_w_in(w_in)
    ssd_p = _prep_ssd(ssd_conv_w, ssd_conv_b, ssd_dt_bias, ssd_a_log, ssd_d, ssd_norm)
    lru_p = _prep_lru(lru_conv_w, lru_conv_b, lru_wa, lru_ba, lru_wx, lru_bx, lru_lambda)
    ffn_p = _prep_ffn(w_out, norm_ffn, w_gate, w_up, w_down, norm_final)
    x2 = x.reshape(batch * seq, D_MODEL)
    for l in range(depth):
        qkv, z, xbc, g_lru, x_lru, dt = _inproj(x2, g_mix, w_proj, w_tail, l)
        att = _dilated_attention(qkv, batch)
        ssd = _ssd(z, xbc, dt, ssd_p, l, batch)
        lru = _lru(g_lru, x_lru, lru_p, l, batch)
        x2 = _outffn(x2, att, ssd, lru, ffn_p, l, l == depth - 1)
    return x2.reshape(batch, seq, D_MODEL)
```

```python
import functools
import math

import jax
import jax.numpy as jnp
from jax import lax
from jax.experimental import pallas as pl
from jax.experimental.pallas import tpu as pltpu

F32 = jnp.float32
BF16 = jnp.bfloat16

D_MODEL = 1024
ATT_HEADS = 8
ATT_HEAD_DIM = 64
ATT_WIDTH = ATT_HEADS * ATT_HEAD_DIM
ATT_BLOCK = 128
ATT_DILATIONS = (1, 4, 16)
SSD_HEADS = 8
SSD_HEAD_DIM = 64
SSD_WIDTH = SSD_HEADS * SSD_HEAD_DIM
SSD_GROUPS = 2
SSD_STATE = 128
SSD_CHUNK = 128
SSD_CONV_CH = SSD_WIDTH + 2 * SSD_GROUPS * SSD_STATE
LRU_WIDTH = 512
LRU_BLOCKS = 8
LRU_BLOCK_W = LRU_WIDTH // LRU_BLOCKS
LRU_C = 8.0
CONV_WIDTH = 4
D_MIX = ATT_WIDTH + SSD_WIDTH + LRU_WIDTH
D_FF = 2816
NORM_EPS = 1e-6
SSD_NORM_EPS = 1e-5

LANES = 128
SUBLANES = 8
VMEM_LIMIT_BYTES = 56 * 1024 * 1024

ROW_TILE = 1024
LRU_TILE = 2048
SSD_STEP_CHUNKS = 4
DT_PAD = LANES
FF_CHUNKS = ((0, 768), (768, 1536), (1536, 2304), (2304, 2816))
CONV_PAD = SUBLANES

DIL_MID, DIL_MAX = ATT_DILATIONS[1], ATT_DILATIONS[2]
DIL_STEP = DIL_MAX // DIL_MID
ATT_TILE = ATT_BLOCK * DIL_MAX
MID_LEN = ATT_TILE // DIL_MID
HEADS_PER_PAIR = LANES // ATT_HEAD_DIM
ATT_PAIRS = ATT_WIDTH // LANES
ATT_UNROLL = 16
assert ATT_HEADS == 8 and HEADS_PER_PAIR == 2 and DIL_STEP == DIL_MID and ATT_DILATIONS[0] == 1

LOG2E = 1.4426950408889634
NEG_INF = float("-inf")


def _params(n_axes):
    return pltpu.CompilerParams(
        dimension_semantics=("arbitrary",) * n_axes,
        vmem_limit_bytes=VMEM_LIMIT_BYTES)


def _const_spec(shape):
    nd = len(shape)
    return pl.BlockSpec(shape, lambda *_: (0,) * nd, pipeline_mode=pl.Buffered(1))


def _layer_spec(shape, layer):
    nd = len(shape)
    return pl.BlockSpec((None,) + tuple(shape), lambda *_: (layer,) + (0,) * nd,
                        pipeline_mode=pl.Buffered(1))


def _rmsnorm(x, g):
    return x * lax.rsqrt(jnp.mean(x * x, axis=-1, keepdims=True) + NORM_EPS) * g


def _softplus(x):
    return jnp.maximum(x, 0.0) + jnp.log1p(jnp.exp(-jnp.abs(x)))


def _silu(x):
    return x * jax.nn.sigmoid(x)


_C_QKV = 3 * ATT_WIDTH
_C_Z = _C_QKV + SSD_WIDTH
_C_XBC = _C_Z + SSD_CONV_CH
_T_XL = LRU_WIDTH
_T_DT = 2 * LRU_WIDTH
_T_END = _T_DT + DT_PAD


def _inproj_kernel(x_ref, g_ref, w_ref, wt_ref, qkv_ref, z_ref, xbc_ref, gl_ref, xl_ref, dt_ref):
    h = _rmsnorm(x_ref[...], g_ref[...]).astype(BF16)

    def seg(ref, a, b):
        return jnp.dot(h, ref[:, a:b], preferred_element_type=F32)

    qkv_ref[:, 0:ATT_WIDTH] = seg(w_ref, 0, ATT_WIDTH) * (ATT_HEAD_DIM ** -0.5 * LOG2E)
    qkv_ref[:, ATT_WIDTH:_C_QKV] = seg(w_ref, ATT_WIDTH, _C_QKV)
    z_ref[...] = seg(w_ref, _C_QKV, _C_Z)
    xbc_ref[...] = seg(w_ref, _C_Z, _C_XBC)
    gl_ref[...] = seg(wt_ref, 0, _T_XL)
    xl_ref[...] = seg(wt_ref, _T_XL, _T_DT)
    dt_ref[...] = seg(wt_ref, _T_DT, _T_END)


def _inproj(x2, g, w, w_tail, layer):
    t = x2.shape[0]
    row = lambda width: pl.BlockSpec((ROW_TILE, width), lambda i: (i, 0))
    widths = (_C_QKV, SSD_WIDTH, SSD_CONV_CH, LRU_WIDTH, LRU_WIDTH, DT_PAD)
    return pl.pallas_call(
        _inproj_kernel,
        grid=(t // ROW_TILE,),
        in_specs=[row(D_MODEL), _layer_spec((1, D_MODEL), layer),
                  _layer_spec((D_MODEL, _C_XBC), layer),
                  _layer_spec((D_MODEL, _T_END), layer)],
        out_specs=[row(wd) for wd in widths],
        out_shape=[jax.ShapeDtypeStruct((t, wd), F32) for wd in widths],
        compiler_params=_params(1),
        name="inproj",
    )(x2, g, w, w_tail)


def _scores(q_lo, q_hi, kk, bias):
    q_both = jnp.concatenate([q_lo, q_hi], axis=0)
    return lax.dot_general(q_both, kk, (((1,), (1,)), ((), ())),
                           preferred_element_type=F32) + bias


def _attn_kernel(q_ref, k_ref, v_ref, att_ref,
                 qn_s, kn_s, vn_s, tmp_s, qm_s, km_s, vm_s, qx_s, kx_s, vx_s,
                 mn_s, mm_s, mx_s, acc_s, bias_s, sc_s):
    pair = pl.program_id(1)
    tile = pl.program_id(2)
    blk = ATT_BLOCK
    first_tile = tile == 0
    mid_blocks = MID_LEN // blk

    @pl.when(first_tile)
    def _start_sequence():
        qi = lax.broadcasted_iota(jnp.int32, (blk, 2 * blk), 0)
        ki = lax.broadcasted_iota(jnp.int32, (blk, 2 * blk), 1)
        dist = blk + qi - ki
        band = (dist >= 0) & (dist <= blk)
        band_first = band & (ki >= blk)
        for hh in range(HEADS_PER_PAIR):
            expo = (126 - HEADS_PER_PAIR * pair - hh) << 23
            slope = lax.bitcast_convert_type(jnp.full((blk, 2 * blk), expo, jnp.int32), F32)
            rows = slice(hh * blk, (hh + 1) * blk)
            for br, dil in enumerate((DIL_MAX, DIL_MID, 1)):
                alibi = (-slope * (dil * dist).astype(F32)) * LOG2E
                bias_s[2 * br, rows, :] = jnp.where(band, alibi, NEG_INF)
                bias_s[2 * br + 1, rows, :] = jnp.where(band_first, alibi, NEG_INF)
        kn_s[0:blk, :] = jnp.zeros((blk, LANES), BF16)
        vn_s[:, 0:blk, :] = jnp.zeros((2, blk, LANES), BF16)
        km_s[:, 0:blk, :] = jnp.zeros((DIL_MID, blk, LANES), BF16)
        vm_s[:, :, 0:blk, :] = jnp.zeros((2, DIL_MID, blk, LANES), BF16)
        kx_s[:, 0:blk, :] = jnp.zeros((DIL_MAX, blk, LANES), BF16)
        vx_s[:, :, 0:blk, :] = jnp.zeros((2, DIL_MAX, blk, LANES), BF16)

    @pl.when(tile > 0)
    def _carry_history():
        kn_s[0:blk, :] = kn_s[ATT_TILE:ATT_TILE + blk, :]
        vn_s[:, 0:blk, :] = vn_s[:, ATT_TILE:ATT_TILE + blk, :]
        km_s[:, 0:blk, :] = km_s[:, MID_LEN:MID_LEN + blk, :]
        vm_s[:, :, 0:blk, :] = vm_s[:, :, MID_LEN:MID_LEN + blk, :]
        kx_s[:, 0:blk, :] = kx_s[:, blk:2 * blk, :]
        vx_s[:, :, 0:blk, :] = vx_s[:, :, blk:2 * blk, :]

    lane_m = lax.broadcasted_iota(jnp.int32, (MID_LEN, LANES), 1) < ATT_HEAD_DIM
    lane_b = lax.broadcasted_iota(jnp.int32, (blk, LANES), 1) < ATT_HEAD_DIM

    def put_q(dst, idx, x, mask):
        dst[(0,) + idx] = jnp.where(mask, x, 0.0).astype(BF16)
        dst[(1,) + idx] = jnp.where(mask, 0.0, x).astype(BF16)

    def put_k(dst, idx, x, mask):
        del mask
        dst[idx] = x.astype(BF16)

    def put_v(dst, idx, x, mask):
        dst[(0,) + idx] = jnp.where(mask, x, 1.0).astype(BF16)
        dst[(1,) + idx] = jnp.where(mask, 1.0, x).astype(BF16)

    for src, nat, mid, big, put, hist in (
            (q_ref, qn_s, qm_s, qx_s, put_q, 0),
            (k_ref, kn_s, km_s, kx_s, put_k, blk),
            (v_ref, vn_s, vm_s, vx_s, put_v, blk)):
        for c in range(DIL_MID):
            put(nat, (slice(hist + c * MID_LEN, hist + (c + 1) * MID_LEN), slice(None)),
                src[c * MID_LEN:(c + 1) * MID_LEN, :], lane_m)
        for r in range(DIL_MID):
            x = src[pl.ds(r, MID_LEN, stride=DIL_MID), :]
            tmp_s[r] = x
            put(mid, (r, slice(hist, hist + MID_LEN), slice(None)), x, lane_m)
        for r in range(DIL_MID):
            for c in range(DIL_STEP):
                y = tmp_s.at[r][pl.ds(c, blk, stride=DIL_STEP), :]
                put(big, (r + DIL_MID * c, slice(hist, hist + blk), slice(None)), y, lane_b)

    def bias_of(branch, is_first):
        return bias_s[2 * branch + jnp.where(is_first, 1, 0)]

    def big_ops(r):
        return (qx_s[0, r], qx_s[1, r], kx_s[r], bias_of(0, first_tile),
                pl.ds(r, blk, stride=DIL_MAX))

    def mid_ops(i):
        r = i // mid_blocks
        j = i % mid_blocks
        q0 = pl.multiple_of(j * blk, blk)
        return (qm_s[0, r, pl.ds(q0, blk), :], qm_s[1, r, pl.ds(q0, blk), :],
                km_s[r, pl.ds(q0, 2 * blk), :], bias_of(1, first_tile & (j == 0)),
                pl.ds(j * (blk * DIL_MID) + r, blk, stride=DIL_MID))

    def nat_ops(j):
        q0 = pl.multiple_of(j * blk, blk)
        return (qn_s[0, pl.ds(q0, blk), :], qn_s[1, pl.ds(q0, blk), :],
                kn_s[pl.ds(q0, 2 * blk), :], bias_of(2, first_tile & (j == 0)),
                pl.ds(q0, blk))

    def loop(n, body):
        lax.fori_loop(0, n, lambda i, c: (body(i), c)[1], 0, unroll=ATT_UNROLL)

    n_big, n_mid, n_nat = DIL_MAX, DIL_MID * mid_blocks, ATT_TILE // blk

    def row_max(ops, slot):
        q_lo, q_hi, kk, bias, rows = ops
        s = _scores(q_lo, q_hi, kk, bias)
        sc_s[slot] = s
        m = jnp.broadcast_to(jnp.max(s, axis=-1, keepdims=True), (2 * blk, LANES))
        return jnp.where(lane_b, m[:blk], m[blk:]), rows

    def max_big(r):
        m, rows = row_max(big_ops(r), r)
        mn_s[rows, :] = m

    def max_more(ops, slot):
        m, rows = row_max(ops, slot)
        mn_s[rows, :] = jnp.maximum(mn_s[rows, :], m)

    loop(n_big, max_big)
    loop(n_mid, lambda i: max_more(mid_ops(i), n_big + i))
    loop(n_nat, lambda j: max_more(nat_ops(j), n_big + n_mid + j))

    for r in range(DIL_MID):
        mm_s[r] = mn_s[pl.ds(r, MID_LEN, stride=DIL_MID), :]
    for r in range(DIL_MID):
        for c in range(DIL_STEP):
            mx_s[r + DIL_MID * c] = mm_s.at[r][pl.ds(c, blk, stride=DIL_STEP), :]

    def num_den(slot, m_pair, v_lo, v_hi):
        swapped = pltpu.roll(m_pair, ATT_HEAD_DIM, 1)
        m = jnp.concatenate([jnp.where(lane_b, m_pair, swapped),
                             jnp.where(lane_b, swapped, m_pair)], axis=0)
        e = jnp.exp2(sc_s[slot] - jnp.concatenate([m, m], axis=1)).astype(BF16)
        return (jnp.dot(e[:blk], v_lo, preferred_element_type=F32),
                jnp.dot(e[blk:], v_hi, preferred_element_type=F32))

    def acc_big(r):
        o_lo, o_hi = num_den(r, mx_s[r], vx_s[0, r], vx_s[1, r])
        rows = pl.ds(r, blk, stride=DIL_MAX)
        acc_s[0, rows, :] = o_lo
        acc_s[1, rows, :] = o_hi

    def acc_mid(i):
        r = i // mid_blocks
        j = i % mid_blocks
        q0 = pl.multiple_of(j * blk, blk)
        o_lo, o_hi = num_den(
            n_big + i, mm_s[r, pl.ds(q0, blk), :],
            vm_s[0, r, pl.ds(q0, 2 * blk), :], vm_s[1, r, pl.ds(q0, 2 * blk), :])
        rows = pl.ds(j * (blk * DIL_MID) + r, blk, stride=DIL_MID)
        acc_s[0, rows, :] += o_lo
        acc_s[1, rows, :] += o_hi

    def acc_nat(j):
        q0 = pl.multiple_of(j * blk, blk)
        rows = pl.ds(q0, blk)
        o_lo, o_hi = num_den(
            n_big + n_mid + j, mn_s[rows, :],
            vn_s[0, pl.ds(q0, 2 * blk), :], vn_s[1, pl.ds(q0, 2 * blk), :])
        a_lo = acc_s[0, rows, :] + o_lo
        a_hi = acc_s[1, rows, :] + o_hi
        num = jnp.where(lane_b, a_lo, a_hi)
        den = pltpu.roll(jnp.where(lane_b, a_hi, a_lo), ATT_HEAD_DIM, 1)
        att_ref[rows, :] = (num / den).astype(BF16)

    @pl.when(tile >= 0)
    def _pass_two():
        loop(n_big, acc_big)
        loop(n_mid, acc_mid)
        loop(n_nat, acc_nat)


def _dilated_attention(qkv, batch):
    t = qkv.shape[0]
    tiles = t // batch // ATT_TILE
    blk = ATT_BLOCK

    def spec(col0):
        return pl.BlockSpec((ATT_TILE, LANES), lambda b, p, i: (b * tiles + i, col0 + p))

    bf = lambda *shape: pltpu.VMEM(shape, BF16)
    f32 = lambda *shape: pltpu.VMEM(shape, F32)
    return pl.pallas_call(
        _attn_kernel,
        grid=(batch, ATT_PAIRS, tiles),
        in_specs=[spec(0), spec(ATT_PAIRS), spec(2 * ATT_PAIRS)],
        out_specs=spec(0),
        out_shape=jax.ShapeDtypeStruct((t, ATT_WIDTH), BF16),
        scratch_shapes=[
            bf(2, ATT_TILE, LANES), bf(blk + ATT_TILE, LANES), bf(2, blk + ATT_TILE, LANES),
            f32(DIL_MID, MID_LEN, LANES),
            bf(2, DIL_MID, MID_LEN, LANES), bf(DIL_MID, blk + MID_LEN, LANES),
            bf(2, DIL_MID, blk + MID_LEN, LANES),
            bf(2, DIL_MAX, blk, LANES), bf(DIL_MAX, 2 * blk, LANES),
            bf(2, DIL_MAX, 2 * blk, LANES),
            f32(ATT_TILE, LANES), f32(DIL_MID, MID_LEN, LANES),
            f32(DIL_MAX, blk, LANES), f32(2, ATT_TILE, LANES),
            f32(3 * 2, 2 * blk, 2 * blk),
            f32(3 * DIL_MAX, 2 * blk, 2 * blk),
        ],
        compiler_params=_params(3),
        name="attn",
    )(qkv, qkv, qkv)


def _split3(x):
    hi = x.astype(BF16)
    r1 = x - hi.astype(F32)
    mid = r1.astype(BF16)
    lo = (r1 - mid.astype(F32)).astype(BF16)
    return hi, mid, lo


def _dot_exact_rhs(x, w):
    return sum(jnp.dot(part, w, preferred_element_type=F32) for part in _split3(x))


TM_CHUNK = 128
TM_STEPS = TM_CHUNK // SUBLANES
TM_HIST = (CONV_WIDTH - 1) * SUBLANES
assert TM_CHUNK == SSD_CHUNK


def _to_time_major(src, mid, dst, base):
    quarter = TM_CHUNK // 4
    for r in range(4):
        mid[base + quarter * r:base + quarter * (r + 1), :] = (
            src[pl.ds(base + r, quarter, stride=4), :])
    for r in range(4):
        for c in range(4):
            v = r + 4 * c
            dst[base + SUBLANES * v:base + SUBLANES * (v + 1), :] = (
                mid[pl.ds(base + quarter * r + c, SUBLANES, stride=4), :])


def _tm_time(idx):
    return TM_STEPS * (idx % SUBLANES) + idx // SUBLANES


def _time_major_perm():
    n = jnp.arange(TM_CHUNK)
    return (_tm_time(n)[None, :] == n[:, None]).astype(BF16)


def _conv_time_major(x, hist, cw_ref, cb_ref, first_row):
    width = x.shape[-1]
    groups = TM_HIST // SUBLANES
    tail = x[TM_CHUNK - TM_HIST:, :]
    down = lambda a: pltpu.roll(a.reshape(groups, SUBLANES, width), 1, 1)
    wrapped = jnp.where(first_row, down(hist), down(tail)).reshape(TM_HIST, width)
    ext = jnp.concatenate([wrapped, x], axis=0)
    k_w = CONV_WIDTH
    conv = cb_ref[...] + x * cw_ref[k_w - 1:k_w, :]
    for back in range(1, k_w):
        off = TM_HIST - SUBLANES * back
        conv = conv + ext[off:off + TM_CHUNK, :] * cw_ref[k_w - 1 - back:k_w - back, :]
    return conv, tail


def _ssd_kernel(z_ref, xbc_ref, dt_ref, cw_ref, cb_ref, dtb_ref, alog_ref, dsk_ref, nw_ref,
                triu_ref, exp_ref, y_ref, win_s, state_s):
    c = pl.program_id(0)
    q = SSD_CHUNK
    pad = CONV_PAD
    rows_step = SSD_STEP_CHUNKS * q
    heads_per_group = SSD_HEADS // SSD_GROUPS
    gw = heads_per_group * SSD_HEAD_DIM
    n_seq = z_ref.shape[0]

    @pl.when(c == 0)
    def _reset():
        win_s[:, 0:pad, :] = jnp.zeros((n_seq, pad, SSD_CONV_CH), F32)
        state_s[...] = jnp.zeros_like(state_s)

    win_s[:, pad:, :] = xbc_ref[...]

    ti = lax.broadcasted_iota(jnp.int32, (q, q), 0)
    tj = lax.broadcasted_iota(jnp.int32, (q, q), 1)
    causal = ti >= tj
    low_half = lax.broadcasted_iota(jnp.int32, (q, LANES), 1) < SSD_HEAD_DIM
    a_t = -jnp.exp(alog_ref[...])
    k_w = CONV_WIDTH

    pairs_per_group = heads_per_group // HEADS_PER_PAIR

    def prepare(ci, b):
        r0 = ci * q
        win_b, dt_b = win_s.at[b], dt_ref.at[b]
        conv = cb_ref[...] + win_b[pad + r0:pad + r0 + q, :] * cw_ref[k_w - 1:k_w, :]
        for k in range(k_w - 1):
            off = pad + r0 - (k_w - 1) + k
            conv = conv + win_b[off:off + q, :] * cw_ref[k:k + 1, :]
        xbc = _silu(conv)
        xs = xbc[:, :SSD_WIDTH]
        bm = xbc[:, SSD_WIDTH:SSD_WIDTH + SSD_GROUPS * SSD_STATE]
        cm = xbc[:, SSD_WIDTH + SSD_GROUPS * SSD_STATE:]

        dt_t = _softplus(dt_b[r0:r0 + q, :].T[0:SSD_HEADS, :] + dtb_ref[...])
        acs_t = _dot_exact_rhs(dt_t * a_t, triu_ref[...])
        e_t = jnp.exp(acs_t)
        w_t = jnp.exp(acs_t[:, q - 1:q] - acs_t) * dt_t
        chunk_decay = jnp.sum(e_t[:, q - 1:q] * exp_ref[...], axis=0, keepdims=True)
        acs2_t = acs_t * LOG2E
        key2_t = acs2_t - jnp.log2(dt_t)
        cols_t = jnp.concatenate(
            [acs2_t, e_t, jnp.zeros((q - 2 * SSD_HEADS, q), F32)], axis=0).T

        pairs = []
        for g in range(SSD_GROUPS):
            bm_g = bm[:, g * SSD_STATE:(g + 1) * SSD_STATE]
            cm_g = cm[:, g * SSD_STATE:(g + 1) * SSD_STATE]
            gmat = lax.dot_general(cm_g.astype(BF16), bm_g.astype(BF16),
                                   (((1,), (1,)), ((), ())), preferred_element_type=F32)
            bm_gt = bm_g.T
            for pp in range(pairs_per_group):
                p = g * pairs_per_group + pp
                x_p = xs[:, p * LANES:(p + 1) * LANES]
                lhs_y, lhs_s, x_heads = [], [], []
                for hh in range(HEADS_PER_PAIR):
                    h = HEADS_PER_PAIR * p + hh
                    keep = low_half if hh == 0 else ~low_half
                    seg = cols_t[:, h:h + 1] - key2_t[h:h + 1, :]
                    scores = gmat * jnp.exp2(jnp.where(causal, seg, NEG_INF))
                    c_dec = cm_g * cols_t[:, SSD_HEADS + h:SSD_HEADS + h + 1]
                    lhs_y += [scores.astype(BF16), c_dec.astype(BF16)]
                    lhs_s.append((bm_gt * w_t[h:h + 1, :]).astype(BF16))
                    x_heads.append(jnp.where(keep, x_p, 0.0).astype(BF16))
                pairs.append((jnp.concatenate(lhs_y, axis=1), jnp.concatenate(lhs_s, axis=1),
                              x_heads))
        return xs, chunk_decay, pairs

    def finish(ci, b, prepared):
        xs, chunk_decay, pairs = prepared
        r0 = ci * q
        state_b, z_b, y_b = state_s.at[b], z_ref.at[b], y_ref.at[b]
        y_parts = []
        for p, (lhs_y, lhs_s, x_heads) in enumerate(pairs):
            cols = slice(p * LANES, (p + 1) * LANES)
            s_p = state_b[:, cols]
            rhs_y = []
            for hh in range(HEADS_PER_PAIR):
                keep = low_half if hh == 0 else ~low_half
                rhs_y += [x_heads[hh], jnp.where(keep, s_p, 0.0).astype(BF16)]
            y_parts.append(jnp.dot(lhs_y, jnp.concatenate(rhs_y, axis=0),
                                   preferred_element_type=F32))
            state_b[:, cols] = s_p * chunk_decay[:, cols] + jnp.dot(
                lhs_s, jnp.concatenate(x_heads, axis=0), preferred_element_type=F32)
        y = jnp.concatenate(y_parts, axis=-1) + dsk_ref[...] * xs
        y = y * _silu(z_b[r0:r0 + q, :])
        outs = []
        for g in range(SSD_GROUPS):
            yg = y[:, g * gw:(g + 1) * gw]
            outs.append(yg * lax.rsqrt(jnp.mean(yg * yg, axis=-1, keepdims=True) + SSD_NORM_EPS))
        y_b[r0:r0 + q, :] = (jnp.concatenate(outs, axis=-1) * nw_ref[...]).astype(BF16)

    order = [(ci, b) for ci in range(SSD_STEP_CHUNKS) for b in range(n_seq)]
    prepared = [prepare(ci, b) for ci, b in order]
    for (ci, b), prep in zip(order, prepared):
        finish(ci, b, prep)

    win_s[:, 0:pad, :] = win_s[:, rows_step:rows_step + pad, :]


def _prep_ssd(conv_w, conv_b, dt_bias, a_log, d_skip, norm_w):
    depth = conv_w.shape[0]
    per_time = lambda v: jnp.broadcast_to(v.astype(F32)[:, :, None],
                                          (depth, SSD_HEADS, SSD_CHUNK))
    triu = jnp.triu(jnp.ones((SSD_CHUNK, SSD_CHUNK), BF16))
    expand = (jnp.arange(SSD_HEADS)[:, None] == (jnp.arange(SSD_WIDTH)[None, :] // SSD_HEAD_DIM)
              ).astype(F32)
    d_exp = jnp.repeat(d_skip.astype(F32), SSD_HEAD_DIM, axis=1).reshape(depth, 1, SSD_WIDTH)
    return (conv_w, conv_b.reshape(depth, 1, SSD_CONV_CH), per_time(dt_bias), per_time(a_log),
            d_exp, norm_w.reshape(depth, 1, SSD_WIDTH), triu, expand)


def _ssd(z, xbc, dt, prepared, layer, batch):
    t = z.shape[0]
    seq = t // batch
    rows_step = SSD_STEP_CHUNKS * SSD_CHUNK
    row = lambda width: pl.BlockSpec((batch, rows_step, width), lambda c: (0, c, 0))
    by_seq = lambda a: a.reshape(batch, seq, a.shape[-1])
    lspec = lambda *shape: _layer_spec(shape, layer)
    out = pl.pallas_call(
        _ssd_kernel,
        grid=(seq // rows_step,),
        in_specs=[row(SSD_WIDTH), row(SSD_CONV_CH), row(DT_PAD),
                  lspec(CONV_WIDTH, SSD_CONV_CH), lspec(1, SSD_CONV_CH),
                  lspec(SSD_HEADS, SSD_CHUNK), lspec(SSD_HEADS, SSD_CHUNK),
                  lspec(1, SSD_WIDTH), lspec(1, SSD_WIDTH),
                  _const_spec((SSD_CHUNK, SSD_CHUNK)), _const_spec((SSD_HEADS, SSD_WIDTH))],
        out_specs=row(SSD_WIDTH),
        out_shape=jax.ShapeDtypeStruct((batch, seq, SSD_WIDTH), BF16),
        scratch_shapes=[pltpu.VMEM((batch, CONV_PAD + rows_step, SSD_CONV_CH), F32),
                        pltpu.VMEM((batch, SSD_STATE, SSD_WIDTH), F32)],
        compiler_params=_params(1),
        name="ssd",
    )(by_seq(z), by_seq(xbc), by_seq(dt), *prepared)
    return out.reshape(t, SSD_WIDTH)


def _gelu_tanh(x):
    c = math.sqrt(2.0 / math.pi)
    return 0.5 * x * (1.0 + jnp.tanh(c * (x + 0.044715 * (x * x * x))))


def _lru_kernel(*refs):
    ncol = LRU_WIDTH // LANES
    g_refs, x_refs = refs[:ncol], refs[ncol:2 * ncol]
    (cw_ref, cb_ref, wa_ref, ba_ref, wx_ref, bx_ref, lam_ref, perm_ref, y_ref,
     mid_s, gt_s, xt_s, hist_s, h_s) = refs[2 * ncol:]
    c = pl.program_id(1)
    tt = LRU_TILE
    n_chunks = tt // TM_CHUNK

    @pl.when(c == 0)
    def _reset():
        hist_s[...] = jnp.zeros_like(hist_s)
        h_s[...] = jnp.zeros_like(h_s)

    for j in range(ncol):
        for k in range(n_chunks):
            _to_time_major(g_refs[j], mid_s.at[j], gt_s.at[j], k * TM_CHUNK)
        for k in range(n_chunks):
            _to_time_major(x_refs[j], mid_s.at[j], xt_s.at[j], k * TM_CHUNK)

    def chunk_of(buf, k):
        rows = slice(k * TM_CHUNK, (k + 1) * TM_CHUNK)
        return jnp.concatenate([buf[j, rows, :] for j in range(ncol)], axis=1)

    first_row = lax.broadcasted_iota(
        jnp.int32, (TM_HIST // SUBLANES, SUBLANES, LRU_WIDTH), 1) == 0
    hist = hist_s[...]
    convs = []
    for k in range(n_chunks):
        conv, hist = _conv_time_major(chunk_of(xt_s, k), hist, cw_ref, cb_ref, first_row)
        convs.append(conv)
    hist_s[...] = hist
    xc = jnp.concatenate(convs, axis=0)

    xb = xc.astype(BF16)
    r = jax.nn.sigmoid(jnp.dot(xb, wa_ref[...], preferred_element_type=F32) + ba_ref[...])
    i = jax.nn.sigmoid(jnp.dot(xb, wx_ref[...], preferred_element_type=F32) + bx_ref[...])
    decay = -LRU_C * _softplus(-lam_ref[...])
    log_a = r * decay
    a = jnp.exp2(r * (decay * LOG2E))
    u = jnp.sqrt(-jnp.tanh(log_a) * (a * a + 1.0)) * (i * xc)

    sub = lax.broadcasted_iota(jnp.int32, (SUBLANES, LRU_WIDTH), 0)
    h_prev = h_s[0:1, :]
    for k in range(n_chunks):
        rows = slice(k * TM_CHUNK, (k + 1) * TM_CHUNK)
        a_k = a[rows].reshape(TM_STEPS, SUBLANES, LRU_WIDTH)
        u_k = u[rows].reshape(TM_STEPS, SUBLANES, LRU_WIDTH)
        hh, aa = [u_k[0]], [a_k[0]]
        for v in range(1, TM_STEPS):
            hh.append(a_k[v] * hh[-1] + u_k[v])
            aa.append(a_k[v] * aa[-1])
        pa, ph = aa[-1], hh[-1]
        step = 1
        while step < SUBLANES:
            keep = sub >= step
            pa_sh = jnp.where(keep, pltpu.roll(pa, step, 0), 1.0)
            ph_sh = jnp.where(keep, pltpu.roll(ph, step, 0), 0.0)
            ph = pa * ph_sh + ph
            pa = pa * pa_sh
            step *= 2
        c_in = jnp.where(sub == 0, h_prev, pltpu.roll(pa, 1, 0) * h_prev + pltpu.roll(ph, 1, 0))
        h_k = [hh[v] + aa[v] * c_in for v in range(TM_STEPS)]
        h_prev = h_k[-1][SUBLANES - 1:SUBLANES, :]
        y_k = (jnp.concatenate(h_k, axis=0) * _gelu_tanh(chunk_of(gt_s, k))).astype(BF16)
        y_ref[rows, :] = jnp.dot(perm_ref[...], y_k, preferred_element_type=F32).astype(BF16)
    h_s[...] = jnp.broadcast_to(h_prev, h_s.shape)


def _block_diag(w):
    depth, nb, c, d = w.shape
    eye = jnp.eye(nb, dtype=w.dtype)
    return (eye[None, :, None, :, None] * w[:, :, :, None, :]).reshape(depth, nb * c, nb * d)


def _prep_lru(conv_w, conv_b, wa, ba, wx, bx, lam):
    depth = conv_w.shape[0]
    vec = lambda v: v.astype(F32).reshape(depth, 1, LRU_WIDTH)
    return (conv_w, vec(conv_b), _block_diag(wa).astype(BF16), vec(ba),
            _block_diag(wx).astype(BF16), vec(bx), vec(lam))


def _lru(g_in, x_in, prepared, layer, batch):
    t = g_in.shape[0]
    s = t // batch
    nt = s // LRU_TILE
    ncol = LRU_WIDTH // LANES
    row = pl.BlockSpec((LRU_TILE, LRU_WIDTH), lambda b, c: (b * nt + c, 0))
    cols = [pl.BlockSpec((LRU_TILE, LANES), lambda b, c, j=j: (b * nt + c, j))
            for j in range(ncol)]
    mat = _layer_spec((LRU_WIDTH, LRU_WIDTH), layer)
    one = _layer_spec((1, LRU_WIDTH), layer)
    tile = lambda: pltpu.VMEM((ncol, LRU_TILE, LANES), F32)
    return pl.pallas_call(
        _lru_kernel,
        grid=(batch, nt),
        in_specs=cols + cols + [_layer_spec((CONV_WIDTH, LRU_WIDTH), layer), one, mat, one,
                                mat, one, one, _const_spec((TM_CHUNK, TM_CHUNK))],
        out_specs=row,
        out_shape=jax.ShapeDtypeStruct((t, LRU_WIDTH), BF16),
        scratch_shapes=[tile(), tile(), tile(),
                        pltpu.VMEM((TM_HIST, LRU_WIDTH), F32),
                        pltpu.VMEM((SUBLANES, LRU_WIDTH), F32)],
        compiler_params=_params(2),
        name="rglru",
    )(*([g_in] * ncol), *([x_in] * ncol), *prepared, _time_major_perm())


def _outffn_kernel(x_ref, att_ref, ssd_ref, lru_ref, wo_ref, gf_ref, wg_ref, wu_ref, wd_ref,
                   nfin_ref, o_ref, hn_s, *, final_norm):
    x1 = x_ref[...]
    for j, m_ref in enumerate((att_ref, ssd_ref, lru_ref)):
        x1 = x1 + jnp.dot(m_ref[...], wo_ref[j * ATT_WIDTH:(j + 1) * ATT_WIDTH, :],
                          preferred_element_type=F32)
    hn_s[...] = _rmsnorm(x1, gf_ref[...]).astype(BF16)
    o_ref[...] = x1
    for c0, c1 in FF_CHUNKS:
        hn = hn_s[...]
        gate = jnp.dot(hn, wg_ref[:, c0:c1], preferred_element_type=F32)
        up = jnp.dot(hn, wu_ref[:, c0:c1], preferred_element_type=F32)
        act = (_silu(gate) * up).astype(BF16)
        o_ref[...] += jnp.dot(act, wd_ref[c0:c1, :], preferred_element_type=F32)
    if final_norm:
        o_ref[...] = _rmsnorm(o_ref[...], nfin_ref[...])


def _prep_ffn(w_out, norm_ffn, w_gate, w_up, w_down, norm_final):
    depth = w_out.shape[0]
    return (w_out.astype(BF16), norm_ffn.reshape(depth, 1, D_MODEL), w_gate.astype(BF16),
            w_up.astype(BF16), w_down.astype(BF16), norm_final.reshape(1, D_MODEL))


def _outffn(x2, att, ssd, lru, prepared, layer, final_norm):
    t = x2.shape[0]
    row = lambda width: pl.BlockSpec((ROW_TILE, width), lambda i: (i, 0))
    lspec = lambda *shape: _layer_spec(shape, layer)
    return pl.pallas_call(
        functools.partial(_outffn_kernel, final_norm=final_norm),
        grid=(t // ROW_TILE,),
        in_specs=[row(D_MODEL), row(ATT_WIDTH), row(SSD_WIDTH), row(LRU_WIDTH),
                  lspec(D_MIX, D_MODEL), lspec(1, D_MODEL), lspec(D_MODEL, D_FF),
                  lspec(D_MODEL, D_FF), lspec(D_FF, D_MODEL), _const_spec((1, D_MODEL))],
        out_specs=row(D_MODEL),
        out_shape=jax.ShapeDtypeStruct((t, D_MODEL), F32),
        scratch_shapes=[pltpu.VMEM((ROW_TILE, D_MODEL), BF16)],
        compiler_params=_params(1),
        name="outffn",
    )(x2, att, ssd, lru, *prepared)


def _prep_w_in(w):
    tail = w[:, :, _C_XBC:]
    dt_cols = jnp.pad(tail[:, :, :SSD_HEADS], ((0, 0), (0, 0), (0, DT_PAD - SSD_HEADS)))
    w_tail = jnp.concatenate([tail[:, :, SSD_HEADS:], dt_cols], axis=2).astype(BF16)
    return w[:, :, :_C_XBC].astype(BF16), w_tail


def kernel(x, norm_mix, w_in, ssd_conv_w, ssd_conv_b, ssd_dt_bias, ssd_a_log, ssd_d, ssd_norm,
           lru_conv_w, lru_conv_b, lru_wa, lru_ba, lru_wx, lru_bx, lru_lambda, w_out,
           norm_ffn, w_gate, w_up, w_down, norm_final):
    batch, seq, _ = x.shape
    depth = w_in.shape[0]
    g_mix = norm_mix.reshape(depth, 1, D_MODEL)
    w_proj, w_tail = _prep_w_in(w_in)
    ssd_p = _prep_ssd(ssd_conv_w, ssd_conv_b, ssd_dt_bias, ssd_a_log, ssd_d, ssd_norm)
    lru_p = _prep_lru(lru_conv_w, lru_conv_b, lru_wa, lru_ba, lru_wx, lru_bx, lru_lambda)
    ffn_p = _prep_ffn(w_out, norm_ffn, w_gate, w_up, w_down, norm_final)
    x2 = x.reshape(batch * seq, D_MODEL)
    for l in range(depth):
        qkv, z, xbc, g_lru, x_lru, dt = _inproj(x2, g_mix, w_proj, w_tail, l)
        att = _dilated_attention(qkv, batch)
        ssd = _ssd(z, xbc, dt, ssd_p, l, batch)
        lru = _lru(g_lru, x_lru, lru_p, l, batch)
        x2 = _outffn(x2, att, ssd, lru, ffn_p, l, l == depth - 1)
    return x2.reshape(batch, seq, D_MODEL)
```

```python
import functools
import math

import jax
import jax.numpy as jnp
from jax import lax
from jax.experimental import pallas as pl
from jax.experimental.pallas import tpu as pltpu

F32 = jnp.float32
BF16 = jnp.bfloat16

D_MODEL = 1024
ATT_HEADS = 8
ATT_HEAD_DIM = 64
ATT_WIDTH = ATT_HEADS * ATT_HEAD_DIM
ATT_BLOCK = 128
ATT_DILATIONS = (1, 4, 16)
SSD_HEADS = 8
SSD_HEAD_DIM = 64
SSD_WIDTH = SSD_HEADS * SSD_HEAD_DIM
SSD_GROUPS = 2
SSD_STATE = 128
SSD_CHUNK = 128
SSD_CONV_CH = SSD_WIDTH + 2 * SSD_GROUPS * SSD_STATE
LRU_WIDTH = 512
LRU_BLOCKS = 8
LRU_BLOCK_W = LRU_WIDTH // LRU_BLOCKS
LRU_C = 8.0
CONV_WIDTH = 4
D_MIX = ATT_WIDTH + SSD_WIDTH + LRU_WIDTH
D_FF = 2816
NORM_EPS = 1e-6
SSD_NORM_EPS = 1e-5

LANES = 128
SUBLANES = 8
VMEM_LIMIT_BYTES = 56 * 1024 * 1024

ROW_TILE = 1024
LRU_TILE = 1024
SSD_STEP_CHUNKS = 4
DT_PAD = LANES
FF_CHUNKS = ((0, 768), (768, 1536), (1536, 2304), (2304, 2816))
CONV_PAD = SUBLANES

DIL_MID, DIL_MAX = ATT_DILATIONS[1], ATT_DILATIONS[2]
DIL_STEP = DIL_MAX // DIL_MID
ATT_TILE = ATT_BLOCK * DIL_MAX
MID_LEN = ATT_TILE // DIL_MID
HEADS_PER_PAIR = LANES // ATT_HEAD_DIM
ATT_PAIRS = ATT_WIDTH // LANES
ATT_UNROLL = 16
assert ATT_HEADS == 8 and HEADS_PER_PAIR == 2 and DIL_STEP == DIL_MID and ATT_DILATIONS[0] == 1

LOG2E = 1.4426950408889634
NEG_INF = float("-inf")


def _params(n_axes):
    return pltpu.CompilerParams(
        dimension_semantics=("arbitrary",) * n_axes,
        vmem_limit_bytes=VMEM_LIMIT_BYTES)


def _const_spec(shape):
    nd = len(shape)
    return pl.BlockSpec(shape, lambda *_: (0,) * nd, pipeline_mode=pl.Buffered(1))


def _layer_spec(shape, layer):
    nd = len(shape)
    return pl.BlockSpec((None,) + tuple(shape), lambda *_: (layer,) + (0,) * nd,
                        pipeline_mode=pl.Buffered(1))


def _rmsnorm(x, g):
    return x * lax.rsqrt(jnp.mean(x * x, axis=-1, keepdims=True) + NORM_EPS) * g


def _softplus(x):
    return jnp.maximum(x, 0.0) + jnp.log1p(jnp.exp(-jnp.abs(x)))


def _silu(x):
    return x * jax.nn.sigmoid(x)


_C_QKV = 3 * ATT_WIDTH
_C_Z = _C_QKV + SSD_WIDTH
_C_XBC = _C_Z + SSD_CONV_CH
_T_XL = LRU_WIDTH
_T_DT = 2 * LRU_WIDTH
_T_END = _T_DT + DT_PAD


def _inproj_kernel(x_ref, g_ref, w_ref, wt_ref, qkv_ref, z_ref, xbc_ref, gl_ref, xl_ref, dt_ref):
    h = _rmsnorm(x_ref[...], g_ref[...]).astype(BF16)

    def seg(ref, a, b):
        return jnp.dot(h, ref[:, a:b], preferred_element_type=F32)

    qkv_ref[:, 0:ATT_WIDTH] = seg(w_ref, 0, ATT_WIDTH) * (ATT_HEAD_DIM ** -0.5 * LOG2E)
    qkv_ref[:, ATT_WIDTH:_C_QKV] = seg(w_ref, ATT_WIDTH, _C_QKV)
    z_ref[...] = seg(w_ref, _C_QKV, _C_Z)
    xbc_ref[...] = seg(w_ref, _C_Z, _C_XBC)
    gl_ref[...] = seg(wt_ref, 0, _T_XL)
    xl_ref[...] = seg(wt_ref, _T_XL, _T_DT)
    dt_ref[...] = seg(wt_ref, _T_DT, _T_END)


def _inproj(x2, g, w, w_tail, layer):
    t = x2.shape[0]
    row = lambda width: pl.BlockSpec((ROW_TILE, width), lambda i: (i, 0))
    widths = (_C_QKV, SSD_WIDTH, SSD_CONV_CH, LRU_WIDTH, LRU_WIDTH, DT_PAD)
    return pl.pallas_call(
        _inproj_kernel,
        grid=(t // ROW_TILE,),
        in_specs=[row(D_MODEL), _layer_spec((1, D_MODEL), layer),
                  _layer_spec((D_MODEL, _C_XBC), layer),
                  _layer_spec((D_MODEL, _T_END), layer)],
        out_specs=[row(wd) for wd in widths],
        out_shape=[jax.ShapeDtypeStruct((t, wd), F32) for wd in widths],
        compiler_params=_params(1),
        name="inproj",
    )(x2, g, w, w_tail)


def _scores(q_lo, q_hi, kk, bias):
    q_both = jnp.concatenate([q_lo, q_hi], axis=0)
    return lax.dot_general(q_both, kk, (((1,), (1,)), ((), ())),
                           preferred_element_type=F32) + bias


def _attn_kernel(q_ref, k_ref, v_ref, att_ref,
                 qn_s, kn_s, vn_s, tmp_s, qm_s, km_s, vm_s, qx_s, kx_s, vx_s,
                 mn_s, mm_s, mx_s, acc_s, bias_s, sc_s):
    pair = pl.program_id(1)
    tile = pl.program_id(2)
    blk = ATT_BLOCK
    first_tile = tile == 0
    mid_blocks = MID_LEN // blk

    @pl.when(first_tile)
    def _start_sequence():
        qi = lax.broadcasted_iota(jnp.int32, (blk, 2 * blk), 0)
        ki = lax.broadcasted_iota(jnp.int32, (blk, 2 * blk), 1)
        dist = blk + qi - ki
        band = (dist >= 0) & (dist <= blk)
        band_first = band & (ki >= blk)
        for hh in range(HEADS_PER_PAIR):
            expo = (126 - HEADS_PER_PAIR * pair - hh) << 23
            slope = lax.bitcast_convert_type(jnp.full((blk, 2 * blk), expo, jnp.int32), F32)
            rows = slice(hh * blk, (hh + 1) * blk)
            for br, dil in enumerate((DIL_MAX, DIL_MID, 1)):
                alibi = (-slope * (dil * dist).astype(F32)) * LOG2E
                bias_s[2 * br, rows, :] = jnp.where(band, alibi, NEG_INF)
                bias_s[2 * br + 1, rows, :] = jnp.where(band_first, alibi, NEG_INF)
        kn_s[0:blk, :] = jnp.zeros((blk, LANES), BF16)
        vn_s[:, 0:blk, :] = jnp.zeros((2, blk, LANES), BF16)
        km_s[:, 0:blk, :] = jnp.zeros((DIL_MID, blk, LANES), BF16)
        vm_s[:, :, 0:blk, :] = jnp.zeros((2, DIL_MID, blk, LANES), BF16)
        kx_s[:, 0:blk, :] = jnp.zeros((DIL_MAX, blk, LANES), BF16)
        vx_s[:, :, 0:blk, :] = jnp.zeros((2, DIL_MAX, blk, LANES), BF16)

    @pl.when(tile > 0)
    def _carry_history():
        kn_s[0:blk, :] = kn_s[ATT_TILE:ATT_TILE + blk, :]
        vn_s[:, 0:blk, :] = vn_s[:, ATT_TILE:ATT_TILE + blk, :]
        km_s[:, 0:blk, :] = km_s[:, MID_LEN:MID_LEN + blk, :]
        vm_s[:, :, 0:blk, :] = vm_s[:, :, MID_LEN:MID_LEN + blk, :]
        kx_s[:, 0:blk, :] = kx_s[:, blk:2 * blk, :]
        vx_s[:, :, 0:blk, :] = vx_s[:, :, blk:2 * blk, :]

    lane_m = lax.broadcasted_iota(jnp.int32, (MID_LEN, LANES), 1) < ATT_HEAD_DIM
    lane_b = lax.broadcasted_iota(jnp.int32, (blk, LANES), 1) < ATT_HEAD_DIM

    def put_q(dst, idx, x, mask):
        dst[(0,) + idx] = jnp.where(mask, x, 0.0).astype(BF16)
        dst[(1,) + idx] = jnp.where(mask, 0.0, x).astype(BF16)

    def put_k(dst, idx, x, mask):
        del mask
        dst[idx] = x.astype(BF16)

    def put_v(dst, idx, x, mask):
        dst[(0,) + idx] = jnp.where(mask, x, 1.0).astype(BF16)
        dst[(1,) + idx] = jnp.where(mask, 1.0, x).astype(BF16)

    for src, nat, mid, big, put, hist in (
            (q_ref, qn_s, qm_s, qx_s, put_q, 0),
            (k_ref, kn_s, km_s, kx_s, put_k, blk),
            (v_ref, vn_s, vm_s, vx_s, put_v, blk)):
        for c in range(DIL_MID):
            put(nat, (slice(hist + c * MID_LEN, hist + (c + 1) * MID_LEN), slice(None)),
                src[c * MID_LEN:(c + 1) * MID_LEN, :], lane_m)
        for r in range(DIL_MID):
            x = src[pl.ds(r, MID_LEN, stride=DIL_MID), :]
            tmp_s[r] = x
            put(mid, (r, slice(hist, hist + MID_LEN), slice(None)), x, lane_m)
        for r in range(DIL_MID):
            for c in range(DIL_STEP):
                y = tmp_s.at[r][pl.ds(c, blk, stride=DIL_STEP), :]
                put(big, (r + DIL_MID * c, slice(hist, hist + blk), slice(None)), y, lane_b)

    def bias_of(branch, is_first):
        return bias_s[2 * branch + jnp.where(is_first, 1, 0)]

    def big_ops(r):
        return (qx_s[0, r], qx_s[1, r], kx_s[r], bias_of(0, first_tile),
                pl.ds(r, blk, stride=DIL_MAX))

    def mid_ops(i):
        r = i // mid_blocks
        j = i % mid_blocks
        q0 = pl.multiple_of(j * blk, blk)
        return (qm_s[0, r, pl.ds(q0, blk), :], qm_s[1, r, pl.ds(q0, blk), :],
                km_s[r, pl.ds(q0, 2 * blk), :], bias_of(1, first_tile & (j == 0)),
                pl.ds(j * (blk * DIL_MID) + r, blk, stride=DIL_MID))

    def nat_ops(j):
        q0 = pl.multiple_of(j * blk, blk)
        return (qn_s[0, pl.ds(q0, blk), :], qn_s[1, pl.ds(q0, blk), :],
                kn_s[pl.ds(q0, 2 * blk), :], bias_of(2, first_tile & (j == 0)),
                pl.ds(q0, blk))

    def loop(n, body):
        lax.fori_loop(0, n, lambda i, c: (body(i), c)[1], 0, unroll=ATT_UNROLL)

    n_big, n_mid, n_nat = DIL_MAX, DIL_MID * mid_blocks, ATT_TILE // blk

    def row_max(ops, slot):
        q_lo, q_hi, kk, bias, rows = ops
        s = _scores(q_lo, q_hi, kk, bias)
        sc_s[slot] = s
        m = jnp.broadcast_to(jnp.max(s, axis=-1, keepdims=True), (2 * blk, LANES))
        return jnp.where(lane_b, m[:blk], m[blk:]), rows

    def max_big(r):
        m, rows = row_max(big_ops(r), r)
        mn_s[rows, :] = m

    def max_more(ops, slot):
        m, rows = row_max(ops, slot)
        mn_s[rows, :] = jnp.maximum(mn_s[rows, :], m)

    loop(n_big, max_big)
    loop(n_mid, lambda i: max_more(mid_ops(i), n_big + i))
    loop(n_nat, lambda j: max_more(nat_ops(j), n_big + n_mid + j))

    for r in range(DIL_MID):
        mm_s[r] = mn_s[pl.ds(r, MID_LEN, stride=DIL_MID), :]
    for r in range(DIL_MID):
        for c in range(DIL_STEP):
            mx_s[r + DIL_MID * c] = mm_s.at[r][pl.ds(c, blk, stride=DIL_STEP), :]

    def num_den(slot, m_pair, v_lo, v_hi):
        swapped = pltpu.roll(m_pair, ATT_HEAD_DIM, 1)
        m = jnp.concatenate([jnp.where(lane_b, m_pair, swapped),
                             jnp.where(lane_b, swapped, m_pair)], axis=0)
        e = jnp.exp2(sc_s[slot] - jnp.concatenate([m, m], axis=1)).astype(BF16)
        return (jnp.dot(e[:blk], v_lo, preferred_element_type=F32),
                jnp.dot(e[blk:], v_hi, preferred_element_type=F32))

    def acc_big(r):
        o_lo, o_hi = num_den(r, mx_s[r], vx_s[0, r], vx_s[1, r])
        rows = pl.ds(r, blk, stride=DIL_MAX)
        acc_s[0, rows, :] = o_lo
        acc_s[1, rows, :] = o_hi

    def acc_mid(i):
        r = i // mid_blocks
        j = i % mid_blocks
        q0 = pl.multiple_of(j * blk, blk)
        o_lo, o_hi = num_den(
            n_big + i, mm_s[r, pl.ds(q0, blk), :],
            vm_s[0, r, pl.ds(q0, 2 * blk), :], vm_s[1, r, pl.ds(q0, 2 * blk), :])
        rows = pl.ds(j * (blk * DIL_MID) + r, blk, stride=DIL_MID)
        acc_s[0, rows, :] += o_lo
        acc_s[1, rows, :] += o_hi

    def acc_nat(j):
        q0 = pl.multiple_of(j * blk, blk)
        rows = pl.ds(q0, blk)
        o_lo, o_hi = num_den(
            n_big + n_mid + j, mn_s[rows, :],
            vn_s[0, pl.ds(q0, 2 * blk), :], vn_s[1, pl.ds(q0, 2 * blk), :])
        a_lo = acc_s[0, rows, :] + o_lo
        a_hi = acc_s[1, rows, :] + o_hi
        num = jnp.where(lane_b, a_lo, a_hi)
        den = pltpu.roll(jnp.where(lane_b, a_hi, a_lo), ATT_HEAD_DIM, 1)
        att_ref[rows, :] = (num / den).astype(BF16)

    @pl.when(tile >= 0)
    def _pass_two():
        loop(n_big, acc_big)
        loop(n_mid, acc_mid)
        loop(n_nat, acc_nat)


def _dilated_attention(qkv, batch):
    t = qkv.shape[0]
    tiles = t // batch // ATT_TILE
    blk = ATT_BLOCK

    def spec(col0):
        return pl.BlockSpec((ATT_TILE, LANES), lambda b, p, i: (b * tiles + i, col0 + p))

    bf = lambda *shape: pltpu.VMEM(shape, BF16)
    f32 = lambda *shape: pltpu.VMEM(shape, F32)
    return pl.pallas_call(
        _attn_kernel,
        grid=(batch, ATT_PAIRS, tiles),
        in_specs=[spec(0), spec(ATT_PAIRS), spec(2 * ATT_PAIRS)],
        out_specs=spec(0),
        out_shape=jax.ShapeDtypeStruct((t, ATT_WIDTH), BF16),
        scratch_shapes=[
            bf(2, ATT_TILE, LANES), bf(blk + ATT_TILE, LANES), bf(2, blk + ATT_TILE, LANES),
            f32(DIL_MID, MID_LEN, LANES),
            bf(2, DIL_MID, MID_LEN, LANES), bf(DIL_MID, blk + MID_LEN, LANES),
            bf(2, DIL_MID, blk + MID_LEN, LANES),
            bf(2, DIL_MAX, blk, LANES), bf(DIL_MAX, 2 * blk, LANES),
            bf(2, DIL_MAX, 2 * blk, LANES),
            f32(ATT_TILE, LANES), f32(DIL_MID, MID_LEN, LANES),
            f32(DIL_MAX, blk, LANES), f32(2, ATT_TILE, LANES),
            f32(3 * 2, 2 * blk, 2 * blk),
            f32(3 * DIL_MAX, 2 * blk, 2 * blk),
        ],
        compiler_params=_params(3),
        name="attn",
    )(qkv, qkv, qkv)


def _split3(x):
    hi = x.astype(BF16)
    r1 = x - hi.astype(F32)
    mid = r1.astype(BF16)
    lo = (r1 - mid.astype(F32)).astype(BF16)
    return hi, mid, lo


def _dot_exact_rhs(x, w):
    return sum(jnp.dot(part, w, preferred_element_type=F32) for part in _split3(x))


TM_CHUNK = 128
TM_STEPS = TM_CHUNK // SUBLANES
TM_HIST = (CONV_WIDTH - 1) * SUBLANES
assert TM_CHUNK == SSD_CHUNK


def _to_time_major(src, mid, dst, base):
    quarter = TM_CHUNK // 4
    for r in range(4):
        mid[base + quarter * r:base + quarter * (r + 1), :] = (
            src[pl.ds(base + r, quarter, stride=4), :])
    for r in range(4):
        for c in range(4):
            v = r + 4 * c
            dst[base + SUBLANES * v:base + SUBLANES * (v + 1), :] = (
                mid[pl.ds(base + quarter * r + c, SUBLANES, stride=4), :])


def _tm_time(idx):
    return TM_STEPS * (idx % SUBLANES) + idx // SUBLANES


def _time_major_perm():
    n = jnp.arange(TM_CHUNK)
    return (_tm_time(n)[None, :] == n[:, None]).astype(BF16)


def _conv_time_major(x, hist, cw_ref, cb_ref, first_row):
    width = x.shape[-1]
    groups = TM_HIST // SUBLANES
    tail = x[TM_CHUNK - TM_HIST:, :]
    down = lambda a: pltpu.roll(a.reshape(groups, SUBLANES, width), 1, 1)
    wrapped = jnp.where(first_row, down(hist), down(tail)).reshape(TM_HIST, width)
    ext = jnp.concatenate([wrapped, x], axis=0)
    k_w = CONV_WIDTH
    conv = cb_ref[...] + x * cw_ref[k_w - 1:k_w, :]
    for back in range(1, k_w):
        off = TM_HIST - SUBLANES * back
        conv = conv + ext[off:off + TM_CHUNK, :] * cw_ref[k_w - 1 - back:k_w - back, :]
    return conv, tail


def _ssd_kernel(z_ref, xbc_ref, dt_ref, cw_ref, cb_ref, dtb_ref, alog_ref, dsk_ref, nw_ref,
                triu_ref, exp_ref, y_ref, win_s, state_s):
    c = pl.program_id(0)
    q = SSD_CHUNK
    pad = CONV_PAD
    rows_step = SSD_STEP_CHUNKS * q
    heads_per_group = SSD_HEADS // SSD_GROUPS
    gw = heads_per_group * SSD_HEAD_DIM
    n_seq = z_ref.shape[0]

    @pl.when(c == 0)
    def _reset():
        win_s[:, 0:pad, :] = jnp.zeros((n_seq, pad, SSD_CONV_CH), F32)
        state_s[...] = jnp.zeros_like(state_s)

    win_s[:, pad:, :] = xbc_ref[...]

    ti = lax.broadcasted_iota(jnp.int32, (q, q), 0)
    tj = lax.broadcasted_iota(jnp.int32, (q, q), 1)
    causal = ti >= tj
    low_half = lax.broadcasted_iota(jnp.int32, (q, LANES), 1) < SSD_HEAD_DIM
    a_t = -jnp.exp(alog_ref[...])
    k_w = CONV_WIDTH

    pairs_per_group = heads_per_group // HEADS_PER_PAIR

    def prepare(ci, b):
        r0 = ci * q
        win_b, dt_b = win_s.at[b], dt_ref.at[b]
        conv = cb_ref[...] + win_b[pad + r0:pad + r0 + q, :] * cw_ref[k_w - 1:k_w, :]
        for k in range(k_w - 1):
            off = pad + r0 - (k_w - 1) + k
            conv = conv + win_b[off:off + q, :] * cw_ref[k:k + 1, :]
        xbc = _silu(conv)
        xs = xbc[:, :SSD_WIDTH]
        bm = xbc[:, SSD_WIDTH:SSD_WIDTH + SSD_GROUPS * SSD_STATE]
        cm = xbc[:, SSD_WIDTH + SSD_GROUPS * SSD_STATE:]

        dt_t = _softplus(dt_b[r0:r0 + q, :].T[0:SSD_HEADS, :] + dtb_ref[...])
        acs_t = _dot_exact_rhs(dt_t * a_t, triu_ref[...])
        e_t = jnp.exp(acs_t)
        w_t = jnp.exp(acs_t[:, q - 1:q] - acs_t) * dt_t
        chunk_decay = jnp.sum(e_t[:, q - 1:q] * exp_ref[...], axis=0, keepdims=True)
        acs2_t = acs_t * LOG2E
        key2_t = acs2_t - jnp.log2(dt_t)
        cols_t = jnp.concatenate(
            [acs2_t, e_t, jnp.zeros((q - 2 * SSD_HEADS, q), F32)], axis=0).T

        pairs = []
        for g in range(SSD_GROUPS):
            bm_g = bm[:, g * SSD_STATE:(g + 1) * SSD_STATE]
            cm_g = cm[:, g * SSD_STATE:(g + 1) * SSD_STATE]
            gmat = lax.dot_general(cm_g.astype(BF16), bm_g.astype(BF16),
                                   (((1,), (1,)), ((), ())), preferred_element_type=F32)
            bm_gt = bm_g.T
            for pp in range(pairs_per_group):
                p = g * pairs_per_group + pp
                x_p = xs[:, p * LANES:(p + 1) * LANES]
                lhs_y, lhs_s, x_heads = [], [], []
                for hh in range(HEADS_PER_PAIR):
                    h = HEADS_PER_PAIR * p + hh
                    keep = low_half if hh == 0 else ~low_half
                    seg = cols_t[:, h:h + 1] - key2_t[h:h + 1, :]
                    scores = gmat * jnp.exp2(jnp.where(causal, seg, NEG_INF))
                    c_dec = cm_g * cols_t[:, SSD_HEADS + h:SSD_HEADS + h + 1]
                    lhs_y += [scores.astype(BF16), c_dec.astype(BF16)]
                    lhs_s.append((bm_gt * w_t[h:h + 1, :]).astype(BF16))
                    x_heads.append(jnp.where(keep, x_p, 0.0).astype(BF16))
                pairs.append((jnp.concatenate(lhs_y, axis=1), jnp.concatenate(lhs_s, axis=1),
                              x_heads))
        return xs, chunk_decay, pairs

    def finish(ci, b, prepared):
        xs, chunk_decay, pairs = prepared
        r0 = ci * q
        state_b, z_b, y_b = state_s.at[b], z_ref.at[b], y_ref.at[b]
        y_parts = []
        for p, (lhs_y, lhs_s, x_heads) in enumerate(pairs):
            cols = slice(p * LANES, (p + 1) * LANES)
            s_p = state_b[:, cols]
            rhs_y = []
            for hh in range(HEADS_PER_PAIR):
                keep = low_half if hh == 0 else ~low_half
                rhs_y += [x_heads[hh], jnp.where(keep, s_p, 0.0).astype(BF16)]
            y_parts.append(jnp.dot(lhs_y, jnp.concatenate(rhs_y, axis=0),
                                   preferred_element_type=F32))
            state_b[:, cols] = s_p * chunk_decay[:, cols] + jnp.dot(
                lhs_s, jnp.concatenate(x_heads, axis=0), preferred_element_type=F32)
        y = jnp.concatenate(y_parts, axis=-1) + dsk_ref[...] * xs
        y = y * _silu(z_b[r0:r0 + q, :])
        outs = []
        for g in range(SSD_GROUPS):
            yg = y[:, g * gw:(g + 1) * gw]
            outs.append(yg * lax.rsqrt(jnp.mean(yg * yg, axis=-1, keepdims=True) + SSD_NORM_EPS))
        y_b[r0:r0 + q, :] = (jnp.concatenate(outs, axis=-1) * nw_ref[...]).astype(BF16)

    order = [(ci, b) for ci in range(SSD_STEP_CHUNKS) for b in range(n_seq)]
    prepared = [prepare(ci, b) for ci, b in order]
    for (ci, b), prep in zip(order, prepared):
        finish(ci, b, prep)

    win_s[:, 0:pad, :] = win_s[:, rows_step:rows_step + pad, :]


def _prep_ssd(conv_w, conv_b, dt_bias, a_log, d_skip, norm_w):
    depth = conv_w.shape[0]
    per_time = lambda v: jnp.broadcast_to(v.astype(F32)[:, :, None],
                                          (depth, SSD_HEADS, SSD_CHUNK))
    triu = jnp.triu(jnp.ones((SSD_CHUNK, SSD_CHUNK), BF16))
    expand = (jnp.arange(SSD_HEADS)[:, None] == (jnp.arange(SSD_WIDTH)[None, :] // SSD_HEAD_DIM)
              ).astype(F32)
    d_exp = jnp.repeat(d_skip.astype(F32), SSD_HEAD_DIM, axis=1).reshape(depth, 1, SSD_WIDTH)
    return (conv_w, conv_b.reshape(depth, 1, SSD_CONV_CH), per_time(dt_bias), per_time(a_log),
            d_exp, norm_w.reshape(depth, 1, SSD_WIDTH), triu, expand)


def _ssd(z, xbc, dt, prepared, layer, batch):
    t = z.shape[0]
    seq = t // batch
    rows_step = SSD_STEP_CHUNKS * SSD_CHUNK
    row = lambda width: pl.BlockSpec((batch, rows_step, width), lambda c: (0, c, 0))
    by_seq = lambda a: a.reshape(batch, seq, a.shape[-1])
    lspec = lambda *shape: _layer_spec(shape, layer)
    out = pl.pallas_call(
        _ssd_kernel,
        grid=(seq // rows_step,),
        in_specs=[row(SSD_WIDTH), row(SSD_CONV_CH), row(DT_PAD),
                  lspec(CONV_WIDTH, SSD_CONV_CH), lspec(1, SSD_CONV_CH),
                  lspec(SSD_HEADS, SSD_CHUNK), lspec(SSD_HEADS, SSD_CHUNK),
                  lspec(1, SSD_WIDTH), lspec(1, SSD_WIDTH),
                  _const_spec((SSD_CHUNK, SSD_CHUNK)), _const_spec((SSD_HEADS, SSD_WIDTH))],
        out_specs=row(SSD_WIDTH),
        out_shape=jax.ShapeDtypeStruct((batch, seq, SSD_WIDTH), BF16),
        scratch_shapes=[pltpu.VMEM((batch, CONV_PAD + rows_step, SSD_CONV_CH), F32),
                        pltpu.VMEM((batch, SSD_STATE, SSD_WIDTH), F32)],
        compiler_params=_params(1),
        name="ssd",
    )(by_seq(z), by_seq(xbc), by_seq(dt), *prepared)
    return out.reshape(t, SSD_WIDTH)


def _gelu_tanh(x):
    c = math.sqrt(2.0 / math.pi)
    return 0.5 * x * (1.0 + jnp.tanh(c * (x + 0.044715 * (x * x * x))))


def _lru_kernel(*refs):
    ncol = LRU_WIDTH // LANES
    g_refs, x_refs = refs[:ncol], refs[ncol:2 * ncol]
    (cw_ref, cb_ref, wa_ref, ba_ref, wx_ref, bx_ref, lam_ref, perm_ref, y_ref,
     mid_s, gt_s, xt_s, hist_s, h_s) = refs[2 * ncol:]
    c = pl.program_id(1)
    tt = LRU_TILE
    n_chunks = tt // TM_CHUNK

    @pl.when(c == 0)
    def _reset():
        hist_s[...] = jnp.zeros_like(hist_s)
        h_s[...] = jnp.zeros_like(h_s)

    for j in range(ncol):
        for k in range(n_chunks):
            _to_time_major(g_refs[j], mid_s.at[j], gt_s.at[j], k * TM_CHUNK)
        for k in range(n_chunks):
            _to_time_major(x_refs[j], mid_s.at[j], xt_s.at[j], k * TM_CHUNK)

    def chunk_of(buf, k):
        rows = slice(k * TM_CHUNK, (k + 1) * TM_CHUNK)
        return jnp.concatenate([buf[j, rows, :] for j in range(ncol)], axis=1)

    first_row = lax.broadcasted_iota(
        jnp.int32, (TM_HIST // SUBLANES, SUBLANES, LRU_WIDTH), 1) == 0
    hist = hist_s[...]
    convs = []
    for k in range(n_chunks):
        conv, hist = _conv_time_major(chunk_of(xt_s, k), hist, cw_ref, cb_ref, first_row)
        convs.append(conv)
    hist_s[...] = hist
    xc = jnp.concatenate(convs, axis=0)

    xb = xc.astype(BF16)
    r = jax.nn.sigmoid(jnp.dot(xb, wa_ref[...], preferred_element_type=F32) + ba_ref[...])
    i = jax.nn.sigmoid(jnp.dot(xb, wx_ref[...], preferred_element_type=F32) + bx_ref[...])
    log_a = -LRU_C * r * _softplus(-lam_ref[...])
    a = jnp.exp(log_a)
    u = jnp.sqrt(-jnp.tanh(log_a) * (a * a + 1.0)) * (i * xc)

    sub = lax.broadcasted_iota(jnp.int32, (SUBLANES, LRU_WIDTH), 0)
    h_prev = h_s[0:1, :]
    for k in range(n_chunks):
        rows = slice(k * TM_CHUNK, (k + 1) * TM_CHUNK)
        a_k = a[rows].reshape(TM_STEPS, SUBLANES, LRU_WIDTH)
        u_k = u[rows].reshape(TM_STEPS, SUBLANES, LRU_WIDTH)
        hh, aa = [u_k[0]], [a_k[0]]
        for v in range(1, TM_STEPS):
            hh.append(a_k[v] * hh[-1] + u_k[v])
            aa.append(a_k[v] * aa[-1])
        pa, ph = aa[-1], hh[-1]
        step = 1
        while step < SUBLANES:
            keep = sub >= step
            pa_sh = jnp.where(keep, pltpu.roll(pa, step, 0), 1.0)
            ph_sh = jnp.where(keep, pltpu.roll(ph, step, 0), 0.0)
            ph = pa * ph_sh + ph
            pa = pa * pa_sh
            step *= 2
        c_in = jnp.where(sub == 0, h_prev, pltpu.roll(pa, 1, 0) * h_prev + pltpu.roll(ph, 1, 0))
        h_k = [hh[v] + aa[v] * c_in for v in range(TM_STEPS)]
        h_prev = h_k[-1][SUBLANES - 1:SUBLANES, :]
        y_k = (jnp.concatenate(h_k, axis=0) * _gelu_tanh(chunk_of(gt_s, k))).astype(BF16)
        y_ref[rows, :] = jnp.dot(perm_ref[...], y_k, preferred_element_type=F32).astype(BF16)
    h_s[...] = jnp.broadcast_to(h_prev, h_s.shape)


def _block_diag(w):
    depth, nb, c, d = w.shape
    eye = jnp.eye(nb, dtype=w.dtype)
    return (eye[None, :, None, :, None] * w[:, :, :, None, :]).reshape(depth, nb * c, nb * d)


def _prep_lru(conv_w, conv_b, wa, ba, wx, bx, lam):
    depth = conv_w.shape[0]
    vec = lambda v: v.astype(F32).reshape(depth, 1, LRU_WIDTH)
    return (conv_w, vec(conv_b), _block_diag(wa).astype(BF16), vec(ba),
            _block_diag(wx).astype(BF16), vec(bx), vec(lam))


def _lru(g_in, x_in, prepared, layer, batch):
    t = g_in.shape[0]
    s = t // batch
    nt = s // LRU_TILE
    ncol = LRU_WIDTH // LANES
    row = pl.BlockSpec((LRU_TILE, LRU_WIDTH), lambda b, c: (b * nt + c, 0))
    cols = [pl.BlockSpec((LRU_TILE, LANES), lambda b, c, j=j: (b * nt + c, j))
            for j in range(ncol)]
    mat = _layer_spec((LRU_WIDTH, LRU_WIDTH), layer)
    one = _layer_spec((1, LRU_WIDTH), layer)
    tile = lambda: pltpu.VMEM((ncol, LRU_TILE, LANES), F32)
    return pl.pallas_call(
        _lru_kernel,
        grid=(batch, nt),
        in_specs=cols + cols + [_layer_spec((CONV_WIDTH, LRU_WIDTH), layer), one, mat, one,
                                mat, one, one, _const_spec((TM_CHUNK, TM_CHUNK))],
        out_specs=row,
        out_shape=jax.ShapeDtypeStruct((t, LRU_WIDTH), BF16),
        scratch_shapes=[tile(), tile(), tile(),
                        pltpu.VMEM((TM_HIST, LRU_WIDTH), F32),
                        pltpu.VMEM((SUBLANES, LRU_WIDTH), F32)],
        compiler_params=_params(2),
        name="rglru",
    )(*([g_in] * ncol), *([x_in] * ncol), *prepared, _time_major_perm())


def _outffn_kernel(x_ref, att_ref, ssd_ref, lru_ref, wo_ref, gf_ref, wg_ref, wu_ref, wd_ref,
                   nfin_ref, o_ref, hn_s, *, final_norm):
    x1 = x_ref[...]
    for j, m_ref in enumerate((att_ref, ssd_ref, lru_ref)):
        x1 = x1 + jnp.dot(m_ref[...], wo_ref[j * ATT_WIDTH:(j + 1) * ATT_WIDTH, :],
                          preferred_element_type=F32)
    hn_s[...] = _rmsnorm(x1, gf_ref[...]).astype(BF16)
    o_ref[...] = x1
    for c0, c1 in FF_CHUNKS:
        hn = hn_s[...]
        gate = jnp.dot(hn, wg_ref[:, c0:c1], preferred_element_type=F32)
        up = jnp.dot(hn, wu_ref[:, c0:c1], preferred_element_type=F32)
        act = (_silu(gate) * up).astype(BF16)
        o_ref[...] += jnp.dot(act, wd_ref[c0:c1, :], preferred_element_type=F32)
    if final_norm:
        o_ref[...] = _rmsnorm(o_ref[...], nfin_ref[...])


def _prep_ffn(w_out, norm_ffn, w_gate, w_up, w_down, norm_final):
    depth = w_out.shape[0]
    return (w_out.astype(BF16), norm_ffn.reshape(depth, 1, D_MODEL), w_gate.astype(BF16),
            w_up.astype(BF16), w_down.astype(BF16), norm_final.reshape(1, D_MODEL))


def _outffn(x2, att, ssd, lru, prepared, layer, final_norm):
    t = x2.shape[0]
    row = lambda width: pl.BlockSpec((ROW_TILE, width), lambda i: (i, 0))
    lspec = lambda *shape: _layer_spec(shape, layer)
    return pl.pallas_call(
        functools.partial(_outffn_kernel, final_norm=final_norm),
        grid=(t // ROW_TILE,),
        in_specs=[row(D_MODEL), row(ATT_WIDTH), row(SSD_WIDTH), row(LRU_WIDTH),
                  lspec(D_MIX, D_MODEL), lspec(1, D_MODEL), lspec(D_MODEL, D_FF),
                  lspec(D_MODEL, D_FF), lspec(D_FF, D_MODEL), _const_spec((1, D_MODEL))],
        out_specs=row(D_MODEL),
        out_shape=jax.ShapeDtypeStruct((t, D_MODEL), F32),
        scratch_shapes=[pltpu.VMEM((ROW_TILE, D_MODEL), BF16)],
        compiler_params=_params(1),
        name="outffn",
    )(x2, att, ssd, lru, *prepared)


def _prep_w_in(w):
    tail = w[:, :, _C_XBC:]
    dt_cols = jnp.pad(tail[:, :, :SSD_HEADS], ((0, 0), (0, 0), (0, DT_PAD - SSD_HEADS)))
    w_tail = jnp.concatenate([tail[:, :, SSD_HEADS:], dt_cols], axis=2).astype(BF16)
    return w[:, :, :_C_XBC].astype(BF16), w_tail


def kernel(x, norm_mix, w_in, ssd_conv_w, ssd_conv_b, ssd_dt_bias, ssd_a_log, ssd_d, ssd_norm,
           lru_conv_w, lru_conv_b, lru_wa, lru_ba, lru_wx, lru_bx, lru_lambda, w_out,
           norm_ffn, w_gate, w_up, w_down, norm_final):
    batch, seq, _ = x.shape
    depth = w_in.shape[0]
    g_mix = norm_mix.reshape(depth, 1, D_MODEL)
    w_proj, w_tail = _prep_w_in(w_in)
    ssd_p = _prep_ssd(ssd_conv_w, ssd_conv_b, ssd_dt_bias, ssd_a_log, ssd_d, ssd_norm)
    lru_p = _prep_lru(lru_conv_w, lru_conv_b, lru_wa, lru_ba, lru_wx, lru_bx, lru_lambda)
    ffn_p = _prep_ffn(w_out, norm_ffn, w_gate, w_up, w_down, norm_final)
    x2 = x.reshape(batch * seq, D_MODEL)
    for l in range(depth):
        qkv, z, xbc, g_lru, x_lru, dt = _inproj(x2, g_mix, w_proj, w_tail, l)
        att = _dilated_attention(qkv, batch)
        ssd = _ssd(z, xbc, dt, ssd_p, l, batch)
        lru = _lru(g_lru, x_lru, lru_p, l, batch)
        x2 = _outffn(x2, att, ssd, lru, ffn_p, l, l == depth - 1)
    return x2.reshape(batch, seq, D_MODEL)
```

```python
import functools
import math

import jax
import jax.numpy as jnp
from jax import lax
from jax.experimental import pallas as pl
from jax.experimental.pallas import tpu as pltpu

F32 = jnp.float32
BF16 = jnp.bfloat16

D_MODEL = 1024
ATT_HEADS = 8
ATT_HEAD_DIM = 64
ATT_WIDTH = ATT_HEADS * ATT_HEAD_DIM
ATT_BLOCK = 128
ATT_DILATIONS = (1, 4, 16)
SSD_HEADS = 8
SSD_HEAD_DIM = 64
SSD_WIDTH = SSD_HEADS * SSD_HEAD_DIM
SSD_GROUPS = 2
SSD_STATE = 128
SSD_CHUNK = 128
SSD_CONV_CH = SSD_WIDTH + 2 * SSD_GROUPS * SSD_STATE
LRU_WIDTH = 512
LRU_BLOCKS = 8
LRU_BLOCK_W = LRU_WIDTH // LRU_BLOCKS
LRU_C = 8.0
CONV_WIDTH = 4
D_MIX = ATT_WIDTH + SSD_WIDTH + LRU_WIDTH
D_FF = 2816
NORM_EPS = 1e-6
SSD_NORM_EPS = 1e-5

LANES = 128
SUBLANES = 8
VMEM_LIMIT_BYTES = 56 * 1024 * 1024

ROW_TILE = 1024
LRU_TILE = 1024
SSD_STEP_CHUNKS = 4
DT_PAD = LANES
FF_CHUNKS = ((0, 768), (768, 1536), (1536, 2304), (2304, 2816))
CONV_PAD = SUBLANES

DIL_MID, DIL_MAX = ATT_DILATIONS[1], ATT_DILATIONS[2]
DIL_STEP = DIL_MAX // DIL_MID
ATT_TILE = ATT_BLOCK * DIL_MAX
MID_LEN = ATT_TILE // DIL_MID
HEADS_PER_PAIR = LANES // ATT_HEAD_DIM
ATT_PAIRS = ATT_WIDTH // LANES
ATT_UNROLL = 16
assert ATT_HEADS == 8 and HEADS_PER_PAIR == 2 and DIL_STEP == DIL_MID and ATT_DILATIONS[0] == 1

LOG2E = 1.4426950408889634
NEG_INF = float("-inf")


def _params(n_axes):
    return pltpu.CompilerParams(
        dimension_semantics=("arbitrary",) * n_axes,
        vmem_limit_bytes=VMEM_LIMIT_BYTES)


def _const_spec(shape):
    nd = len(shape)
    return pl.BlockSpec(shape, lambda *_: (0,) * nd, pipeline_mode=pl.Buffered(1))


def _layer_spec(shape, layer):
    nd = len(shape)
    return pl.BlockSpec((None,) + tuple(shape), lambda *_: (layer,) + (0,) * nd,
                        pipeline_mode=pl.Buffered(1))


def _rmsnorm(x, g):
    return x * lax.rsqrt(jnp.mean(x * x, axis=-1, keepdims=True) + NORM_EPS) * g


def _softplus(x):
    return jnp.maximum(x, 0.0) + jnp.log1p(jnp.exp(-jnp.abs(x)))


def _silu(x):
    return x * jax.nn.sigmoid(x)


_C_QKV = 3 * ATT_WIDTH
_C_Z = _C_QKV + SSD_WIDTH
_C_XBC = _C_Z + SSD_CONV_CH
_T_XL = LRU_WIDTH
_T_DT = 2 * LRU_WIDTH
_T_END = _T_DT + DT_PAD


def _inproj_kernel(x_ref, g_ref, w_ref, wt_ref, qkv_ref, z_ref, xbc_ref, gl_ref, xl_ref, dt_ref):
    h = _rmsnorm(x_ref[...], g_ref[...]).astype(BF16)

    def seg(ref, a, b):
        return jnp.dot(h, ref[:, a:b], preferred_element_type=F32)

    qkv_ref[:, 0:ATT_WIDTH] = seg(w_ref, 0, ATT_WIDTH) * (ATT_HEAD_DIM ** -0.5 * LOG2E)
    qkv_ref[:, ATT_WIDTH:_C_QKV] = seg(w_ref, ATT_WIDTH, _C_QKV)
    z_ref[...] = seg(w_ref, _C_QKV, _C_Z)
    xbc_ref[...] = seg(w_ref, _C_Z, _C_XBC)
    gl_ref[...] = seg(wt_ref, 0, _T_XL)
    xl_ref[...] = seg(wt_ref, _T_XL, _T_DT)
    dt_ref[...] = seg(wt_ref, _T_DT, _T_END)


def _inproj(x2, g, w, w_tail, layer):
    t = x2.shape[0]
    row = lambda width: pl.BlockSpec((ROW_TILE, width), lambda i: (i, 0))
    widths = (_C_QKV, SSD_WIDTH, SSD_CONV_CH, LRU_WIDTH, LRU_WIDTH, DT_PAD)
    return pl.pallas_call(
        _inproj_kernel,
        grid=(t // ROW_TILE,),
        in_specs=[row(D_MODEL), _layer_spec((1, D_MODEL), layer),
                  _layer_spec((D_MODEL, _C_XBC), layer),
                  _layer_spec((D_MODEL, _T_END), layer)],
        out_specs=[row(wd) for wd in widths],
        out_shape=[jax.ShapeDtypeStruct((t, wd), F32) for wd in widths],
        compiler_params=_params(1),
        name="inproj",
    )(x2, g, w, w_tail)


def _scores(q_lo, q_hi, kk, bias):
    q_both = jnp.concatenate([q_lo, q_hi], axis=0)
    return lax.dot_general(q_both, kk, (((1,), (1,)), ((), ())),
                           preferred_element_type=F32) + bias


def _attn_kernel(q_ref, k_ref, v_ref, att_ref,
                 qn_s, kn_s, vn_s, tmp_s, qm_s, km_s, vm_s, qx_s, kx_s, vx_s,
                 mn_s, mm_s, mx_s, acc_s, bias_s, sc_s):
    pair = pl.program_id(1)
    tile = pl.program_id(2)
    blk = ATT_BLOCK
    first_tile = tile == 0
    mid_blocks = MID_LEN // blk

    @pl.when(first_tile)
    def _start_sequence():
        qi = lax.broadcasted_iota(jnp.int32, (blk, 2 * blk), 0)
        ki = lax.broadcasted_iota(jnp.int32, (blk, 2 * blk), 1)
        dist = blk + qi - ki
        band = (dist >= 0) & (dist <= blk)
        band_first = band & (ki >= blk)
        for hh in range(HEADS_PER_PAIR):
            expo = (126 - HEADS_PER_PAIR * pair - hh) << 23
            slope = lax.bitcast_convert_type(jnp.full((blk, 2 * blk), expo, jnp.int32), F32)
            rows = slice(hh * blk, (hh + 1) * blk)
            for br, dil in enumerate((DIL_MAX, DIL_MID, 1)):
                alibi = (-slope * (dil * dist).astype(F32)) * LOG2E
                bias_s[2 * br, rows, :] = jnp.where(band, alibi, NEG_INF)
                bias_s[2 * br + 1, rows, :] = jnp.where(band_first, alibi, NEG_INF)
        kn_s[0:blk, :] = jnp.zeros((blk, LANES), BF16)
        vn_s[:, 0:blk, :] = jnp.zeros((2, blk, LANES), BF16)
        km_s[:, 0:blk, :] = jnp.zeros((DIL_MID, blk, LANES), BF16)
        vm_s[:, :, 0:blk, :] = jnp.zeros((2, DIL_MID, blk, LANES), BF16)
        kx_s[:, 0:blk, :] = jnp.zeros((DIL_MAX, blk, LANES), BF16)
        vx_s[:, :, 0:blk, :] = jnp.zeros((2, DIL_MAX, blk, LANES), BF16)

    @pl.when(tile > 0)
    def _carry_history():
        kn_s[0:blk, :] = kn_s[ATT_TILE:ATT_TILE + blk, :]
        vn_s[:, 0:blk, :] = vn_s[:, ATT_TILE:ATT_TILE + blk, :]
        km_s[:, 0:blk, :] = km_s[:, MID_LEN:MID_LEN + blk, :]
        vm_s[:, :, 0:blk, :] = vm_s[:, :, MID_LEN:MID_LEN + blk, :]
        kx_s[:, 0:blk, :] = kx_s[:, blk:2 * blk, :]
        vx_s[:, :, 0:blk, :] = vx_s[:, :, blk:2 * blk, :]

    lane_m = lax.broadcasted_iota(jnp.int32, (MID_LEN, LANES), 1) < ATT_HEAD_DIM
    lane_b = lax.broadcasted_iota(jnp.int32, (blk, LANES), 1) < ATT_HEAD_DIM

    def put_q(dst, idx, x, mask):
        dst[(0,) + idx] = jnp.where(mask, x, 0.0).astype(BF16)
        dst[(1,) + idx] = jnp.where(mask, 0.0, x).astype(BF16)

    def put_k(dst, idx, x, mask):
        del mask
        dst[idx] = x.astype(BF16)

    def put_v(dst, idx, x, mask):
        dst[(0,) + idx] = jnp.where(mask, x, 1.0).astype(BF16)
        dst[(1,) + idx] = jnp.where(mask, 1.0, x).astype(BF16)

    for src, nat, mid, big, put, hist in (
            (q_ref, qn_s, qm_s, qx_s, put_q, 0),
            (k_ref, kn_s, km_s, kx_s, put_k, blk),
            (v_ref, vn_s, vm_s, vx_s, put_v, blk)):
        for c in range(DIL_MID):
            put(nat, (slice(hist + c * MID_LEN, hist + (c + 1) * MID_LEN), slice(None)),
                src[c * MID_LEN:(c + 1) * MID_LEN, :], lane_m)
        for r in range(DIL_MID):
            x = src[pl.ds(r, MID_LEN, stride=DIL_MID), :]
            tmp_s[r] = x
            put(mid, (r, slice(hist, hist + MID_LEN), slice(None)), x, lane_m)
        for r in range(DIL_MID):
            for c in range(DIL_STEP):
                y = tmp_s.at[r][pl.ds(c, blk, stride=DIL_STEP), :]
                put(big, (r + DIL_MID * c, slice(hist, hist + blk), slice(None)), y, lane_b)

    def bias_of(branch, is_first):
        return bias_s[2 * branch + jnp.where(is_first, 1, 0)]

    def big_ops(r):
        return (qx_s[0, r], qx_s[1, r], kx_s[r], bias_of(0, first_tile),
                pl.ds(r, blk, stride=DIL_MAX))

    def mid_ops(i):
        r = i // mid_blocks
        j = i % mid_blocks
        q0 = pl.multiple_of(j * blk, blk)
        return (qm_s[0, r, pl.ds(q0, blk), :], qm_s[1, r, pl.ds(q0, blk), :],
                km_s[r, pl.ds(q0, 2 * blk), :], bias_of(1, first_tile & (j == 0)),
                pl.ds(j * (blk * DIL_MID) + r, blk, stride=DIL_MID))

    def nat_ops(j):
        q0 = pl.multiple_of(j * blk, blk)
        return (qn_s[0, pl.ds(q0, blk), :], qn_s[1, pl.ds(q0, blk), :],
                kn_s[pl.ds(q0, 2 * blk), :], bias_of(2, first_tile & (j == 0)),
                pl.ds(q0, blk))

    def loop(n, body):
        lax.fori_loop(0, n, lambda i, c: (body(i), c)[1], 0, unroll=ATT_UNROLL)

    n_big, n_mid, n_nat = DIL_MAX, DIL_MID * mid_blocks, ATT_TILE // blk

    def row_max(ops, slot):
        q_lo, q_hi, kk, bias, rows = ops
        s = _scores(q_lo, q_hi, kk, bias)
        sc_s[slot] = s
        m = jnp.broadcast_to(jnp.max(s, axis=-1, keepdims=True), (2 * blk, LANES))
        return jnp.where(lane_b, m[:blk], m[blk:]), rows

    def max_big(r):
        m, rows = row_max(big_ops(r), r)
        mn_s[rows, :] = m

    def max_more(ops, slot):
        m, rows = row_max(ops, slot)
        mn_s[rows, :] = jnp.maximum(mn_s[rows, :], m)

    loop(n_big, max_big)
    loop(n_mid, lambda i: max_more(mid_ops(i), n_big + i))
    loop(n_nat, lambda j: max_more(nat_ops(j), n_big + n_mid + j))

    for r in range(DIL_MID):
        mm_s[r] = mn_s[pl.ds(r, MID_LEN, stride=DIL_MID), :]
    for r in range(DIL_MID):
        for c in range(DIL_STEP):
            mx_s[r + DIL_MID * c] = mm_s.at[r][pl.ds(c, blk, stride=DIL_STEP), :]

    def num_den(slot, m_pair, v_lo, v_hi):
        swapped = pltpu.roll(m_pair, ATT_HEAD_DIM, 1)
        m = jnp.concatenate([jnp.where(lane_b, m_pair, swapped),
                             jnp.where(lane_b, swapped, m_pair)], axis=0)
        e = jnp.exp2(sc_s[slot] - jnp.concatenate([m, m], axis=1)).astype(BF16)
        return (jnp.dot(e[:blk], v_lo, preferred_element_type=F32),
                jnp.dot(e[blk:], v_hi, preferred_element_type=F32))

    def acc_big(r):
        o_lo, o_hi = num_den(r, mx_s[r], vx_s[0, r], vx_s[1, r])
        rows = pl.ds(r, blk, stride=DIL_MAX)
        acc_s[0, rows, :] = o_lo
        acc_s[1, rows, :] = o_hi

    def acc_mid(i):
        r = i // mid_blocks
        j = i % mid_blocks
        q0 = pl.multiple_of(j * blk, blk)
        o_lo, o_hi = num_den(
            n_big + i, mm_s[r, pl.ds(q0, blk), :],
            vm_s[0, r, pl.ds(q0, 2 * blk), :], vm_s[1, r, pl.ds(q0, 2 * blk), :])
        rows = pl.ds(j * (blk * DIL_MID) + r, blk, stride=DIL_MID)
        acc_s[0, rows, :] += o_lo
        acc_s[1, rows, :] += o_hi

    def acc_nat(j):
        q0 = pl.multiple_of(j * blk, blk)
        rows = pl.ds(q0, blk)
        o_lo, o_hi = num_den(
            n_big + n_mid + j, mn_s[rows, :],
            vn_s[0, pl.ds(q0, 2 * blk), :], vn_s[1, pl.ds(q0, 2 * blk), :])
        a_lo = acc_s[0, rows, :] + o_lo
        a_hi = acc_s[1, rows, :] + o_hi
        num = jnp.where(lane_b, a_lo, a_hi)
        den = pltpu.roll(jnp.where(lane_b, a_hi, a_lo), ATT_HEAD_DIM, 1)
        att_ref[rows, :] = (num / den).astype(BF16)

    @pl.when(tile >= 0)
    def _pass_two():
        loop(n_big, acc_big)
        loop(n_mid, acc_mid)
        loop(n_nat, acc_nat)


def _dilated_attention(qkv, batch):
    t = qkv.shape[0]
    tiles = t // batch // ATT_TILE
    blk = ATT_BLOCK

    def spec(col0):
        return pl.BlockSpec((ATT_TILE, LANES), lambda b, p, i: (b * tiles + i, col0 + p))

    bf = lambda *shape: pltpu.VMEM(shape, BF16)
    f32 = lambda *shape: pltpu.VMEM(shape, F32)
    return pl.pallas_call(
        _attn_kernel,
        grid=(batch, ATT_PAIRS, tiles),
        in_specs=[spec(0), spec(ATT_PAIRS), spec(2 * ATT_PAIRS)],
        out_specs=spec(0),
        out_shape=jax.ShapeDtypeStruct((t, ATT_WIDTH), BF16),
        scratch_shapes=[
            bf(2, ATT_TILE, LANES), bf(blk + ATT_TILE, LANES), bf(2, blk + ATT_TILE, LANES),
            f32(DIL_MID, MID_LEN, LANES),
            bf(2, DIL_MID, MID_LEN, LANES), bf(DIL_MID, blk + MID_LEN, LANES),
            bf(2, DIL_MID, blk + MID_LEN, LANES),
            bf(2, DIL_MAX, blk, LANES), bf(DIL_MAX, 2 * blk, LANES),
            bf(2, DIL_MAX, 2 * blk, LANES),
            f32(ATT_TILE, LANES), f32(DIL_MID, MID_LEN, LANES),
            f32(DIL_MAX, blk, LANES), f32(2, ATT_TILE, LANES),
            f32(3 * 2, 2 * blk, 2 * blk),
            f32(3 * DIL_MAX, 2 * blk, 2 * blk),
        ],
        compiler_params=_params(3),
        name="attn",
    )(qkv, qkv, qkv)


def _split3(x):
    hi = x.astype(BF16)
    r1 = x - hi.astype(F32)
    mid = r1.astype(BF16)
    lo = (r1 - mid.astype(F32)).astype(BF16)
    return hi, mid, lo


def _dot_exact_rhs(x, w):
    return sum(jnp.dot(part, w, preferred_element_type=F32) for part in _split3(x))


TM_CHUNK = 128
TM_STEPS = TM_CHUNK // SUBLANES
TM_HIST = (CONV_WIDTH - 1) * SUBLANES
assert TM_CHUNK == SSD_CHUNK


def _to_time_major(src, mid, dst, base):
    quarter = TM_CHUNK // 4
    for r in range(4):
        mid[base + quarter * r:base + quarter * (r + 1), :] = (
            src[pl.ds(base + r, quarter, stride=4), :])
    for r in range(4):
        for c in range(4):
            v = r + 4 * c
            dst[base + SUBLANES * v:base + SUBLANES * (v + 1), :] = (
                mid[pl.ds(base + quarter * r + c, SUBLANES, stride=4), :])


def _tm_time(idx):
    return TM_STEPS * (idx % SUBLANES) + idx // SUBLANES


def _time_major_perm():
    n = jnp.arange(TM_CHUNK)
    return (_tm_time(n)[None, :] == n[:, None]).astype(BF16)


def _conv_time_major(x, hist, cw_ref, cb_ref, first_row):
    width = x.shape[-1]
    groups = TM_HIST // SUBLANES
    tail = x[TM_CHUNK - TM_HIST:, :]
    down = lambda a: pltpu.roll(a.reshape(groups, SUBLANES, width), 1, 1)
    wrapped = jnp.where(first_row, down(hist), down(tail)).reshape(TM_HIST, width)
    ext = jnp.concatenate([wrapped, x], axis=0)
    k_w = CONV_WIDTH
    conv = cb_ref[...] + x * cw_ref[k_w - 1:k_w, :]
    for back in range(1, k_w):
        off = TM_HIST - SUBLANES * back
        conv = conv + ext[off:off + TM_CHUNK, :] * cw_ref[k_w - 1 - back:k_w - back, :]
    return conv, tail


def _ssd_kernel(z_ref, xbc_ref, dt_ref, cw_ref, cb_ref, dtb_ref, alog_ref, dsk_ref, nw_ref,
                triu_ref, exp_ref, y_ref, win_s, state_s):
    c = pl.program_id(0)
    q = SSD_CHUNK
    pad = CONV_PAD
    rows_step = SSD_STEP_CHUNKS * q
    heads_per_group = SSD_HEADS // SSD_GROUPS
    gw = heads_per_group * SSD_HEAD_DIM
    n_seq = z_ref.shape[0]

    @pl.when(c == 0)
    def _reset():
        win_s[:, 0:pad, :] = jnp.zeros((n_seq, pad, SSD_CONV_CH), F32)
        state_s[...] = jnp.zeros_like(state_s)

    win_s[:, pad:, :] = xbc_ref[...]

    ti = lax.broadcasted_iota(jnp.int32, (q, q), 0)
    tj = lax.broadcasted_iota(jnp.int32, (q, q), 1)
    causal = ti >= tj
    low_half = lax.broadcasted_iota(jnp.int32, (q, LANES), 1) < SSD_HEAD_DIM
    a_t = -jnp.exp(alog_ref[...])
    k_w = CONV_WIDTH

    pairs_per_group = heads_per_group // HEADS_PER_PAIR

    def prepare(ci, b):
        r0 = ci * q
        win_b, dt_b = win_s.at[b], dt_ref.at[b]
        conv = cb_ref[...] + win_b[pad + r0:pad + r0 + q, :] * cw_ref[k_w - 1:k_w, :]
        for k in range(k_w - 1):
            off = pad + r0 - (k_w - 1) + k
            conv = conv + win_b[off:off + q, :] * cw_ref[k:k + 1, :]
        xbc = _silu(conv)
        xs = xbc[:, :SSD_WIDTH]
        bm = xbc[:, SSD_WIDTH:SSD_WIDTH + SSD_GROUPS * SSD_STATE]
        cm = xbc[:, SSD_WIDTH + SSD_GROUPS * SSD_STATE:]

        dt_t = _softplus(dt_b[r0:r0 + q, :].T[0:SSD_HEADS, :] + dtb_ref[...])
        acs_t = _dot_exact_rhs(dt_t * a_t, triu_ref[...])
        e_t = jnp.exp(acs_t)
        w_t = jnp.exp(acs_t[:, q - 1:q] - acs_t) * dt_t
        chunk_decay = jnp.sum(e_t[:, q - 1:q] * exp_ref[...], axis=0, keepdims=True)
        cols_t = jnp.concatenate(
            [acs_t, e_t, jnp.zeros((q - 2 * SSD_HEADS, q), F32)], axis=0).T

        pairs = []
        for g in range(SSD_GROUPS):
            bm_g = bm[:, g * SSD_STATE:(g + 1) * SSD_STATE]
            cm_g = cm[:, g * SSD_STATE:(g + 1) * SSD_STATE]
            gmat = lax.dot_general(cm_g.astype(BF16), bm_g.astype(BF16),
                                   (((1,), (1,)), ((), ())), preferred_element_type=F32)
            bm_gt = bm_g.T
            for pp in range(pairs_per_group):
                p = g * pairs_per_group + pp
                x_p = xs[:, p * LANES:(p + 1) * LANES]
                lhs_y, lhs_s, x_heads = [], [], []
                for hh in range(HEADS_PER_PAIR):
                    h = HEADS_PER_PAIR * p + hh
                    keep = low_half if hh == 0 else ~low_half
                    seg = cols_t[:, h:h + 1] - acs_t[h:h + 1, :]
                    lmat = jnp.exp(jnp.where(causal, seg, NEG_INF))
                    scores = gmat * lmat * dt_t[h:h + 1, :]
                    c_dec = cm_g * cols_t[:, SSD_HEADS + h:SSD_HEADS + h + 1]
                    lhs_y += [scores.astype(BF16), c_dec.astype(BF16)]
                    lhs_s.append((bm_gt * w_t[h:h + 1, :]).astype(BF16))
                    x_heads.append(jnp.where(keep, x_p, 0.0).astype(BF16))
                pairs.append((jnp.concatenate(lhs_y, axis=1), jnp.concatenate(lhs_s, axis=1),
                              x_heads))
        return xs, chunk_decay, pairs

    def finish(ci, b, prepared):
        xs, chunk_decay, pairs = prepared
        r0 = ci * q
        state_b, z_b, y_b = state_s.at[b], z_ref.at[b], y_ref.at[b]
        y_parts = []
        for p, (lhs_y, lhs_s, x_heads) in enumerate(pairs):
            cols = slice(p * LANES, (p + 1) * LANES)
            s_p = state_b[:, cols]
            rhs_y = []
            for hh in range(HEADS_PER_PAIR):
                keep = low_half if hh == 0 else ~low_half
                rhs_y += [x_heads[hh], jnp.where(keep, s_p, 0.0).astype(BF16)]
            y_parts.append(jnp.dot(lhs_y, jnp.concatenate(rhs_y, axis=0),
                                   preferred_element_type=F32))
            state_b[:, cols] = s_p * chunk_decay[:, cols] + jnp.dot(
                lhs_s, jnp.concatenate(x_heads, axis=0), preferred_element_type=F32)
        y = jnp.concatenate(y_parts, axis=-1) + dsk_ref[...] * xs
        y = y * _silu(z_b[r0:r0 + q, :])
        outs = []
        for g in range(SSD_GROUPS):
            yg = y[:, g * gw:(g + 1) * gw]
            outs.append(yg * lax.rsqrt(jnp.mean(yg * yg, axis=-1, keepdims=True) + SSD_NORM_EPS))
        y_b[r0:r0 + q, :] = (jnp.concatenate(outs, axis=-1) * nw_ref[...]).astype(BF16)

    order = [(ci, b) for ci in range(SSD_STEP_CHUNKS) for b in range(n_seq)]
    prepared = [prepare(ci, b) for ci, b in order]
    for (ci, b), prep in zip(order, prepared):
        finish(ci, b, prep)

    win_s[:, 0:pad, :] = win_s[:, rows_step:rows_step + pad, :]


def _prep_ssd(conv_w, conv_b, dt_bias, a_log, d_skip, norm_w):
    depth = conv_w.shape[0]
    per_time = lambda v: jnp.broadcast_to(v.astype(F32)[:, :, None],
                                          (depth, SSD_HEADS, SSD_CHUNK))
    triu = jnp.triu(jnp.ones((SSD_CHUNK, SSD_CHUNK), BF16))
    expand = (jnp.arange(SSD_HEADS)[:, None] == (jnp.arange(SSD_WIDTH)[None, :] // SSD_HEAD_DIM)
              ).astype(F32)
    d_exp = jnp.repeat(d_skip.astype(F32), SSD_HEAD_DIM, axis=1).reshape(depth, 1, SSD_WIDTH)
    return (conv_w, conv_b.reshape(depth, 1, SSD_CONV_CH), per_time(dt_bias), per_time(a_log),
            d_exp, norm_w.reshape(depth, 1, SSD_WIDTH), triu, expand)


def _ssd(z, xbc, dt, prepared, layer, batch):
    t = z.shape[0]
    seq = t // batch
    rows_step = SSD_STEP_CHUNKS * SSD_CHUNK
    row = lambda width: pl.BlockSpec((batch, rows_step, width), lambda c: (0, c, 0))
    by_seq = lambda a: a.reshape(batch, seq, a.shape[-1])
    lspec = lambda *shape: _layer_spec(shape, layer)
    out = pl.pallas_call(
        _ssd_kernel,
        grid=(seq // rows_step,),
        in_specs=[row(SSD_WIDTH), row(SSD_CONV_CH), row(DT_PAD),
                  lspec(CONV_WIDTH, SSD_CONV_CH), lspec(1, SSD_CONV_CH),
                  lspec(SSD_HEADS, SSD_CHUNK), lspec(SSD_HEADS, SSD_CHUNK),
                  lspec(1, SSD_WIDTH), lspec(1, SSD_WIDTH),
                  _const_spec((SSD_CHUNK, SSD_CHUNK)), _const_spec((SSD_HEADS, SSD_WIDTH))],
        out_specs=row(SSD_WIDTH),
        out_shape=jax.ShapeDtypeStruct((batch, seq, SSD_WIDTH), BF16),
        scratch_shapes=[pltpu.VMEM((batch, CONV_PAD + rows_step, SSD_CONV_CH), F32),
                        pltpu.VMEM((batch, SSD_STATE, SSD_WIDTH), F32)],
        compiler_params=_params(1),
        name="ssd",
    )(by_seq(z), by_seq(xbc), by_seq(dt), *prepared)
    return out.reshape(t, SSD_WIDTH)


def _gelu_tanh(x):
    c = math.sqrt(2.0 / math.pi)
    return 0.5 * x * (1.0 + jnp.tanh(c * (x + 0.044715 * (x * x * x))))


def _lru_kernel(*refs):
    ncol = LRU_WIDTH // LANES
    g_refs, x_refs = refs[:ncol], refs[ncol:2 * ncol]
    (cw_ref, cb_ref, wa_ref, ba_ref, wx_ref, bx_ref, lam_ref, perm_ref, y_ref,
     mid_s, gt_s, xt_s, hist_s, h_s) = refs[2 * ncol:]
    c = pl.program_id(1)
    tt = LRU_TILE
    n_chunks = tt // TM_CHUNK

    @pl.when(c == 0)
    def _reset():
        hist_s[...] = jnp.zeros_like(hist_s)
        h_s[...] = jnp.zeros_like(h_s)

    for j in range(ncol):
        for k in range(n_chunks):
            _to_time_major(g_refs[j], mid_s.at[j], gt_s.at[j], k * TM_CHUNK)
        for k in range(n_chunks):
            _to_time_major(x_refs[j], mid_s.at[j], xt_s.at[j], k * TM_CHUNK)

    def chunk_of(buf, k):
        rows = slice(k * TM_CHUNK, (k + 1) * TM_CHUNK)
        return jnp.concatenate([buf[j, rows, :] for j in range(ncol)], axis=1)

    first_row = lax.broadcasted_iota(
        jnp.int32, (TM_HIST // SUBLANES, SUBLANES, LRU_WIDTH), 1) == 0
    hist = hist_s[...]
    convs = []
    for k in range(n_chunks):
        conv, hist = _conv_time_major(chunk_of(xt_s, k), hist, cw_ref, cb_ref, first_row)
        convs.append(conv)
    hist_s[...] = hist
    xc = jnp.concatenate(convs, axis=0)

    xb = xc.astype(BF16)
    r = jax.nn.sigmoid(jnp.dot(xb, wa_ref[...], preferred_element_type=F32) + ba_ref[...])
    i = jax.nn.sigmoid(jnp.dot(xb, wx_ref[...], preferred_element_type=F32) + bx_ref[...])
    log_a = -LRU_C * r * _softplus(-lam_ref[...])
    a = jnp.exp(log_a)
    u = jnp.sqrt(-jnp.tanh(log_a) * (a * a + 1.0)) * (i * xc)

    sub = lax.broadcasted_iota(jnp.int32, (SUBLANES, LRU_WIDTH), 0)
    h_prev = h_s[0:1, :]
    for k in range(n_chunks):
        rows = slice(k * TM_CHUNK, (k + 1) * TM_CHUNK)
        a_k = a[rows].reshape(TM_STEPS, SUBLANES, LRU_WIDTH)
        u_k = u[rows].reshape(TM_STEPS, SUBLANES, LRU_WIDTH)
        hh, aa = [u_k[0]], [a_k[0]]
        for v in range(1, TM_STEPS):
            hh.append(a_k[v] * hh[-1] + u_k[v])
            aa.append(a_k[v] * aa[-1])
        pa, ph = aa[-1], hh[-1]
        step = 1
        while step < SUBLANES:
            keep = sub >= step
            pa_sh = jnp.where(keep, pltpu.roll(pa, step, 0), 1.0)
            ph_sh = jnp.where(keep, pltpu.roll(ph, step, 0), 0.0)
            ph = pa * ph_sh + ph
            pa = pa * pa_sh
            step *= 2
        c_in = jnp.where(sub == 0, h_prev, pltpu.roll(pa, 1, 0) * h_prev + pltpu.roll(ph, 1, 0))
        h_k = [hh[v] + aa[v] * c_in for v in range(TM_STEPS)]
        h_prev = h_k[-1][SUBLANES - 1:SUBLANES, :]
        y_k = (jnp.concatenate(h_k, axis=0) * _gelu_tanh(chunk_of(gt_s, k))).astype(BF16)
        y_ref[rows, :] = jnp.dot(perm_ref[...], y_k, preferred_element_type=F32).astype(BF16)
    h_s[...] = jnp.broadcast_to(h_prev, h_s.shape)


def _block_diag(w):
    depth, nb, c, d = w.shape
    eye = jnp.eye(nb, dtype=w.dtype)
    return (eye[None, :, None, :, None] * w[:, :, :, None, :]).reshape(depth, nb * c, nb * d)


def _prep_lru(conv_w, conv_b, wa, ba, wx, bx, lam):
    depth = conv_w.shape[0]
    vec = lambda v: v.astype(F32).reshape(depth, 1, LRU_WIDTH)
    return (conv_w, vec(conv_b), _block_diag(wa).astype(BF16), vec(ba),
            _block_diag(wx).astype(BF16), vec(bx), vec(lam))


def _lru(g_in, x_in, prepared, layer, batch):
    t = g_in.shape[0]
    s = t // batch
    nt = s // LRU_TILE
    ncol = LRU_WIDTH // LANES
    row = pl.BlockSpec((LRU_TILE, LRU_WIDTH), lambda b, c: (b * nt + c, 0))
    cols = [pl.BlockSpec((LRU_TILE, LANES), lambda b, c, j=j: (b * nt + c, j))
            for j in range(ncol)]
    mat = _layer_spec((LRU_WIDTH, LRU_WIDTH), layer)
    one = _layer_spec((1, LRU_WIDTH), layer)
    tile = lambda: pltpu.VMEM((ncol, LRU_TILE, LANES), F32)
    return pl.pallas_call(
        _lru_kernel,
        grid=(batch, nt),
        in_specs=cols + cols + [_layer_spec((CONV_WIDTH, LRU_WIDTH), layer), one, mat, one,
                                mat, one, one, _const_spec((TM_CHUNK, TM_CHUNK))],
        out_specs=row,
        out_shape=jax.ShapeDtypeStruct((t, LRU_WIDTH), BF16),
        scratch_shapes=[tile(), tile(), tile(),
                        pltpu.VMEM((TM_HIST, LRU_WIDTH), F32),
                        pltpu.VMEM((SUBLANES, LRU_WIDTH), F32)],
        compiler_params=_params(2),
        name="rglru",
    )(*([g_in] * ncol), *([x_in] * ncol), *prepared, _time_major_perm())


def _outffn_kernel(x_ref, att_ref, ssd_ref, lru_ref, wo_hbm, gf_ref, wg_hbm, wu_hbm, wd_hbm,
                   nfin_ref, o_ref, hn_s, wo_s, wg_s, wu_s, wd_s, sems, *, final_norm, layer):
    first = pl.program_id(0) == 0
    copies = [pltpu.make_async_copy(wo_hbm.at[layer], wo_s, sems.at[0])]
    for k, (c0, c1) in enumerate(FF_CHUNKS):
        cols = pl.ds(c0, c1 - c0)
        copies += [
            pltpu.make_async_copy(wg_hbm.at[layer, :, cols], wg_s.at[:, cols], sems.at[3 * k + 1]),
            pltpu.make_async_copy(wu_hbm.at[layer, :, cols], wu_s.at[:, cols], sems.at[3 * k + 2]),
            pltpu.make_async_copy(wd_hbm.at[layer, cols, :], wd_s.at[cols, :], sems.at[3 * k + 3])]

    @pl.when(first)
    def _start_weight_copies():
        for cp in copies:
            cp.start()
        copies[0].wait()

    x1 = x_ref[...]
    for j, m_ref in enumerate((att_ref, ssd_ref, lru_ref)):
        x1 = x1 + jnp.dot(m_ref[...], wo_s[j * ATT_WIDTH:(j + 1) * ATT_WIDTH, :],
                          preferred_element_type=F32)
    hn_s[...] = _rmsnorm(x1, gf_ref[...]).astype(BF16)
    o_ref[...] = x1
    @pl.when(first)
    def _ffn_weights_ready():
        for cp in copies[1:]:
            cp.wait()

    for c0, c1 in FF_CHUNKS:
        hn = hn_s[...]
        gate = jnp.dot(hn, wg_s[:, c0:c1], preferred_element_type=F32)
        up = jnp.dot(hn, wu_s[:, c0:c1], preferred_element_type=F32)
        act = (_silu(gate) * up).astype(BF16)
        o_ref[...] += jnp.dot(act, wd_s[c0:c1, :], preferred_element_type=F32)
    if final_norm:
        o_ref[...] = _rmsnorm(o_ref[...], nfin_ref[...])


def _prep_ffn(w_out, norm_ffn, w_gate, w_up, w_down, norm_final):
    depth = w_out.shape[0]
    return (w_out.astype(BF16), norm_ffn.reshape(depth, 1, D_MODEL), w_gate.astype(BF16),
            w_up.astype(BF16), w_down.astype(BF16), norm_final.reshape(1, D_MODEL))


def _outffn(x2, att, ssd, lru, prepared, layer, final_norm):
    t = x2.shape[0]
    row = lambda width: pl.BlockSpec((ROW_TILE, width), lambda i: (i, 0))
    lspec = lambda *shape: _layer_spec(shape, layer)
    in_hbm = pl.BlockSpec(memory_space=pl.ANY)
    bf = lambda *shape: pltpu.VMEM(shape, BF16)
    return pl.pallas_call(
        functools.partial(_outffn_kernel, final_norm=final_norm, layer=layer),
        grid=(t // ROW_TILE,),
        in_specs=[row(D_MODEL), row(ATT_WIDTH), row(SSD_WIDTH), row(LRU_WIDTH),
                  in_hbm, lspec(1, D_MODEL), in_hbm, in_hbm, in_hbm, _const_spec((1, D_MODEL))],
        out_specs=row(D_MODEL),
        out_shape=jax.ShapeDtypeStruct((t, D_MODEL), F32),
        scratch_shapes=[bf(ROW_TILE, D_MODEL), bf(D_MIX, D_MODEL), bf(D_MODEL, D_FF),
                        bf(D_MODEL, D_FF), bf(D_FF, D_MODEL),
                        pltpu.SemaphoreType.DMA((1 + 3 * len(FF_CHUNKS),))],
        compiler_params=_params(1),
        name="outffn",
    )(x2, att, ssd, lru, *prepared)


def _prep_w_in(w):
    tail = w[:, :, _C_XBC:]
    dt_cols = jnp.pad(tail[:, :, :SSD_HEADS], ((0, 0), (0, 0), (0, DT_PAD - SSD_HEADS)))
    w_tail = jnp.concatenate([tail[:, :, SSD_HEADS:], dt_cols], axis=2).astype(BF16)
    return w[:, :, :_C_XBC].astype(BF16), w_tail


def kernel(x, norm_mix, w_in, ssd_conv_w, ssd_conv_b, ssd_dt_bias, ssd_a_log, ssd_d, ssd_norm,
           lru_conv_w, lru_conv_b, lru_wa, lru_ba, lru_wx, lru_bx, lru_lambda, w_out,
           norm_ffn, w_gate, w_up, w_down, norm_final):
    batch, seq, _ = x.shape
    depth = w_in.shape[0]
    g_mix = norm_mix.reshape(depth, 1, D_MODEL)
    w_proj, w_tail = _prep_w_in(w_in)
    ssd_p = _prep_ssd(ssd_conv_w, ssd_conv_b, ssd_dt_bias, ssd_a_log, ssd_d, ssd_norm)
    lru_p = _prep_lru(lru_conv_w, lru_conv_b, lru_wa, lru_ba, lru_wx, lru_bx, lru_lambda)
    ffn_p = _prep_ffn(w_out, norm_ffn, w_gate, w_up, w_down, norm_final)
    x2 = x.reshape(batch * seq, D_MODEL)
    for l in range(depth):
        qkv, z, xbc, g_lru, x_lru, dt = _inproj(x2, g_mix, w_proj, w_tail, l)
        att = _dilated_attention(qkv, batch)
        ssd = _ssd(z, xbc, dt, ssd_p, l, batch)
        lru = _lru(g_lru, x_lru, lru_p, l, batch)
        x2 = _outffn(x2, att, ssd, lru, ffn_p, l, l == depth - 1)
    return x2.reshape(batch, seq, D_MODEL)
```

```python
import functools
import math

import jax
import jax.numpy as jnp
from jax import lax
from jax.experimental import pallas as pl
from jax.experimental.pallas import tpu as pltpu

F32 = jnp.float32
BF16 = jnp.bfloat16

D_MODEL = 1024
ATT_HEADS = 8
ATT_HEAD_DIM = 64
ATT_WIDTH = ATT_HEADS * ATT_HEAD_DIM
ATT_BLOCK = 128
ATT_DILATIONS = (1, 4, 16)
SSD_HEADS = 8
SSD_HEAD_DIM = 64
SSD_WIDTH = SSD_HEADS * SSD_HEAD_DIM
SSD_GROUPS = 2
SSD_STATE = 128
SSD_CHUNK = 128
SSD_CONV_CH = SSD_WIDTH + 2 * SSD_GROUPS * SSD_STATE
LRU_WIDTH = 512
LRU_BLOCKS = 8
LRU_BLOCK_W = LRU_WIDTH // LRU_BLOCKS
LRU_C = 8.0
CONV_WIDTH = 4
D_MIX = ATT_WIDTH + SSD_WIDTH + LRU_WIDTH
D_FF = 2816
NORM_EPS = 1e-6
SSD_NORM_EPS = 1e-5

LANES = 128
SUBLANES = 8
VMEM_LIMIT_BYTES = 56 * 1024 * 1024

ROW_TILE = 1024
LRU_TILE = 1024
SSD_STEP_CHUNKS = 4
DT_PAD = LANES
FF_CHUNKS = ((0, 768), (768, 1536), (1536, 2304), (2304, 2816))
CONV_PAD = SUBLANES

DIL_MID, DIL_MAX = ATT_DILATIONS[1], ATT_DILATIONS[2]
DIL_STEP = DIL_MAX // DIL_MID
ATT_TILE = ATT_BLOCK * DIL_MAX
MID_LEN = ATT_TILE // DIL_MID
HEADS_PER_PAIR = LANES // ATT_HEAD_DIM
ATT_PAIRS = ATT_WIDTH // LANES
ATT_UNROLL = 16
assert ATT_HEADS == 8 and HEADS_PER_PAIR == 2 and DIL_STEP == DIL_MID and ATT_DILATIONS[0] == 1

LOG2E = 1.4426950408889634
NEG_INF = float("-inf")


def _params(n_axes):
    return pltpu.CompilerParams(
        dimension_semantics=("arbitrary",) * n_axes,
        vmem_limit_bytes=VMEM_LIMIT_BYTES)


def _const_spec(shape):
    nd = len(shape)
    return pl.BlockSpec(shape, lambda *_: (0,) * nd, pipeline_mode=pl.Buffered(1))


def _layer_spec(shape, layer):
    nd = len(shape)
    return pl.BlockSpec((None,) + tuple(shape), lambda *_: (layer,) + (0,) * nd,
                        pipeline_mode=pl.Buffered(1))


def _rmsnorm(x, g):
    return x * lax.rsqrt(jnp.mean(x * x, axis=-1, keepdims=True) + NORM_EPS) * g


def _softplus(x):
    return jnp.maximum(x, 0.0) + jnp.log1p(jnp.exp(-jnp.abs(x)))


def _silu(x):
    return x * jax.nn.sigmoid(x)


_C_QKV = 3 * ATT_WIDTH
_C_Z = _C_QKV + SSD_WIDTH
_C_XBC = _C_Z + SSD_CONV_CH
_T_XL = LRU_WIDTH
_T_DT = 2 * LRU_WIDTH
_T_END = _T_DT + DT_PAD


def _inproj_kernel(x_ref, g_ref, w_ref, wt_ref, qkv_ref, z_ref, xbc_ref, gl_ref, xl_ref, dt_ref):
    h = _rmsnorm(x_ref[...], g_ref[...]).astype(BF16)

    def seg(ref, a, b):
        return jnp.dot(h, ref[:, a:b], preferred_element_type=F32)

    qkv_ref[:, 0:ATT_WIDTH] = seg(w_ref, 0, ATT_WIDTH) * (ATT_HEAD_DIM ** -0.5 * LOG2E)
    qkv_ref[:, ATT_WIDTH:_C_QKV] = seg(w_ref, ATT_WIDTH, _C_QKV)
    z_ref[...] = _silu(seg(w_ref, _C_QKV, _C_Z))
    xbc_ref[...] = seg(w_ref, _C_Z, _C_XBC)
    gl_ref[...] = seg(wt_ref, 0, _T_XL)
    xl_ref[...] = seg(wt_ref, _T_XL, _T_DT)
    dt_ref[...] = seg(wt_ref, _T_DT, _T_END)


def _inproj(x2, g, w, w_tail, layer):
    t = x2.shape[0]
    row = lambda width: pl.BlockSpec((ROW_TILE, width), lambda i: (i, 0))
    widths = (_C_QKV, SSD_WIDTH, SSD_CONV_CH, LRU_WIDTH, LRU_WIDTH, DT_PAD)
    return pl.pallas_call(
        _inproj_kernel,
        grid=(t // ROW_TILE,),
        in_specs=[row(D_MODEL), _layer_spec((1, D_MODEL), layer),
                  _layer_spec((D_MODEL, _C_XBC), layer),
                  _layer_spec((D_MODEL, _T_END), layer)],
        out_specs=[row(wd) for wd in widths],
        out_shape=[jax.ShapeDtypeStruct((t, wd), F32) for wd in widths],
        compiler_params=_params(1),
        name="inproj",
    )(x2, g, w, w_tail)


def _scores(q_lo, q_hi, kk, bias):
    q_both = jnp.concatenate([q_lo, q_hi], axis=0)
    return lax.dot_general(q_both, kk, (((1,), (1,)), ((), ())),
                           preferred_element_type=F32) + bias


def _attn_kernel(q_ref, k_ref, v_ref, att_ref,
                 qn_s, kn_s, vn_s, tmp_s, qm_s, km_s, vm_s, qx_s, kx_s, vx_s,
                 mn_s, mm_s, mx_s, acc_s, bias_s, sc_s):
    pair = pl.program_id(1)
    tile = pl.program_id(2)
    blk = ATT_BLOCK
    first_tile = tile == 0
    mid_blocks = MID_LEN // blk

    @pl.when(first_tile)
    def _start_sequence():
        qi = lax.broadcasted_iota(jnp.int32, (blk, 2 * blk), 0)
        ki = lax.broadcasted_iota(jnp.int32, (blk, 2 * blk), 1)
        dist = blk + qi - ki
        band = (dist >= 0) & (dist <= blk)
        band_first = band & (ki >= blk)
        for hh in range(HEADS_PER_PAIR):
            expo = (126 - HEADS_PER_PAIR * pair - hh) << 23
            slope = lax.bitcast_convert_type(jnp.full((blk, 2 * blk), expo, jnp.int32), F32)
            rows = slice(hh * blk, (hh + 1) * blk)
            for br, dil in enumerate((DIL_MAX, DIL_MID, 1)):
                alibi = (-slope * (dil * dist).astype(F32)) * LOG2E
                bias_s[2 * br, rows, :] = jnp.where(band, alibi, NEG_INF)
                bias_s[2 * br + 1, rows, :] = jnp.where(band_first, alibi, NEG_INF)
        kn_s[0:blk, :] = jnp.zeros((blk, LANES), BF16)
        vn_s[:, 0:blk, :] = jnp.zeros((2, blk, LANES), BF16)
        km_s[:, 0:blk, :] = jnp.zeros((DIL_MID, blk, LANES), BF16)
        vm_s[:, :, 0:blk, :] = jnp.zeros((2, DIL_MID, blk, LANES), BF16)
        kx_s[:, 0:blk, :] = jnp.zeros((DIL_MAX, blk, LANES), BF16)
        vx_s[:, :, 0:blk, :] = jnp.zeros((2, DIL_MAX, blk, LANES), BF16)

    @pl.when(tile > 0)
    def _carry_history():
        kn_s[0:blk, :] = kn_s[ATT_TILE:ATT_TILE + blk, :]
        vn_s[:, 0:blk, :] = vn_s[:, ATT_TILE:ATT_TILE + blk, :]
        km_s[:, 0:blk, :] = km_s[:, MID_LEN:MID_LEN + blk, :]
        vm_s[:, :, 0:blk, :] = vm_s[:, :, MID_LEN:MID_LEN + blk, :]
        kx_s[:, 0:blk, :] = kx_s[:, blk:2 * blk, :]
        vx_s[:, :, 0:blk, :] = vx_s[:, :, blk:2 * blk, :]

    lane_m = lax.broadcasted_iota(jnp.int32, (MID_LEN, LANES), 1) < ATT_HEAD_DIM
    lane_b = lax.broadcasted_iota(jnp.int32, (blk, LANES), 1) < ATT_HEAD_DIM

    def put_q(dst, idx, x, mask):
        dst[(0,) + idx] = jnp.where(mask, x, 0.0).astype(BF16)
        dst[(1,) + idx] = jnp.where(mask, 0.0, x).astype(BF16)

    def put_k(dst, idx, x, mask):
        del mask
        dst[idx] = x.astype(BF16)

    def put_v(dst, idx, x, mask):
        dst[(0,) + idx] = jnp.where(mask, x, 1.0).astype(BF16)
        dst[(1,) + idx] = jnp.where(mask, 1.0, x).astype(BF16)

    for src, nat, mid, big, put, hist in (
            (q_ref, qn_s, qm_s, qx_s, put_q, 0),
            (k_ref, kn_s, km_s, kx_s, put_k, blk),
            (v_ref, vn_s, vm_s, vx_s, put_v, blk)):
        for c in range(DIL_MID):
            put(nat, (slice(hist + c * MID_LEN, hist + (c + 1) * MID_LEN), slice(None)),
                src[c * MID_LEN:(c + 1) * MID_LEN, :], lane_m)
        for r in range(DIL_MID):
            x = src[pl.ds(r, MID_LEN, stride=DIL_MID), :]
            tmp_s[r] = x
            put(mid, (r, slice(hist, hist + MID_LEN), slice(None)), x, lane_m)
        for r in range(DIL_MID):
            for c in range(DIL_STEP):
                y = tmp_s.at[r][pl.ds(c, blk, stride=DIL_STEP), :]
                put(big, (r + DIL_MID * c, slice(hist, hist + blk), slice(None)), y, lane_b)

    def bias_of(branch, is_first):
        return bias_s[2 * branch + jnp.where(is_first, 1, 0)]

    def big_ops(r):
        return (qx_s[0, r], qx_s[1, r], kx_s[r], bias_of(0, first_tile),
                pl.ds(r, blk, stride=DIL_MAX))

    def mid_ops(i):
        r = i // mid_blocks
        j = i % mid_blocks
        q0 = pl.multiple_of(j * blk, blk)
        return (qm_s[0, r, pl.ds(q0, blk), :], qm_s[1, r, pl.ds(q0, blk), :],
                km_s[r, pl.ds(q0, 2 * blk), :], bias_of(1, first_tile & (j == 0)),
                pl.ds(j * (blk * DIL_MID) + r, blk, stride=DIL_MID))

    def nat_ops(j):
        q0 = pl.multiple_of(j * blk, blk)
        return (qn_s[0, pl.ds(q0, blk), :], qn_s[1, pl.ds(q0, blk), :],
                kn_s[pl.ds(q0, 2 * blk), :], bias_of(2, first_tile & (j == 0)),
                pl.ds(q0, blk))

    def loop(n, body):
        lax.fori_loop(0, n, lambda i, c: (body(i), c)[1], 0, unroll=ATT_UNROLL)

    n_big, n_mid, n_nat = DIL_MAX, DIL_MID * mid_blocks, ATT_TILE // blk

    def row_max(ops, slot):
        q_lo, q_hi, kk, bias, rows = ops
        s = _scores(q_lo, q_hi, kk, bias)
        sc_s[slot] = s
        m = jnp.broadcast_to(jnp.max(s, axis=-1, keepdims=True), (2 * blk, LANES))
        return jnp.where(lane_b, m[:blk], m[blk:]), rows

    def max_big(r):
        m, rows = row_max(big_ops(r), r)
        mn_s[rows, :] = m

    def max_more(ops, slot):
        m, rows = row_max(ops, slot)
        mn_s[rows, :] = jnp.maximum(mn_s[rows, :], m)

    loop(n_big, max_big)
    loop(n_mid, lambda i: max_more(mid_ops(i), n_big + i))
    loop(n_nat, lambda j: max_more(nat_ops(j), n_big + n_mid + j))

    for r in range(DIL_MID):
        mm_s[r] = mn_s[pl.ds(r, MID_LEN, stride=DIL_MID), :]
    for r in range(DIL_MID):
        for c in range(DIL_STEP):
            mx_s[r + DIL_MID * c] = mm_s.at[r][pl.ds(c, blk, stride=DIL_STEP), :]

    def num_den(slot, m_pair, v_lo, v_hi):
        swapped = pltpu.roll(m_pair, ATT_HEAD_DIM, 1)
        m = jnp.concatenate([jnp.where(lane_b, m_pair, swapped),
                             jnp.where(lane_b, swapped, m_pair)], axis=0)
        e = jnp.exp2(sc_s[slot] - jnp.concatenate([m, m], axis=1)).astype(BF16)
        return (jnp.dot(e[:blk], v_lo, preferred_element_type=F32),
                jnp.dot(e[blk:], v_hi, preferred_element_type=F32))

    def acc_big(r):
        o_lo, o_hi = num_den(r, mx_s[r], vx_s[0, r], vx_s[1, r])
        rows = pl.ds(r, blk, stride=DIL_MAX)
        acc_s[0, rows, :] = o_lo
        acc_s[1, rows, :] = o_hi

    def acc_mid(i):
        r = i // mid_blocks
        j = i % mid_blocks
        q0 = pl.multiple_of(j * blk, blk)
        o_lo, o_hi = num_den(
            n_big + i, mm_s[r, pl.ds(q0, blk), :],
            vm_s[0, r, pl.ds(q0, 2 * blk), :], vm_s[1, r, pl.ds(q0, 2 * blk), :])
        rows = pl.ds(j * (blk * DIL_MID) + r, blk, stride=DIL_MID)
        acc_s[0, rows, :] += o_lo
        acc_s[1, rows, :] += o_hi

    def acc_nat(j):
        q0 = pl.multiple_of(j * blk, blk)
        rows = pl.ds(q0, blk)
        o_lo, o_hi = num_den(
            n_big + n_mid + j, mn_s[rows, :],
            vn_s[0, pl.ds(q0, 2 * blk), :], vn_s[1, pl.ds(q0, 2 * blk), :])
        a_lo = acc_s[0, rows, :] + o_lo
        a_hi = acc_s[1, rows, :] + o_hi
        num = jnp.where(lane_b, a_lo, a_hi)
        den = pltpu.roll(jnp.where(lane_b, a_hi, a_lo), ATT_HEAD_DIM, 1)
        att_ref[rows, :] = (num / den).astype(BF16)

    @pl.when(tile >= 0)
    def _pass_two():
        loop(n_big, acc_big)
        loop(n_mid, acc_mid)
        loop(n_nat, acc_nat)


def _dilated_attention(qkv, batch):
    t = qkv.shape[0]
    tiles = t // batch // ATT_TILE
    blk = ATT_BLOCK

    def spec(col0):
        return pl.BlockSpec((ATT_TILE, LANES), lambda b, p, i: (b * tiles + i, col0 + p))

    bf = lambda *shape: pltpu.VMEM(shape, BF16)
    f32 = lambda *shape: pltpu.VMEM(shape, F32)
    return pl.pallas_call(
        _attn_kernel,
        grid=(batch, ATT_PAIRS, tiles),
        in_specs=[spec(0), spec(ATT_PAIRS), spec(2 * ATT_PAIRS)],
        out_specs=spec(0),
        out_shape=jax.ShapeDtypeStruct((t, ATT_WIDTH), BF16),
        scratch_shapes=[
            bf(2, ATT_TILE, LANES), bf(blk + ATT_TILE, LANES), bf(2, blk + ATT_TILE, LANES),
            f32(DIL_MID, MID_LEN, LANES),
            bf(2, DIL_MID, MID_LEN, LANES), bf(DIL_MID, blk + MID_LEN, LANES),
            bf(2, DIL_MID, blk + MID_LEN, LANES),
            bf(2, DIL_MAX, blk, LANES), bf(DIL_MAX, 2 * blk, LANES),
            bf(2, DIL_MAX, 2 * blk, LANES),
            f32(ATT_TILE, LANES), f32(DIL_MID, MID_LEN, LANES),
            f32(DIL_MAX, blk, LANES), f32(2, ATT_TILE, LANES),
            f32(3 * 2, 2 * blk, 2 * blk),
            f32(3 * DIL_MAX, 2 * blk, 2 * blk),
        ],
        compiler_params=_params(3),
        name="attn",
    )(qkv, qkv, qkv)


def _split3(x):
    hi = x.astype(BF16)
    r1 = x - hi.astype(F32)
    mid = r1.astype(BF16)
    lo = (r1 - mid.astype(F32)).astype(BF16)
    return hi, mid, lo


def _dot_exact_rhs(x, w):
    return sum(jnp.dot(part, w, preferred_element_type=F32) for part in _split3(x))


TM_CHUNK = 128
TM_STEPS = TM_CHUNK // SUBLANES
TM_HIST = (CONV_WIDTH - 1) * SUBLANES
assert TM_CHUNK == SSD_CHUNK


def _to_time_major(src, mid, dst, base):
    quarter = TM_CHUNK // 4
    for r in range(4):
        mid[base + quarter * r:base + quarter * (r + 1), :] = (
            src[pl.ds(base + r, quarter, stride=4), :])
    for r in range(4):
        for c in range(4):
            v = r + 4 * c
            dst[base + SUBLANES * v:base + SUBLANES * (v + 1), :] = (
                mid[pl.ds(base + quarter * r + c, SUBLANES, stride=4), :])


def _tm_time(idx):
    return TM_STEPS * (idx % SUBLANES) + idx // SUBLANES


def _time_major_perm():
    n = jnp.arange(TM_CHUNK)
    return (_tm_time(n)[None, :] == n[:, None]).astype(BF16)


def _conv_time_major(x, hist, cw_ref, cb_ref, first_row):
    width = x.shape[-1]
    groups = TM_HIST // SUBLANES
    tail = x[TM_CHUNK - TM_HIST:, :]
    down = lambda a: pltpu.roll(a.reshape(groups, SUBLANES, width), 1, 1)
    wrapped = jnp.where(first_row, down(hist), down(tail)).reshape(TM_HIST, width)
    ext = jnp.concatenate([wrapped, x], axis=0)
    k_w = CONV_WIDTH
    conv = cb_ref[...] + x * cw_ref[k_w - 1:k_w, :]
    for back in range(1, k_w):
        off = TM_HIST - SUBLANES * back
        conv = conv + ext[off:off + TM_CHUNK, :] * cw_ref[k_w - 1 - back:k_w - back, :]
    return conv, tail


def _ssd_kernel(z_ref, xbc_ref, dt_ref, cw_ref, cb_ref, dtb_ref, alog_ref, dsk_ref, nw_ref,
                triu_ref, exp_ref, y_ref, win_s, state_s):
    c = pl.program_id(0)
    q = SSD_CHUNK
    pad = CONV_PAD
    rows_step = SSD_STEP_CHUNKS * q
    heads_per_group = SSD_HEADS // SSD_GROUPS
    gw = heads_per_group * SSD_HEAD_DIM
    n_seq = z_ref.shape[0]

    @pl.when(c == 0)
    def _reset():
        win_s[:, 0:pad, :] = jnp.zeros((n_seq, pad, SSD_CONV_CH), F32)
        state_s[...] = jnp.zeros_like(state_s)

    win_s[:, pad:, :] = xbc_ref[...]

    ti = lax.broadcasted_iota(jnp.int32, (q, q), 0)
    tj = lax.broadcasted_iota(jnp.int32, (q, q), 1)
    causal = ti >= tj
    low_half = lax.broadcasted_iota(jnp.int32, (q, LANES), 1) < SSD_HEAD_DIM
    a_t = -jnp.exp(alog_ref[...])
    k_w = CONV_WIDTH

    pairs_per_group = heads_per_group // HEADS_PER_PAIR

    def prepare(ci, b):
        r0 = ci * q
        win_b, dt_b = win_s.at[b], dt_ref.at[b]
        conv = cb_ref[...] + win_b[pad + r0:pad + r0 + q, :] * cw_ref[k_w - 1:k_w, :]
        for k in range(k_w - 1):
            off = pad + r0 - (k_w - 1) + k
            conv = conv + win_b[off:off + q, :] * cw_ref[k:k + 1, :]
        xbc = _silu(conv)
        xs = xbc[:, :SSD_WIDTH]
        bm = xbc[:, SSD_WIDTH:SSD_WIDTH + SSD_GROUPS * SSD_STATE]
        cm = xbc[:, SSD_WIDTH + SSD_GROUPS * SSD_STATE:]

        dt_t = _softplus(dt_b[r0:r0 + q, :].T[0:SSD_HEADS, :] + dtb_ref[...])
        acs_t = _dot_exact_rhs(dt_t * a_t, triu_ref[...])
        e_t = jnp.exp(acs_t)
        w_t = jnp.exp(acs_t[:, q - 1:q] - acs_t) * dt_t
        chunk_decay = jnp.sum(e_t[:, q - 1:q] * exp_ref[...], axis=0, keepdims=True)
        cols_t = jnp.concatenate(
            [acs_t, e_t, jnp.zeros((q - 2 * SSD_HEADS, q), F32)], axis=0).T

        pairs = []
        for g in range(SSD_GROUPS):
            bm_g = bm[:, g * SSD_STATE:(g + 1) * SSD_STATE]
            cm_g = cm[:, g * SSD_STATE:(g + 1) * SSD_STATE]
            gmat = lax.dot_general(cm_g.astype(BF16), bm_g.astype(BF16),
                                   (((1,), (1,)), ((), ())), preferred_element_type=F32)
            bm_gt = bm_g.T
            for pp in range(pairs_per_group):
                p = g * pairs_per_group + pp
                x_p = xs[:, p * LANES:(p + 1) * LANES]
                lhs_y, lhs_s, x_heads = [], [], []
                for hh in range(HEADS_PER_PAIR):
                    h = HEADS_PER_PAIR * p + hh
                    keep = low_half if hh == 0 else ~low_half
                    seg = cols_t[:, h:h + 1] - acs_t[h:h + 1, :]
                    lmat = jnp.exp(jnp.where(causal, seg, NEG_INF))
                    scores = gmat * lmat * dt_t[h:h + 1, :]
                    c_dec = cm_g * cols_t[:, SSD_HEADS + h:SSD_HEADS + h + 1]
                    lhs_y += [scores.astype(BF16), c_dec.astype(BF16)]
                    lhs_s.append((bm_gt * w_t[h:h + 1, :]).astype(BF16))
                    x_heads.append(jnp.where(keep, x_p, 0.0).astype(BF16))
                pairs.append((jnp.concatenate(lhs_y, axis=1), jnp.concatenate(lhs_s, axis=1),
                              x_heads))
        return xs, chunk_decay, pairs

    def finish(ci, b, prepared):
        xs, chunk_decay, pairs = prepared
        r0 = ci * q
        state_b, z_b, y_b = state_s.at[b], z_ref.at[b], y_ref.at[b]
        y_parts = []
        for p, (lhs_y, lhs_s, x_heads) in enumerate(pairs):
            cols = slice(p * LANES, (p + 1) * LANES)
            s_p = state_b[:, cols]
            rhs_y = []
            for hh in range(HEADS_PER_PAIR):
                keep = low_half if hh == 0 else ~low_half
                rhs_y += [x_heads[hh], jnp.where(keep, s_p, 0.0).astype(BF16)]
            y_parts.append(jnp.dot(lhs_y, jnp.concatenate(rhs_y, axis=0),
                                   preferred_element_type=F32))
            state_b[:, cols] = s_p * chunk_decay[:, cols] + jnp.dot(
                lhs_s, jnp.concatenate(x_heads, axis=0), preferred_element_type=F32)
        y = jnp.concatenate(y_parts, axis=-1) + dsk_ref[...] * xs
        y = y * z_b[r0:r0 + q, :]
        outs = []
        for g in range(SSD_GROUPS):
            yg = y[:, g * gw:(g + 1) * gw]
            outs.append(yg * lax.rsqrt(jnp.mean(yg * yg, axis=-1, keepdims=True) + SSD_NORM_EPS))
        y_b[r0:r0 + q, :] = (jnp.concatenate(outs, axis=-1) * nw_ref[...]).astype(BF16)

    order = [(ci, b) for ci in range(SSD_STEP_CHUNKS) for b in range(n_seq)]
    prepared = [prepare(ci, b) for ci, b in order]
    for (ci, b), prep in zip(order, prepared):
        finish(ci, b, prep)

    win_s[:, 0:pad, :] = win_s[:, rows_step:rows_step + pad, :]


def _prep_ssd(conv_w, conv_b, dt_bias, a_log, d_skip, norm_w):
    depth = conv_w.shape[0]
    per_time = lambda v: jnp.broadcast_to(v.astype(F32)[:, :, None],
                                          (depth, SSD_HEADS, SSD_CHUNK))
    triu = jnp.triu(jnp.ones((SSD_CHUNK, SSD_CHUNK), BF16))
    expand = (jnp.arange(SSD_HEADS)[:, None] == (jnp.arange(SSD_WIDTH)[None, :] // SSD_HEAD_DIM)
              ).astype(F32)
    d_exp = jnp.repeat(d_skip.astype(F32), SSD_HEAD_DIM, axis=1).reshape(depth, 1, SSD_WIDTH)
    return (conv_w, conv_b.reshape(depth, 1, SSD_CONV_CH), per_time(dt_bias), per_time(a_log),
            d_exp, norm_w.reshape(depth, 1, SSD_WIDTH), triu, expand)


def _ssd(z, xbc, dt, prepared, layer, batch):
    t = z.shape[0]
    seq = t // batch
    rows_step = SSD_STEP_CHUNKS * SSD_CHUNK
    row = lambda width: pl.BlockSpec((batch, rows_step, width), lambda c: (0, c, 0))
    by_seq = lambda a: a.reshape(batch, seq, a.shape[-1])
    lspec = lambda *shape: _layer_spec(shape, layer)
    out = pl.pallas_call(
        _ssd_kernel,
        grid=(seq // rows_step,),
        in_specs=[row(SSD_WIDTH), row(SSD_CONV_CH), row(DT_PAD),
                  lspec(CONV_WIDTH, SSD_CONV_CH), lspec(1, SSD_CONV_CH),
                  lspec(SSD_HEADS, SSD_CHUNK), lspec(SSD_HEADS, SSD_CHUNK),
                  lspec(1, SSD_WIDTH), lspec(1, SSD_WIDTH),
                  _const_spec((SSD_CHUNK, SSD_CHUNK)), _const_spec((SSD_HEADS, SSD_WIDTH))],
        out_specs=row(SSD_WIDTH),
        out_shape=jax.ShapeDtypeStruct((batch, seq, SSD_WIDTH), BF16),
        scratch_shapes=[pltpu.VMEM((batch, CONV_PAD + rows_step, SSD_CONV_CH), F32),
                        pltpu.VMEM((batch, SSD_STATE, SSD_WIDTH), F32)],
        compiler_params=_params(1),
        name="ssd",
    )(by_seq(z), by_seq(xbc), by_seq(dt), *prepared)
    return out.reshape(t, SSD_WIDTH)


def _gelu_tanh(x):
    c = math.sqrt(2.0 / math.pi)
    return 0.5 * x * (1.0 + jnp.tanh(c * (x + 0.044715 * (x * x * x))))


def _lru_kernel(*refs):
    ncol = LRU_WIDTH // LANES
    g_refs, x_refs = refs[:ncol], refs[ncol:2 * ncol]
    (cw_ref, cb_ref, wa_ref, ba_ref, wx_ref, bx_ref, lam_ref, perm_ref, y_ref,
     mid_s, gt_s, xt_s, hist_s, h_s) = refs[2 * ncol:]
    c = pl.program_id(1)
    tt = LRU_TILE
    n_chunks = tt // TM_CHUNK

    @pl.when(c == 0)
    def _reset():
        hist_s[...] = jnp.zeros_like(hist_s)
        h_s[...] = jnp.zeros_like(h_s)

    for j in range(ncol):
        for k in range(n_chunks):
            _to_time_major(g_refs[j], mid_s.at[j], gt_s.at[j], k * TM_CHUNK)
        for k in range(n_chunks):
            _to_time_major(x_refs[j], mid_s.at[j], xt_s.at[j], k * TM_CHUNK)

    def chunk_of(buf, k):
        rows = slice(k * TM_CHUNK, (k + 1) * TM_CHUNK)
        return jnp.concatenate([buf[j, rows, :] for j in range(ncol)], axis=1)

    first_row = lax.broadcasted_iota(
        jnp.int32, (TM_HIST // SUBLANES, SUBLANES, LRU_WIDTH), 1) == 0
    hist = hist_s[...]
    convs = []
    for k in range(n_chunks):
        conv, hist = _conv_time_major(chunk_of(xt_s, k), hist, cw_ref, cb_ref, first_row)
        convs.append(conv)
    hist_s[...] = hist
    xc = jnp.concatenate(convs, axis=0)

    xb = xc.astype(BF16)
    r = jax.nn.sigmoid(jnp.dot(xb, wa_ref[...], preferred_element_type=F32) + ba_ref[...])
    i = jax.nn.sigmoid(jnp.dot(xb, wx_ref[...], preferred_element_type=F32) + bx_ref[...])
    log_a = -LRU_C * r * _softplus(-lam_ref[...])
    a = jnp.exp(log_a)
    u = jnp.sqrt(-jnp.tanh(log_a) * (a * a + 1.0)) * (i * xc)

    sub = lax.broadcasted_iota(jnp.int32, (SUBLANES, LRU_WIDTH), 0)
    h_prev = h_s[0:1, :]
    for k in range(n_chunks):
        rows = slice(k * TM_CHUNK, (k + 1) * TM_CHUNK)
        a_k = a[rows].reshape(TM_STEPS, SUBLANES, LRU_WIDTH)
        u_k = u[rows].reshape(TM_STEPS, SUBLANES, LRU_WIDTH)
        hh, aa = [u_k[0]], [a_k[0]]
        for v in range(1, TM_STEPS):
            hh.append(a_k[v] * hh[-1] + u_k[v])
            aa.append(a_k[v] * aa[-1])
        pa, ph = aa[-1], hh[-1]
        step = 1
        while step < SUBLANES:
            keep = sub >= step
            pa_sh = jnp.where(keep, pltpu.roll(pa, step, 0), 1.0)
            ph_sh = jnp.where(keep, pltpu.roll(ph, step, 0), 0.0)
            ph = pa * ph_sh + ph
            pa = pa * pa_sh
            step *= 2
        c_in = jnp.where(sub == 0, h_prev, pltpu.roll(pa, 1, 0) * h_prev + pltpu.roll(ph, 1, 0))
        h_k = [hh[v] + aa[v] * c_in for v in range(TM_STEPS)]
        h_prev = h_k[-1][SUBLANES - 1:SUBLANES, :]
        y_k = (jnp.concatenate(h_k, axis=0) * _gelu_tanh(chunk_of(gt_s, k))).astype(BF16)
        y_ref[rows, :] = jnp.dot(perm_ref[...], y_k, preferred_element_type=F32).astype(BF16)
    h_s[...] = jnp.broadcast_to(h_prev, h_s.shape)


def _block_diag(w):
    depth, nb, c, d = w.shape
    eye = jnp.eye(nb, dtype=w.dtype)
    return (eye[None, :, None, :, None] * w[:, :, :, None, :]).reshape(depth, nb * c, nb * d)


def _prep_lru(conv_w, conv_b, wa, ba, wx, bx, lam):
    depth = conv_w.shape[0]
    vec = lambda v: v.astype(F32).reshape(depth, 1, LRU_WIDTH)
    return (conv_w, vec(conv_b), _block_diag(wa).astype(BF16), vec(ba),
            _block_diag(wx).astype(BF16), vec(bx), vec(lam))


def _lru(g_in, x_in, prepared, layer, batch):
    t = g_in.shape[0]
    s = t // batch
    nt = s // LRU_TILE
    ncol = LRU_WIDTH // LANES
    row = pl.BlockSpec((LRU_TILE, LRU_WIDTH), lambda b, c: (b * nt + c, 0))
    cols = [pl.BlockSpec((LRU_TILE, LANES), lambda b, c, j=j: (b * nt + c, j))
            for j in range(ncol)]
    mat = _layer_spec((LRU_WIDTH, LRU_WIDTH), layer)
    one = _layer_spec((1, LRU_WIDTH), layer)
    tile = lambda: pltpu.VMEM((ncol, LRU_TILE, LANES), F32)
    return pl.pallas_call(
        _lru_kernel,
        grid=(batch, nt),
        in_specs=cols + cols + [_layer_spec((CONV_WIDTH, LRU_WIDTH), layer), one, mat, one,
                                mat, one, one, _const_spec((TM_CHUNK, TM_CHUNK))],
        out_specs=row,
        out_shape=jax.ShapeDtypeStruct((t, LRU_WIDTH), BF16),
        scratch_shapes=[tile(), tile(), tile(),
                        pltpu.VMEM((TM_HIST, LRU_WIDTH), F32),
                        pltpu.VMEM((SUBLANES, LRU_WIDTH), F32)],
        compiler_params=_params(2),
        name="rglru",
    )(*([g_in] * ncol), *([x_in] * ncol), *prepared, _time_major_perm())


def _outffn_kernel(x_ref, att_ref, ssd_ref, lru_ref, wo_ref, gf_ref, wg_ref, wu_ref, wd_ref,
                   nfin_ref, o_ref, hn_s, *, final_norm):
    x1 = x_ref[...]
    for j, m_ref in enumerate((att_ref, ssd_ref, lru_ref)):
        x1 = x1 + jnp.dot(m_ref[...], wo_ref[j * ATT_WIDTH:(j + 1) * ATT_WIDTH, :],
                          preferred_element_type=F32)
    hn_s[...] = _rmsnorm(x1, gf_ref[...]).astype(BF16)
    o_ref[...] = x1
    for c0, c1 in FF_CHUNKS:
        hn = hn_s[...]
        gate = jnp.dot(hn, wg_ref[:, c0:c1], preferred_element_type=F32)
        up = jnp.dot(hn, wu_ref[:, c0:c1], preferred_element_type=F32)
        act = (_silu(gate) * up).astype(BF16)
        o_ref[...] += jnp.dot(act, wd_ref[c0:c1, :], preferred_element_type=F32)
    if final_norm:
        o_ref[...] = _rmsnorm(o_ref[...], nfin_ref[...])


def _prep_ffn(w_out, norm_ffn, w_gate, w_up, w_down, norm_final):
    depth = w_out.shape[0]
    return (w_out.astype(BF16), norm_ffn.reshape(depth, 1, D_MODEL), w_gate.astype(BF16),
            w_up.astype(BF16), w_down.astype(BF16), norm_final.reshape(1, D_MODEL))


def _outffn(x2, att, ssd, lru, prepared, layer, final_norm):
    t = x2.shape[0]
    row = lambda width: pl.BlockSpec((ROW_TILE, width), lambda i: (i, 0))
    lspec = lambda *shape: _layer_spec(shape, layer)
    return pl.pallas_call(
        functools.partial(_outffn_kernel, final_norm=final_norm),
        grid=(t // ROW_TILE,),
        in_specs=[row(D_MODEL), row(ATT_WIDTH), row(SSD_WIDTH), row(LRU_WIDTH),
                  lspec(D_MIX, D_MODEL), lspec(1, D_MODEL), lspec(D_MODEL, D_FF),
                  lspec(D_MODEL, D_FF), lspec(D_FF, D_MODEL), _const_spec((1, D_MODEL))],
        out_specs=row(D_MODEL),
        out_shape=jax.ShapeDtypeStruct((t, D_MODEL), F32),
        scratch_shapes=[pltpu.VMEM((ROW_TILE, D_MODEL), BF16)],
        compiler_params=_params(1),
        name="outffn",
    )(x2, att, ssd, lru, *prepared)


def _prep_w_in(w):
    tail = w[:, :, _C_XBC:]
    dt_cols = jnp.pad(tail[:, :, :SSD_HEADS], ((0, 0), (0, 0), (0, DT_PAD - SSD_HEADS)))
    w_tail = jnp.concatenate([tail[:, :, SSD_HEADS:], dt_cols], axis=2).astype(BF16)
    return w[:, :, :_C_XBC].astype(BF16), w_tail


def kernel(x, norm_mix, w_in, ssd_conv_w, ssd_conv_b, ssd_dt_bias, ssd_a_log, ssd_d, ssd_norm,
           lru_conv_w, lru_conv_b, lru_wa, lru_ba, lru_wx, lru_bx, lru_lambda, w_out,
           norm_ffn, w_gate, w_up, w_down, norm_final):
    batch, seq, _ = x.shape
    depth = w_in.shape[0]
    g_mix = norm_mix.reshape(depth, 1, D_MODEL)
    w_proj, w_tail = _prep_w_in(w_in)
    ssd_p = _prep_ssd(ssd_conv_w, ssd_conv_b, ssd_dt_bias, ssd_a_log, ssd_d, ssd_norm)
    lru_p = _prep_lru(lru_conv_w, lru_conv_b, lru_wa, lru_ba, lru_wx, lru_bx, lru_lambda)
    ffn_p = _prep_ffn(w_out, norm_ffn, w_gate, w_up, w_down, norm_final)
    x2 = x.reshape(batch * seq, D_MODEL)
    for l in range(depth):
        qkv, z, xbc, g_lru, x_lru, dt = _inproj(x2, g_mix, w_proj, w_tail, l)
        att = _dilated_attention(qkv, batch)
        ssd = _ssd(z, xbc, dt, ssd_p, l, batch)
        lru = _lru(g_lru, x_lru, lru_p, l, batch)
        x2 = _outffn(x2, att, ssd, lru, ffn_p, l, l == depth - 1)
    return x2.reshape(batch, seq, D_MODEL)
```
